```python
import math
import jax, jax.numpy as jnp
from jax import lax
import numpy as np

D_MODEL = 1024
BATCH = 4
SEQ = 8192
DEPTH = 1
DEC_BATCH = 32
DEC_SEQ = 4
PAST_LEN = 16384
PAGE_SIZE = 128

HEAD_DIM = 64
HEADS_PER_GROUP = 4
DIL_PATTERNS = ((128, 1), (512, 4), (2048, 16))
N_DIL_GROUPS = 3
N_ATTN_HEADS = HEADS_PER_GROUP * N_DIL_GROUPS
QK_WIDTH = N_ATTN_HEADS * HEAD_DIM
ATTN_OUT_WIDTH = HEADS_PER_GROUP * HEAD_DIM
QBLOCK = 128
SSM_GROUP = 16
SSM_STATE = 64
SSM_WIDTH = D_MODEL // 2
N_SSM_GROUPS = SSM_WIDTH // SSM_GROUP
DT_MIN = 0.001
DT_MAX = 0.1
D_FF = 2816
CONV_W = 3
N_BUCKETS = 32
MAX_DISTANCE = 2048
NORM_EPS = 1e-6
NEG_INF = -1e30
U_START = 3 * QK_WIDTH
GA_START = U_START + SSM_WIDTH
GS_START = GA_START + D_MODEL
IN_WIDTH = GS_START + D_MODEL

kernel_name = 'dilated_attn_s5_convffn_hybrid_step'


def rmsnorm(x, g):
    xf = x.astype(jnp.float32)
    y = xf * lax.rsqrt(jnp.mean(xf * xf, axis=-1, keepdims=True) + NORM_EPS)
    return (y * g.astype(jnp.float32)).astype(x.dtype)


def rel_bucket(dist):
    max_exact = N_BUCKETS // 2
    n = jnp.maximum(dist, 0)
    nf = jnp.maximum(n, 1).astype(jnp.float32)
    large = max_exact + (jnp.log(nf / max_exact) / math.log(MAX_DISTANCE / max_exact)
                         * (N_BUCKETS - max_exact)).astype(jnp.int32)
    large = jnp.minimum(large, N_BUCKETS - 1)
    return jnp.where(n < max_exact, n, large)


def dilated_attn_prompt(q, k, v, bias_tab, window, dil):
    b, s, h, e = q.shape
    n_dist = window // dil
    L = s // dil
    nb = -(-L // QBLOCK)
    lp = nb * QBLOCK

    def to_residue(t):
        t = t.astype(jnp.float32).reshape(b, L, dil, h, e).transpose(0, 2, 1, 3, 4)
        return jnp.pad(t, ((0, 0), (0, 0), (0, lp - L), (0, 0), (0, 0)))

    def band(t):
        t = jnp.pad(t, ((0, 0), (0, 0), (QBLOCK, 0), (0, 0), (0, 0)))
        prev = t[:, :, :lp].reshape(b, dil, nb, QBLOCK, h, e)
        cur = t[:, :, QBLOCK:].reshape(b, dil, nb, QBLOCK, h, e)
        return jnp.concatenate([prev, cur], axis=3)

    qb = to_residue(q).reshape(b, dil, nb, QBLOCK, h, e)
    kb = band(to_residue(k))
    vb = band(to_residue(v))
    qi = jnp.arange(QBLOCK)[:, None]
    ki = jnp.arange(2 * QBLOCK)[None, :]
    j = qi + QBLOCK - ki
    kpos = jnp.arange(nb)[:, None, None] * QBLOCK + ki[None] - QBLOCK
    valid = ((j >= 0) & (j <= n_dist))[None] & (kpos >= 0)
    bias = bias_tab.astype(jnp.float32)[rel_bucket(jnp.clip(j, 0, n_dist) * dil)].transpose(2, 0, 1)
    logits = jnp.einsum('bdnqhe,bdnkhe->bdnhqk', qb, kb) * (HEAD_DIM ** -0.5) + bias
    logits = jnp.where(valid[:, None], logits, NEG_INF)
    m = jnp.max(logits, axis=-1, keepdims=True)
    p = jnp.exp(logits - m)
    den = jnp.sum(p, axis=-1)
    o = jnp.einsum('bdnhqk,bdnkhe->bdnqhe', p, vb) / den.transpose(0, 1, 2, 4, 3)[..., None]
    lse = (m[..., 0] + jnp.log(den)).transpose(0, 1, 2, 4, 3)
    o = o.reshape(b, dil, lp, h, e)[:, :, :L].transpose(0, 2, 1, 3, 4).reshape(b, s, h, e)
    lse = lse.reshape(b, dil, lp, h)[:, :, :L].transpose(0, 2, 1, 3).reshape(b, s, h)
    return o, lse


def dilated_attn_sample(q, k, v, kv_buf, bias_tab, window, dil):
    t = q.shape[1]
    lb = kv_buf.shape[1]
    n_dist = window // dil
    f32 = jnp.float32
    k_all = jnp.concatenate([kv_buf[:, :, 0].astype(f32), k.astype(f32)], axis=1)
    v_all = jnp.concatenate([kv_buf[:, :, 1].astype(f32), v.astype(f32)], axis=1)
    dist = jnp.arange(n_dist + 1)
    idx = lb + jnp.arange(t)[:, None] - dist[None, :] * dil
    valid = idx >= 0
    idx = jnp.maximum(idx, 0)
    kg = k_all[:, idx]
    vg = v_all[:, idx]
    bias = bias_tab.astype(f32)[rel_bucket(dist * dil)].T
    logits = jnp.einsum('bthe,btjhe->bthj', q.astype(f32), kg) * (HEAD_DIM ** -0.5) + bias
    logits = jnp.where(valid[:, None, :], logits, NEG_INF)
    m = jnp.max(logits, axis=-1, keepdims=True)
    p = jnp.exp(logits - m)
    den = jnp.sum(p, axis=-1)
    o = jnp.einsum('bthj,btjhe->bthe', p, vg) / den[..., None]
    lse = m[..., 0] + jnp.log(den)
    return o, lse


def _complex_scan_combine(e1, e2):
    a1r, a1i, b1r, b1i = e1
    a2r, a2i, b2r, b2i = e2
    return (a2r * a1r - a2i * a1i,
            a2r * a1i + a2i * a1r,
            a2r * b1r - a2i * b1i + b2r,
            a2r * b1i + a2i * b1r + b2i)


def s5_branch(u, h0_re, h0_im, log_dt, lam_re, lam_im, b_re, b_im, c_re, c_im, d, w_glu, b_glu):
    b, s, _ = u.shape
    f32 = jnp.float32
    uf = u.astype(f32)
    ug = uf.reshape(b, s, N_SSM_GROUPS, SSM_GROUP)
    dt = jnp.exp(log_dt.astype(f32))[:, None]
    lr, li = lam_re.astype(f32), lam_im.astype(f32)
    mag = jnp.exp(lr * dt)
    ab_re, ab_im = mag * jnp.cos(li * dt), mag * jnp.sin(li * dt)
    den = lr * lr + li * li
    nr, ni = ab_re - 1.0, ab_im
    coef_re = (nr * lr + ni * li) / den
    coef_im = (ni * lr - nr * li) / den
    br, bi = b_re.astype(f32), b_im.astype(f32)
    bb_re = coef_re[..., None] * br - coef_im[..., None] * bi
    bb_im = coef_re[..., None] * bi + coef_im[..., None] * br
    bu_re = jnp.einsum('bsgp,gnp->bsgn', ug, bb_re)
    bu_im = jnp.einsum('bsgp,gnp->bsgn', ug, bb_im)
    a_re = jnp.broadcast_to(ab_re, bu_re.shape)
    a_im = jnp.broadcast_to(ab_im, bu_im.shape)
    acc_re, acc_im, s_re, s_im = lax.associative_scan(
        _complex_scan_combine, (a_re, a_im, bu_re, bu_im), axis=1)
    h0r = h0_re.astype(f32)[:, None]
    h0i = h0_im.astype(f32)[:, None]
    h_re = acc_re * h0r - acc_im * h0i + s_re
    h_im = acc_re * h0i + acc_im * h0r + s_im
    y = (jnp.einsum('bsgn,gpn->bsgp', h_re, c_re.astype(f32))
         - jnp.einsum('bsgn,gpn->bsgp', h_im, c_im.astype(f32)))
    y = y.reshape(b, s, SSM_WIDTH) + d.astype(f32) * uf
    y = jax.nn.gelu(y)
    y = y * jax.nn.sigmoid(y @ w_glu.astype(f32) + b_glu.astype(f32))
    return y.astype(u.dtype), h_re[:, -1], h_im[:, -1]


def trunk_layer(x, kv_bufs, ssm_h_re, ssm_h_im, conv_buf, rel_bias,
                norm1_g, w_in, ssm_log_dt, ssm_lambda_re, ssm_lambda_im,
                ssm_b_re, ssm_b_im, ssm_c_re, ssm_c_im, ssm_d, w_glu, b_glu,
                w_branch_attn, w_branch_ssm, w_out, norm2_g, w_up, conv_w, conv_b, w_down):
    b, s, _ = x.shape
    proj = rmsnorm(x, norm1_g) @ w_in
    q = proj[..., :QK_WIDTH].reshape(b, s, N_ATTN_HEADS, HEAD_DIM)
    k = proj[..., QK_WIDTH:2 * QK_WIDTH].reshape(b, s, N_ATTN_HEADS, HEAD_DIM)
    v = proj[..., 2 * QK_WIDTH:U_START].reshape(b, s, N_ATTN_HEADS, HEAD_DIM)
    u = proj[..., U_START:GA_START]
    gate_a = proj[..., GA_START:GS_START]
    gate_s = proj[..., GS_START:]

    outs, lses, kv_new = [], [], []
    for g, (window, dil) in enumerate(DIL_PATTERNS):
        hs = slice(g * HEADS_PER_GROUP, (g + 1) * HEADS_PER_GROUP)
        qg, kg, vg = q[:, :, hs], k[:, :, hs], v[:, :, hs]
        tab = rel_bias[:, hs]
        rows = jnp.stack([kg, vg], axis=2)
        if kv_bufs is None:
            o, lse = dilated_attn_prompt(qg, kg, vg, tab, window, dil)
            kv_new.append(rows[:, -min(window, s):])
        else:
            o, lse = dilated_attn_sample(qg, kg, vg, kv_bufs[g], tab, window, dil)
            kv_new.append(rows)
        outs.append(o)
        lses.append(lse)
    wts = jax.nn.softmax(jnp.stack(lses, axis=0), axis=0)
    merged = jnp.sum(wts[..., None] * jnp.stack(outs, axis=0), axis=0)
    attn_out = merged.reshape(b, s, ATTN_OUT_WIDTH).astype(x.dtype)

    ssm_out, h_re, h_im = s5_branch(u, ssm_h_re, ssm_h_im, ssm_log_dt, ssm_lambda_re, ssm_lambda_im,
                                    ssm_b_re, ssm_b_im, ssm_c_re, ssm_c_im, ssm_d, w_glu, b_glu)
    mix = (jax.nn.sigmoid(gate_a) * (attn_out @ w_branch_attn)
           + jax.nn.sigmoid(gate_s) * (ssm_out @ w_branch_ssm))
    x = x + mix @ w_out

    up = rmsnorm(x, norm2_g) @ w_up
    a, val = up[..., :D_FF], up[..., D_FF:]
    a_ext = jnp.concatenate([conv_buf.astype(a.dtype), a], axis=1)
    a_conv = conv_b
    for i in range(CONV_W):
        a_conv = a_conv + conv_w[i] * a_ext[:, i:i + s]
    conv_state = a_ext[:, s:]
    x = x + (jax.nn.silu(a_conv) * val) @ w_down
    return x, (kv_new[0], kv_new[1], kv_new[2], h_re, h_im, conv_state)


def _stack_layers(states, i):
    return jnp.stack([st[i] for st in states], axis=0)


def setup_inputs(seed: int = 0) -> dict:
    key = jax.random.key(seed)
    ks = iter(jax.random.split(key, 32))
    f32 = jnp.float32

    def nrm(shape, scale=1.0):
        return jax.random.normal(next(ks), shape, f32) * scale

    def kv_shape(window):
        return (DEPTH, DEC_BATCH, min(window, PAST_LEN), 2, HEADS_PER_GROUP, HEAD_DIM)

    G, N, P = N_SSM_GROUPS, SSM_STATE, SSM_GROUP
    return {
        'x_prompt': nrm((BATCH, SEQ, D_MODEL)),
        'x_sample': nrm((DEC_BATCH, DEC_SEQ, D_MODEL)),
        'cache_kv_w128': nrm(kv_shape(DIL_PATTERNS[0][0])),
        'cache_kv_w512': nrm(kv_shape(DIL_PATTERNS[1][0])),
        'cache_kv_w2048': nrm(kv_shape(DIL_PATTERNS[2][0])),
        'state_ssm_re': nrm((DEPTH, DEC_BATCH, G, N), 0.3),
        'state_ssm_im': nrm((DEPTH, DEC_BATCH, G, N), 0.3),
        'state_ffn_conv': nrm((DEPTH, DEC_BATCH, CONV_W - 1, D_FF)),
        'rel_bias': nrm((N_BUCKETS, N_ATTN_HEADS), 0.5),
        'norm1_g': 1.0 + nrm((DEPTH, D_MODEL), 0.01),
        'w_in': nrm((DEPTH, D_MODEL, IN_WIDTH), D_MODEL ** -0.5),
        'ssm_log_dt': jax.random.uniform(next(ks), (DEPTH, G), f32, math.log(DT_MIN), math.log(DT_MAX)),
        'ssm_lambda_re': -0.5 + nrm((DEPTH, G, N), 0.01),
        'ssm_lambda_im': math.pi * jnp.arange(N, dtype=f32) + nrm((DEPTH, G, N), 0.01),
        'ssm_b_re': nrm((DEPTH, G, N, P), (2 * P) ** -0.5),
        'ssm_b_im': nrm((DEPTH, G, N, P), (2 * P) ** -0.5),
        'ssm_c_re': nrm((DEPTH, G, P, N), (2 * N) ** -0.5),
        'ssm_c_im': nrm((DEPTH, G, P, N), (2 * N) ** -0.5),
        'ssm_d': nrm((DEPTH, SSM_WIDTH)),
        'w_glu': nrm((DEPTH, SSM_WIDTH, SSM_WIDTH), SSM_WIDTH ** -0.5),
        'b_glu': nrm((DEPTH, SSM_WIDTH), 0.01),
        'w_branch_attn': nrm((DEPTH, ATTN_OUT_WIDTH, D_MODEL), ATTN_OUT_WIDTH ** -0.5),
        'w_branch_ssm': nrm((DEPTH, SSM_WIDTH, D_MODEL), SSM_WIDTH ** -0.5),
        'w_out': nrm((DEPTH, D_MODEL, D_MODEL), D_MODEL ** -0.5),
        'norm2_g': 1.0 + nrm((DEPTH, D_MODEL), 0.01),
        'w_up': nrm((DEPTH, D_MODEL, 2 * D_FF), D_MODEL ** -0.5),
        'conv_w': nrm((DEPTH, CONV_W, D_FF), CONV_W ** -0.5),
        'conv_b': nrm((DEPTH, D_FF), 0.01),
        'w_down': nrm((DEPTH, D_FF, D_MODEL), D_FF ** -0.5),
        'norm_f_g': 1.0 + nrm((D_MODEL,), 0.01),
    }


def reference(x_prompt, x_sample, cache_kv_w128, cache_kv_w512, cache_kv_w2048,
              state_ssm_re, state_ssm_im, state_ffn_conv, rel_bias, norm1_g, w_in,
              ssm_log_dt, ssm_lambda_re, ssm_lambda_im, ssm_b_re, ssm_b_im, ssm_c_re, ssm_c_im,
              ssm_d, w_glu, b_glu, w_branch_attn, w_branch_ssm, w_out, norm2_g, w_up,
              conv_w, conv_b, w_down, norm_f_g):
    hp, hs = x_prompt, x_sample
    bp = x_prompt.shape[0]
    st_p, st_s = [], []
    for l in range(DEPTH):
        lw = (norm1_g[l], w_in[l], ssm_log_dt[l], ssm_lambda_re[l], ssm_lambda_im[l],
              ssm_b_re[l], ssm_b_im[l], ssm_c_re[l], ssm_c_im[l], ssm_d[l], w_glu[l], b_glu[l],
              w_branch_attn[l], w_branch_ssm[l], w_out[l], norm2_g[l], w_up[l],
              conv_w[l], conv_b[l], w_down[l])
        zero_h = jnp.zeros((bp, N_SSM_GROUPS, SSM_STATE), jnp.float32)
        zero_conv = jnp.zeros((bp, CONV_W - 1, D_FF), x_prompt.dtype)
        hp, sp = trunk_layer(hp, None, zero_h, zero_h, zero_conv, rel_bias, *lw)
        hs, ss = trunk_layer(hs, (cache_kv_w128[l], cache_kv_w512[l], cache_kv_w2048[l]),
                             state_ssm_re[l], state_ssm_im[l], state_ffn_conv[l], rel_bias, *lw)
        st_p.append(sp)
        st_s.append(ss)
    y_prompt = rmsnorm(hp, norm_f_g)
    y_sample = rmsnorm(hs, norm_f_g)
    kv128_p = _stack_layers(st_p, 0)
    kv512_p = _stack_layers(st_p, 1)
    kv2048_p = _stack_layers(st_p, 2)
    ssm_re_p = _stack_layers(st_p, 3)
    ssm_im_p = _stack_layers(st_p, 4)
    conv_p = _stack_layers(st_p, 5)
    kv128_s = _stack_layers(st_s, 0)
    kv512_s = _stack_layers(st_s, 1)
    kv2048_s = _stack_layers(st_s, 2)
    ssm_re_s = _stack_layers(st_s, 3)
    ssm_im_s = _stack_layers(st_s, 4)
    conv_s = _stack_layers(st_s, 5)
    return (y_prompt, y_sample, kv128_p, kv512_p, kv2048_p, ssm_re_p, ssm_im_p, conv_p,
            kv128_s, kv512_s, kv2048_s, ssm_re_s, ssm_im_s, conv_s)
```

```python
import functools
import math

import jax
import jax.numpy as jnp
from jax import lax
from jax.experimental import pallas as pl
from jax.experimental.pallas import tpu as pltpu

F32 = jnp.float32
BF16 = jnp.bfloat16

D_MODEL = 1024
HEAD_DIM = 64
HEADS_PER_GROUP = 4
DIL_PATTERNS = ((128, 1), (512, 4), (2048, 16))
N_DIL_GROUPS = len(DIL_PATTERNS)
GROUP_WIDTH = HEADS_PER_GROUP * HEAD_DIM
QK_WIDTH = N_DIL_GROUPS * GROUP_WIDTH
QKV_WIDTH = 3 * QK_WIDTH
QBLOCK = 128
SSM_GROUP = 16
SSM_STATE = 64
SSM_WIDTH = D_MODEL // 2
N_SSM_GROUPS = SSM_WIDTH // SSM_GROUP
N_SSM_PAIRS = N_SSM_GROUPS // 2
SSM_CHUNK = 16
D_FF = 2816
CONV_W = 3
N_BUCKETS = 32
MAX_DISTANCE = 2048
NORM_EPS = 1e-6
NEG_INF = -1e30
U_START = QKV_WIDTH
GATE_START = U_START + SSM_WIDTH
IN_WIDTH = GATE_START + 2 * D_MODEL
QK_SCALE = HEAD_DIM ** -0.5

VMEM_LIMIT_BYTES = 56 * 1024 * 1024
SUBLANES = 8


def _compiler_params(*semantics):
    return pltpu.CompilerParams(dimension_semantics=semantics, vmem_limit_bytes=VMEM_LIMIT_BYTES)


def _resident(shape):
    nd = len(shape)
    return pl.BlockSpec(shape, lambda *_: (0,) * nd, pipeline_mode=pl.Buffered(1))


def _rmsnorm(xf, g):
    y = xf * lax.rsqrt(jnp.mean(xf * xf, axis=-1, keepdims=True) + NORM_EPS)
    return y * g


def _sigmoid(x):
    return 1.0 / (1.0 + jnp.exp(-x))


def _dot(a, b):
    return jnp.dot(a, b, preferred_element_type=F32)


def _dot_nt(a, b):
    return lax.dot_general(a, b, (((1,), (1,)), ((), ())), preferred_element_type=F32)


def _in_proj_kernel(x_ref, g_ref, w_ref, qkv_ref, u_ref, gate_ref):
    xn = _rmsnorm(x_ref[...], g_ref[...]).astype(BF16)
    for ref, start, width, step in ((qkv_ref, 0, QKV_WIDTH, 768),
                                    (u_ref, U_START, SSM_WIDTH, 512),
                                    (gate_ref, GATE_START, 2 * D_MODEL, 1024)):
        for c0 in range(0, width, step):
            ref[:, c0:c0 + step] = _dot(xn, w_ref[:, start + c0:start + c0 + step])


def _in_proj(x2d, g, w_bf16, tm):
    m = x2d.shape[0]
    row = lambda width: pl.BlockSpec((tm, width), lambda i: (i, 0))
    return pl.pallas_call(
        _in_proj_kernel,
        grid=(m // tm,),
        in_specs=[row(D_MODEL), _resident((1, D_MODEL)), _resident((D_MODEL, IN_WIDTH))],
        out_specs=[row(QKV_WIDTH), row(SSM_WIDTH), row(2 * D_MODEL)],
        out_shape=[jax.ShapeDtypeStruct((m, QKV_WIDTH), F32),
                   jax.ShapeDtypeStruct((m, SSM_WIDTH), F32),
                   jax.ShapeDtypeStruct((m, 2 * D_MODEL), F32)],
        compiler_params=_compiler_params("arbitrary"),
        name="in_proj",
    )(x2d, g, w_bf16)


def _rel_bucket(dist):
    max_exact = N_BUCKETS // 2
    n = jnp.maximum(dist, 0)
    nf = jnp.maximum(n, 1).astype(F32)
    large = max_exact + (jnp.log(nf / max_exact) / math.log(MAX_DISTANCE / max_exact)
                         * (N_BUCKETS - max_exact)).astype(jnp.int32)
    large = jnp.minimum(large, N_BUCKETS - 1)
    return jnp.where(n < max_exact, n, large)


def _masked_bias(tab, strides, valid, dil):
    n_dist = QBLOCK
    bias = tab.astype(F32)[_rel_bucket(jnp.clip(strides, 0, n_dist) * dil)]
    bias = jnp.where(valid[..., None], bias, NEG_INF)
    return jnp.moveaxis(bias, -1, 0)


def _prompt_bias(tab, dil):
    qi = jnp.arange(QBLOCK)[:, None]
    ki = jnp.arange(QBLOCK)[None, :]
    j_prev = qi + QBLOCK - ki
    j_cur = qi - ki
    return jnp.stack([_masked_bias(tab, j_prev, j_prev <= QBLOCK, dil),
                      _masked_bias(tab, j_cur, j_cur >= 0, dil)], axis=0)


def _sample_bias(tab, dil, t_new):
    t = jnp.arange(t_new)[:, None]
    m = jnp.arange(QBLOCK)[None, :]
    if dil == 1:
        j_buf = QBLOCK + t - m
        ok_buf = m >= t
    else:
        j_buf = QBLOCK - m + 0 * t
        ok_buf = jnp.ones_like(j_buf, dtype=bool)
    delta = t - m
    ok_new = (m < t_new) & (delta >= 0) & (delta % dil == 0)
    b_buf = _masked_bias(tab, j_buf, ok_buf, dil)
    b_new = _masked_bias(tab, delta // dil, ok_new, dil)
    return (b_buf.reshape(HEADS_PER_GROUP * t_new, QBLOCK),
            b_new.reshape(HEADS_PER_GROUP * t_new, QBLOCK))


def _attn_prompt_kernel(q_ref, kp_ref, kc_ref, vp_ref, vc_ref, bias_ref, o_ref, lse_ref, *, nq):
    first_tile = pl.program_id(2) == 0
    for i in range(nq):
        rows = slice(i * QBLOCK, (i + 1) * QBLOCK)
        q = (q_ref[0, rows, :] * QK_SCALE).astype(BF16)
        if i == 0:
            k_prev, v_prev = kp_ref[0], vp_ref[0]
        else:
            prev_rows = slice((i - 1) * QBLOCK, i * QBLOCK)
            k_prev, v_prev = kc_ref[0, prev_rows, :], vc_ref[0, prev_rows, :]
        k_prev, v_prev = k_prev.astype(BF16), v_prev.astype(BF16)
        k_cur, v_cur = kc_ref[0, rows, :].astype(BF16), vc_ref[0, rows, :].astype(BF16)
        for h in range(HEADS_PER_GROUP):
            cols = slice(h * HEAD_DIM, (h + 1) * HEAD_DIM)
            s_prev = _dot_nt(q[:, cols], k_prev[:, cols]) + bias_ref[0, h]
            if i == 0:
                s_prev = jnp.where(first_tile, NEG_INF, s_prev)
            s_cur = _dot_nt(q[:, cols], k_cur[:, cols]) + bias_ref[1, h]
            m = jnp.maximum(jnp.max(s_prev, axis=-1, keepdims=True),
                            jnp.max(s_cur, axis=-1, keepdims=True))
            p_prev = jnp.exp(s_prev - m)
            p_cur = jnp.exp(s_cur - m)
            den = jnp.sum(p_prev, axis=-1, keepdims=True) + jnp.sum(p_cur, axis=-1, keepdims=True)
            o = _dot(p_prev.astype(BF16), v_prev[:, cols]) + _dot(p_cur.astype(BF16), v_cur[:, cols])
            o_ref[0, rows, cols] = o / den
            lse_ref[0, rows, cols] = jnp.broadcast_to(m + jnp.log(den), (QBLOCK, HEAD_DIM))


def _attn_prompt(qkv, bias, group, dil, batch, seq):
    length = seq // dil
    tq = min(4 * QBLOCK, length)
    nq = tq // QBLOCK
    blocks_per_row = QKV_WIDTH // GROUP_WIDTH
    view = qkv.reshape(batch, length, dil * QKV_WIDTH)
    qcol, kcol, vcol = group, N_DIL_GROUPS + group, 2 * N_DIL_GROUPS + group

    def cur(col):
        return pl.BlockSpec((1, tq, GROUP_WIDTH), lambda b, r, n: (b, n, r * blocks_per_row + col))

    def prev(col):
        return pl.BlockSpec((1, QBLOCK, GROUP_WIDTH),
                            lambda b, r, n: (b, jnp.maximum(n * nq - 1, 0), r * blocks_per_row + col))

    out_spec = pl.BlockSpec((1, tq, GROUP_WIDTH), lambda b, r, n: (b, n, r))
    out_sds = jax.ShapeDtypeStruct((batch, length, dil * GROUP_WIDTH), F32)
    o, lse = pl.pallas_call(
        functools.partial(_attn_prompt_kernel, nq=nq),
        grid=(batch, dil, length // tq),
        in_specs=[cur(qcol), prev(kcol), cur(kcol), prev(vcol), cur(vcol),
                  _resident((2, HEADS_PER_GROUP, QBLOCK, QBLOCK))],
        out_specs=[out_spec, out_spec],
        out_shape=[out_sds, out_sds],
        compiler_params=_compiler_params("arbitrary", "arbitrary", "arbitrary"),
        name=f"attn_prompt_d{dil}",
    )(view, view, view, view, view, bias)
    return o.reshape(batch * seq, GROUP_WIDTH), lse.reshape(batch * seq, GROUP_WIDTH)


def _attn_sample_kernel(q_ref, c0_ref, c1_ref, c2_ref, tb_ref, tn_ref,
                        o0_ref, l0_ref, o1_ref, l1_ref, o2_ref, l2_ref, kn_scr, vn_scr, *, t_new):
    n_rows = HEADS_PER_GROUP * t_new
    row_w = lax.broadcasted_iota(jnp.int32, (n_rows, GROUP_WIDTH), 0)
    lane_w = lax.broadcasted_iota(jnp.int32, (n_rows, GROUP_WIDTH), 1)
    own_head = (row_w // t_new) == (lane_w // HEAD_DIM)
    tok_w = row_w % t_new
    tok_k = lax.broadcasted_iota(jnp.int32, (n_rows, QBLOCK), 0) % t_new

    def fold_heads(x):
        x = jnp.where(own_head, x, 0.0)
        out = x[0:t_new]
        for h in range(1, HEADS_PER_GROUP):
            out = out + x[h * t_new:(h + 1) * t_new]
        return out

    caches = (c0_ref, c1_ref, c2_ref)
    outs = ((o0_ref, l0_ref), (o1_ref, l1_ref), (o2_ref, l2_ref))
    kv_width = 2 * GROUP_WIDTH
    for g, (_, dil) in enumerate(DIL_PATTERNS):
        q = q_ref[0, :, g * GROUP_WIDTH:(g + 1) * GROUP_WIDTH] * QK_SCALE
        q_rows = jnp.where(own_head, jnp.concatenate([q] * HEADS_PER_GROUP, axis=0), 0.0).astype(BF16)
        kn_scr[...] = jnp.zeros_like(kn_scr)
        vn_scr[...] = jnp.zeros_like(vn_scr)
        kn_scr[0:t_new, :] = q_ref[0, :, QK_WIDTH + g * GROUP_WIDTH:QK_WIDTH + (g + 1) * GROUP_WIDTH]
        vn_scr[0:t_new, :] = q_ref[0, :, 2 * QK_WIDTH + g * GROUP_WIDTH:2 * QK_WIDTH + (g + 1) * GROUP_WIDTH]
        cache = caches[g]
        n_views = 1 if dil == 1 else t_new

        def k_view(t):
            return cache[0, :, t * kv_width:t * kv_width + GROUP_WIDTH].astype(BF16)

        def v_view(t):
            return cache[0, :, t * kv_width + GROUP_WIDTH:(t + 1) * kv_width].astype(BF16)

        if n_views == 1:
            s_buf = _dot_nt(q_rows, k_view(0))
        else:
            s_buf = jnp.zeros((n_rows, QBLOCK), F32)
            for t in range(n_views):
                s_buf = jnp.where(tok_k == t, _dot_nt(q_rows, k_view(t)), s_buf)
        s_buf = s_buf + tb_ref[g]
        s_new = _dot_nt(q_rows, kn_scr[...].astype(BF16)) + tn_ref[g]
        m = jnp.maximum(jnp.max(s_buf, axis=-1, keepdims=True), jnp.max(s_new, axis=-1, keepdims=True))
        p_buf = jnp.exp(s_buf - m)
        p_new = jnp.exp(s_new - m)
        den = jnp.sum(p_buf, axis=-1, keepdims=True) + jnp.sum(p_new, axis=-1, keepdims=True)
        p_buf16 = p_buf.astype(BF16)
        if n_views == 1:
            o = _dot(p_buf16, v_view(0))
        else:
            o = jnp.zeros((n_rows, GROUP_WIDTH), F32)
            for t in range(n_views):
                o = jnp.where(tok_w == t, _dot(p_buf16, v_view(t)), o)
        o = o + _dot(p_new.astype(BF16), vn_scr[...].astype(BF16))
        o_ref, l_ref = outs[g]
        o_ref[0] = fold_heads(o / den)
        l_ref[0] = fold_heads(jnp.broadcast_to(m + jnp.log(den), (n_rows, GROUP_WIDTH)))


def _attn_sample(qkv_bt, caches, tb, tn):
    batch, t_new, _ = qkv_bt.shape
    n_rows = HEADS_PER_GROUP * t_new
    cache_specs = []
    for (_, dil), c in zip(DIL_PATTERNS, caches):
        width = 2 * GROUP_WIDTH * min(dil, t_new)
        cache_specs.append(pl.BlockSpec((1, QBLOCK, width), lambda b: (b, 0, 0)))
    out_spec = pl.BlockSpec((1, t_new, GROUP_WIDTH), lambda b: (b, 0, 0))
    out_sds = jax.ShapeDtypeStruct((batch, t_new, GROUP_WIDTH), F32)
    return pl.pallas_call(
        functools.partial(_attn_sample_kernel, t_new=t_new),
        grid=(batch,),
        in_specs=[pl.BlockSpec((1, t_new, QKV_WIDTH), lambda b: (b, 0, 0)), *cache_specs,
                  _resident((N_DIL_GROUPS, n_rows, QBLOCK)), _resident((N_DIL_GROUPS, n_rows, QBLOCK))],
        out_specs=[out_spec] * (2 * N_DIL_GROUPS),
        out_shape=[out_sds] * (2 * N_DIL_GROUPS),
        scratch_shapes=[pltpu.VMEM((QBLOCK, GROUP_WIDTH), F32), pltpu.VMEM((QBLOCK, GROUP_WIDTH), F32)],
        compiler_params=_compiler_params("arbitrary"),
        name="attn_sample",
    )(qkv_bt, *caches, tb, tn)


def _pair_block_diag(a):
    g, r, c = a.shape
    a = a.reshape(g // 2, 2, r, c)
    z = jnp.zeros((g // 2, r, c), a.dtype)
    top = jnp.concatenate([a[:, 0], z], axis=2)
    bot = jnp.concatenate([z, a[:, 1]], axis=2)
    return jnp.concatenate([top, bot], axis=1)


def _ssm_operators(chunk, log_dt, lam_re, lam_im, b_re, b_im, c_re, c_im):
    hi = lax.Precision.HIGHEST
    dt = jnp.exp(log_dt.astype(F32))[:, None]
    lr, li = lam_re.astype(F32), lam_im.astype(F32)
    mag = jnp.exp(lr * dt)
    ab_re, ab_im = mag * jnp.cos(li * dt), mag * jnp.sin(li * dt)
    den = lr * lr + li * li
    nr, ni = ab_re - 1.0, ab_im
    coef_re = (nr * lr + ni * li) / den
    coef_im = (ni * lr - nr * li) / den
    br, bi = b_re.astype(F32), b_im.astype(F32)
    bb_re = coef_re[..., None] * br - coef_im[..., None] * bi
    bb_im = coef_re[..., None] * bi + coef_im[..., None] * br
    pw_re, pw_im = [jnp.ones_like(ab_re)], [jnp.zeros_like(ab_im)]
    for _ in range(chunk):
        pr, pi = pw_re[-1], pw_im[-1]
        pw_re.append(pr * ab_re - pi * ab_im)
        pw_im.append(pr * ab_im + pi * ab_re)
    pw_re, pw_im = jnp.stack(pw_re), jnp.stack(pw_im)
    akb_re = pw_re[:chunk, :, :, None] * bb_re[None] - pw_im[:chunk, :, :, None] * bb_im[None]
    akb_im = pw_re[:chunk, :, :, None] * bb_im[None] + pw_im[:chunk, :, :, None] * bb_re[None]
    g, n, p = bb_re.shape
    w_re = akb_re[::-1].transpose(1, 0, 3, 2).reshape(g, chunk * p, n)
    w_im = akb_im[::-1].transpose(1, 0, 3, 2).reshape(g, chunk * p, n)
    cr, ci = c_re.astype(F32), c_im.astype(F32)
    e_re = cr[None] * pw_re[1:, :, None, :] - ci[None] * pw_im[1:, :, None, :]
    e_im = cr[None] * pw_im[1:, :, None, :] + ci[None] * pw_re[1:, :, None, :]
    et_re = e_re.transpose(1, 3, 0, 2).reshape(g, n, chunk * p)
    et_im = (-e_im).transpose(1, 3, 0, 2).reshape(g, n, chunk * p)
    kern = (jnp.einsum('gpn,kgnq->kgpq', cr, akb_re, precision=hi)
            - jnp.einsum('gpn,kgnq->kgpq', ci, akb_im, precision=hi))
    lag = jnp.arange(chunk)[None, :] - jnp.arange(chunk)[:, None]
    mt = jnp.where((lag >= 0)[:, :, None, None, None], kern[jnp.maximum(lag, 0)], 0.0)
    mt = mt.transpose(2, 0, 4, 1, 3).reshape(g, chunk * p, chunk * p)
    a_re = pw_re[chunk].reshape(1, g * n)
    a_im = pw_im[chunk].reshape(1, g * n)
    ops = [_pair_block_diag(x).astype(BF16) for x in (w_re, w_im, mt, et_re, et_im)]
    return ops, a_re, a_im


PAIR_LANES = 2 * SSM_STATE


def _ssm_prompt_kernel(x_ref, wre_ref, wim_ref, mt_ref, etre_ref, etim_ref, are_ref, aim_ref,
                       y_ref, hre_ref, him_ref, gre_scr, gim_scr, pre_scr, pim_scr, *, pairs, batch, chunks):
    blocks = [(gp, b) for gp in range(pairs) for b in range(batch)]
    for idx, (gp, b) in enumerate(blocks):
        lanes = slice(idx * PAIR_LANES, (idx + 1) * PAIR_LANES)
        x = x_ref[gp, b]
        gre_scr[:, lanes] = _dot(x, wre_ref[gp])
        gim_scr[:, lanes] = _dot(x, wim_ref[gp])
    a_re, a_im = are_ref[0], aim_ref[0]

    def step(c, carry):
        h_re, h_im = carry
        pre_scr[pl.ds(c, 1), :] = h_re
        pim_scr[pl.ds(c, 1), :] = h_im
        n_re = a_re * h_re - a_im * h_im + gre_scr[pl.ds(c, 1), :]
        n_im = a_re * h_im + a_im * h_re + gim_scr[pl.ds(c, 1), :]
        return n_re, n_im

    zero = jnp.zeros_like(a_re)
    h_re, h_im = lax.fori_loop(0, chunks, step, (zero, zero))
    hre_ref[0] = h_re
    him_ref[0] = h_im
    for idx, (gp, b) in enumerate(blocks):
        lanes = slice(idx * PAIR_LANES, (idx + 1) * PAIR_LANES)
        y_ref[gp, b] = (_dot(x_ref[gp, b], mt_ref[gp])
                        + _dot(pre_scr[:, lanes].astype(BF16), etre_ref[gp])
                        + _dot(pim_scr[:, lanes].astype(BF16), etim_ref[gp]))


def _ssm_prompt(x, ops, a_re, a_im, pairs):
    n_pairs, batch, chunks, width = x.shape
    steps = n_pairs // pairs
    lanes = pairs * batch * PAIR_LANES
    def tile_a(a):
        a = a.reshape(steps, pairs, 1, PAIR_LANES)
        return jnp.broadcast_to(a, (steps, pairs, batch, PAIR_LANES)).reshape(steps, 1, lanes)

    def op_spec(arr):
        return pl.BlockSpec((pairs,) + arr.shape[1:], lambda s: (s, 0, 0))

    xy_spec = pl.BlockSpec((pairs, batch, chunks, width), lambda s: (s, 0, 0, 0))
    row_spec = pl.BlockSpec((1, 1, lanes), lambda s: (s, 0, 0))
    y, h_re, h_im = pl.pallas_call(
        functools.partial(_ssm_prompt_kernel, pairs=pairs, batch=batch, chunks=chunks),
        grid=(steps,),
        in_specs=[xy_spec, *[op_spec(o) for o in ops], row_spec, row_spec],
        out_specs=[xy_spec, row_spec, row_spec],
        out_shape=[jax.ShapeDtypeStruct(x.shape, F32),
                   jax.ShapeDtypeStruct((steps, 1, lanes), F32),
                   jax.ShapeDtypeStruct((steps, 1, lanes), F32)],
        scratch_shapes=[pltpu.VMEM((chunks, lanes), F32)] * 4,
        compiler_params=_compiler_params("arbitrary"),
        name="ssm_prompt",
    )(x, *ops, tile_a(a_re), tile_a(a_im))

    def untile(h):
        h = h.reshape(steps, pairs, batch, 2, SSM_STATE).transpose(2, 0, 1, 3, 4)
        return h.reshape(batch, N_SSM_GROUPS, SSM_STATE)

    return y, untile(h_re), untile(h_im)


def _ssm_sample_kernel(x_ref, wre_ref, wim_ref, mt_ref, etre_ref, etim_ref, are_ref, aim_ref,
                       h0re_ref, h0im_ref, y_ref, hre_ref, him_ref):
    for gp in range(N_SSM_PAIRS):
        lanes = slice(gp * PAIR_LANES, (gp + 1) * PAIR_LANES)
        x = x_ref[gp]
        h_re, h_im = h0re_ref[:, lanes], h0im_ref[:, lanes]
        a_re, a_im = are_ref[:, lanes], aim_ref[:, lanes]
        hre_ref[:, lanes] = a_re * h_re - a_im * h_im + _dot(x, wre_ref[gp])
        him_ref[:, lanes] = a_re * h_im + a_im * h_re + _dot(x, wim_ref[gp])
        y_ref[gp] = (_dot(x, mt_ref[gp]) + _dot(h_re.astype(BF16), etre_ref[gp])
                     + _dot(h_im.astype(BF16), etim_ref[gp]))


def _ssm_sample(x, ops, a_re, a_im, h0_re, h0_im):
    batch = x.shape[1]
    args = (x, *ops, a_re, a_im, h0_re, h0_im)
    state_sds = jax.ShapeDtypeStruct((batch, N_SSM_GROUPS * SSM_STATE), F32)
    out_shape = [jax.ShapeDtypeStruct(x.shape, F32), state_sds, state_sds]
    whole = lambda shape: pl.BlockSpec(shape, lambda i, nd=len(shape): (0,) * nd)
    return pl.pallas_call(
        _ssm_sample_kernel,
        grid=(1,),
        in_specs=[whole(a.shape) for a in args],
        out_specs=[whole(s.shape) for s in out_shape],
        out_shape=out_shape,
        compiler_params=_compiler_params("arbitrary"),
        name="ssm_sample",
    )(*args)


def _gelu_tanh(x):
    return 0.5 * x * (1.0 + jnp.tanh(math.sqrt(2.0 / math.pi) * (x + 0.044715 * (x * x * x))))


def _merge_kernel(o0_ref, l0_ref, o1_ref, l1_ref, o2_ref, l2_ref, ys_ref, u_ref, gate_ref, x_ref,
                  d_ref, wglu_ref, bglu_ref, wba_ref, wbs_ref, wout_ref, out_ref):
    l0, l1, l2 = l0_ref[...], l1_ref[...], l2_ref[...]
    mx = jnp.maximum(jnp.maximum(l0, l1), l2)
    e0, e1, e2 = jnp.exp(l0 - mx), jnp.exp(l1 - mx), jnp.exp(l2 - mx)
    attn = (e0 * o0_ref[...] + e1 * o1_ref[...] + e2 * o2_ref[...]) / (e0 + e1 + e2)
    branch_a = _dot(attn.astype(BF16), wba_ref[...])
    y = _gelu_tanh(ys_ref[...] + d_ref[...] * u_ref[...])
    y = y * _sigmoid(_dot(y.astype(BF16), wglu_ref[...]) + bglu_ref[...])
    branch_s = _dot(y.astype(BF16), wbs_ref[...])
    mix = (_sigmoid(gate_ref[:, 0:D_MODEL]) * branch_a
           + _sigmoid(gate_ref[:, D_MODEL:2 * D_MODEL]) * branch_s)
    out_ref[...] = x_ref[...] + _dot(mix.astype(BF16), wout_ref[...])


def _merge(attn_parts, ys, u, gates, x2d, d, w_glu, b_glu, w_ba, w_bs, w_out, tm):
    m = x2d.shape[0]
    row = lambda width: pl.BlockSpec((tm, width), lambda i: (i, 0))
    weights = (d, w_glu, b_glu, w_ba, w_bs, w_out)
    return pl.pallas_call(
        _merge_kernel,
        grid=(m // tm,),
        in_specs=[*[row(GROUP_WIDTH)] * 6, row(SSM_WIDTH), row(SSM_WIDTH), row(2 * D_MODEL), row(D_MODEL),
                  *[_resident(w.shape) for w in weights]],
        out_specs=row(D_MODEL),
        out_shape=jax.ShapeDtypeStruct((m, D_MODEL), F32),
        compiler_params=_compiler_params("arbitrary"),
        name="merge",
    )(*attn_parts, ys, u, gates, x2d, *weights)


FF_CHUNK = D_FF // 2


def _ffn_kernel(x_ref, carry_ref, g2_ref, wup_ref, cw_ref, cb_ref, wdn_ref, gf_ref,
                y_ref, state_ref, a_scr, *, tm, shift, pad):
    hist = 2 * shift

    @pl.when(pl.program_id(1) == 0)
    def _():
        a_scr[pad - hist:pad, :] = carry_ref[0]

    xf = x_ref[...]
    xn = _rmsnorm(xf, g2_ref[...]).astype(BF16)
    acc = jnp.zeros((tm, D_MODEL), F32)
    for c0 in range(0, D_FF, FF_CHUNK):
        cols = slice(c0, c0 + FF_CHUNK)
        a = _dot(xn, wup_ref[:, cols])
        val = _dot(xn, wup_ref[:, D_FF + c0:D_FF + c0 + FF_CHUNK])
        a_scr[pad:pad + tm, cols] = a
        a_m1 = a_scr[pad - shift:pad - shift + tm, cols]
        a_m2 = a_scr[pad - hist:pad - hist + tm, cols]
        conv = cb_ref[:, cols] + cw_ref[0:1, cols] * a_m2
        conv = conv + cw_ref[1:2, cols] * a_m1
        conv = conv + cw_ref[2:3, cols] * a
        act = conv * _sigmoid(conv) * val
        acc = acc + _dot(act.astype(BF16), wdn_ref[cols, :])
    tail = a_scr[pad + tm - hist:pad + tm, :]
    a_scr[pad - hist:pad, :] = tail
    state_ref[0] = tail
    y_ref[...] = _rmsnorm(xf + acc, gf_ref[...])


def _ffn(x2d, carry, g2, w_up, conv_w, conv_b, w_down, gf, n_seq, tm, shift):
    m = x2d.shape[0]
    tiles = m // n_seq // tm
    hist = 2 * shift
    pad = -(-hist // SUBLANES) * SUBLANES
    row_spec = pl.BlockSpec((tm, D_MODEL), lambda b, j: (b * tiles + j, 0))
    state_spec = pl.BlockSpec((1, hist, D_FF), lambda b, j: (b, 0, 0))
    weights = (g2, w_up, conv_w, conv_b, w_down, gf)
    return pl.pallas_call(
        functools.partial(_ffn_kernel, tm=tm, shift=shift, pad=pad),
        grid=(n_seq, tiles),
        in_specs=[row_spec, state_spec, *[_resident(w.shape) for w in weights]],
        out_specs=[row_spec, state_spec],
        out_shape=[jax.ShapeDtypeStruct((m, D_MODEL), F32),
                   jax.ShapeDtypeStruct((n_seq, hist, D_FF), F32)],
        scratch_shapes=[pltpu.VMEM((pad + tm, D_FF), F32)],
        compiler_params=_compiler_params("arbitrary", "arbitrary"),
        name="ffn",
    )(x2d, carry, *weights)


def _kv_rows(qkv, group, batch, seq, keep):
    rows = qkv.reshape(batch, seq, QKV_WIDTH)[:, seq - keep:]
    k = rows[..., QK_WIDTH + group * GROUP_WIDTH:QK_WIDTH + (group + 1) * GROUP_WIDTH]
    v = rows[..., 2 * QK_WIDTH + group * GROUP_WIDTH:2 * QK_WIDTH + (group + 1) * GROUP_WIDTH]
    return jnp.stack([k, v], axis=2).reshape(batch, keep, 2, HEADS_PER_GROUP, HEAD_DIM)


def _prompt_layer(x, rel_bias, lw, tm):
    batch, seq, _ = x.shape
    x2d = x.reshape(batch * seq, D_MODEL)
    qkv, u, gates = _in_proj(x2d, lw['norm1_g'], lw['w_in'], tm)

    attn_parts, kv_new = [], []
    for g, (window, dil) in enumerate(DIL_PATTERNS):
        tab = rel_bias[:, g * HEADS_PER_GROUP:(g + 1) * HEADS_PER_GROUP]
        attn_parts.extend(_attn_prompt(qkv, _prompt_bias(tab, dil), g, dil, batch, seq))
        kv_new.append(_kv_rows(qkv, g, batch, seq, min(window, seq)))

    chunks = seq // SSM_CHUNK
    ops, a_re, a_im = _ssm_operators(SSM_CHUNK, *lw['ssm'])
    xs = u.reshape(batch, chunks, SSM_CHUNK, N_SSM_PAIRS, 2, SSM_GROUP)
    xs = xs.transpose(3, 0, 1, 4, 2, 5).reshape(N_SSM_PAIRS, batch, chunks, 2 * SSM_CHUNK * SSM_GROUP)
    ys, h_re, h_im = _ssm_prompt(xs.astype(BF16), ops, a_re, a_im, pairs=2)
    ys = ys.reshape(N_SSM_PAIRS, batch, chunks, 2, SSM_CHUNK, SSM_GROUP)
    ys = ys.transpose(1, 2, 4, 0, 3, 5).reshape(batch * seq, SSM_WIDTH)

    x1 = _merge(attn_parts, ys, u, gates, x2d, *lw['merge'], tm)
    carry = jnp.zeros((batch, CONV_W - 1, D_FF), F32)
    y, conv_state = _ffn(x1, carry, *lw['ffn'], n_seq=batch, tm=tm, shift=1)
    return y.reshape(batch, seq, D_MODEL), (*kv_new, h_re, h_im, conv_state)


def _sample_layer(x, caches, h0_re, h0_im, conv_buf, rel_bias, lw):
    batch, t_new, _ = x.shape
    m = batch * t_new
    x2d = x.transpose(1, 0, 2).reshape(m, D_MODEL)
    qkv, u, gates = _in_proj(x2d, lw['norm1_g'], lw['w_in'], m)
    qkv_bt = qkv.reshape(t_new, batch, QKV_WIDTH).transpose(1, 0, 2)

    tbs, tns, views = [], [], []
    for g, (window, dil) in enumerate(DIL_PATTERNS):
        tab = rel_bias[:, g * HEADS_PER_GROUP:(g + 1) * HEADS_PER_GROUP]
        tb, tn = _sample_bias(tab, dil, t_new)
        tbs.append(tb)
        tns.append(tn)
        assert caches[g].shape[1] == window == QBLOCK * dil and (dil == 1 or t_new <= dil)
        views.append(caches[g].reshape(batch, QBLOCK, dil * 2 * GROUP_WIDTH))
    parts = _attn_sample(qkv_bt, views, jnp.stack(tbs), jnp.stack(tns))
    attn_parts = [p.transpose(1, 0, 2).reshape(m, GROUP_WIDTH) for p in parts]
    kv_new = [_kv_rows(qkv_bt.reshape(m, QKV_WIDTH), g, batch, t_new, t_new) for g in range(N_DIL_GROUPS)]

    ops, a_re, a_im = _ssm_operators(t_new, *lw['ssm'])
    xs = u.reshape(t_new, batch, N_SSM_PAIRS, 2, SSM_GROUP)
    xs = xs.transpose(2, 1, 3, 0, 4).reshape(N_SSM_PAIRS, batch, 2 * t_new * SSM_GROUP)
    ys, h_re, h_im = _ssm_sample(xs.astype(BF16), ops, a_re, a_im,
                                 h0_re.reshape(batch, -1), h0_im.reshape(batch, -1))
    ys = ys.reshape(N_SSM_PAIRS, batch, 2, t_new, SSM_GROUP).transpose(3, 1, 0, 2, 4).reshape(m, SSM_WIDTH)

    x1 = _merge(attn_parts, ys, u, gates, x2d, *lw['merge'], m)
    carry = conv_buf.transpose(1, 0, 2).reshape(1, (CONV_W - 1) * batch, D_FF)
    y, conv_state = _ffn(x1, carry, *lw['ffn'], n_seq=1, tm=m, shift=batch)
    y = y.reshape(t_new, batch, D_MODEL).transpose(1, 0, 2)
    conv_state = conv_state.reshape(CONV_W - 1, batch, D_FF).transpose(1, 0, 2)
    state_shape = (batch, N_SSM_GROUPS, SSM_STATE)
    return y, (*kv_new, h_re.reshape(state_shape), h_im.reshape(state_shape), conv_state)


PROMPT_TILE = 256


def kernel(x_prompt, x_sample, cache_kv_w128, cache_kv_w512, cache_kv_w2048, state_ssm_re, state_ssm_im, state_ffn_conv, rel_bias, norm1_g, w_in, ssm_log_dt, ssm_lambda_re, ssm_lambda_im, ssm_b_re, ssm_b_im, ssm_c_re, ssm_c_im, ssm_d, w_glu, b_glu, w_branch_attn, w_branch_ssm, w_out, norm2_g, w_up, conv_w, conv_b, w_down, norm_f_g):
    depth = w_in.shape[0]
    hp, hs = x_prompt, x_sample
    st_p, st_s = [], []
    gf = norm_f_g.reshape(1, D_MODEL)
    for l in range(depth):
        last = l == depth - 1
        lw = {
            'norm1_g': norm1_g[l].reshape(1, D_MODEL),
            'w_in': w_in[l].astype(BF16),
            'ssm': (ssm_log_dt[l], ssm_lambda_re[l], ssm_lambda_im[l],
                    ssm_b_re[l], ssm_b_im[l], ssm_c_re[l], ssm_c_im[l]),
            'merge': (ssm_d[l].reshape(1, SSM_WIDTH), w_glu[l].astype(BF16), b_glu[l].reshape(1, SSM_WIDTH),
                      w_branch_attn[l].astype(BF16), w_branch_ssm[l].astype(BF16), w_out[l].astype(BF16)),
            'ffn': (norm2_g[l].reshape(1, D_MODEL), w_up[l].astype(BF16), conv_w[l],
                    conv_b[l].reshape(1, D_FF), w_down[l].astype(BF16), gf),
        }
        assert last, "the final RMSNorm is fused into the last layer's ffn kernel"
        hp, sp = _prompt_layer(hp, rel_bias, lw, PROMPT_TILE)
        hs, ss = _sample_layer(hs, (cache_kv_w128[l], cache_kv_w512[l], cache_kv_w2048[l]),
                               state_ssm_re[l], state_ssm_im[l], state_ffn_conv[l], rel_bias, lw)
        st_p.append(sp)
        st_s.append(ss)
    stack = lambda states, i: jnp.stack([st[i] for st in states], axis=0)
    return (hp, hs, *[stack(st_p, i) for i in range(6)], *[stack(st_s, i) for i in range(6)])
```

```python
import functools
import math

import jax
import jax.numpy as jnp
from jax import lax
from jax.experimental import pallas as pl
from jax.experimental.pallas import tpu as pltpu

F32 = jnp.float32
BF16 = jnp.bfloat16

D_MODEL = 1024
HEAD_DIM = 64
HEADS_PER_GROUP = 4
DIL_PATTERNS = ((128, 1), (512, 4), (2048, 16))
N_DIL_GROUPS = len(DIL_PATTERNS)
GROUP_WIDTH = HEADS_PER_GROUP * HEAD_DIM
QK_WIDTH = N_DIL_GROUPS * GROUP_WIDTH
QKV_WIDTH = 3 * QK_WIDTH
GROUP_QKV = 3 * GROUP_WIDTH
QBLOCK = 128
SSM_GROUP = 16
SSM_STATE = 64
SSM_WIDTH = D_MODEL // 2
N_SSM_GROUPS = SSM_WIDTH // SSM_GROUP
N_SSM_PAIRS = N_SSM_GROUPS // 2
SSM_CHUNK = 16
D_FF = 2816
CONV_W = 3
N_BUCKETS = 32
MAX_DISTANCE = 2048
NORM_EPS = 1e-6
NEG_INF = -1e30
U_START = QKV_WIDTH
GATE_START = U_START + SSM_WIDTH
IN_WIDTH = GATE_START + 2 * D_MODEL
QK_SCALE = HEAD_DIM ** -0.5

VMEM_LIMIT_BYTES = 56 * 1024 * 1024
SUBLANES = 8
LANES = 128


def _compiler_params(*semantics):
    return pltpu.CompilerParams(dimension_semantics=semantics, vmem_limit_bytes=VMEM_LIMIT_BYTES)


def _resident(shape):
    nd = len(shape)
    return pl.BlockSpec(shape, lambda *_: (0,) * nd, pipeline_mode=pl.Buffered(1))


def _rmsnorm(xf, g):
    y = xf * lax.rsqrt(jnp.mean(xf * xf, axis=-1, keepdims=True) + NORM_EPS)
    return y * g


def _sigmoid(x):
    return 1.0 / (1.0 + jnp.exp(-x))


def _dot(a, b):
    return jnp.dot(a, b, preferred_element_type=F32)


def _dot_nt(a, b):
    return lax.dot_general(a, b, (((1,), (1,)), ((), ())), preferred_element_type=F32)


def _in_proj_kernel(x_ref, g_ref, w_ref, qkv0_ref, qkv1_ref, qkv2_ref, u_ref, gate_ref, xn_scr, *, tm, dils):
    xf = _rmsnorm(x_ref[...], g_ref[...])
    xn = xf.astype(BF16)
    n_lane_blocks = D_MODEL // LANES
    if any(dil > 1 for dil in dils):
        for k in range(n_lane_blocks):
            xn_scr[k] = xf[:, k * LANES:(k + 1) * LANES]
    for g, (ref, dil) in enumerate(zip((qkv0_ref, qkv1_ref, qkv2_ref), dils)):
        w = w_ref[:, g * GROUP_QKV:(g + 1) * GROUP_QKV]
        if dil == 1:
            ref[0, 0] = _dot(xn, w)
        else:
            n = tm // dil
            xr = jnp.concatenate(
                [jnp.concatenate([xn_scr[k, pl.ds(r, n, stride=dil), :] for k in range(n_lane_blocks)], axis=1)
                 for r in range(dil)], axis=0)
            res = _dot(xr.astype(BF16), w)
            for r in range(dil):
                ref[0, r] = res[r * n:(r + 1) * n]
    u_ref[...] = _dot(xn, w_ref[:, U_START:U_START + SSM_WIDTH])
    for c0 in range(0, 2 * D_MODEL, D_MODEL):
        gate_ref[:, c0:c0 + D_MODEL] = _dot(xn, w_ref[:, GATE_START + c0:GATE_START + c0 + D_MODEL])


def _group_major_columns(w_in):
    parts = []
    for g in range(N_DIL_GROUPS):
        for base in (0, QK_WIDTH, 2 * QK_WIDTH):
            parts.append(w_in[:, base + g * GROUP_WIDTH:base + (g + 1) * GROUP_WIDTH])
    parts.append(w_in[:, U_START:])
    return jnp.concatenate(parts, axis=1)


def _in_proj(x2d, g, w_bf16, n_seq, tm, dils):
    m = x2d.shape[0]
    seq = m // n_seq
    tiles = seq // tm
    row = lambda width: pl.BlockSpec((tm, width), lambda b, j: (b * tiles + j, 0))
    res_spec = lambda dil: pl.BlockSpec((1, dil, tm // dil, GROUP_QKV), lambda b, j: (b, 0, j, 0))
    return pl.pallas_call(
        functools.partial(_in_proj_kernel, tm=tm, dils=dils),
        grid=(n_seq, tiles),
        in_specs=[row(D_MODEL), _resident((1, D_MODEL)), _resident((D_MODEL, IN_WIDTH))],
        out_specs=[*[res_spec(d) for d in dils], row(SSM_WIDTH), row(2 * D_MODEL)],
        out_shape=[*[jax.ShapeDtypeStruct((n_seq, d, seq // d, GROUP_QKV), F32) for d in dils],
                   jax.ShapeDtypeStruct((m, SSM_WIDTH), F32),
                   jax.ShapeDtypeStruct((m, 2 * D_MODEL), F32)],
        scratch_shapes=[pltpu.VMEM((D_MODEL // LANES, tm, LANES), F32)],
        compiler_params=_compiler_params("arbitrary", "arbitrary"),
        name="in_proj",
    )(x2d, g, w_bf16)


def _rel_bucket(dist):
    max_exact = N_BUCKETS // 2
    n = jnp.maximum(dist, 0)
    nf = jnp.maximum(n, 1).astype(F32)
    large = max_exact + (jnp.log(nf / max_exact) / math.log(MAX_DISTANCE / max_exact)
                         * (N_BUCKETS - max_exact)).astype(jnp.int32)
    large = jnp.minimum(large, N_BUCKETS - 1)
    return jnp.where(n < max_exact, n, large)


def _masked_bias(tab, strides, valid, dil):
    n_dist = QBLOCK
    bucket = _rel_bucket(jnp.clip(strides, 0, n_dist) * dil)
    tab = tab.astype(F32)
    bias = jnp.zeros((tab.shape[1],) + strides.shape, F32)
    for k in range(N_BUCKETS):
        bias = jnp.where(bucket == k, tab[k].reshape((-1,) + (1,) * strides.ndim), bias)
    return jnp.where(valid, bias, NEG_INF)


def _prompt_bias(tab, dil):
    qi = jnp.arange(QBLOCK)[:, None]
    ki = jnp.arange(QBLOCK)[None, :]
    j_prev = qi + QBLOCK - ki
    j_cur = qi - ki
    return jnp.stack([_masked_bias(tab, j_prev, j_prev <= QBLOCK, dil),
                      _masked_bias(tab, j_cur, j_cur >= 0, dil)], axis=0)


def _sample_bias(tab, dil, t_new):
    t = jnp.arange(t_new)[:, None]
    m = jnp.arange(QBLOCK)[None, :]
    if dil == 1:
        j_buf = QBLOCK + t - m
        ok_buf = m >= t
    else:
        j_buf = QBLOCK - m + 0 * t
        ok_buf = jnp.ones_like(j_buf, dtype=bool)
    delta = t - m
    ok_new = (m < t_new) & (delta >= 0) & (delta % dil == 0)
    b_buf = _masked_bias(tab, j_buf, ok_buf, dil)
    b_new = _masked_bias(tab, delta // dil, ok_new, dil)
    return (b_buf.reshape(HEADS_PER_GROUP * t_new, QBLOCK),
            b_new.reshape(HEADS_PER_GROUP * t_new, QBLOCK))


def _attn_prompt_kernel(q_ref, kp_ref, kc_ref, vp_ref, vc_ref, bias_ref, o_ref, lse_ref, *, nq):
    first_tile = pl.program_id(2) == 0
    for i in range(nq):
        rows = slice(i * QBLOCK, (i + 1) * QBLOCK)
        q = (q_ref[0, 0, rows, :] * QK_SCALE).astype(BF16)
        if i == 0:
            k_prev, v_prev = kp_ref[0, 0], vp_ref[0, 0]
        else:
            prev_rows = slice((i - 1) * QBLOCK, i * QBLOCK)
            k_prev, v_prev = kc_ref[0, 0, prev_rows, :], vc_ref[0, 0, prev_rows, :]
        k_prev, v_prev = k_prev.astype(BF16), v_prev.astype(BF16)
        k_cur, v_cur = kc_ref[0, 0, rows, :].astype(BF16), vc_ref[0, 0, rows, :].astype(BF16)
        for h in range(HEADS_PER_GROUP):
            cols = slice(h * HEAD_DIM, (h + 1) * HEAD_DIM)
            s_prev = _dot_nt(q[:, cols], k_prev[:, cols]) + bias_ref[0, h]
            if i == 0:
                s_prev = jnp.where(first_tile, NEG_INF, s_prev)
            s_cur = _dot_nt(q[:, cols], k_cur[:, cols]) + bias_ref[1, h]
            m = jnp.maximum(jnp.max(s_prev, axis=-1, keepdims=True),
                            jnp.max(s_cur, axis=-1, keepdims=True))
            p_prev = jnp.exp(s_prev - m)
            p_cur = jnp.exp(s_cur - m)
            den = jnp.sum(p_prev, axis=-1, keepdims=True) + jnp.sum(p_cur, axis=-1, keepdims=True)
            o = _dot(p_prev.astype(BF16), v_prev[:, cols]) + _dot(p_cur.astype(BF16), v_cur[:, cols])
            o_ref[0, 0, rows, cols] = o / den
            lse_ref[0, 0, rows, cols] = jnp.broadcast_to(m + jnp.log(den), (QBLOCK, HEAD_DIM))


def _attn_prompt(qkv, bias):
    batch, dil, length, _ = qkv.shape
    tq = min(4 * QBLOCK, length)
    nq = tq // QBLOCK

    def cur(col):
        return pl.BlockSpec((1, 1, tq, GROUP_WIDTH), lambda b, r, n: (b, r, n, col))

    def prev(col):
        return pl.BlockSpec((1, 1, QBLOCK, GROUP_WIDTH),
                            lambda b, r, n: (b, r, jnp.maximum(n * nq - 1, 0), col))

    out_sds = jax.ShapeDtypeStruct((batch, dil, length, GROUP_WIDTH), F32)
    return pl.pallas_call(
        functools.partial(_attn_prompt_kernel, nq=nq),
        grid=(batch, dil, length // tq),
        in_specs=[cur(0), prev(1), cur(1), prev(2), cur(2),
                  _resident((2, HEADS_PER_GROUP, QBLOCK, QBLOCK))],
        out_specs=[cur(0), cur(0)],
        out_shape=[out_sds, out_sds],
        compiler_params=_compiler_params("arbitrary", "arbitrary", "arbitrary"),
        name=f"attn_prompt_d{dil}",
    )(qkv, qkv, qkv, qkv, qkv, bias)


def _attn_sample_kernel(q0_ref, q1_ref, q2_ref, c0_ref, c1_ref, c2_ref, tb_ref, tn_ref,
                        o0_ref, l0_ref, o1_ref, l1_ref, o2_ref, l2_ref, kn_scr, vn_scr, *, t_new):
    n_rows = HEADS_PER_GROUP * t_new
    row_w = lax.broadcasted_iota(jnp.int32, (n_rows, GROUP_WIDTH), 0)
    lane_w = lax.broadcasted_iota(jnp.int32, (n_rows, GROUP_WIDTH), 1)
    own_head = (row_w // t_new) == (lane_w // HEAD_DIM)
    tok_w = row_w % t_new
    tok_k = lax.broadcasted_iota(jnp.int32, (n_rows, QBLOCK), 0) % t_new

    def fold_heads(x):
        x = jnp.where(own_head, x, 0.0)
        out = x[0:t_new]
        for h in range(1, HEADS_PER_GROUP):
            out = out + x[h * t_new:(h + 1) * t_new]
        return out

    caches = (c0_ref, c1_ref, c2_ref)
    outs = ((o0_ref, l0_ref), (o1_ref, l1_ref), (o2_ref, l2_ref))
    kv_width = 2 * GROUP_WIDTH
    for g, (_, dil) in enumerate(DIL_PATTERNS):
        qkv_ref = (q0_ref, q1_ref, q2_ref)[g]
        q = qkv_ref[0, :, 0:GROUP_WIDTH] * QK_SCALE
        q_rows = jnp.where(own_head, jnp.concatenate([q] * HEADS_PER_GROUP, axis=0), 0.0).astype(BF16)
        kn_scr[...] = jnp.zeros_like(kn_scr)
        vn_scr[...] = jnp.zeros_like(vn_scr)
        kn_scr[0:t_new, :] = qkv_ref[0, :, GROUP_WIDTH:2 * GROUP_WIDTH]
        vn_scr[0:t_new, :] = qkv_ref[0, :, 2 * GROUP_WIDTH:3 * GROUP_WIDTH]
        cache = caches[g]
        n_views = 1 if dil == 1 else t_new

        def k_view(t):
            return cache[0, :, t * kv_width:t * kv_width + GROUP_WIDTH].astype(BF16)

        def v_view(t):
            return cache[0, :, t * kv_width + GROUP_WIDTH:(t + 1) * kv_width].astype(BF16)

        if n_views == 1:
            s_buf = _dot_nt(q_rows, k_view(0))
        else:
            s_buf = jnp.zeros((n_rows, QBLOCK), F32)
            for t in range(n_views):
                s_buf = jnp.where(tok_k == t, _dot_nt(q_rows, k_view(t)), s_buf)
        s_buf = s_buf + tb_ref[g]
        s_new = _dot_nt(q_rows, kn_scr[...].astype(BF16)) + tn_ref[g]
        m = jnp.maximum(jnp.max(s_buf, axis=-1, keepdims=True), jnp.max(s_new, axis=-1, keepdims=True))
        p_buf = jnp.exp(s_buf - m)
        p_new = jnp.exp(s_new - m)
        den = jnp.sum(p_buf, axis=-1, keepdims=True) + jnp.sum(p_new, axis=-1, keepdims=True)
        p_buf16 = p_buf.astype(BF16)
        if n_views == 1:
            o = _dot(p_buf16, v_view(0))
        else:
            o = jnp.zeros((n_rows, GROUP_WIDTH), F32)
            for t in range(n_views):
                o = jnp.where(tok_w == t, _dot(p_buf16, v_view(t)), o)
        o = o + _dot(p_new.astype(BF16), vn_scr[...].astype(BF16))
        o_ref, l_ref = outs[g]
        o_ref[0] = fold_heads(o / den)
        l_ref[0] = fold_heads(jnp.broadcast_to(m + jnp.log(den), (n_rows, GROUP_WIDTH)))


def _attn_sample(qkvs, caches, tb, tn):
    batch, t_new, _ = qkvs[0].shape
    n_rows = HEADS_PER_GROUP * t_new
    cache_specs = []
    for (_, dil), c in zip(DIL_PATTERNS, caches):
        width = 2 * GROUP_WIDTH * min(dil, t_new)
        cache_specs.append(pl.BlockSpec((1, QBLOCK, width), lambda b: (b, 0, 0)))
    qkv_spec = pl.BlockSpec((1, t_new, GROUP_QKV), lambda b: (b, 0, 0))
    out_spec = pl.BlockSpec((1, t_new, GROUP_WIDTH), lambda b: (b, 0, 0))
    out_sds = jax.ShapeDtypeStruct((batch, t_new, GROUP_WIDTH), F32)
    return pl.pallas_call(
        functools.partial(_attn_sample_kernel, t_new=t_new),
        grid=(batch,),
        in_specs=[*[qkv_spec] * N_DIL_GROUPS, *cache_specs,
                  _resident((N_DIL_GROUPS, n_rows, QBLOCK)), _resident((N_DIL_GROUPS, n_rows, QBLOCK))],
        out_specs=[out_spec] * (2 * N_DIL_GROUPS),
        out_shape=[out_sds] * (2 * N_DIL_GROUPS),
        scratch_shapes=[pltpu.VMEM((QBLOCK, GROUP_WIDTH), F32), pltpu.VMEM((QBLOCK, GROUP_WIDTH), F32)],
        compiler_params=_compiler_params("arbitrary"),
        name="attn_sample",
    )(*qkvs, *caches, tb, tn)


def _pair_block_diag(a):
    g, r, c = a.shape
    a = a.reshape(g // 2, 2, r, c)
    z = jnp.zeros((g // 2, r, c), a.dtype)
    top = jnp.concatenate([a[:, 0], z], axis=2)
    bot = jnp.concatenate([z, a[:, 1]], axis=2)
    return jnp.concatenate([top, bot], axis=1)


def _ssm_chunk_terms(chunk, log_dt, lam_re, lam_im, b_re, b_im, c_re, c_im):
    hi = lax.Precision.HIGHEST
    dt = jnp.exp(log_dt.astype(F32))[:, None]
    lr, li = lam_re.astype(F32), lam_im.astype(F32)
    mag = jnp.exp(lr * dt)
    ab_re, ab_im = mag * jnp.cos(li * dt), mag * jnp.sin(li * dt)
    den = lr * lr + li * li
    nr, ni = ab_re - 1.0, ab_im
    coef_re = (nr * lr + ni * li) / den
    coef_im = (ni * lr - nr * li) / den
    br, bi = b_re.astype(F32), b_im.astype(F32)
    bb_re = coef_re[..., None] * br - coef_im[..., None] * bi
    bb_im = coef_re[..., None] * bi + coef_im[..., None] * br
    pw_re, pw_im = [jnp.ones_like(ab_re)], [jnp.zeros_like(ab_im)]
    for _ in range(chunk):
        pr, pi = pw_re[-1], pw_im[-1]
        pw_re.append(pr * ab_re - pi * ab_im)
        pw_im.append(pr * ab_im + pi * ab_re)
    pw_re, pw_im = jnp.stack(pw_re), jnp.stack(pw_im)
    akb_re = pw_re[:chunk, :, :, None] * bb_re[None] - pw_im[:chunk, :, :, None] * bb_im[None]
    akb_im = pw_re[:chunk, :, :, None] * bb_im[None] + pw_im[:chunk, :, :, None] * bb_re[None]
    cr, ci = c_re.astype(F32), c_im.astype(F32)
    e_re = cr[None] * pw_re[1:, :, None, :] - ci[None] * pw_im[1:, :, None, :]
    e_im = cr[None] * pw_im[1:, :, None, :] + ci[None] * pw_re[1:, :, None, :]
    kern = (jnp.einsum('gpn,kgnq->kgpq', cr, akb_re, precision=hi)
            - jnp.einsum('gpn,kgnq->kgpq', ci, akb_im, precision=hi))
    return (akb_re, akb_im), (e_re, e_im), kern, (pw_re[chunk], pw_im[chunk])


def _causal_lag_kernels(kern):
    chunk = kern.shape[0]
    lag = jnp.arange(chunk)[None, :] - jnp.arange(chunk)[:, None]
    return jnp.where((lag >= 0)[:, :, None, None, None], kern[jnp.maximum(lag, 0)], 0.0)


def _ssm_pair_operators(chunk, *params):
    (akb_re, akb_im), (e_re, e_im), kern, (a_re, a_im) = _ssm_chunk_terms(chunk, *params)
    _, g, n, p = akb_re.shape
    w_re = akb_re[::-1].transpose(1, 0, 3, 2).reshape(g, chunk * p, n)
    w_im = akb_im[::-1].transpose(1, 0, 3, 2).reshape(g, chunk * p, n)
    et_re = e_re.transpose(1, 3, 0, 2).reshape(g, n, chunk * p)
    et_im = (-e_im).transpose(1, 3, 0, 2).reshape(g, n, chunk * p)
    mt = _causal_lag_kernels(kern).transpose(2, 0, 4, 1, 3).reshape(g, chunk * p, chunk * p)
    ops = [_pair_block_diag(x).astype(BF16) for x in (w_re, w_im, mt, et_re, et_im)]
    return ops, a_re.reshape(1, g * n), a_im.reshape(1, g * n)


SSM_BLOCK_GROUPS = 8
SSM_BLOCK_CH = SSM_BLOCK_GROUPS * SSM_GROUP
SSM_BLOCK_STATE = SSM_BLOCK_GROUPS * SSM_STATE
N_SSM_BLOCKS = N_SSM_GROUPS // SSM_BLOCK_GROUPS


def _ssm_block_operators(chunk, *params):
    (akb_re, akb_im), (e_re, e_im), kern, (a_re, a_im) = _ssm_chunk_terms(chunk, *params)
    _, g, n, p = akb_re.shape
    gb, nb = SSM_BLOCK_GROUPS, N_SSM_BLOCKS
    eye = jnp.eye(gb, dtype=F32)

    def w_op(akb):
        x = akb[::-1].reshape(chunk, nb, gb, n, p).transpose(1, 0, 2, 4, 3)
        x = x[:, :, :, :, None, :] * eye[None, None, :, None, :, None]
        return x.reshape(nb, chunk * gb * p, gb * n).astype(BF16)

    def e_op(e):
        x = e.reshape(chunk, nb, gb, p, n).transpose(1, 2, 4, 0, 3)
        x = x[:, :, :, :, None, :] * eye[None, :, None, None, :, None]
        return x.reshape(nb, gb * n, chunk * gb * p).astype(BF16)

    m = _causal_lag_kernels(kern).reshape(chunk, chunk, nb, gb, p, p).transpose(2, 0, 3, 5, 1, 4)
    m = m[:, :, :, :, :, None, :] * eye[None, None, :, None, None, :, None]
    m = m.reshape(nb, chunk * gb * p, chunk * gb * p).astype(BF16)
    ops = [w_op(akb_re), w_op(akb_im), m, e_op(e_re), e_op(-e_im)]
    return ops, a_re.reshape(nb, 1, gb * n), a_im.reshape(nb, 1, gb * n)


PAIR_LANES = 2 * SSM_STATE
SSM_X_WIDTH = SSM_CHUNK * SSM_BLOCK_CH


def _ssm_prompt_kernel(u_ref, wre_ref, wim_ref, m_ref, etre_ref, etim_ref, are_ref, aim_ref,
                       y_ref, hre_ref, him_ref, x_scr, sre_scr, sim_scr, *, batch, chunks):
    phase, b = pl.program_id(1), pl.program_id(2)
    for t in range(SSM_CHUNK):
        x_scr[:, t * SSM_BLOCK_CH:(t + 1) * SSM_BLOCK_CH] = (
            u_ref[pl.ds(t, chunks, stride=SSM_CHUNK), :].astype(BF16))
    rows = pl.ds(pl.multiple_of(b * chunks, chunks), chunks)
    state_blocks = SSM_BLOCK_STATE // LANES

    @pl.when(phase == 0)
    def _():
        x = x_scr[...]
        g_re, g_im = _dot(x, wre_ref[0]), _dot(x, wim_ref[0])
        for k in range(state_blocks):
            sre_scr[k, rows, :] = g_re[:, k * LANES:(k + 1) * LANES]
            sim_scr[k, rows, :] = g_im[:, k * LANES:(k + 1) * LANES]

    @pl.when((phase == 1) & (b == 0))
    def _():
        same_lanes = lambda ref, k: ref[0][:, k * LANES:(k + 1) * LANES]

        def step(c, carry):
            same_chunk = pl.ds(c, batch, stride=chunks)
            out = []
            for k, (h_re, h_im) in enumerate(carry):
                a_re, a_im = same_lanes(are_ref, k), same_lanes(aim_ref, k)
                g_re, g_im = sre_scr[k, same_chunk, :], sim_scr[k, same_chunk, :]
                sre_scr[k, same_chunk, :] = h_re
                sim_scr[k, same_chunk, :] = h_im
                out.append((a_re * h_re - a_im * h_im + g_re, a_re * h_im + a_im * h_re + g_im))
            return tuple(out)

        zero = jnp.zeros((batch, LANES), F32)
        final = lax.fori_loop(0, chunks, step, ((zero, zero),) * state_blocks)
        for k, (h_re, h_im) in enumerate(final):
            hre_ref[0, :, k * LANES:(k + 1) * LANES] = h_re
            him_ref[0, :, k * LANES:(k + 1) * LANES] = h_im

    @pl.when(phase == 1)
    def _():
        h_re = jnp.concatenate([sre_scr[k, rows, :] for k in range(state_blocks)], axis=1).astype(BF16)
        h_im = jnp.concatenate([sim_scr[k, rows, :] for k in range(state_blocks)], axis=1).astype(BF16)
        pair_w = 2 * SSM_BLOCK_CH
        for j in range(SSM_CHUNK // 2):
            cols = slice(j * pair_w, (j + 1) * pair_w)
            k_in = (j + 1) * pair_w
            yj = (_dot(x_scr[:, :k_in], m_ref[0, :k_in, cols])
                  + _dot(h_re, etre_ref[0, :, cols]) + _dot(h_im, etim_ref[0, :, cols]))
            for i in range(2):
                y_ref[pl.ds(2 * j + i, chunks, stride=SSM_CHUNK), :] = (
                    yj[:, i * SSM_BLOCK_CH:(i + 1) * SSM_BLOCK_CH])


def _ssm_prompt(u, ops, a_re, a_im, batch, seq):
    chunks = seq // SSM_CHUNK
    op_spec = lambda arr: pl.BlockSpec((1,) + arr.shape[1:], lambda g, ph, b: (g, 0, 0),
                                       pipeline_mode=pl.Buffered(1))
    state_spec = pl.BlockSpec((1, batch, SSM_BLOCK_STATE), lambda g, ph, b: (g, 0, 0))
    state_sds = jax.ShapeDtypeStruct((N_SSM_BLOCKS, batch, SSM_BLOCK_STATE), F32)
    y, h_re, h_im = pl.pallas_call(
        functools.partial(_ssm_prompt_kernel, batch=batch, chunks=chunks),
        grid=(N_SSM_BLOCKS, 2, batch),
        in_specs=[pl.BlockSpec((seq, SSM_BLOCK_CH), lambda g, ph, b: (b, g)),
                  *[op_spec(o) for o in ops], op_spec(a_re), op_spec(a_im)],
        out_specs=[pl.BlockSpec((seq, SSM_BLOCK_CH), lambda g, ph, b: (b * ph, g)), state_spec, state_spec],
        out_shape=[jax.ShapeDtypeStruct((batch * seq, SSM_WIDTH), F32), state_sds, state_sds],
        scratch_shapes=[pltpu.VMEM((chunks, SSM_X_WIDTH), BF16),
                        pltpu.VMEM((SSM_BLOCK_STATE // LANES, batch * chunks, LANES), F32),
                        pltpu.VMEM((SSM_BLOCK_STATE // LANES, batch * chunks, LANES), F32)],
        compiler_params=_compiler_params("arbitrary", "arbitrary", "arbitrary"),
        name="ssm_prompt",
    )(u, *ops, a_re, a_im)

    def by_sequence(h):
        h = h.reshape(N_SSM_BLOCKS, batch, SSM_BLOCK_GROUPS, SSM_STATE).transpose(1, 0, 2, 3)
        return h.reshape(batch, N_SSM_GROUPS, SSM_STATE)

    return y, by_sequence(h_re), by_sequence(h_im)


def _ssm_sample_kernel(x_ref, wre_ref, wim_ref, mt_ref, etre_ref, etim_ref, are_ref, aim_ref,
                       h0re_ref, h0im_ref, y_ref, hre_ref, him_ref):
    for gp in range(N_SSM_PAIRS):
        lanes = slice(gp * PAIR_LANES, (gp + 1) * PAIR_LANES)
        x = x_ref[gp]
        h_re, h_im = h0re_ref[:, lanes], h0im_ref[:, lanes]
        a_re, a_im = are_ref[:, lanes], aim_ref[:, lanes]
        hre_ref[:, lanes] = a_re * h_re - a_im * h_im + _dot(x, wre_ref[gp])
        him_ref[:, lanes] = a_re * h_im + a_im * h_re + _dot(x, wim_ref[gp])
        y_ref[gp] = (_dot(x, mt_ref[gp]) + _dot(h_re.astype(BF16), etre_ref[gp])
                     + _dot(h_im.astype(BF16), etim_ref[gp]))


def _ssm_sample(x, ops, a_re, a_im, h0_re, h0_im):
    batch = x.shape[1]
    args = (x, *ops, a_re, a_im, h0_re, h0_im)
    state_sds = jax.ShapeDtypeStruct((batch, N_SSM_GROUPS * SSM_STATE), F32)
    out_shape = [jax.ShapeDtypeStruct(x.shape, F32), state_sds, state_sds]
    whole = lambda shape: pl.BlockSpec(shape, lambda i, nd=len(shape): (0,) * nd)
    return pl.pallas_call(
        _ssm_sample_kernel,
        grid=(1,),
        in_specs=[whole(a.shape) for a in args],
        out_specs=[whole(s.shape) for s in out_shape],
        out_shape=out_shape,
        compiler_params=_compiler_params("arbitrary"),
        name="ssm_sample",
    )(*args)


def _gelu_tanh(x):
    return 0.5 * x * (1.0 + jnp.tanh(math.sqrt(2.0 / math.pi) * (x + 0.044715 * (x * x * x))))


def _merge_kernel(o0_ref, l0_ref, o1_ref, l1_ref, o2_ref, l2_ref, ys_ref, u_ref, gate_ref, x_ref,
                  d_ref, wglu_ref, bglu_ref, wba_ref, wbs_ref, wout_ref, out_ref, order_scr, *, tm, dils):
    def row_order(ref, dil, slot):
        if dil == 1:
            return ref[0, 0]
        n = tm // dil
        halves = GROUP_WIDTH // LANES
        for r in range(dil):
            for k in range(halves):
                order_scr[slot * halves + k, pl.ds(r, n, stride=dil), :] = ref[0, r, :, k * LANES:(k + 1) * LANES]
        return jnp.concatenate([order_scr[slot * halves + k] for k in range(halves)], axis=1)

    parts = [row_order(ref, dils[i // 2], i) for i, ref in
             enumerate((o0_ref, l0_ref, o1_ref, l1_ref, o2_ref, l2_ref))]
    o0, l0, o1, l1, o2, l2 = parts
    mx = jnp.maximum(jnp.maximum(l0, l1), l2)
    e0, e1, e2 = jnp.exp(l0 - mx), jnp.exp(l1 - mx), jnp.exp(l2 - mx)
    attn = (e0 * o0 + e1 * o1 + e2 * o2) / (e0 + e1 + e2)
    branch_a = _dot(attn.astype(BF16), wba_ref[...])
    y = _gelu_tanh(ys_ref[...] + d_ref[...] * u_ref[...])
    y = y * _sigmoid(_dot(y.astype(BF16), wglu_ref[...]) + bglu_ref[...])
    branch_s = _dot(y.astype(BF16), wbs_ref[...])
    mix = (_sigmoid(gate_ref[:, 0:D_MODEL]) * branch_a
           + _sigmoid(gate_ref[:, D_MODEL:2 * D_MODEL]) * branch_s)
    out_ref[...] = x_ref[...] + _dot(mix.astype(BF16), wout_ref[...])


def _merge(attn_parts, ys, u, gates, x2d, d, w_glu, b_glu, w_ba, w_bs, w_out, n_seq, tm):
    m = x2d.shape[0]
    tiles = m // n_seq // tm
    dils = tuple(p.shape[1] for p in attn_parts[::2])
    row = lambda width: pl.BlockSpec((tm, width), lambda b, j: (b * tiles + j, 0))
    res_spec = lambda dil: pl.BlockSpec((1, dil, tm // dil, GROUP_WIDTH), lambda b, j: (b, 0, j, 0))
    weights = (d, w_glu, b_glu, w_ba, w_bs, w_out)
    return pl.pallas_call(
        functools.partial(_merge_kernel, tm=tm, dils=dils),
        grid=(n_seq, tiles),
        in_specs=[*[res_spec(p.shape[1]) for p in attn_parts],
                  row(SSM_WIDTH), row(SSM_WIDTH), row(2 * D_MODEL), row(D_MODEL),
                  *[_resident(w.shape) for w in weights]],
        out_specs=row(D_MODEL),
        out_shape=jax.ShapeDtypeStruct((m, D_MODEL), F32),
        scratch_shapes=[pltpu.VMEM((len(attn_parts) * GROUP_WIDTH // LANES, tm, LANES), F32)],
        compiler_params=_compiler_params("arbitrary", "arbitrary"),
        name="merge",
    )(*attn_parts, ys, u, gates, x2d, *weights)


FF_CHUNK = D_FF // 2


def _ffn_kernel(x_ref, carry_ref, g2_ref, wup_ref, cw_ref, cb_ref, wdn_ref, gf_ref,
                y_ref, state_ref, a_scr, *, tm, shift, pad):
    hist = 2 * shift

    @pl.when(pl.program_id(1) == 0)
    def _():
        a_scr[pad - hist:pad, :] = carry_ref[0]

    xf = x_ref[...]
    xn = _rmsnorm(xf, g2_ref[...]).astype(BF16)
    acc = jnp.zeros((tm, D_MODEL), F32)
    for c0 in range(0, D_FF, FF_CHUNK):
        cols = slice(c0, c0 + FF_CHUNK)
        a = _dot(xn, wup_ref[:, cols])
        val = _dot(xn, wup_ref[:, D_FF + c0:D_FF + c0 + FF_CHUNK])
        a_scr[pad:pad + tm, cols] = a
        a_m1 = a_scr[pad - shift:pad - shift + tm, cols]
        a_m2 = a_scr[pad - hist:pad - hist + tm, cols]
        conv = cb_ref[:, cols] + cw_ref[0:1, cols] * a_m2
        conv = conv + cw_ref[1:2, cols] * a_m1
        conv = conv + cw_ref[2:3, cols] * a
        act = conv * _sigmoid(conv) * val
        acc = acc + _dot(act.astype(BF16), wdn_ref[cols, :])
    tail = a_scr[pad + tm - hist:pad + tm, :]
    a_scr[pad - hist:pad, :] = tail
    state_ref[0] = tail
    y_ref[...] = _rmsnorm(xf + acc, gf_ref[...])


def _ffn(x2d, carry, g2, w_up, conv_w, conv_b, w_down, gf, n_seq, tm, shift):
    m = x2d.shape[0]
    tiles = m // n_seq // tm
    hist = 2 * shift
    pad = -(-hist // SUBLANES) * SUBLANES
    row_spec = pl.BlockSpec((tm, D_MODEL), lambda b, j: (b * tiles + j, 0))
    state_spec = pl.BlockSpec((1, hist, D_FF), lambda b, j: (b, 0, 0))
    weights = (g2, w_up, conv_w, conv_b, w_down, gf)
    return pl.pallas_call(
        functools.partial(_ffn_kernel, tm=tm, shift=shift, pad=pad),
        grid=(n_seq, tiles),
        in_specs=[row_spec, state_spec, *[_resident(w.shape) for w in weights]],
        out_specs=[row_spec, state_spec],
        out_shape=[jax.ShapeDtypeStruct((m, D_MODEL), F32),
                   jax.ShapeDtypeStruct((n_seq, hist, D_FF), F32)],
        scratch_shapes=[pltpu.VMEM((pad + tm, D_FF), F32)],
        compiler_params=_compiler_params("arbitrary", "arbitrary"),
        name="ffn",
    )(x2d, carry, *weights)


def _kv_rows(qkv, keep):
    batch, dil, length, _ = qkv.shape
    n = keep // dil
    rows = qkv[:, :, length - n:, GROUP_WIDTH:].reshape(batch, dil, n, 2, HEADS_PER_GROUP, HEAD_DIM)
    return rows.transpose(0, 2, 1, 3, 4, 5).reshape(batch, keep, 2, HEADS_PER_GROUP, HEAD_DIM)


def _prompt_layer(x, rel_bias, lw, tm):
    batch, seq, _ = x.shape
    x2d = x.reshape(batch * seq, D_MODEL)
    dils = tuple(dil for _, dil in DIL_PATTERNS)
    *qkvs, u, gates = _in_proj(x2d, lw['norm1_g'], lw['w_in'], batch, tm, dils)

    attn_parts, kv_new = [], []
    for g, (window, dil) in enumerate(DIL_PATTERNS):
        tab = rel_bias[:, g * HEADS_PER_GROUP:(g + 1) * HEADS_PER_GROUP]
        attn_parts.extend(_attn_prompt(qkvs[g], _prompt_bias(tab, dil)))
        kv_new.append(_kv_rows(qkvs[g], min(window, seq)))

    ops, a_re, a_im = _ssm_block_operators(SSM_CHUNK, *lw['ssm'])
    ys, h_re, h_im = _ssm_prompt(u, ops, a_re, a_im, batch, seq)

    x1 = _merge(attn_parts, ys, u, gates, x2d, *lw['merge'], batch, tm)
    carry = jnp.zeros((batch, CONV_W - 1, D_FF), F32)
    y, conv_state = _ffn(x1, carry, *lw['ffn'], n_seq=batch, tm=tm, shift=1)
    return y.reshape(batch, seq, D_MODEL), (*kv_new, h_re, h_im, conv_state)


def _sample_layer(x, caches, h0_re, h0_im, conv_buf, rel_bias, lw):
    batch, t_new, _ = x.shape
    m = batch * t_new
    x2d = x.transpose(1, 0, 2).reshape(m, D_MODEL)
    *qkvs, u, gates = _in_proj(x2d, lw['norm1_g'], lw['w_in'], 1, m, (1,) * N_DIL_GROUPS)
    qkvs_bt = [q.reshape(t_new, batch, GROUP_QKV).transpose(1, 0, 2) for q in qkvs]

    tbs, tns, views = [], [], []
    for g, (window, dil) in enumerate(DIL_PATTERNS):
        tab = rel_bias[:, g * HEADS_PER_GROUP:(g + 1) * HEADS_PER_GROUP]
        tb, tn = _sample_bias(tab, dil, t_new)
        tbs.append(tb)
        tns.append(tn)
        assert caches[g].shape[1] == window == QBLOCK * dil and (dil == 1 or t_new <= dil)
        views.append(caches[g].reshape(batch, QBLOCK, dil * 2 * GROUP_WIDTH))
    parts = _attn_sample(qkvs_bt, views, jnp.stack(tbs), jnp.stack(tns))
    attn_parts = [p.transpose(1, 0, 2).reshape(1, 1, m, GROUP_WIDTH) for p in parts]
    kv_new = [_kv_rows(q.reshape(batch, 1, t_new, GROUP_QKV), t_new) for q in qkvs_bt]

    ops, a_re, a_im = _ssm_pair_operators(t_new, *lw['ssm'])
    xs = u.reshape(t_new, batch, N_SSM_PAIRS, 2, SSM_GROUP)
    xs = xs.transpose(2, 1, 3, 0, 4).reshape(N_SSM_PAIRS, batch, 2 * t_new * SSM_GROUP)
    ys, h_re, h_im = _ssm_sample(xs.astype(BF16), ops, a_re, a_im,
                                 h0_re.reshape(batch, -1), h0_im.reshape(batch, -1))
    ys = ys.reshape(N_SSM_PAIRS, batch, 2, t_new, SSM_GROUP).transpose(3, 1, 0, 2, 4).reshape(m, SSM_WIDTH)

    x1 = _merge(attn_parts, ys, u, gates, x2d, *lw['merge'], 1, m)
    carry = conv_buf.transpose(1, 0, 2).reshape(1, (CONV_W - 1) * batch, D_FF)
    y, conv_state = _ffn(x1, carry, *lw['ffn'], n_seq=1, tm=m, shift=batch)
    y = y.reshape(t_new, batch, D_MODEL).transpose(1, 0, 2)
    conv_state = conv_state.reshape(CONV_W - 1, batch, D_FF).transpose(1, 0, 2)
    state_shape = (batch, N_SSM_GROUPS, SSM_STATE)
    return y, (*kv_new, h_re.reshape(state_shape), h_im.reshape(state_shape), conv_state)


PROMPT_TILE = 256


def kernel(x_prompt, x_sample, cache_kv_w128, cache_kv_w512, cache_kv_w2048, state_ssm_re, state_ssm_im, state_ffn_conv, rel_bias, norm1_g, w_in, ssm_log_dt, ssm_lambda_re, ssm_lambda_im, ssm_b_re, ssm_b_im, ssm_c_re, ssm_c_im, ssm_d, w_glu, b_glu, w_branch_attn, w_branch_ssm, w_out, norm2_g, w_up, conv_w, conv_b, w_down, norm_f_g):
    depth = w_in.shape[0]
    hp, hs = x_prompt, x_sample
    st_p, st_s = [], []
    gf = norm_f_g.reshape(1, D_MODEL)
    for l in range(depth):
        last = l == depth - 1
        lw = {
            'norm1_g': norm1_g[l].reshape(1, D_MODEL),
            'w_in': _group_major_columns(w_in[l]).astype(BF16),
            'ssm': (ssm_log_dt[l], ssm_lambda_re[l], ssm_lambda_im[l],
                    ssm_b_re[l], ssm_b_im[l], ssm_c_re[l], ssm_c_im[l]),
            'merge': (ssm_d[l].reshape(1, SSM_WIDTH), w_glu[l].astype(BF16), b_glu[l].reshape(1, SSM_WIDTH),
                      w_branch_attn[l].astype(BF16), w_branch_ssm[l].astype(BF16), w_out[l].astype(BF16)),
            'ffn': (norm2_g[l].reshape(1, D_MODEL), w_up[l].astype(BF16), conv_w[l],
                    conv_b[l].reshape(1, D_FF), w_down[l].astype(BF16), gf),
        }
        assert last, "the final RMSNorm is fused into the last layer's ffn kernel"
        hp, sp = _prompt_layer(hp, rel_bias, lw, PROMPT_TILE)
        hs, ss = _sample_layer(hs, (cache_kv_w128[l], cache_kv_w512[l], cache_kv_w2048[l]),
                               state_ssm_re[l], state_ssm_im[l], state_ffn_conv[l], rel_bias, lw)
        st_p.append(sp)
        st_s.append(ss)
    stack = lambda states, i: jnp.stack([st[i] for st in states], axis=0)
    return (hp, hs, *[stack(st_p, i) for i in range(6)], *[stack(st_s, i) for i in range(6)])
```

```python
import functools
import math

import jax
import jax.numpy as jnp
from jax import lax
from jax.experimental import pallas as pl
from jax.experimental.pallas import tpu as pltpu

F32 = jnp.float32
BF16 = jnp.bfloat16

D_MODEL = 1024
HEAD_DIM = 64
HEADS_PER_GROUP = 4
DIL_PATTERNS = ((128, 1), (512, 4), (2048, 16))
N_DIL_GROUPS = len(DIL_PATTERNS)
GROUP_WIDTH = HEADS_PER_GROUP * HEAD_DIM
QK_WIDTH = N_DIL_GROUPS * GROUP_WIDTH
QKV_WIDTH = 3 * QK_WIDTH
GROUP_QKV = 3 * GROUP_WIDTH
QBLOCK = 128
SSM_GROUP = 16
SSM_STATE = 64
SSM_WIDTH = D_MODEL // 2
N_SSM_GROUPS = SSM_WIDTH // SSM_GROUP
N_SSM_PAIRS = N_SSM_GROUPS // 2
SSM_CHUNK = 16
D_FF = 2816
CONV_W = 3
N_BUCKETS = 32
MAX_DISTANCE = 2048
NORM_EPS = 1e-6
NEG_INF = -1e30
U_START = QKV_WIDTH
GATE_START = U_START + SSM_WIDTH
IN_WIDTH = GATE_START + 2 * D_MODEL
QK_SCALE = HEAD_DIM ** -0.5

VMEM_LIMIT_BYTES = 56 * 1024 * 1024
SUBLANES = 8
LANES = 128


def _compiler_params(*semantics):
    return pltpu.CompilerParams(dimension_semantics=semantics, vmem_limit_bytes=VMEM_LIMIT_BYTES)


def _resident(shape):
    nd = len(shape)
    return pl.BlockSpec(shape, lambda *_: (0,) * nd, pipeline_mode=pl.Buffered(1))


def _rmsnorm(xf, g):
    y = xf * lax.rsqrt(jnp.mean(xf * xf, axis=-1, keepdims=True) + NORM_EPS)
    return y * g


def _sigmoid(x):
    return 1.0 / (1.0 + jnp.exp(-x))


def _dot(a, b):
    return jnp.dot(a, b, preferred_element_type=F32)


def _dot_nt(a, b):
    return lax.dot_general(a, b, (((1,), (1,)), ((), ())), preferred_element_type=F32)


def _in_proj_kernel(x_ref, g_ref, w_ref, qkv0_ref, qkv1_ref, qkv2_ref, u_ref, gate_ref, xn_scr, *, tm, dils):
    xf = _rmsnorm(x_ref[...], g_ref[...])
    xn = xf.astype(BF16)
    n_lane_blocks = D_MODEL // LANES
    if any(dil > 1 for dil in dils):
        for k in range(n_lane_blocks):
            xn_scr[k] = xf[:, k * LANES:(k + 1) * LANES]
    for g, (ref, dil) in enumerate(zip((qkv0_ref, qkv1_ref, qkv2_ref), dils)):
        w = w_ref[:, g * GROUP_QKV:(g + 1) * GROUP_QKV]
        if dil == 1:
            ref[0, 0] = _dot(xn, w)
        else:
            n = tm // dil
            xr = jnp.concatenate(
                [jnp.concatenate([xn_scr[k, pl.ds(r, n, stride=dil), :] for k in range(n_lane_blocks)], axis=1)
                 for r in range(dil)], axis=0)
            res = _dot(xr.astype(BF16), w)
            for r in range(dil):
                ref[0, r] = res[r * n:(r + 1) * n]
    u_ref[...] = _dot(xn, w_ref[:, U_START:U_START + SSM_WIDTH])
    for c0 in range(0, 2 * D_MODEL, D_MODEL):
        gate_ref[:, c0:c0 + D_MODEL] = _dot(xn, w_ref[:, GATE_START + c0:GATE_START + c0 + D_MODEL])


def _group_major_columns(w_in):
    parts = []
    for g in range(N_DIL_GROUPS):
        for base in (0, QK_WIDTH, 2 * QK_WIDTH):
            parts.append(w_in[:, base + g * GROUP_WIDTH:base + (g + 1) * GROUP_WIDTH])
    parts.append(w_in[:, U_START:])
    return jnp.concatenate(parts, axis=1)


def _in_proj(x2d, g, w_bf16, n_seq, tm, dils):
    m = x2d.shape[0]
    seq = m // n_seq
    tiles = seq // tm
    row = lambda width: pl.BlockSpec((tm, width), lambda b, j: (b * tiles + j, 0))
    res_spec = lambda dil: pl.BlockSpec((1, dil, tm // dil, GROUP_QKV), lambda b, j: (b, 0, j, 0))
    return pl.pallas_call(
        functools.partial(_in_proj_kernel, tm=tm, dils=dils),
        grid=(n_seq, tiles),
        in_specs=[row(D_MODEL), _resident((1, D_MODEL)), _resident((D_MODEL, IN_WIDTH))],
        out_specs=[*[res_spec(d) for d in dils], row(SSM_WIDTH), row(2 * D_MODEL)],
        out_shape=[*[jax.ShapeDtypeStruct((n_seq, d, seq // d, GROUP_QKV), F32) for d in dils],
                   jax.ShapeDtypeStruct((m, SSM_WIDTH), F32),
                   jax.ShapeDtypeStruct((m, 2 * D_MODEL), F32)],
        scratch_shapes=[pltpu.VMEM((D_MODEL // LANES, tm, LANES), F32)],
        compiler_params=_compiler_params("arbitrary", "arbitrary"),
        name="in_proj",
    )(x2d, g, w_bf16)


def _rel_bucket(dist):
    max_exact = N_BUCKETS // 2
    n = jnp.maximum(dist, 0)
    nf = jnp.maximum(n, 1).astype(F32)
    large = max_exact + (jnp.log(nf / max_exact) / math.log(MAX_DISTANCE / max_exact)
                         * (N_BUCKETS - max_exact)).astype(jnp.int32)
    large = jnp.minimum(large, N_BUCKETS - 1)
    return jnp.where(n < max_exact, n, large)


def _masked_bias(tab, strides, valid, dil):
    n_dist = QBLOCK
    bucket = _rel_bucket(jnp.clip(strides, 0, n_dist) * dil)
    tab = tab.astype(F32)
    bias = jnp.zeros((tab.shape[1],) + strides.shape, F32)
    for k in range(N_BUCKETS):
        bias = jnp.where(bucket == k, tab[k].reshape((-1,) + (1,) * strides.ndim), bias)
    return jnp.where(valid, bias, NEG_INF)


def _prompt_bias(tab, dil):
    qi = jnp.arange(QBLOCK)[:, None]
    ki = jnp.arange(QBLOCK)[None, :]
    j_prev = qi + QBLOCK - ki
    j_cur = qi - ki
    bias = jnp.stack([_masked_bias(tab, j_prev, j_prev <= QBLOCK, dil),
                      _masked_bias(tab, j_cur, j_cur >= 0, dil)], axis=0)
    return bias.transpose(0, 2, 1, 3).reshape(2, QBLOCK, HEADS_PER_GROUP * QBLOCK)


def _sample_bias(tab, dil, t_new):
    t = jnp.arange(t_new)[:, None]
    m = jnp.arange(QBLOCK)[None, :]
    if dil == 1:
        j_buf = QBLOCK + t - m
        ok_buf = m >= t
    else:
        j_buf = QBLOCK - m + 0 * t
        ok_buf = jnp.ones_like(j_buf, dtype=bool)
    delta = t - m
    ok_new = (m < t_new) & (delta >= 0) & (delta % dil == 0)
    b_buf = _masked_bias(tab, j_buf, ok_buf, dil)
    b_new = _masked_bias(tab, delta // dil, ok_new, dil)
    return (b_buf.reshape(HEADS_PER_GROUP * t_new, QBLOCK),
            b_new.reshape(HEADS_PER_GROUP * t_new, QBLOCK))


def _attn_prompt_kernel(q_ref, kp_ref, kc_ref, vp_ref, vc_ref, bias_ref, o_ref, lse_ref, *, nq):
    first_tile = pl.program_id(2) == 0
    stacked = (HEADS_PER_GROUP * QBLOCK, GROUP_WIDTH)
    own_head = (lax.broadcasted_iota(jnp.int32, stacked, 0) // QBLOCK
                == lax.broadcasted_iota(jnp.int32, stacked, 1) // HEAD_DIM)
    lane_head = lax.broadcasted_iota(jnp.int32, (QBLOCK, GROUP_WIDTH), 1) // HEAD_DIM

    def per_head(x):
        xb = x.astype(BF16)
        return jnp.where(own_head, jnp.concatenate([xb] * HEADS_PER_GROUP, axis=0), 0)

    def on_head_lanes(cols):
        out = jnp.broadcast_to(cols[-1], (QBLOCK, GROUP_WIDTH))
        for h in range(HEADS_PER_GROUP - 2, -1, -1):
            out = jnp.where(lane_head == h, cols[h], out)
        return out

    k_prev, v_prev = per_head(kp_ref[0, 0]), per_head(vp_ref[0, 0])
    for i in range(nq):
        rows = slice(i * QBLOCK, (i + 1) * QBLOCK)
        q = (q_ref[0, 0, rows, :] * QK_SCALE).astype(BF16)
        k_cur, v_cur = per_head(kc_ref[0, 0, rows, :]), per_head(vc_ref[0, 0, rows, :])
        s_prev = _dot_nt(q, k_prev) + bias_ref[0]
        if i == 0:
            s_prev = jnp.where(first_tile, NEG_INF, s_prev)
        s_cur = _dot_nt(q, k_cur) + bias_ref[1]
        p_prev, p_cur, dens, lses = [], [], [], []
        for h in range(HEADS_PER_GROUP):
            keys = slice(h * QBLOCK, (h + 1) * QBLOCK)
            sp, sc = s_prev[:, keys], s_cur[:, keys]
            m = jnp.max(jnp.maximum(sp, sc), axis=-1, keepdims=True)
            pp, pc = jnp.exp(sp - m), jnp.exp(sc - m)
            den = jnp.sum(pp + pc, axis=-1, keepdims=True)
            p_prev.append(pp.astype(BF16))
            p_cur.append(pc.astype(BF16))
            dens.append(den)
            lses.append(m + jnp.log(den))
        o = _dot(jnp.concatenate(p_prev, axis=1), v_prev) + _dot(jnp.concatenate(p_cur, axis=1), v_cur)
        o_ref[0, 0, rows, :] = o / on_head_lanes(dens)
        lse_ref[0, 0, rows, :] = on_head_lanes(lses)
        k_prev, v_prev = k_cur, v_cur


def _attn_prompt(qkv, bias):
    batch, dil, length, _ = qkv.shape
    tq = min(4 * QBLOCK, length)
    nq = tq // QBLOCK

    def cur(col):
        return pl.BlockSpec((1, 1, tq, GROUP_WIDTH), lambda b, r, n: (b, r, n, col))

    def prev(col):
        return pl.BlockSpec((1, 1, QBLOCK, GROUP_WIDTH),
                            lambda b, r, n: (b, r, jnp.maximum(n * nq - 1, 0), col))

    out_sds = jax.ShapeDtypeStruct((batch, dil, length, GROUP_WIDTH), F32)
    return pl.pallas_call(
        functools.partial(_attn_prompt_kernel, nq=nq),
        grid=(batch, dil, length // tq),
        in_specs=[cur(0), prev(1), cur(1), prev(2), cur(2),
                  _resident((2, QBLOCK, HEADS_PER_GROUP * QBLOCK))],
        out_specs=[cur(0), cur(0)],
        out_shape=[out_sds, out_sds],
        compiler_params=_compiler_params("arbitrary", "arbitrary", "arbitrary"),
        name=f"attn_prompt_d{dil}",
    )(qkv, qkv, qkv, qkv, qkv, bias)


def _attn_sample_kernel(q0_ref, q1_ref, q2_ref, c0_ref, c1_ref, c2_ref, tb_ref, tn_ref,
                        o0_ref, l0_ref, o1_ref, l1_ref, o2_ref, l2_ref, kn_scr, vn_scr, *, t_new):
    n_rows = HEADS_PER_GROUP * t_new
    row_w = lax.broadcasted_iota(jnp.int32, (n_rows, GROUP_WIDTH), 0)
    lane_w = lax.broadcasted_iota(jnp.int32, (n_rows, GROUP_WIDTH), 1)
    own_head = (row_w // t_new) == (lane_w // HEAD_DIM)
    tok_w = row_w % t_new
    tok_k = lax.broadcasted_iota(jnp.int32, (n_rows, QBLOCK), 0) % t_new

    def fold_heads(x):
        x = jnp.where(own_head, x, 0.0)
        out = x[0:t_new]
        for h in range(1, HEADS_PER_GROUP):
            out = out + x[h * t_new:(h + 1) * t_new]
        return out

    caches = (c0_ref, c1_ref, c2_ref)
    outs = ((o0_ref, l0_ref), (o1_ref, l1_ref), (o2_ref, l2_ref))
    kv_width = 2 * GROUP_WIDTH
    for g, (_, dil) in enumerate(DIL_PATTERNS):
        qkv_ref = (q0_ref, q1_ref, q2_ref)[g]
        q = qkv_ref[0, :, 0:GROUP_WIDTH] * QK_SCALE
        q_rows = jnp.where(own_head, jnp.concatenate([q] * HEADS_PER_GROUP, axis=0), 0.0).astype(BF16)
        kn_scr[...] = jnp.zeros_like(kn_scr)
        vn_scr[...] = jnp.zeros_like(vn_scr)
        kn_scr[0:t_new, :] = qkv_ref[0, :, GROUP_WIDTH:2 * GROUP_WIDTH]
        vn_scr[0:t_new, :] = qkv_ref[0, :, 2 * GROUP_WIDTH:3 * GROUP_WIDTH]
        cache = caches[g]
        n_views = 1 if dil == 1 else t_new

        def k_view(t):
            return cache[0, :, t * kv_width:t * kv_width + GROUP_WIDTH].astype(BF16)

        def v_view(t):
            return cache[0, :, t * kv_width + GROUP_WIDTH:(t + 1) * kv_width].astype(BF16)

        if n_views == 1:
            s_buf = _dot_nt(q_rows, k_view(0))
        else:
            s_buf = jnp.zeros((n_rows, QBLOCK), F32)
            for t in range(n_views):
                s_buf = jnp.where(tok_k == t, _dot_nt(q_rows, k_view(t)), s_buf)
        s_buf = s_buf + tb_ref[g]
        s_new = _dot_nt(q_rows, kn_scr[...].astype(BF16)) + tn_ref[g]
        m = jnp.maximum(jnp.max(s_buf, axis=-1, keepdims=True), jnp.max(s_new, axis=-1, keepdims=True))
        p_buf = jnp.exp(s_buf - m)
        p_new = jnp.exp(s_new - m)
        den = jnp.sum(p_buf, axis=-1, keepdims=True) + jnp.sum(p_new, axis=-1, keepdims=True)
        p_buf16 = p_buf.astype(BF16)
        if n_views == 1:
            o = _dot(p_buf16, v_view(0))
        else:
            o = jnp.zeros((n_rows, GROUP_WIDTH), F32)
            for t in range(n_views):
                o = jnp.where(tok_w == t, _dot(p_buf16, v_view(t)), o)
        o = o + _dot(p_new.astype(BF16), vn_scr[...].astype(BF16))
        o_ref, l_ref = outs[g]
        o_ref[0] = fold_heads(o / den)
        l_ref[0] = fold_heads(jnp.broadcast_to(m + jnp.log(den), (n_rows, GROUP_WIDTH)))


def _attn_sample(qkvs, caches, tb, tn):
    batch, t_new, _ = qkvs[0].shape
    n_rows = HEADS_PER_GROUP * t_new
    cache_specs = []
    for (_, dil), c in zip(DIL_PATTERNS, caches):
        width = 2 * GROUP_WIDTH * min(dil, t_new)
        cache_specs.append(pl.BlockSpec((1, QBLOCK, width), lambda b: (b, 0, 0)))
    qkv_spec = pl.BlockSpec((1, t_new, GROUP_QKV), lambda b: (b, 0, 0))
    out_spec = pl.BlockSpec((1, t_new, GROUP_WIDTH), lambda b: (b, 0, 0))
    out_sds = jax.ShapeDtypeStruct((batch, t_new, GROUP_WIDTH), F32)
    return pl.pallas_call(
        functools.partial(_attn_sample_kernel, t_new=t_new),
        grid=(batch,),
        in_specs=[*[qkv_spec] * N_DIL_GROUPS, *cache_specs,
                  _resident((N_DIL_GROUPS, n_rows, QBLOCK)), _resident((N_DIL_GROUPS, n_rows, QBLOCK))],
        out_specs=[out_spec] * (2 * N_DIL_GROUPS),
        out_shape=[out_sds] * (2 * N_DIL_GROUPS),
        scratch_shapes=[pltpu.VMEM((QBLOCK, GROUP_WIDTH), F32), pltpu.VMEM((QBLOCK, GROUP_WIDTH), F32)],
        compiler_params=_compiler_params("arbitrary"),
        name="attn_sample",
    )(*qkvs, *caches, tb, tn)


def _pair_block_diag(a):
    g, r, c = a.shape
    a = a.reshape(g // 2, 2, r, c)
    z = jnp.zeros((g // 2, r, c), a.dtype)
    top = jnp.concatenate([a[:, 0], z], axis=2)
    bot = jnp.concatenate([z, a[:, 1]], axis=2)
    return jnp.concatenate([top, bot], axis=1)


def _ssm_chunk_terms(chunk, log_dt, lam_re, lam_im, b_re, b_im, c_re, c_im):
    hi = lax.Precision.HIGHEST
    dt = jnp.exp(log_dt.astype(F32))[:, None]
    lr, li = lam_re.astype(F32), lam_im.astype(F32)
    mag = jnp.exp(lr * dt)
    ab_re, ab_im = mag * jnp.cos(li * dt), mag * jnp.sin(li * dt)
    den = lr * lr + li * li
    nr, ni = ab_re - 1.0, ab_im
    coef_re = (nr * lr + ni * li) / den
    coef_im = (ni * lr - nr * li) / den
    br, bi = b_re.astype(F32), b_im.astype(F32)
    bb_re = coef_re[..., None] * br - coef_im[..., None] * bi
    bb_im = coef_re[..., None] * bi + coef_im[..., None] * br
    pw_re, pw_im = [jnp.ones_like(ab_re)], [jnp.zeros_like(ab_im)]
    for _ in range(chunk):
        pr, pi = pw_re[-1], pw_im[-1]
        pw_re.append(pr * ab_re - pi * ab_im)
        pw_im.append(pr * ab_im + pi * ab_re)
    pw_re, pw_im = jnp.stack(pw_re), jnp.stack(pw_im)
    akb_re = pw_re[:chunk, :, :, None] * bb_re[None] - pw_im[:chunk, :, :, None] * bb_im[None]
    akb_im = pw_re[:chunk, :, :, None] * bb_im[None] + pw_im[:chunk, :, :, None] * bb_re[None]
    cr, ci = c_re.astype(F32), c_im.astype(F32)
    e_re = cr[None] * pw_re[1:, :, None, :] - ci[None] * pw_im[1:, :, None, :]
    e_im = cr[None] * pw_im[1:, :, None, :] + ci[None] * pw_re[1:, :, None, :]
    kern = (jnp.einsum('gpn,kgnq->kgpq', cr, akb_re, precision=hi)
            - jnp.einsum('gpn,kgnq->kgpq', ci, akb_im, precision=hi))
    return (akb_re, akb_im), (e_re, e_im), kern, (pw_re[chunk], pw_im[chunk])


def _causal_lag_kernels(kern):
    chunk = kern.shape[0]
    lag = (jnp.arange(chunk)[None, :] - jnp.arange(chunk)[:, None])[:, :, None, None, None]
    out = jnp.zeros((chunk, chunk) + kern.shape[1:], F32)
    for k in range(chunk):
        out = jnp.where(lag == k, kern[k][None, None], out)
    return out


def _ssm_pair_operators(chunk, *params):
    (akb_re, akb_im), (e_re, e_im), kern, (a_re, a_im) = _ssm_chunk_terms(chunk, *params)
    _, g, n, p = akb_re.shape
    w_re = akb_re[::-1].transpose(1, 0, 3, 2).reshape(g, chunk * p, n)
    w_im = akb_im[::-1].transpose(1, 0, 3, 2).reshape(g, chunk * p, n)
    et_re = e_re.transpose(1, 3, 0, 2).reshape(g, n, chunk * p)
    et_im = (-e_im).transpose(1, 3, 0, 2).reshape(g, n, chunk * p)
    mt = _causal_lag_kernels(kern).transpose(2, 0, 4, 1, 3).reshape(g, chunk * p, chunk * p)
    ops = [_pair_block_diag(x).astype(BF16) for x in (w_re, w_im, mt, et_re, et_im)]
    return ops, a_re.reshape(1, g * n), a_im.reshape(1, g * n)


SSM_BLOCK_GROUPS = 8
SSM_BLOCK_CH = SSM_BLOCK_GROUPS * SSM_GROUP
SSM_BLOCK_STATE = SSM_BLOCK_GROUPS * SSM_STATE
N_SSM_BLOCKS = N_SSM_GROUPS // SSM_BLOCK_GROUPS


def _ssm_block_operators(chunk, *params):
    (akb_re, akb_im), (e_re, e_im), kern, (a_re, a_im) = _ssm_chunk_terms(chunk, *params)
    _, g, n, p = akb_re.shape
    gb, nb = SSM_BLOCK_GROUPS, N_SSM_BLOCKS

    def spread(x, row_group, outer, inner):
        src = jnp.arange(outer * inner)[:, None]
        dst = jnp.arange(outer * gb * inner)[None, :]
        copy = ((src // inner == dst // (gb * inner)) & (src % inner == dst % inner)).astype(BF16)
        wide = jnp.einsum('nrc,cd->nrd', x.astype(BF16), copy, preferred_element_type=F32)
        own = row_group(jnp.arange(x.shape[1]))[:, None] == (dst // inner) % gb
        return jnp.where(own[None], wide, 0.0).astype(BF16)

    group_of_input_row = lambda r: (r // p) % gb
    group_of_state_row = lambda r: r // n

    def w_op(akb):
        x = akb[::-1].reshape(chunk, nb, gb, n, p).transpose(1, 0, 2, 4, 3)
        return spread(x.reshape(nb, chunk * gb * p, n), group_of_input_row, 1, n)

    def e_op(e):
        x = e.reshape(chunk, nb, gb, p, n).transpose(1, 2, 4, 0, 3)
        return spread(x.reshape(nb, gb * n, chunk * p), group_of_state_row, chunk, p)

    m = _causal_lag_kernels(kern).reshape(chunk, chunk, nb, gb, p, p).transpose(2, 0, 3, 5, 1, 4)
    m = spread(m.reshape(nb, chunk * gb * p, chunk * p), group_of_input_row, chunk, p)
    ops = [w_op(akb_re), w_op(akb_im), m, e_op(e_re), e_op(-e_im)]
    return ops, a_re.reshape(nb, 1, gb * n), a_im.reshape(nb, 1, gb * n)


PAIR_LANES = 2 * SSM_STATE
SSM_X_WIDTH = SSM_CHUNK * SSM_BLOCK_CH


def _ssm_prompt_kernel(u_ref, wre_ref, wim_ref, m_ref, etre_ref, etim_ref, are_ref, aim_ref,
                       y_ref, hre_ref, him_ref, x_scr, sre_scr, sim_scr, *, batch, chunks):
    phase, b = pl.program_id(1), pl.program_id(2)
    for t in range(SSM_CHUNK):
        x_scr[:, t * SSM_BLOCK_CH:(t + 1) * SSM_BLOCK_CH] = (
            u_ref[pl.ds(t, chunks, stride=SSM_CHUNK), :].astype(BF16))
    rows = pl.ds(pl.multiple_of(b * chunks, chunks), chunks)
    state_blocks = SSM_BLOCK_STATE // LANES

    @pl.when(phase == 0)
    def _():
        x = x_scr[...]
        g_re, g_im = _dot(x, wre_ref[0]), _dot(x, wim_ref[0])
        for k in range(state_blocks):
            sre_scr[k, rows, :] = g_re[:, k * LANES:(k + 1) * LANES]
            sim_scr[k, rows, :] = g_im[:, k * LANES:(k + 1) * LANES]

    @pl.when((phase == 1) & (b == 0))
    def _():
        same_lanes = lambda ref, k: ref[0][:, k * LANES:(k + 1) * LANES]

        def step(c, carry):
            same_chunk = pl.ds(c, batch, stride=chunks)
            out = []
            for k, (h_re, h_im) in enumerate(carry):
                a_re, a_im = same_lanes(are_ref, k), same_lanes(aim_ref, k)
                g_re, g_im = sre_scr[k, same_chunk, :], sim_scr[k, same_chunk, :]
                sre_scr[k, same_chunk, :] = h_re
                sim_scr[k, same_chunk, :] = h_im
                out.append((a_re * h_re - a_im * h_im + g_re, a_re * h_im + a_im * h_re + g_im))
            return tuple(out)

        zero = jnp.zeros((batch, LANES), F32)
        final = lax.fori_loop(0, chunks, step, ((zero, zero),) * state_blocks)
        for k, (h_re, h_im) in enumerate(final):
            hre_ref[0, :, k * LANES:(k + 1) * LANES] = h_re
            him_ref[0, :, k * LANES:(k + 1) * LANES] = h_im

    @pl.when(phase == 1)
    def _():
        h_re = jnp.concatenate([sre_scr[k, rows, :] for k in range(state_blocks)], axis=1).astype(BF16)
        h_im = jnp.concatenate([sim_scr[k, rows, :] for k in range(state_blocks)], axis=1).astype(BF16)
        pair_w = 2 * SSM_BLOCK_CH
        for j in range(SSM_CHUNK // 2):
            cols = slice(j * pair_w, (j + 1) * pair_w)
            k_in = (j + 1) * pair_w
            yj = (_dot(x_scr[:, :k_in], m_ref[0, :k_in, cols])
                  + _dot(h_re, etre_ref[0, :, cols]) + _dot(h_im, etim_ref[0, :, cols]))
            for i in range(2):
                y_ref[pl.ds(2 * j + i, chunks, stride=SSM_CHUNK), :] = (
                    yj[:, i * SSM_BLOCK_CH:(i + 1) * SSM_BLOCK_CH])


def _ssm_prompt(u, ops, a_re, a_im, batch, seq):
    chunks = seq // SSM_CHUNK
    op_spec = lambda arr: pl.BlockSpec((1,) + arr.shape[1:], lambda g, ph, b: (g, 0, 0),
                                       pipeline_mode=pl.Buffered(1))
    state_spec = pl.BlockSpec((1, batch, SSM_BLOCK_STATE), lambda g, ph, b: (g, 0, 0))
    state_sds = jax.ShapeDtypeStruct((N_SSM_BLOCKS, batch, SSM_BLOCK_STATE), F32)
    y, h_re, h_im = pl.pallas_call(
        functools.partial(_ssm_prompt_kernel, batch=batch, chunks=chunks),
        grid=(N_SSM_BLOCKS, 2, batch),
        in_specs=[pl.BlockSpec((seq, SSM_BLOCK_CH), lambda g, ph, b: (b, g)),
                  *[op_spec(o) for o in ops], op_spec(a_re), op_spec(a_im)],
        out_specs=[pl.BlockSpec((seq, SSM_BLOCK_CH), lambda g, ph, b: (b * ph, g)), state_spec, state_spec],
        out_shape=[jax.ShapeDtypeStruct((batch * seq, SSM_WIDTH), F32), state_sds, state_sds],
        scratch_shapes=[pltpu.VMEM((chunks, SSM_X_WIDTH), BF16),
                        pltpu.VMEM((SSM_BLOCK_STATE // LANES, batch * chunks, LANES), F32),
                        pltpu.VMEM((SSM_BLOCK_STATE // LANES, batch * chunks, LANES), F32)],
        compiler_params=_compiler_params("arbitrary", "arbitrary", "arbitrary"),
        name="ssm_prompt",
    )(u, *ops, a_re, a_im)

    def by_sequence(h):
        h = h.reshape(N_SSM_BLOCKS, batch, SSM_BLOCK_GROUPS, SSM_STATE).transpose(1, 0, 2, 3)
        return h.reshape(batch, N_SSM_GROUPS, SSM_STATE)

    return y, by_sequence(h_re), by_sequence(h_im)


def _ssm_sample_kernel(x_ref, wre_ref, wim_ref, mt_ref, etre_ref, etim_ref, are_ref, aim_ref,
                       h0re_ref, h0im_ref, y_ref, hre_ref, him_ref):
    for gp in range(N_SSM_PAIRS):
        lanes = slice(gp * PAIR_LANES, (gp + 1) * PAIR_LANES)
        x = x_ref[gp]
        h_re, h_im = h0re_ref[:, lanes], h0im_ref[:, lanes]
        a_re, a_im = are_ref[:, lanes], aim_ref[:, lanes]
        hre_ref[:, lanes] = a_re * h_re - a_im * h_im + _dot(x, wre_ref[gp])
        him_ref[:, lanes] = a_re * h_im + a_im * h_re + _dot(x, wim_ref[gp])
        y_ref[gp] = (_dot(x, mt_ref[gp]) + _dot(h_re.astype(BF16), etre_ref[gp])
                     + _dot(h_im.astype(BF16), etim_ref[gp]))


def _ssm_sample(x, ops, a_re, a_im, h0_re, h0_im):
    batch = x.shape[1]
    args = (x, *ops, a_re, a_im, h0_re, h0_im)
    state_sds = jax.ShapeDtypeStruct((batch, N_SSM_GROUPS * SSM_STATE), F32)
    out_shape = [jax.ShapeDtypeStruct(x.shape, F32), state_sds, state_sds]
    whole = lambda shape: pl.BlockSpec(shape, lambda i, nd=len(shape): (0,) * nd)
    return pl.pallas_call(
        _ssm_sample_kernel,
        grid=(1,),
        in_specs=[whole(a.shape) for a in args],
        out_specs=[whole(s.shape) for s in out_shape],
        out_shape=out_shape,
        compiler_params=_compiler_params("arbitrary"),
        name="ssm_sample",
    )(*args)


def _gelu_tanh(x):
    return 0.5 * x * (1.0 + jnp.tanh(math.sqrt(2.0 / math.pi) * (x + 0.044715 * (x * x * x))))


def _merge_kernel(o0_ref, l0_ref, o1_ref, l1_ref, o2_ref, l2_ref, ys_ref, u_ref, gate_ref, x_ref,
                  d_ref, wglu_ref, bglu_ref, wba_ref, wbs_ref, wout_ref, out_ref, order_scr, *, tm, dils):
    def row_order(ref, dil, slot):
        if dil == 1:
            return ref[0, 0]
        n = tm // dil
        halves = GROUP_WIDTH // LANES
        for r in range(dil):
            for k in range(halves):
                order_scr[slot * halves + k, pl.ds(r, n, stride=dil), :] = ref[0, r, :, k * LANES:(k + 1) * LANES]
        return jnp.concatenate([order_scr[slot * halves + k] for k in range(halves)], axis=1)

    parts = [row_order(ref, dils[i // 2], i) for i, ref in
             enumerate((o0_ref, l0_ref, o1_ref, l1_ref, o2_ref, l2_ref))]
    o0, l0, o1, l1, o2, l2 = parts
    mx = jnp.maximum(jnp.maximum(l0, l1), l2)
    e0, e1, e2 = jnp.exp(l0 - mx), jnp.exp(l1 - mx), jnp.exp(l2 - mx)
    attn = (e0 * o0 + e1 * o1 + e2 * o2) / (e0 + e1 + e2)
    branch_a = _dot(attn.astype(BF16), wba_ref[...])
    y = _gelu_tanh(ys_ref[...] + d_ref[...] * u_ref[...])
    y = y * _sigmoid(_dot(y.astype(BF16), wglu_ref[...]) + bglu_ref[...])
    branch_s = _dot(y.astype(BF16), wbs_ref[...])
    mix = (_sigmoid(gate_ref[:, 0:D_MODEL]) * branch_a
           + _sigmoid(gate_ref[:, D_MODEL:2 * D_MODEL]) * branch_s)
    out_ref[...] = x_ref[...] + _dot(mix.astype(BF16), wout_ref[...])


def _merge(attn_parts, ys, u, gates, x2d, d, w_glu, b_glu, w_ba, w_bs, w_out, n_seq, tm):
    m = x2d.shape[0]
    tiles = m // n_seq // tm
    dils = tuple(p.shape[1] for p in attn_parts[::2])
    row = lambda width: pl.BlockSpec((tm, width), lambda b, j: (b * tiles + j, 0))
    res_spec = lambda dil: pl.BlockSpec((1, dil, tm // dil, GROUP_WIDTH), lambda b, j: (b, 0, j, 0))
    weights = (d, w_glu, b_glu, w_ba, w_bs, w_out)
    return pl.pallas_call(
        functools.partial(_merge_kernel, tm=tm, dils=dils),
        grid=(n_seq, tiles),
        in_specs=[*[res_spec(p.shape[1]) for p in attn_parts],
                  row(SSM_WIDTH), row(SSM_WIDTH), row(2 * D_MODEL), row(D_MODEL),
                  *[_resident(w.shape) for w in weights]],
        out_specs=row(D_MODEL),
        out_shape=jax.ShapeDtypeStruct((m, D_MODEL), F32),
        scratch_shapes=[pltpu.VMEM((len(attn_parts) * GROUP_WIDTH // LANES, tm, LANES), F32)],
        compiler_params=_compiler_params("arbitrary", "arbitrary"),
        name="merge",
    )(*attn_parts, ys, u, gates, x2d, *weights)


FF_CHUNK = D_FF // 2


def _ffn_kernel(x_ref, carry_ref, g2_ref, wup_ref, cw_ref, cb_ref, wdn_ref, gf_ref,
                y_ref, state_ref, a_scr, *, tm, shift, pad):
    hist = 2 * shift

    @pl.when(pl.program_id(1) == 0)
    def _():
        a_scr[pad - hist:pad, :] = carry_ref[0]

    xf = x_ref[...]
    xn = _rmsnorm(xf, g2_ref[...]).astype(BF16)
    acc = jnp.zeros((tm, D_MODEL), F32)
    for c0 in range(0, D_FF, FF_CHUNK):
        cols = slice(c0, c0 + FF_CHUNK)
        a = _dot(xn, wup_ref[:, cols])
        val = _dot(xn, wup_ref[:, D_FF + c0:D_FF + c0 + FF_CHUNK])
        a_scr[pad:pad + tm, cols] = a
        a_m1 = a_scr[pad - shift:pad - shift + tm, cols]
        a_m2 = a_scr[pad - hist:pad - hist + tm, cols]
        conv = cb_ref[:, cols] + cw_ref[0:1, cols] * a_m2
        conv = conv + cw_ref[1:2, cols] * a_m1
        conv = conv + cw_ref[2:3, cols] * a
        act = conv * _sigmoid(conv) * val
        acc = acc + _dot(act.astype(BF16), wdn_ref[cols, :])
    tail = a_scr[pad + tm - hist:pad + tm, :]
    a_scr[pad - hist:pad, :] = tail
    state_ref[0] = tail
    y_ref[...] = _rmsnorm(xf + acc, gf_ref[...])


def _ffn(x2d, carry, g2, w_up, conv_w, conv_b, w_down, gf, n_seq, tm, shift):
    m = x2d.shape[0]
    tiles = m // n_seq // tm
    hist = 2 * shift
    pad = -(-hist // SUBLANES) * SUBLANES
    row_spec = pl.BlockSpec((tm, D_MODEL), lambda b, j: (b * tiles + j, 0))
    state_spec = pl.BlockSpec((1, hist, D_FF), lambda b, j: (b, 0, 0))
    weights = (g2, w_up, conv_w, conv_b, w_down, gf)
    return pl.pallas_call(
        functools.partial(_ffn_kernel, tm=tm, shift=shift, pad=pad),
        grid=(n_seq, tiles),
        in_specs=[row_spec, state_spec, *[_resident(w.shape) for w in weights]],
        out_specs=[row_spec, state_spec],
        out_shape=[jax.ShapeDtypeStruct((m, D_MODEL), F32),
                   jax.ShapeDtypeStruct((n_seq, hist, D_FF), F32)],
        scratch_shapes=[pltpu.VMEM((pad + tm, D_FF), F32)],
        compiler_params=_compiler_params("arbitrary", "arbitrary"),
        name="ffn",
    )(x2d, carry, *weights)


def _kv_rows(qkv, keep):
    batch, dil, length, _ = qkv.shape
    n = keep // dil
    rows = qkv[:, :, length - n:, GROUP_WIDTH:].reshape(batch, dil, n, 2, HEADS_PER_GROUP, HEAD_DIM)
    return rows.transpose(0, 2, 1, 3, 4, 5).reshape(batch, keep, 2, HEADS_PER_GROUP, HEAD_DIM)


def _prompt_layer(x, rel_bias, lw, tm):
    batch, seq, _ = x.shape
    x2d = x.reshape(batch * seq, D_MODEL)
    dils = tuple(dil for _, dil in DIL_PATTERNS)
    *qkvs, u, gates = _in_proj(x2d, lw['norm1_g'], lw['w_in'], batch, tm, dils)

    attn_parts, kv_new = [], []
    for g, (window, dil) in enumerate(DIL_PATTERNS):
        tab = rel_bias[:, g * HEADS_PER_GROUP:(g + 1) * HEADS_PER_GROUP]
        attn_parts.extend(_attn_prompt(qkvs[g], _prompt_bias(tab, dil)))
        kv_new.append(_kv_rows(qkvs[g], min(window, seq)))

    ops, a_re, a_im = _ssm_block_operators(SSM_CHUNK, *lw['ssm'])
    ys, h_re, h_im = _ssm_prompt(u, ops, a_re, a_im, batch, seq)

    x1 = _merge(attn_parts, ys, u, gates, x2d, *lw['merge'], batch, tm)
    carry = jnp.zeros((batch, CONV_W - 1, D_FF), F32)
    y, conv_state = _ffn(x1, carry, *lw['ffn'], n_seq=batch, tm=tm, shift=1)
    return y.reshape(batch, seq, D_MODEL), (*kv_new, h_re, h_im, conv_state)


def _sample_layer(x, caches, h0_re, h0_im, conv_buf, rel_bias, lw):
    batch, t_new, _ = x.shape
    m = batch * t_new
    x2d = x.transpose(1, 0, 2).reshape(m, D_MODEL)
    *qkvs, u, gates = _in_proj(x2d, lw['norm1_g'], lw['w_in'], 1, m, (1,) * N_DIL_GROUPS)
    qkvs_bt = [q.reshape(t_new, batch, GROUP_QKV).transpose(1, 0, 2) for q in qkvs]

    tbs, tns, views = [], [], []
    for g, (window, dil) in enumerate(DIL_PATTERNS):
        tab = rel_bias[:, g * HEADS_PER_GROUP:(g + 1) * HEADS_PER_GROUP]
        tb, tn = _sample_bias(tab, dil, t_new)
        tbs.append(tb)
        tns.append(tn)
        assert caches[g].shape[1] == window == QBLOCK * dil and (dil == 1 or t_new <= dil)
        used = min(dil, t_new)
        view = caches[g].reshape(batch, QBLOCK, dil, 2 * GROUP_WIDTH)[:, :, :used]
        views.append(view.reshape(batch, QBLOCK, used * 2 * GROUP_WIDTH))
    parts = _attn_sample(qkvs_bt, views, jnp.stack(tbs), jnp.stack(tns))
    attn_parts = [p.transpose(1, 0, 2).reshape(1, 1, m, GROUP_WIDTH) for p in parts]
    kv_new = [_kv_rows(q.reshape(batch, 1, t_new, GROUP_QKV), t_new) for q in qkvs_bt]

    ops, a_re, a_im = _ssm_pair_operators(t_new, *lw['ssm'])
    xs = u.reshape(t_new, batch, N_SSM_PAIRS, 2, SSM_GROUP)
    xs = xs.transpose(2, 1, 3, 0, 4).reshape(N_SSM_PAIRS, batch, 2 * t_new * SSM_GROUP)
    ys, h_re, h_im = _ssm_sample(xs.astype(BF16), ops, a_re, a_im,
                                 h0_re.reshape(batch, -1), h0_im.reshape(batch, -1))
    ys = ys.reshape(N_SSM_PAIRS, batch, 2, t_new, SSM_GROUP).transpose(3, 1, 0, 2, 4).reshape(m, SSM_WIDTH)

    x1 = _merge(attn_parts, ys, u, gates, x2d, *lw['merge'], 1, m)
    carry = conv_buf.transpose(1, 0, 2).reshape(1, (CONV_W - 1) * batch, D_FF)
    y, conv_state = _ffn(x1, carry, *lw['ffn'], n_seq=1, tm=m, shift=batch)
    y = y.reshape(t_new, batch, D_MODEL).transpose(1, 0, 2)
    conv_state = conv_state.reshape(CONV_W - 1, batch, D_FF).transpose(1, 0, 2)
    state_shape = (batch, N_SSM_GROUPS, SSM_STATE)
    return y, (*kv_new, h_re.reshape(state_shape), h_im.reshape(state_shape), conv_state)


PROMPT_TILE = 256


def kernel(x_prompt, x_sample, cache_kv_w128, cache_kv_w512, cache_kv_w2048, state_ssm_re, state_ssm_im, state_ffn_conv, rel_bias, norm1_g, w_in, ssm_log_dt, ssm_lambda_re, ssm_lambda_im, ssm_b_re, ssm_b_im, ssm_c_re, ssm_c_im, ssm_d, w_glu, b_glu, w_branch_attn, w_branch_ssm, w_out, norm2_g, w_up, conv_w, conv_b, w_down, norm_f_g):
    depth = w_in.shape[0]
    hp, hs = x_prompt, x_sample
    st_p, st_s = [], []
    gf = norm_f_g.reshape(1, D_MODEL)
    for l in range(depth):
        last = l == depth - 1
        lw = {
            'norm1_g': norm1_g[l].reshape(1, D_MODEL),
            'w_in': _group_major_columns(w_in[l]).astype(BF16),
            'ssm': (ssm_log_dt[l], ssm_lambda_re[l], ssm_lambda_im[l],
                    ssm_b_re[l], ssm_b_im[l], ssm_c_re[l], ssm_c_im[l]),
            'merge': (ssm_d[l].reshape(1, SSM_WIDTH), w_glu[l].astype(BF16), b_glu[l].reshape(1, SSM_WIDTH),
                      w_branch_attn[l].astype(BF16), w_branch_ssm[l].astype(BF16), w_out[l].astype(BF16)),
            'ffn': (norm2_g[l].reshape(1, D_MODEL), w_up[l].astype(BF16), conv_w[l],
                    conv_b[l].reshape(1, D_FF), w_down[l].astype(BF16), gf),
        }
        assert last, "the final RMSNorm is fused into the last layer's ffn kernel"
        hp, sp = _prompt_layer(hp, rel_bias, lw, PROMPT_TILE)
        hs, ss = _sample_layer(hs, (cache_kv_w128[l], cache_kv_w512[l], cache_kv_w2048[l]),
                               state_ssm_re[l], state_ssm_im[l], state_ffn_conv[l], rel_bias, lw)
        st_p.append(sp)
        st_s.append(ss)
    stack = lambda states, i: jnp.stack([st[i] for st in states], axis=0)
    return (hp, hs, *[stack(st_p, i) for i in range(6)], *[stack(st_s, i) for i in range(6)])
```

```python
import functools
import math

import jax
import jax.numpy as jnp
from jax import lax
from jax.experimental import pallas as pl
from jax.experimental.pallas import tpu as pltpu

F32 = jnp.float32
BF16 = jnp.bfloat16

D_MODEL = 1024
HEAD_DIM = 64
HEADS_PER_GROUP = 4
DIL_PATTERNS = ((128, 1), (512, 4), (2048, 16))
N_DIL_GROUPS = len(DIL_PATTERNS)
GROUP_WIDTH = HEADS_PER_GROUP * HEAD_DIM
QK_WIDTH = N_DIL_GROUPS * GROUP_WIDTH
QKV_WIDTH = 3 * QK_WIDTH
GROUP_QKV = 3 * GROUP_WIDTH
QBLOCK = 128
SSM_GROUP = 16
SSM_STATE = 64
SSM_WIDTH = D_MODEL // 2
N_SSM_GROUPS = SSM_WIDTH // SSM_GROUP
SSM_CHUNK = 16
D_FF = 2816
CONV_W = 3
N_BUCKETS = 32
MAX_DISTANCE = 2048
NORM_EPS = 1e-6
NEG_INF = -1e30
U_START = QKV_WIDTH
GATE_START = U_START + SSM_WIDTH
IN_WIDTH = GATE_START + 2 * D_MODEL
QK_SCALE = HEAD_DIM ** -0.5

VMEM_LIMIT_BYTES = 56 * 1024 * 1024
SUBLANES = 8
LANES = 128


def _compiler_params(*semantics):
    return pltpu.CompilerParams(dimension_semantics=semantics, vmem_limit_bytes=VMEM_LIMIT_BYTES)


def _resident(shape):
    nd = len(shape)
    return pl.BlockSpec(shape, lambda *_: (0,) * nd, pipeline_mode=pl.Buffered(1))


def _rmsnorm(xf, g):
    y = xf * lax.rsqrt(jnp.mean(xf * xf, axis=-1, keepdims=True) + NORM_EPS)
    return y * g


def _sigmoid(x):
    return 1.0 / (1.0 + jnp.exp(-x))


def _dot(a, b):
    return jnp.dot(a, b, preferred_element_type=F32)


def _dot_nt(a, b):
    return lax.dot_general(a, b, (((1,), (1,)), ((), ())), preferred_element_type=F32)


def _in_proj_kernel(x_ref, g_ref, w_ref, qkv0_ref, qkv1_ref, qkv2_ref, u_ref, gate_ref, xn_scr, *, tm, dils):
    xf = _rmsnorm(x_ref[...], g_ref[...])
    xn = xf.astype(BF16)
    n_lane_blocks = D_MODEL // LANES
    if any(dil > 1 for dil in dils):
        for k in range(n_lane_blocks):
            xn_scr[k] = xf[:, k * LANES:(k + 1) * LANES]
    for g, (ref, dil) in enumerate(zip((qkv0_ref, qkv1_ref, qkv2_ref), dils)):
        w = w_ref[:, g * GROUP_QKV:(g + 1) * GROUP_QKV]
        if dil == 1:
            ref[0, 0] = _dot(xn, w)
        else:
            n = tm // dil
            xr = jnp.concatenate(
                [jnp.concatenate([xn_scr[k, pl.ds(r, n, stride=dil), :] for k in range(n_lane_blocks)], axis=1)
                 for r in range(dil)], axis=0)
            res = _dot(xr.astype(BF16), w)
            for r in range(dil):
                ref[0, r] = res[r * n:(r + 1) * n]
    u_ref[...] = _dot(xn, w_ref[:, U_START:U_START + SSM_WIDTH])
    for c0 in range(0, 2 * D_MODEL, D_MODEL):
        gate_ref[:, c0:c0 + D_MODEL] = _dot(xn, w_ref[:, GATE_START + c0:GATE_START + c0 + D_MODEL])


def _group_major_columns(w_in):
    parts = []
    for g in range(N_DIL_GROUPS):
        for base in (0, QK_WIDTH, 2 * QK_WIDTH):
            parts.append(w_in[:, base + g * GROUP_WIDTH:base + (g + 1) * GROUP_WIDTH])
    parts.append(w_in[:, U_START:])
    return jnp.concatenate(parts, axis=1)


def _in_proj(x2d, g, w_bf16, n_seq, tm, dils):
    m = x2d.shape[0]
    seq = m // n_seq
    tiles = seq // tm
    row = lambda width: pl.BlockSpec((tm, width), lambda b, j: (b * tiles + j, 0))
    res_spec = lambda dil: pl.BlockSpec((1, dil, tm // dil, GROUP_QKV), lambda b, j: (b, 0, j, 0))
    return pl.pallas_call(
        functools.partial(_in_proj_kernel, tm=tm, dils=dils),
        grid=(n_seq, tiles),
        in_specs=[row(D_MODEL), _resident((1, D_MODEL)), _resident((D_MODEL, IN_WIDTH))],
        out_specs=[*[res_spec(d) for d in dils], row(SSM_WIDTH), row(2 * D_MODEL)],
        out_shape=[*[jax.ShapeDtypeStruct((n_seq, d, seq // d, GROUP_QKV), F32) for d in dils],
                   jax.ShapeDtypeStruct((m, SSM_WIDTH), F32),
                   jax.ShapeDtypeStruct((m, 2 * D_MODEL), F32)],
        scratch_shapes=[pltpu.VMEM((D_MODEL // LANES, tm, LANES), F32)],
        compiler_params=_compiler_params("arbitrary", "arbitrary"),
        name="in_proj",
    )(x2d, g, w_bf16)


def _rel_bucket(dist):
    max_exact = N_BUCKETS // 2
    n = jnp.maximum(dist, 0)
    nf = jnp.maximum(n, 1).astype(F32)
    large = max_exact + (jnp.log(nf / max_exact) / math.log(MAX_DISTANCE / max_exact)
                         * (N_BUCKETS - max_exact)).astype(jnp.int32)
    large = jnp.minimum(large, N_BUCKETS - 1)
    return jnp.where(n < max_exact, n, large)


def _masked_bias(tab, strides, valid, dil):
    n_dist = QBLOCK
    bucket = _rel_bucket(jnp.clip(strides, 0, n_dist) * dil).reshape(1, -1)
    onehot = (bucket == jnp.arange(N_BUCKETS)[:, None]).astype(F32)
    bias = jnp.dot(tab.astype(F32).T, onehot, precision=lax.Precision.HIGHEST)
    bias = jnp.where(valid.reshape(1, -1), bias, NEG_INF)
    return bias.reshape((tab.shape[1],) + strides.shape)


def _prompt_bias(tab, dil):
    qi = jnp.arange(QBLOCK)[:, None]
    ki = jnp.arange(QBLOCK)[None, :]
    j_prev = qi + QBLOCK - ki
    j_cur = qi - ki
    bias = jnp.stack([_masked_bias(tab, j_prev, j_prev <= QBLOCK, dil),
                      _masked_bias(tab, j_cur, j_cur >= 0, dil)], axis=0)
    return bias.transpose(0, 2, 1, 3).reshape(2, QBLOCK, HEADS_PER_GROUP * QBLOCK)


def _sample_bias(tab, dil, t_new, n_cached):
    t = jnp.arange(t_new)[:, None]
    delta_buf = n_cached + t - jnp.arange(n_cached)[None, :]
    ok_buf = (delta_buf % dil == 0) & (delta_buf // dil <= QBLOCK)
    m = jnp.arange(QBLOCK)[None, :]
    delta_new = t - m
    ok_new = (m < t_new) & (delta_new >= 0) & (delta_new % dil == 0)
    b_buf = _masked_bias(tab, delta_buf // dil, ok_buf, dil)
    b_new = _masked_bias(tab, delta_new // dil, ok_new, dil)
    return (b_buf.reshape(HEADS_PER_GROUP * t_new, n_cached),
            b_new.reshape(HEADS_PER_GROUP * t_new, QBLOCK))


def _attn_prompt_kernel(q_ref, kp_ref, kc_ref, vp_ref, vc_ref, bias_ref, o_ref, lse_ref, *, nq):
    first_tile = pl.program_id(2) == 0
    stacked = (HEADS_PER_GROUP * QBLOCK, GROUP_WIDTH)
    own_head = (lax.broadcasted_iota(jnp.int32, stacked, 0) // QBLOCK
                == lax.broadcasted_iota(jnp.int32, stacked, 1) // HEAD_DIM)
    lane_head = lax.broadcasted_iota(jnp.int32, (QBLOCK, GROUP_WIDTH), 1) // HEAD_DIM

    def per_head(x):
        xb = x.astype(BF16)
        return jnp.where(own_head, jnp.concatenate([xb] * HEADS_PER_GROUP, axis=0), 0)

    def on_head_lanes(cols):
        out = jnp.broadcast_to(cols[-1], (QBLOCK, GROUP_WIDTH))
        for h in range(HEADS_PER_GROUP - 2, -1, -1):
            out = jnp.where(lane_head == h, cols[h], out)
        return out

    k_prev, v_prev = per_head(kp_ref[0, 0]), per_head(vp_ref[0, 0])
    for i in range(nq):
        rows = slice(i * QBLOCK, (i + 1) * QBLOCK)
        q = (q_ref[0, 0, rows, :] * QK_SCALE).astype(BF16)
        k_cur, v_cur = per_head(kc_ref[0, 0, rows, :]), per_head(vc_ref[0, 0, rows, :])
        s_prev = _dot_nt(q, k_prev) + bias_ref[0]
        if i == 0:
            s_prev = jnp.where(first_tile, NEG_INF, s_prev)
        s_cur = _dot_nt(q, k_cur) + bias_ref[1]
        p_prev, p_cur, dens, lses = [], [], [], []
        for h in range(HEADS_PER_GROUP):
            keys = slice(h * QBLOCK, (h + 1) * QBLOCK)
            sp, sc = s_prev[:, keys], s_cur[:, keys]
            m = jnp.max(jnp.maximum(sp, sc), axis=-1, keepdims=True)
            pp, pc = jnp.exp(sp - m), jnp.exp(sc - m)
            den = jnp.sum(pp + pc, axis=-1, keepdims=True)
            p_prev.append(pp.astype(BF16))
            p_cur.append(pc.astype(BF16))
            dens.append(den)
            lses.append(m + jnp.log(den))
        o = _dot(jnp.concatenate(p_prev, axis=1), v_prev) + _dot(jnp.concatenate(p_cur, axis=1), v_cur)
        o_ref[0, 0, rows, :] = o / on_head_lanes(dens)
        lse_ref[0, 0, rows, :] = on_head_lanes(lses)
        k_prev, v_prev = k_cur, v_cur


def _attn_prompt(qkv, bias):
    batch, dil, length, _ = qkv.shape
    tq = min(4 * QBLOCK, length)
    nq = tq // QBLOCK

    def cur(col):
        return pl.BlockSpec((1, 1, tq, GROUP_WIDTH), lambda b, r, n: (b, r, n, col))

    def prev(col):
        return pl.BlockSpec((1, 1, QBLOCK, GROUP_WIDTH),
                            lambda b, r, n: (b, r, jnp.maximum(n * nq - 1, 0), col))

    out_sds = jax.ShapeDtypeStruct((batch, dil, length, GROUP_WIDTH), F32)
    return pl.pallas_call(
        functools.partial(_attn_prompt_kernel, nq=nq),
        grid=(batch, dil, length // tq),
        in_specs=[cur(0), prev(1), cur(1), prev(2), cur(2),
                  _resident((2, QBLOCK, HEADS_PER_GROUP * QBLOCK))],
        out_specs=[cur(0), cur(0)],
        out_shape=[out_sds, out_sds],
        compiler_params=_compiler_params("arbitrary", "arbitrary", "arbitrary"),
        name=f"attn_prompt_d{dil}",
    )(qkv, qkv, qkv, qkv, qkv, bias)


def _attn_sample_kernel(q0_ref, q1_ref, q2_ref, c0_ref, c1_ref, c2_ref, tb0_ref, tb1_ref, tb2_ref, tn_ref,
                        o0_ref, l0_ref, o1_ref, l1_ref, o2_ref, l2_ref, kn_scr, vn_scr, *, t_new):
    n_rows = HEADS_PER_GROUP * t_new
    row_w = lax.broadcasted_iota(jnp.int32, (n_rows, GROUP_WIDTH), 0)
    lane_w = lax.broadcasted_iota(jnp.int32, (n_rows, GROUP_WIDTH), 1)
    own_head = (row_w // t_new) == (lane_w // HEAD_DIM)

    def fold_heads(x):
        x = jnp.where(own_head, x, 0.0)
        out = x[0:t_new]
        for h in range(1, HEADS_PER_GROUP):
            out = out + x[h * t_new:(h + 1) * t_new]
        return out

    caches = (c0_ref, c1_ref, c2_ref)
    cache_bias = (tb0_ref, tb1_ref, tb2_ref)
    outs = ((o0_ref, l0_ref), (o1_ref, l1_ref), (o2_ref, l2_ref))
    for g in range(N_DIL_GROUPS):
        qkv_ref = (q0_ref, q1_ref, q2_ref)[g]
        q = qkv_ref[0, :, 0:GROUP_WIDTH] * QK_SCALE
        q_rows = jnp.where(own_head, jnp.concatenate([q] * HEADS_PER_GROUP, axis=0), 0.0).astype(BF16)
        kn_scr[...] = jnp.zeros_like(kn_scr)
        vn_scr[...] = jnp.zeros_like(vn_scr)
        kn_scr[0:t_new, :] = qkv_ref[0, :, GROUP_WIDTH:2 * GROUP_WIDTH]
        vn_scr[0:t_new, :] = qkv_ref[0, :, 2 * GROUP_WIDTH:3 * GROUP_WIDTH]
        cache = caches[g]
        k_t = cache[0, 0:GROUP_WIDTH, :].astype(BF16)
        v_t = cache[0, GROUP_WIDTH:2 * GROUP_WIDTH, :].astype(BF16)
        s_buf = _dot(q_rows, k_t) + cache_bias[g][...]
        s_new = _dot_nt(q_rows, kn_scr[...].astype(BF16)) + tn_ref[g]
        m = jnp.maximum(jnp.max(s_buf, axis=-1, keepdims=True), jnp.max(s_new, axis=-1, keepdims=True))
        p_buf = jnp.exp(s_buf - m)
        p_new = jnp.exp(s_new - m)
        den = jnp.sum(p_buf, axis=-1, keepdims=True) + jnp.sum(p_new, axis=-1, keepdims=True)
        o = _dot_nt(p_buf.astype(BF16), v_t) + _dot(p_new.astype(BF16), vn_scr[...].astype(BF16))
        o_ref, l_ref = outs[g]
        o_ref[0] = fold_heads(o / den)
        l_ref[0] = fold_heads(jnp.broadcast_to(m + jnp.log(den), (n_rows, GROUP_WIDTH)))


def _attn_sample(qkvs, caches, tbs, tn):
    batch, t_new, _ = qkvs[0].shape
    n_rows = HEADS_PER_GROUP * t_new
    cache_specs = [pl.BlockSpec((1,) + c.shape[1:], lambda b: (b, 0, 0)) for c in caches]
    qkv_spec = pl.BlockSpec((1, t_new, GROUP_QKV), lambda b: (b, 0, 0))
    out_spec = pl.BlockSpec((1, t_new, GROUP_WIDTH), lambda b: (b, 0, 0))
    out_sds = jax.ShapeDtypeStruct((batch, t_new, GROUP_WIDTH), F32)
    return pl.pallas_call(
        functools.partial(_attn_sample_kernel, t_new=t_new),
        grid=(batch,),
        in_specs=[*[qkv_spec] * N_DIL_GROUPS, *cache_specs, *[_resident(t.shape) for t in tbs],
                  _resident((N_DIL_GROUPS, n_rows, QBLOCK))],
        out_specs=[out_spec] * (2 * N_DIL_GROUPS),
        out_shape=[out_sds] * (2 * N_DIL_GROUPS),
        scratch_shapes=[pltpu.VMEM((QBLOCK, GROUP_WIDTH), F32), pltpu.VMEM((QBLOCK, GROUP_WIDTH), F32)],
        compiler_params=_compiler_params("arbitrary"),
        name="attn_sample",
    )(*qkvs, *caches, *tbs, tn)


def _ssm_chunk_terms(chunk, log_dt, lam_re, lam_im, b_re, b_im, c_re, c_im):
    hi = lax.Precision.HIGHEST
    dt = jnp.exp(log_dt.astype(F32))[:, None]
    lr, li = lam_re.astype(F32), lam_im.astype(F32)
    mag = jnp.exp(lr * dt)
    ab_re, ab_im = mag * jnp.cos(li * dt), mag * jnp.sin(li * dt)
    den = lr * lr + li * li
    nr, ni = ab_re - 1.0, ab_im
    coef_re = (nr * lr + ni * li) / den
    coef_im = (ni * lr - nr * li) / den
    br, bi = b_re.astype(F32), b_im.astype(F32)
    bb_re = coef_re[..., None] * br - coef_im[..., None] * bi
    bb_im = coef_re[..., None] * bi + coef_im[..., None] * br
    pw_re, pw_im = [jnp.ones_like(ab_re)], [jnp.zeros_like(ab_im)]
    for _ in range(chunk):
        pr, pi = pw_re[-1], pw_im[-1]
        pw_re.append(pr * ab_re - pi * ab_im)
        pw_im.append(pr * ab_im + pi * ab_re)
    pw_re, pw_im = jnp.stack(pw_re), jnp.stack(pw_im)
    bt_re, bt_im = bb_re.transpose(0, 2, 1), bb_im.transpose(0, 2, 1)
    pk_re, pk_im = pw_re[:chunk, :, None, :], pw_im[:chunk, :, None, :]
    akb_re = pk_re * bt_re[None] - pk_im * bt_im[None]
    akb_im = pk_re * bt_im[None] + pk_im * bt_re[None]
    cr, ci = c_re.astype(F32), c_im.astype(F32)
    e_re = cr[None] * pw_re[1:, :, None, :] - ci[None] * pw_im[1:, :, None, :]
    e_im = cr[None] * pw_im[1:, :, None, :] + ci[None] * pw_re[1:, :, None, :]
    kern = (jnp.einsum('kgqn,gpn->kgqp', akb_re, cr, precision=hi)
            - jnp.einsum('kgqn,gpn->kgqp', akb_im, ci, precision=hi))
    return (akb_re, akb_im), (e_re, e_im), kern, (pw_re[chunk], pw_im[chunk])


SSM_BLOCK_GROUPS = 8
SSM_BLOCK_CH = SSM_BLOCK_GROUPS * SSM_GROUP
SSM_BLOCK_STATE = SSM_BLOCK_GROUPS * SSM_STATE
N_SSM_BLOCKS = N_SSM_GROUPS // SSM_BLOCK_GROUPS


def _ssm_block_operators(chunk, *params):
    (akb_re, akb_im), (e_re, e_im), kern, (a_re, a_im) = _ssm_chunk_terms(chunk, *params)
    _, g, p, n = akb_re.shape
    gb, nb = SSM_BLOCK_GROUPS, N_SSM_BLOCKS
    block_ch = gb * p

    def group_diag(x):
        k, _, _, w = x.shape
        x = x.reshape(k, nb, block_ch, w).transpose(1, 0, 2, 3)
        row_g = jnp.arange(block_ch)[:, None] // p
        col_g = jnp.arange(gb * w)[None, :] // w
        return jnp.where(row_g == col_g, jnp.tile(x, (1, 1, 1, gb)), 0.0)

    def w_op(akb):
        return group_diag(akb[::-1]).reshape(nb, chunk * block_ch, gb * n).astype(BF16)

    def e_op(e):
        et = group_diag(e).reshape(nb, chunk * block_ch, gb * n)
        return jnp.swapaxes(et, 1, 2).astype(BF16)

    lag_blocks = group_diag(kern).transpose(0, 2, 1, 3).reshape(nb, block_ch, chunk * block_ch)
    padded = jnp.pad(lag_blocks, ((0, 0), (0, 0), ((chunk - 1) * block_ch, 0)))
    m = jnp.stack([padded[:, :, (chunk - 1 - s) * block_ch:(2 * chunk - 1 - s) * block_ch]
                   for s in range(chunk)], axis=1)
    m = m.reshape(nb, chunk * block_ch, chunk * block_ch).astype(BF16)
    ops = [w_op(akb_re), w_op(akb_im), m, e_op(e_re), e_op(-e_im)]
    return ops, a_re.reshape(nb, 1, gb * n), a_im.reshape(nb, 1, gb * n)


SSM_X_WIDTH = SSM_CHUNK * SSM_BLOCK_CH


def _ssm_prompt_kernel(u_ref, wre_ref, wim_ref, m_ref, etre_ref, etim_ref, are_ref, aim_ref,
                       y_ref, hre_ref, him_ref, x_scr, sre_scr, sim_scr, *, batch, chunks):
    phase, b = pl.program_id(1), pl.program_id(2)
    for t in range(SSM_CHUNK):
        x_scr[:, t * SSM_BLOCK_CH:(t + 1) * SSM_BLOCK_CH] = (
            u_ref[pl.ds(t, chunks, stride=SSM_CHUNK), :].astype(BF16))
    rows = pl.ds(pl.multiple_of(b * chunks, chunks), chunks)
    state_blocks = SSM_BLOCK_STATE // LANES

    @pl.when(phase == 0)
    def _():
        x = x_scr[...]
        g_re, g_im = _dot(x, wre_ref[0]), _dot(x, wim_ref[0])
        for k in range(state_blocks):
            sre_scr[k, rows, :] = g_re[:, k * LANES:(k + 1) * LANES]
            sim_scr[k, rows, :] = g_im[:, k * LANES:(k + 1) * LANES]

    @pl.when((phase == 1) & (b == 0))
    def _():
        same_lanes = lambda ref, k: ref[0][:, k * LANES:(k + 1) * LANES]

        def step(c, carry):
            same_chunk = pl.ds(c, batch, stride=chunks)
            out = []
            for k, (h_re, h_im) in enumerate(carry):
                a_re, a_im = same_lanes(are_ref, k), same_lanes(aim_ref, k)
                g_re, g_im = sre_scr[k, same_chunk, :], sim_scr[k, same_chunk, :]
                sre_scr[k, same_chunk, :] = h_re
                sim_scr[k, same_chunk, :] = h_im
                out.append((a_re * h_re - a_im * h_im + g_re, a_re * h_im + a_im * h_re + g_im))
            return tuple(out)

        zero = jnp.zeros((batch, LANES), F32)
        final = lax.fori_loop(0, chunks, step, ((zero, zero),) * state_blocks)
        for k, (h_re, h_im) in enumerate(final):
            hre_ref[0, :, k * LANES:(k + 1) * LANES] = h_re
            him_ref[0, :, k * LANES:(k + 1) * LANES] = h_im

    @pl.when(phase == 1)
    def _():
        h_re = jnp.concatenate([sre_scr[k, rows, :] for k in range(state_blocks)], axis=1).astype(BF16)
        h_im = jnp.concatenate([sim_scr[k, rows, :] for k in range(state_blocks)], axis=1).astype(BF16)
        pair_w = 2 * SSM_BLOCK_CH
        for j in range(SSM_CHUNK // 2):
            cols = slice(j * pair_w, (j + 1) * pair_w)
            k_in = (j + 1) * pair_w
            yj = (_dot(x_scr[:, :k_in], m_ref[0, :k_in, cols])
                  + _dot(h_re, etre_ref[0, :, cols]) + _dot(h_im, etim_ref[0, :, cols]))
            for i in range(2):
                y_ref[pl.ds(2 * j + i, chunks, stride=SSM_CHUNK), :] = (
                    yj[:, i * SSM_BLOCK_CH:(i + 1) * SSM_BLOCK_CH])


def _ssm_prompt(u, ops, a_re, a_im, batch, seq):
    chunks = seq // SSM_CHUNK
    op_spec = lambda arr: pl.BlockSpec((1,) + arr.shape[1:], lambda g, ph, b: (g, 0, 0),
                                       pipeline_mode=pl.Buffered(1))
    state_spec = pl.BlockSpec((1, batch, SSM_BLOCK_STATE), lambda g, ph, b: (g, 0, 0))
    state_sds = jax.ShapeDtypeStruct((N_SSM_BLOCKS, batch, SSM_BLOCK_STATE), F32)
    y, h_re, h_im = pl.pallas_call(
        functools.partial(_ssm_prompt_kernel, batch=batch, chunks=chunks),
        grid=(N_SSM_BLOCKS, 2, batch),
        in_specs=[pl.BlockSpec((seq, SSM_BLOCK_CH), lambda g, ph, b: (b, g)),
                  *[op_spec(o) for o in ops], op_spec(a_re), op_spec(a_im)],
        out_specs=[pl.BlockSpec((seq, SSM_BLOCK_CH), lambda g, ph, b: (b * ph, g)), state_spec, state_spec],
        out_shape=[jax.ShapeDtypeStruct((batch * seq, SSM_WIDTH), F32), state_sds, state_sds],
        scratch_shapes=[pltpu.VMEM((chunks, SSM_X_WIDTH), BF16),
                        pltpu.VMEM((SSM_BLOCK_STATE // LANES, batch * chunks, LANES), F32),
                        pltpu.VMEM((SSM_BLOCK_STATE // LANES, batch * chunks, LANES), F32)],
        compiler_params=_compiler_params("arbitrary", "arbitrary", "arbitrary"),
        name="ssm_prompt",
    )(u, *ops, a_re, a_im)

    def by_sequence(h):
        h = h.reshape(N_SSM_BLOCKS, batch, SSM_BLOCK_GROUPS, SSM_STATE).transpose(1, 0, 2, 3)
        return h.reshape(batch, N_SSM_GROUPS, SSM_STATE)

    return y, by_sequence(h_re), by_sequence(h_im)


def _ssm_sample_kernel(u_ref, wre_ref, wim_ref, m_ref, etre_ref, etim_ref, are_ref, aim_ref,
                       h0re_ref, h0im_ref, y_ref, hre_ref, him_ref, *, t_new, batch):
    for nb in range(N_SSM_BLOCKS):
        ch = slice(nb * SSM_BLOCK_CH, (nb + 1) * SSM_BLOCK_CH)
        st = slice(nb * SSM_BLOCK_STATE, (nb + 1) * SSM_BLOCK_STATE)
        x = jnp.concatenate([u_ref[t * batch:(t + 1) * batch, ch] for t in range(t_new)], axis=1).astype(BF16)
        h_re, h_im = h0re_ref[:, st], h0im_ref[:, st]
        a_re, a_im = are_ref[nb], aim_ref[nb]
        hre_ref[:, st] = a_re * h_re - a_im * h_im + _dot(x, wre_ref[nb])
        him_ref[:, st] = a_re * h_im + a_im * h_re + _dot(x, wim_ref[nb])
        y = (_dot(x, m_ref[nb]) + _dot(h_re.astype(BF16), etre_ref[nb])
             + _dot(h_im.astype(BF16), etim_ref[nb]))
        for t in range(t_new):
            y_ref[t * batch:(t + 1) * batch, ch] = y[:, t * SSM_BLOCK_CH:(t + 1) * SSM_BLOCK_CH]


def _ssm_sample(u, ops, a_re, a_im, h0_re, h0_im, t_new):
    batch = u.shape[0] // t_new
    args = (u, *ops, a_re, a_im, h0_re, h0_im)
    state_sds = jax.ShapeDtypeStruct((batch, N_SSM_GROUPS * SSM_STATE), F32)
    out_shape = [jax.ShapeDtypeStruct(u.shape, F32), state_sds, state_sds]
    whole = lambda shape: pl.BlockSpec(shape, lambda i, nd=len(shape): (0,) * nd)
    return pl.pallas_call(
        functools.partial(_ssm_sample_kernel, t_new=t_new, batch=batch),
        grid=(1,),
        in_specs=[whole(a.shape) for a in args],
        out_specs=[whole(s.shape) for s in out_shape],
        out_shape=out_shape,
        compiler_params=_compiler_params("arbitrary"),
        name="ssm_sample",
    )(*args)


def _gelu_tanh(x):
    return 0.5 * x * (1.0 + jnp.tanh(math.sqrt(2.0 / math.pi) * (x + 0.044715 * (x * x * x))))


def _merge_kernel(o0_ref, l0_ref, o1_ref, l1_ref, o2_ref, l2_ref, ys_ref, u_ref, gate_ref, x_ref,
                  d_ref, wglu_ref, bglu_ref, wba_ref, wbs_ref, wout_ref, out_ref, order_scr, *, tm, dils):
    def row_order(ref, dil, slot):
        if dil == 1:
            return ref[0, 0]
        n = tm // dil
        halves = GROUP_WIDTH // LANES
        for r in range(dil):
            for k in range(halves):
                order_scr[slot * halves + k, pl.ds(r, n, stride=dil), :] = ref[0, r, :, k * LANES:(k + 1) * LANES]
        return jnp.concatenate([order_scr[slot * halves + k] for k in range(halves)], axis=1)

    parts = [row_order(ref, dils[i // 2], i) for i, ref in
             enumerate((o0_ref, l0_ref, o1_ref, l1_ref, o2_ref, l2_ref))]
    o0, l0, o1, l1, o2, l2 = parts
    mx = jnp.maximum(jnp.maximum(l0, l1), l2)
    e0, e1, e2 = jnp.exp(l0 - mx), jnp.exp(l1 - mx), jnp.exp(l2 - mx)
    attn = (e0 * o0 + e1 * o1 + e2 * o2) / (e0 + e1 + e2)
    branch_a = _dot(attn.astype(BF16), wba_ref[...])
    y = _gelu_tanh(ys_ref[...] + d_ref[...] * u_ref[...])
    y = y * _sigmoid(_dot(y.astype(BF16), wglu_ref[...]) + bglu_ref[...])
    branch_s = _dot(y.astype(BF16), wbs_ref[...])
    mix = (_sigmoid(gate_ref[:, 0:D_MODEL]) * branch_a
           + _sigmoid(gate_ref[:, D_MODEL:2 * D_MODEL]) * branch_s)
    out_ref[...] = x_ref[...] + _dot(mix.astype(BF16), wout_ref[...])


def _merge(attn_parts, ys, u, gates, x2d, d, w_glu, b_glu, w_ba, w_bs, w_out, n_seq, tm):
    m = x2d.shape[0]
    tiles = m // n_seq // tm
    dils = tuple(p.shape[1] for p in attn_parts[::2])
    row = lambda width: pl.BlockSpec((tm, width), lambda b, j: (b * tiles + j, 0))
    res_spec = lambda dil: pl.BlockSpec((1, dil, tm // dil, GROUP_WIDTH), lambda b, j: (b, 0, j, 0))
    weights = (d, w_glu, b_glu, w_ba, w_bs, w_out)
    return pl.pallas_call(
        functools.partial(_merge_kernel, tm=tm, dils=dils),
        grid=(n_seq, tiles),
        in_specs=[*[res_spec(p.shape[1]) for p in attn_parts],
                  row(SSM_WIDTH), row(SSM_WIDTH), row(2 * D_MODEL), row(D_MODEL),
                  *[_resident(w.shape) for w in weights]],
        out_specs=row(D_MODEL),
        out_shape=jax.ShapeDtypeStruct((m, D_MODEL), F32),
        scratch_shapes=[pltpu.VMEM((len(attn_parts) * GROUP_WIDTH // LANES, tm, LANES), F32)],
        compiler_params=_compiler_params("arbitrary", "arbitrary"),
        name="merge",
    )(*attn_parts, ys, u, gates, x2d, *weights)


FF_CHUNK = D_FF // 2


def _ffn_kernel(x_ref, carry_ref, g2_ref, wup_ref, cw_ref, cb_ref, wdn_ref, gf_ref,
                y_ref, state_ref, a_scr, *, tm, shift, pad):
    hist = 2 * shift

    @pl.when(pl.program_id(1) == 0)
    def _():
        a_scr[pad - hist:pad, :] = carry_ref[0]

    xf = x_ref[...]
    xn = _rmsnorm(xf, g2_ref[...]).astype(BF16)
    acc = jnp.zeros((tm, D_MODEL), F32)
    for c0 in range(0, D_FF, FF_CHUNK):
        cols = slice(c0, c0 + FF_CHUNK)
        a = _dot(xn, wup_ref[:, cols])
        val = _dot(xn, wup_ref[:, D_FF + c0:D_FF + c0 + FF_CHUNK])
        a_scr[pad:pad + tm, cols] = a
        a_m1 = a_scr[pad - shift:pad - shift + tm, cols]
        a_m2 = a_scr[pad - hist:pad - hist + tm, cols]
        conv = cb_ref[:, cols] + cw_ref[0:1, cols] * a_m2
        conv = conv + cw_ref[1:2, cols] * a_m1
        conv = conv + cw_ref[2:3, cols] * a
        act = conv * _sigmoid(conv) * val
        acc = acc + _dot(act.astype(BF16), wdn_ref[cols, :])
    tail = a_scr[pad + tm - hist:pad + tm, :]
    a_scr[pad - hist:pad, :] = tail
    state_ref[0] = tail
    y_ref[...] = _rmsnorm(xf + acc, gf_ref[...])


def _ffn(x2d, carry, g2, w_up, conv_w, conv_b, w_down, gf, n_seq, tm, shift):
    m = x2d.shape[0]
    tiles = m // n_seq // tm
    hist = 2 * shift
    pad = -(-hist // SUBLANES) * SUBLANES
    row_spec = pl.BlockSpec((tm, D_MODEL), lambda b, j: (b * tiles + j, 0))
    state_spec = pl.BlockSpec((1, hist, D_FF), lambda b, j: (b, 0, 0))
    weights = (g2, w_up, conv_w, conv_b, w_down, gf)
    return pl.pallas_call(
        functools.partial(_ffn_kernel, tm=tm, shift=shift, pad=pad),
        grid=(n_seq, tiles),
        in_specs=[row_spec, state_spec, *[_resident(w.shape) for w in weights]],
        out_specs=[row_spec, state_spec],
        out_shape=[jax.ShapeDtypeStruct((m, D_MODEL), F32),
                   jax.ShapeDtypeStruct((n_seq, hist, D_FF), F32)],
        scratch_shapes=[pltpu.VMEM((pad + tm, D_FF), F32)],
        compiler_params=_compiler_params("arbitrary", "arbitrary"),
        name="ffn",
    )(x2d, carry, *weights)


def _kv_rows(qkv, keep):
    batch, dil, length, _ = qkv.shape
    n = keep // dil
    rows = qkv[:, :, length - n:, GROUP_WIDTH:]
    cols = rows.transpose(0, 3, 2, 1).reshape(batch, 2, HEADS_PER_GROUP, HEAD_DIM, keep)
    return cols.transpose(0, 4, 1, 2, 3)


def _prompt_layer(x, rel_bias, lw, tm):
    batch, seq, _ = x.shape
    x2d = x.reshape(batch * seq, D_MODEL)
    dils = tuple(dil for _, dil in DIL_PATTERNS)
    *qkvs, u, gates = _in_proj(x2d, lw['norm1_g'], lw['w_in'], batch, tm, dils)

    attn_parts, kv_new = [], []
    for g, (window, dil) in enumerate(DIL_PATTERNS):
        tab = rel_bias[:, g * HEADS_PER_GROUP:(g + 1) * HEADS_PER_GROUP]
        attn_parts.extend(_attn_prompt(qkvs[g], _prompt_bias(tab, dil)))
        kv_new.append(_kv_rows(qkvs[g], min(window, seq)))

    ops, a_re, a_im = _ssm_block_operators(SSM_CHUNK, *lw['ssm'])
    ys, h_re, h_im = _ssm_prompt(u, ops, a_re, a_im, batch, seq)

    x1 = _merge(attn_parts, ys, u, gates, x2d, *lw['merge'], batch, tm)
    carry = jnp.zeros((batch, CONV_W - 1, D_FF), F32)
    y, conv_state = _ffn(x1, carry, *lw['ffn'], n_seq=batch, tm=tm, shift=1)
    return y.reshape(batch, seq, D_MODEL), (*kv_new, h_re, h_im, conv_state)


def _sample_layer(x, caches, h0_re, h0_im, conv_buf, rel_bias, lw):
    batch, t_new, _ = x.shape
    m = batch * t_new
    x2d = x.transpose(1, 0, 2).reshape(m, D_MODEL)
    *qkvs, u, gates = _in_proj(x2d, lw['norm1_g'], lw['w_in'], 1, m, (1,) * N_DIL_GROUPS)
    qkvs_bt = [q.reshape(t_new, batch, GROUP_QKV).transpose(1, 0, 2) for q in qkvs]

    tbs, tns, views = [], [], []
    for g, (window, dil) in enumerate(DIL_PATTERNS):
        tab = rel_bias[:, g * HEADS_PER_GROUP:(g + 1) * HEADS_PER_GROUP]
        n_cached = caches[g].shape[1]
        tb, tn = _sample_bias(tab, dil, t_new, n_cached)
        tbs.append(tb)
        tns.append(tn)
        views.append(caches[g].transpose(0, 2, 3, 4, 1).reshape(batch, 2 * GROUP_WIDTH, n_cached))
    parts = _attn_sample(qkvs_bt, views, tbs, jnp.stack(tns))
    attn_parts = [p.transpose(1, 0, 2).reshape(1, 1, m, GROUP_WIDTH) for p in parts]
    kv_new = [_kv_rows(q.reshape(batch, 1, t_new, GROUP_QKV), t_new) for q in qkvs_bt]

    ops, a_re, a_im = _ssm_block_operators(t_new, *lw['ssm'])
    ys, h_re, h_im = _ssm_sample(u, ops, a_re, a_im, h0_re.reshape(batch, -1), h0_im.reshape(batch, -1), t_new)

    x1 = _merge(attn_parts, ys, u, gates, x2d, *lw['merge'], 1, m)
    carry = conv_buf.transpose(1, 0, 2).reshape(1, (CONV_W - 1) * batch, D_FF)
    y, conv_state = _ffn(x1, carry, *lw['ffn'], n_seq=1, tm=m, shift=batch)
    y = y.reshape(t_new, batch, D_MODEL).transpose(1, 0, 2)
    conv_state = conv_state.reshape(CONV_W - 1, batch, D_FF).transpose(1, 0, 2)
    state_shape = (batch, N_SSM_GROUPS, SSM_STATE)
    return y, (*kv_new, h_re.reshape(state_shape), h_im.reshape(state_shape), conv_state)


PROMPT_TILE = 256


def kernel(x_prompt, x_sample, cache_kv_w128, cache_kv_w512, cache_kv_w2048, state_ssm_re, state_ssm_im, state_ffn_conv, rel_bias, norm1_g, w_in, ssm_log_dt, ssm_lambda_re, ssm_lambda_im, ssm_b_re, ssm_b_im, ssm_c_re, ssm_c_im, ssm_d, w_glu, b_glu, w_branch_attn, w_branch_ssm, w_out, norm2_g, w_up, conv_w, conv_b, w_down, norm_f_g):
    depth = w_in.shape[0]
    hp, hs = x_prompt, x_sample
    st_p, st_s = [], []
    gf = norm_f_g.reshape(1, D_MODEL)
    for l in range(depth):
        last = l == depth - 1
        lw = {
            'norm1_g': norm1_g[l].reshape(1, D_MODEL),
            'w_in': _group_major_columns(w_in[l]).astype(BF16),
            'ssm': (ssm_log_dt[l], ssm_lambda_re[l], ssm_lambda_im[l],
                    ssm_b_re[l], ssm_b_im[l], ssm_c_re[l], ssm_c_im[l]),
            'merge': (ssm_d[l].reshape(1, SSM_WIDTH), w_glu[l].astype(BF16), b_glu[l].reshape(1, SSM_WIDTH),
                      w_branch_attn[l].astype(BF16), w_branch_ssm[l].astype(BF16), w_out[l].astype(BF16)),
            'ffn': (norm2_g[l].reshape(1, D_MODEL), w_up[l].astype(BF16), conv_w[l],
                    conv_b[l].reshape(1, D_FF), w_down[l].astype(BF16), gf),
        }
        assert last, "the final RMSNorm is fused into the last layer's ffn kernel"
        hp, sp = _prompt_layer(hp, rel_bias, lw, PROMPT_TILE)
        hs, ss = _sample_layer(hs, (cache_kv_w128[l], cache_kv_w512[l], cache_kv_w2048[l]),
                               state_ssm_re[l], state_ssm_im[l], state_ffn_conv[l], rel_bias, lw)
        st_p.append(sp)
        st_s.append(ss)
    stack = lambda states, i: jnp.stack([st[i] for st in states], axis=0)
    return (hp, hs, *[stack(st_p, i) for i in range(6)], *[stack(st_s, i) for i in range(6)])
```

```python
import functools
import math

import jax
import jax.numpy as jnp
from jax import lax
from jax.experimental import pallas as pl
from jax.experimental.pallas import tpu as pltpu

F32 = jnp.float32
BF16 = jnp.bfloat16

D_MODEL = 1024
HEAD_DIM = 64
HEADS_PER_GROUP = 4
DIL_PATTERNS = ((128, 1), (512, 4), (2048, 16))
N_DIL_GROUPS = len(DIL_PATTERNS)
GROUP_WIDTH = HEADS_PER_GROUP * HEAD_DIM
QK_WIDTH = N_DIL_GROUPS * GROUP_WIDTH
QKV_WIDTH = 3 * QK_WIDTH
GROUP_QKV = 3 * GROUP_WIDTH
QBLOCK = 128
SSM_GROUP = 16
SSM_STATE = 64
SSM_WIDTH = D_MODEL // 2
N_SSM_GROUPS = SSM_WIDTH // SSM_GROUP
SSM_CHUNK = 16
D_FF = 2816
CONV_W = 3
N_BUCKETS = 32
MAX_DISTANCE = 2048
NORM_EPS = 1e-6
NEG_INF = -1e30
U_START = QKV_WIDTH
GATE_START = U_START + SSM_WIDTH
IN_WIDTH = GATE_START + 2 * D_MODEL
QK_SCALE = HEAD_DIM ** -0.5

VMEM_LIMIT_BYTES = 56 * 1024 * 1024
SUBLANES = 8
LANES = 128


def _compiler_params(*semantics):
    return pltpu.CompilerParams(dimension_semantics=semantics, vmem_limit_bytes=VMEM_LIMIT_BYTES)


def _resident(shape):
    nd = len(shape)
    return pl.BlockSpec(shape, lambda *_: (0,) * nd, pipeline_mode=pl.Buffered(1))


def _rmsnorm(xf, g):
    y = xf * lax.rsqrt(jnp.mean(xf * xf, axis=-1, keepdims=True) + NORM_EPS)
    return y * g


def _sigmoid(x):
    return 1.0 / (1.0 + jnp.exp(-x))


def _dot(a, b):
    return jnp.dot(a, b, preferred_element_type=F32)


def _dot_nt(a, b):
    return lax.dot_general(a, b, (((1,), (1,)), ((), ())), preferred_element_type=F32)


def _in_proj_kernel(x_ref, g_ref, w_ref, qkv0_ref, qkv1_ref, qkv2_ref, u_ref, gate_ref, xn_scr, *, tm, dils):
    xf = _rmsnorm(x_ref[...], g_ref[...])
    xn = xf.astype(BF16)
    n_lane_blocks = D_MODEL // LANES
    if any(dil > 1 for dil in dils):
        for k in range(n_lane_blocks):
            xn_scr[k] = xf[:, k * LANES:(k + 1) * LANES]
    for g, (ref, dil) in enumerate(zip((qkv0_ref, qkv1_ref, qkv2_ref), dils)):
        w = w_ref[:, g * GROUP_QKV:(g + 1) * GROUP_QKV]
        if dil == 1:
            ref[0, 0] = _dot(xn, w)
        else:
            n = tm // dil
            xr = jnp.concatenate(
                [jnp.concatenate([xn_scr[k, pl.ds(r, n, stride=dil), :] for k in range(n_lane_blocks)], axis=1)
                 for r in range(dil)], axis=0)
            res = _dot(xr.astype(BF16), w)
            for r in range(dil):
                ref[0, r] = res[r * n:(r + 1) * n]
    u_ref[...] = _dot(xn, w_ref[:, U_START:U_START + SSM_WIDTH])
    for c0 in range(0, 2 * D_MODEL, D_MODEL):
        gate_ref[:, c0:c0 + D_MODEL] = _dot(xn, w_ref[:, GATE_START + c0:GATE_START + c0 + D_MODEL])


def _group_major_columns(w_in):
    parts = []
    for g in range(N_DIL_GROUPS):
        for base in (0, QK_WIDTH, 2 * QK_WIDTH):
            parts.append(w_in[:, base + g * GROUP_WIDTH:base + (g + 1) * GROUP_WIDTH])
    parts.append(w_in[:, U_START:])
    return jnp.concatenate(parts, axis=1)


def _in_proj(x2d, g, w_bf16, n_seq, tm, dils):
    m = x2d.shape[0]
    seq = m // n_seq
    tiles = seq // tm
    row = lambda width: pl.BlockSpec((tm, width), lambda b, j: (b * tiles + j, 0))
    res_spec = lambda dil: pl.BlockSpec((1, dil, tm // dil, GROUP_QKV), lambda b, j: (b, 0, j, 0))
    return pl.pallas_call(
        functools.partial(_in_proj_kernel, tm=tm, dils=dils),
        grid=(n_seq, tiles),
        in_specs=[row(D_MODEL), _resident((1, D_MODEL)), _resident((D_MODEL, IN_WIDTH))],
        out_specs=[*[res_spec(d) for d in dils], row(SSM_WIDTH), row(2 * D_MODEL)],
        out_shape=[*[jax.ShapeDtypeStruct((n_seq, d, seq // d, GROUP_QKV), F32) for d in dils],
                   jax.ShapeDtypeStruct((m, SSM_WIDTH), F32),
                   jax.ShapeDtypeStruct((m, 2 * D_MODEL), F32)],
        scratch_shapes=[pltpu.VMEM((D_MODEL // LANES, tm, LANES), F32)],
        compiler_params=_compiler_params("arbitrary", "arbitrary"),
        name="in_proj",
    )(x2d, g, w_bf16)


def _rel_bucket(dist):
    max_exact = N_BUCKETS // 2
    n = jnp.maximum(dist, 0)
    nf = jnp.maximum(n, 1).astype(F32)
    large = max_exact + (jnp.log(nf / max_exact) / math.log(MAX_DISTANCE / max_exact)
                         * (N_BUCKETS - max_exact)).astype(jnp.int32)
    large = jnp.minimum(large, N_BUCKETS - 1)
    return jnp.where(n < max_exact, n, large)


def _masked_bias(tab, strides, valid, dil):
    n_dist = QBLOCK
    bucket = _rel_bucket(jnp.clip(strides, 0, n_dist) * dil).reshape(1, -1)
    onehot = (bucket == jnp.arange(N_BUCKETS)[:, None]).astype(F32)
    bias = jnp.dot(tab.astype(F32).T, onehot, precision=lax.Precision.HIGHEST)
    bias = jnp.where(valid.reshape(1, -1), bias, NEG_INF)
    return bias.reshape((tab.shape[1],) + strides.shape)


def _prompt_bias(tab, dil):
    qi = jnp.arange(QBLOCK)[:, None]
    ki = jnp.arange(QBLOCK)[None, :]
    j_prev = qi + QBLOCK - ki
    j_cur = qi - ki
    bias = jnp.stack([_masked_bias(tab, j_prev, j_prev <= QBLOCK, dil),
                      _masked_bias(tab, j_cur, j_cur >= 0, dil)], axis=0)
    return bias.transpose(0, 2, 1, 3).reshape(2, QBLOCK, HEADS_PER_GROUP * QBLOCK)


def _sample_bias(tab, dil, t_new, n_cached):
    t = jnp.arange(t_new)[:, None]
    delta_buf = n_cached + t - jnp.arange(n_cached)[None, :]
    ok_buf = (delta_buf % dil == 0) & (delta_buf // dil <= QBLOCK)
    m = jnp.arange(QBLOCK)[None, :]
    delta_new = t - m
    ok_new = (m < t_new) & (delta_new >= 0) & (delta_new % dil == 0)
    b_buf = _masked_bias(tab, delta_buf // dil, ok_buf, dil)
    b_new = _masked_bias(tab, delta_new // dil, ok_new, dil)
    return (b_buf.reshape(HEADS_PER_GROUP * t_new, n_cached),
            b_new.reshape(HEADS_PER_GROUP * t_new, QBLOCK))


def _attn_prompt_kernel(q_ref, kp_ref, kc_ref, vp_ref, vc_ref, bias_ref, o_ref, lse_ref, *, nq):
    first_tile = pl.program_id(2) == 0
    stacked = (HEADS_PER_GROUP * QBLOCK, GROUP_WIDTH)
    own_head = (lax.broadcasted_iota(jnp.int32, stacked, 0) // QBLOCK
                == lax.broadcasted_iota(jnp.int32, stacked, 1) // HEAD_DIM)
    lane_head = lax.broadcasted_iota(jnp.int32, (QBLOCK, GROUP_WIDTH), 1) // HEAD_DIM

    def per_head(x):
        xb = x.astype(BF16)
        return jnp.where(own_head, jnp.concatenate([xb] * HEADS_PER_GROUP, axis=0), 0)

    def on_head_lanes(cols):
        out = jnp.broadcast_to(cols[-1], (QBLOCK, GROUP_WIDTH))
        for h in range(HEADS_PER_GROUP - 2, -1, -1):
            out = jnp.where(lane_head == h, cols[h], out)
        return out

    k_prev, v_prev = per_head(kp_ref[0, 0]), per_head(vp_ref[0, 0])
    for i in range(nq):
        rows = slice(i * QBLOCK, (i + 1) * QBLOCK)
        q = (q_ref[0, 0, rows, :] * QK_SCALE).astype(BF16)
        k_cur, v_cur = per_head(kc_ref[0, 0, rows, :]), per_head(vc_ref[0, 0, rows, :])
        s_prev = _dot_nt(q, k_prev) + bias_ref[0]
        if i == 0:
            s_prev = jnp.where(first_tile, NEG_INF, s_prev)
        s_cur = _dot_nt(q, k_cur) + bias_ref[1]
        p_prev, p_cur, dens, lses = [], [], [], []
        for h in range(HEADS_PER_GROUP):
            keys = slice(h * QBLOCK, (h + 1) * QBLOCK)
            sp, sc = s_prev[:, keys], s_cur[:, keys]
            m = jnp.max(jnp.maximum(sp, sc), axis=-1, keepdims=True)
            pp, pc = jnp.exp(sp - m), jnp.exp(sc - m)
            den = jnp.sum(pp + pc, axis=-1, keepdims=True)
            p_prev.append(pp.astype(BF16))
            p_cur.append(pc.astype(BF16))
            dens.append(den)
            lses.append(m + jnp.log(den))
        o = _dot(jnp.concatenate(p_prev, axis=1), v_prev) + _dot(jnp.concatenate(p_cur, axis=1), v_cur)
        o_ref[0, 0, rows, :] = o / on_head_lanes(dens)
        lse_ref[0, 0, rows, :] = on_head_lanes(lses)
        k_prev, v_prev = k_cur, v_cur


def _attn_prompt(qkv, bias):
    batch, dil, length, _ = qkv.shape
    tq = min(4 * QBLOCK, length)
    nq = tq // QBLOCK

    def cur(col):
        return pl.BlockSpec((1, 1, tq, GROUP_WIDTH), lambda b, r, n: (b, r, n, col))

    def prev(col):
        return pl.BlockSpec((1, 1, QBLOCK, GROUP_WIDTH),
                            lambda b, r, n: (b, r, jnp.maximum(n * nq - 1, 0), col))

    out_sds = jax.ShapeDtypeStruct((batch, dil, length, GROUP_WIDTH), F32)
    return pl.pallas_call(
        functools.partial(_attn_prompt_kernel, nq=nq),
        grid=(batch, dil, length // tq),
        in_specs=[cur(0), prev(1), cur(1), prev(2), cur(2),
                  _resident((2, QBLOCK, HEADS_PER_GROUP * QBLOCK))],
        out_specs=[cur(0), cur(0)],
        out_shape=[out_sds, out_sds],
        compiler_params=_compiler_params("arbitrary", "arbitrary", "arbitrary"),
        name=f"attn_prompt_d{dil}",
    )(qkv, qkv, qkv, qkv, qkv, bias)


def _attn_sample_kernel(q0_ref, q1_ref, q2_ref, c0_ref, c1_ref, c2_ref, tb0_ref, tb1_ref, tb2_ref, tn_ref,
                        o0_ref, l0_ref, o1_ref, l1_ref, o2_ref, l2_ref, kn_scr, vn_scr, *, t_new):
    n_rows = HEADS_PER_GROUP * t_new
    row_w = lax.broadcasted_iota(jnp.int32, (n_rows, GROUP_WIDTH), 0)
    lane_w = lax.broadcasted_iota(jnp.int32, (n_rows, GROUP_WIDTH), 1)
    own_head = (row_w // t_new) == (lane_w // HEAD_DIM)

    def fold_heads(x):
        x = jnp.where(own_head, x, 0.0)
        out = x[0:t_new]
        for h in range(1, HEADS_PER_GROUP):
            out = out + x[h * t_new:(h + 1) * t_new]
        return out

    caches = (c0_ref, c1_ref, c2_ref)
    cache_bias = (tb0_ref, tb1_ref, tb2_ref)
    outs = ((o0_ref, l0_ref), (o1_ref, l1_ref), (o2_ref, l2_ref))
    for g in range(N_DIL_GROUPS):
        qkv_ref = (q0_ref, q1_ref, q2_ref)[g]
        q = qkv_ref[0, :, 0:GROUP_WIDTH] * QK_SCALE
        q_rows = jnp.where(own_head, jnp.concatenate([q] * HEADS_PER_GROUP, axis=0), 0.0).astype(BF16)
        kn_scr[...] = jnp.zeros_like(kn_scr)
        vn_scr[...] = jnp.zeros_like(vn_scr)
        kn_scr[0:t_new, :] = qkv_ref[0, :, GROUP_WIDTH:2 * GROUP_WIDTH]
        vn_scr[0:t_new, :] = qkv_ref[0, :, 2 * GROUP_WIDTH:3 * GROUP_WIDTH]
        cache = caches[g]
        k_t = cache[0, 0:GROUP_WIDTH, :].astype(BF16)
        v_t = cache[0, GROUP_WIDTH:2 * GROUP_WIDTH, :].astype(BF16)
        s_buf = _dot(q_rows, k_t) + cache_bias[g][...]
        s_new = _dot_nt(q_rows, kn_scr[...].astype(BF16)) + tn_ref[g]
        m = jnp.maximum(jnp.max(s_buf, axis=-1, keepdims=True), jnp.max(s_new, axis=-1, keepdims=True))
        p_buf = jnp.exp(s_buf - m)
        p_new = jnp.exp(s_new - m)
        den = jnp.sum(p_buf, axis=-1, keepdims=True) + jnp.sum(p_new, axis=-1, keepdims=True)
        o = _dot_nt(p_buf.astype(BF16), v_t) + _dot(p_new.astype(BF16), vn_scr[...].astype(BF16))
        o_ref, l_ref = outs[g]
        o_ref[0] = fold_heads(o / den)
        l_ref[0] = fold_heads(jnp.broadcast_to(m + jnp.log(den), (n_rows, GROUP_WIDTH)))


def _attn_sample(qkvs, caches, tbs, tn):
    batch, t_new, _ = qkvs[0].shape
    n_rows = HEADS_PER_GROUP * t_new
    cache_specs = [pl.BlockSpec((1,) + c.shape[1:], lambda b: (b, 0, 0)) for c in caches]
    qkv_spec = pl.BlockSpec((1, t_new, GROUP_QKV), lambda b: (b, 0, 0))
    out_spec = pl.BlockSpec((1, t_new, GROUP_WIDTH), lambda b: (b, 0, 0))
    out_sds = jax.ShapeDtypeStruct((batch, t_new, GROUP_WIDTH), F32)
    return pl.pallas_call(
        functools.partial(_attn_sample_kernel, t_new=t_new),
        grid=(batch,),
        in_specs=[*[qkv_spec] * N_DIL_GROUPS, *cache_specs, *[_resident(t.shape) for t in tbs],
                  _resident((N_DIL_GROUPS, n_rows, QBLOCK))],
        out_specs=[out_spec] * (2 * N_DIL_GROUPS),
        out_shape=[out_sds] * (2 * N_DIL_GROUPS),
        scratch_shapes=[pltpu.VMEM((QBLOCK, GROUP_WIDTH), F32), pltpu.VMEM((QBLOCK, GROUP_WIDTH), F32)],
        compiler_params=_compiler_params("arbitrary"),
        name="attn_sample",
    )(*qkvs, *caches, *tbs, tn)


def _ssm_chunk_terms(chunk, log_dt, lam_re, lam_im, b_re, b_im, c_re, c_im):
    hi = lax.Precision.HIGHEST
    dt = jnp.exp(log_dt.astype(F32))[:, None]
    lr, li = lam_re.astype(F32), lam_im.astype(F32)
    mag = jnp.exp(lr * dt)
    ab_re, ab_im = mag * jnp.cos(li * dt), mag * jnp.sin(li * dt)
    den = lr * lr + li * li
    nr, ni = ab_re - 1.0, ab_im
    coef_re = (nr * lr + ni * li) / den
    coef_im = (ni * lr - nr * li) / den
    br, bi = b_re.astype(F32), b_im.astype(F32)
    bb_re = coef_re[..., None] * br - coef_im[..., None] * bi
    bb_im = coef_re[..., None] * bi + coef_im[..., None] * br
    pw_re, pw_im = [jnp.ones_like(ab_re)], [jnp.zeros_like(ab_im)]
    for _ in range(chunk):
        pr, pi = pw_re[-1], pw_im[-1]
        pw_re.append(pr * ab_re - pi * ab_im)
        pw_im.append(pr * ab_im + pi * ab_re)
    pw_re, pw_im = jnp.stack(pw_re), jnp.stack(pw_im)
    bt_re, bt_im = bb_re.transpose(0, 2, 1), bb_im.transpose(0, 2, 1)
    pk_re, pk_im = pw_re[:chunk, :, None, :], pw_im[:chunk, :, None, :]
    akb_re = pk_re * bt_re[None] - pk_im * bt_im[None]
    akb_im = pk_re * bt_im[None] + pk_im * bt_re[None]
    cr, ci = c_re.astype(F32), c_im.astype(F32)
    e_re = cr[None] * pw_re[1:, :, None, :] - ci[None] * pw_im[1:, :, None, :]
    e_im = cr[None] * pw_im[1:, :, None, :] + ci[None] * pw_re[1:, :, None, :]
    kern = (jnp.einsum('kgqn,gpn->kgqp', akb_re, cr, precision=hi)
            - jnp.einsum('kgqn,gpn->kgqp', akb_im, ci, precision=hi))
    return (akb_re, akb_im), (e_re, e_im), kern, (pw_re[chunk], pw_im[chunk])


SSM_BLOCK_GROUPS = 8
SSM_BLOCK_CH = SSM_BLOCK_GROUPS * SSM_GROUP
SSM_BLOCK_STATE = SSM_BLOCK_GROUPS * SSM_STATE
N_SSM_BLOCKS = N_SSM_GROUPS // SSM_BLOCK_GROUPS


def _ssm_block_operators(chunk, *params):
    (akb_re, akb_im), (e_re, e_im), kern, (a_re, a_im) = _ssm_chunk_terms(chunk, *params)
    _, g, p, n = akb_re.shape
    gb, nb = SSM_BLOCK_GROUPS, N_SSM_BLOCKS
    block_ch = gb * p

    def group_diag(x):
        k, _, _, w = x.shape
        x = x.reshape(k, nb, block_ch, w).transpose(1, 0, 2, 3)
        row_g = jnp.arange(block_ch)[:, None] // p
        col_g = jnp.arange(gb * w)[None, :] // w
        return jnp.where(row_g == col_g, jnp.tile(x, (1, 1, 1, gb)), 0.0)

    def w_op(akb):
        return group_diag(akb[::-1]).reshape(nb, chunk * block_ch, gb * n).astype(BF16)

    def e_op(e):
        et = group_diag(e).reshape(nb, chunk * block_ch, gb * n)
        return jnp.swapaxes(et, 1, 2).astype(BF16)

    lag_blocks = group_diag(kern).transpose(0, 2, 1, 3).reshape(nb, block_ch, chunk * block_ch)
    padded = jnp.pad(lag_blocks, ((0, 0), (0, 0), ((chunk - 1) * block_ch, 0)))
    m = jnp.stack([padded[:, :, (chunk - 1 - s) * block_ch:(2 * chunk - 1 - s) * block_ch]
                   for s in range(chunk)], axis=1)
    m = m.reshape(nb, chunk * block_ch, chunk * block_ch).astype(BF16)
    ops = [w_op(akb_re), w_op(akb_im), m, e_op(e_re), e_op(-e_im)]
    return ops, a_re.reshape(nb, 1, gb * n), a_im.reshape(nb, 1, gb * n)


SSM_X_WIDTH = SSM_CHUNK * SSM_BLOCK_CH


def _ssm_prompt_kernel(u_ref, wre_ref, wim_ref, m_ref, etre_ref, etim_ref, are_ref, aim_ref,
                       y_ref, hre_ref, him_ref, x_scr, sre_scr, sim_scr, *, batch, chunks):
    phase, b = pl.program_id(1), pl.program_id(2)
    for t in range(SSM_CHUNK):
        x_scr[:, t * SSM_BLOCK_CH:(t + 1) * SSM_BLOCK_CH] = (
            u_ref[pl.ds(t, chunks, stride=SSM_CHUNK), :].astype(BF16))
    rows = pl.ds(pl.multiple_of(b * chunks, chunks), chunks)
    state_blocks = SSM_BLOCK_STATE // LANES

    @pl.when(phase == 0)
    def _():
        x = x_scr[...]
        g_re, g_im = _dot(x, wre_ref[0]), _dot(x, wim_ref[0])
        for k in range(state_blocks):
            sre_scr[k, rows, :] = g_re[:, k * LANES:(k + 1) * LANES]
            sim_scr[k, rows, :] = g_im[:, k * LANES:(k + 1) * LANES]

    @pl.when((phase == 1) & (b == 0))
    def _():
        same_lanes = lambda ref, k: ref[0][:, k * LANES:(k + 1) * LANES]

        def step(c, carry):
            same_chunk = pl.ds(c, batch, stride=chunks)
            out = []
            for k, (h_re, h_im) in enumerate(carry):
                a_re, a_im = same_lanes(are_ref, k), same_lanes(aim_ref, k)
                g_re, g_im = sre_scr[k, same_chunk, :], sim_scr[k, same_chunk, :]
                sre_scr[k, same_chunk, :] = h_re
                sim_scr[k, same_chunk, :] = h_im
                out.append((a_re * h_re - a_im * h_im + g_re, a_re * h_im + a_im * h_re + g_im))
            return tuple(out)

        zero = jnp.zeros((batch, LANES), F32)
        final = lax.fori_loop(0, chunks, step, ((zero, zero),) * state_blocks)
        for k, (h_re, h_im) in enumerate(final):
            hre_ref[0, :, k * LANES:(k + 1) * LANES] = h_re
            him_ref[0, :, k * LANES:(k + 1) * LANES] = h_im

    @pl.when(phase == 1)
    def _():
        h_re = jnp.concatenate([sre_scr[k, rows, :] for k in range(state_blocks)], axis=1).astype(BF16)
        h_im = jnp.concatenate([sim_scr[k, rows, :] for k in range(state_blocks)], axis=1).astype(BF16)
        pair_w = 2 * SSM_BLOCK_CH
        for j in range(SSM_CHUNK // 2):
            cols = slice(j * pair_w, (j + 1) * pair_w)
            k_in = (j + 1) * pair_w
            yj = (_dot(x_scr[:, :k_in], m_ref[0, :k_in, cols])
                  + _dot(h_re, etre_ref[0, :, cols]) + _dot(h_im, etim_ref[0, :, cols]))
            for i in range(2):
                y_ref[pl.ds(2 * j + i, chunks, stride=SSM_CHUNK), :] = (
                    yj[:, i * SSM_BLOCK_CH:(i + 1) * SSM_BLOCK_CH])


def _ssm_prompt(u, ops, a_re, a_im, batch, seq):
    chunks = seq // SSM_CHUNK
    op_spec = lambda arr: pl.BlockSpec((1,) + arr.shape[1:], lambda g, ph, b: (g, 0, 0),
                                       pipeline_mode=pl.Buffered(1))
    state_spec = pl.BlockSpec((1, batch, SSM_BLOCK_STATE), lambda g, ph, b: (g, 0, 0))
    state_sds = jax.ShapeDtypeStruct((N_SSM_BLOCKS, batch, SSM_BLOCK_STATE), F32)
    y, h_re, h_im = pl.pallas_call(
        functools.partial(_ssm_prompt_kernel, batch=batch, chunks=chunks),
        grid=(N_SSM_BLOCKS, 2, batch),
        in_specs=[pl.BlockSpec((seq, SSM_BLOCK_CH), lambda g, ph, b: (b, g)),
                  *[op_spec(o) for o in ops], op_spec(a_re), op_spec(a_im)],
        out_specs=[pl.BlockSpec((seq, SSM_BLOCK_CH), lambda g, ph, b: (b * ph, g)), state_spec, state_spec],
        out_shape=[jax.ShapeDtypeStruct((batch * seq, SSM_WIDTH), F32), state_sds, state_sds],
        scratch_shapes=[pltpu.VMEM((chunks, SSM_X_WIDTH), BF16),
                        pltpu.VMEM((SSM_BLOCK_STATE // LANES, batch * chunks, LANES), F32),
                        pltpu.VMEM((SSM_BLOCK_STATE // LANES, batch * chunks, LANES), F32)],
        compiler_params=_compiler_params("arbitrary", "arbitrary", "arbitrary"),
        name="ssm_prompt",
    )(u, *ops, a_re, a_im)

    def by_sequence(h):
        h = h.reshape(N_SSM_BLOCKS, batch, SSM_BLOCK_GROUPS, SSM_STATE).transpose(1, 0, 2, 3)
        return h.reshape(batch, N_SSM_GROUPS, SSM_STATE)

    return y, by_sequence(h_re), by_sequence(h_im)


def _ssm_sample_kernel(u_ref, wre_ref, wim_ref, m_ref, etre_ref, etim_ref, are_ref, aim_ref,
                       h0re_ref, h0im_ref, y_ref, hre_ref, him_ref, *, t_new, batch):
    for nb in range(N_SSM_BLOCKS):
        ch = slice(nb * SSM_BLOCK_CH, (nb + 1) * SSM_BLOCK_CH)
        st = slice(nb * SSM_BLOCK_STATE, (nb + 1) * SSM_BLOCK_STATE)
        x = jnp.concatenate([u_ref[t * batch:(t + 1) * batch, ch] for t in range(t_new)], axis=1).astype(BF16)
        h_re, h_im = h0re_ref[:, st], h0im_ref[:, st]
        a_re, a_im = are_ref[nb], aim_ref[nb]
        hre_ref[:, st] = a_re * h_re - a_im * h_im + _dot(x, wre_ref[nb])
        him_ref[:, st] = a_re * h_im + a_im * h_re + _dot(x, wim_ref[nb])
        y = (_dot(x, m_ref[nb]) + _dot(h_re.astype(BF16), etre_ref[nb])
             + _dot(h_im.astype(BF16), etim_ref[nb]))
        for t in range(t_new):
            y_ref[t * batch:(t + 1) * batch, ch] = y[:, t * SSM_BLOCK_CH:(t + 1) * SSM_BLOCK_CH]


def _ssm_sample(u, ops, a_re, a_im, h0_re, h0_im, t_new):
    batch = u.shape[0] // t_new
    args = (u, *ops, a_re, a_im, h0_re, h0_im)
    state_sds = jax.ShapeDtypeStruct((batch, N_SSM_GROUPS * SSM_STATE), F32)
    out_shape = [jax.ShapeDtypeStruct(u.shape, F32), state_sds, state_sds]
    whole = lambda shape: pl.BlockSpec(shape, lambda i, nd=len(shape): (0,) * nd)
    return pl.pallas_call(
        functools.partial(_ssm_sample_kernel, t_new=t_new, batch=batch),
        grid=(1,),
        in_specs=[whole(a.shape) for a in args],
        out_specs=[whole(s.shape) for s in out_shape],
        out_shape=out_shape,
        compiler_params=_compiler_params("arbitrary"),
        name="ssm_sample",
    )(*args)


def _gelu_tanh(x):
    return 0.5 * x * (1.0 + jnp.tanh(math.sqrt(2.0 / math.pi) * (x + 0.044715 * (x * x * x))))


def _merge_tile(o0_ref, l0_ref, o1_ref, l1_ref, o2_ref, l2_ref, ys_ref, u_ref, gate_ref, x_ref,
                d_ref, wglu_ref, bglu_ref, wba_ref, wbs_ref, wout_ref, order_scr, tm, dils):
    def row_order(ref, dil, slot):
        if dil == 1:
            return ref[0, 0]
        n = tm // dil
        halves = GROUP_WIDTH // LANES
        for r in range(dil):
            for k in range(halves):
                order_scr[slot * halves + k, pl.ds(r, n, stride=dil), :] = ref[0, r, :, k * LANES:(k + 1) * LANES]
        return jnp.concatenate([order_scr[slot * halves + k] for k in range(halves)], axis=1)

    parts = [row_order(ref, dils[i // 2], i) for i, ref in
             enumerate((o0_ref, l0_ref, o1_ref, l1_ref, o2_ref, l2_ref))]
    o0, l0, o1, l1, o2, l2 = parts
    mx = jnp.maximum(jnp.maximum(l0, l1), l2)
    e0, e1, e2 = jnp.exp(l0 - mx), jnp.exp(l1 - mx), jnp.exp(l2 - mx)
    attn = (e0 * o0 + e1 * o1 + e2 * o2) / (e0 + e1 + e2)
    branch_a = _dot(attn.astype(BF16), wba_ref[...])
    y = _gelu_tanh(ys_ref[...] + d_ref[...] * u_ref[...])
    y = y * _sigmoid(_dot(y.astype(BF16), wglu_ref[...]) + bglu_ref[...])
    branch_s = _dot(y.astype(BF16), wbs_ref[...])
    mix = (_sigmoid(gate_ref[:, 0:D_MODEL]) * branch_a
           + _sigmoid(gate_ref[:, D_MODEL:2 * D_MODEL]) * branch_s)
    return x_ref[...] + _dot(mix.astype(BF16), wout_ref[...])


FF_CHUNK = D_FF // 2
N_MERGE_ROW_INPUTS = 2 * N_DIL_GROUPS + 4
N_MERGE_WEIGHTS = 6


def _merge_ffn_kernel(*refs, tm, shift, pad, dils):
    n_merge = N_MERGE_ROW_INPUTS + N_MERGE_WEIGHTS
    merge_refs, rest = refs[:n_merge], refs[n_merge:]
    carry_ref, g2_ref, wup_ref, cw_ref, cb_ref, wdn_ref, gf_ref, y_ref, state_ref, order_scr, a_scr = rest
    hist = 2 * shift

    @pl.when(pl.program_id(1) == 0)
    def _():
        a_scr[pad - hist:pad, :] = carry_ref[0]

    xf = _merge_tile(*merge_refs, order_scr, tm, dils)
    xn = _rmsnorm(xf, g2_ref[...]).astype(BF16)
    acc = jnp.zeros((tm, D_MODEL), F32)
    for c0 in range(0, D_FF, FF_CHUNK):
        cols = slice(c0, c0 + FF_CHUNK)
        a = _dot(xn, wup_ref[:, cols])
        val = _dot(xn, wup_ref[:, D_FF + c0:D_FF + c0 + FF_CHUNK])
        a_scr[pad:pad + tm, cols] = a
        a_m1 = a_scr[pad - shift:pad - shift + tm, cols]
        a_m2 = a_scr[pad - hist:pad - hist + tm, cols]
        conv = cb_ref[:, cols] + cw_ref[0:1, cols] * a_m2
        conv = conv + cw_ref[1:2, cols] * a_m1
        conv = conv + cw_ref[2:3, cols] * a
        act = conv * _sigmoid(conv) * val
        acc = acc + _dot(act.astype(BF16), wdn_ref[cols, :])
    tail = a_scr[pad + tm - hist:pad + tm, :]
    a_scr[pad - hist:pad, :] = tail
    state_ref[0] = tail
    y_ref[...] = _rmsnorm(xf + acc, gf_ref[...])


def _merge_ffn(attn_parts, ys, u, gates, x2d, merge_weights, carry, ffn_weights, n_seq, tm, shift):
    m = x2d.shape[0]
    tiles = m // n_seq // tm
    hist = 2 * shift
    pad = -(-hist // SUBLANES) * SUBLANES
    dils = tuple(p.shape[1] for p in attn_parts[::2])
    row = lambda width: pl.BlockSpec((tm, width), lambda b, j: (b * tiles + j, 0))
    res_spec = lambda dil: pl.BlockSpec((1, dil, tm // dil, GROUP_WIDTH), lambda b, j: (b, 0, j, 0))
    state_spec = pl.BlockSpec((1, hist, D_FF), lambda b, j: (b, 0, 0))
    assert len(merge_weights) == N_MERGE_WEIGHTS
    return pl.pallas_call(
        functools.partial(_merge_ffn_kernel, tm=tm, shift=shift, pad=pad, dils=dils),
        grid=(n_seq, tiles),
        in_specs=[*[res_spec(p.shape[1]) for p in attn_parts],
                  row(SSM_WIDTH), row(SSM_WIDTH), row(2 * D_MODEL), row(D_MODEL),
                  *[_resident(w.shape) for w in merge_weights],
                  state_spec, *[_resident(w.shape) for w in ffn_weights]],
        out_specs=[row(D_MODEL), state_spec],
        out_shape=[jax.ShapeDtypeStruct((m, D_MODEL), F32),
                   jax.ShapeDtypeStruct((n_seq, hist, D_FF), F32)],
        scratch_shapes=[pltpu.VMEM((len(attn_parts) * GROUP_WIDTH // LANES, tm, LANES), F32),
                        pltpu.VMEM((pad + tm, D_FF), F32)],
        compiler_params=_compiler_params("arbitrary", "arbitrary"),
        name="merge_ffn",
    )(*attn_parts, ys, u, gates, x2d, *merge_weights, carry, *ffn_weights)


def _kv_rows(qkv, keep):
    batch, dil, length, _ = qkv.shape
    n = keep // dil
    rows = qkv[:, :, length - n:, GROUP_WIDTH:]
    cols = rows.transpose(0, 3, 2, 1).reshape(batch, 2, HEADS_PER_GROUP, HEAD_DIM, keep)
    return cols.transpose(0, 4, 1, 2, 3)


def _prompt_layer(x, rel_bias, lw):
    batch, seq, _ = x.shape
    x2d = x.reshape(batch * seq, D_MODEL)
    dils = tuple(dil for _, dil in DIL_PATTERNS)
    *qkvs, u, gates = _in_proj(x2d, lw['norm1_g'], lw['w_in'], batch, IN_PROJ_TILE, dils)

    attn_parts, kv_new = [], []
    for g, (window, dil) in enumerate(DIL_PATTERNS):
        tab = rel_bias[:, g * HEADS_PER_GROUP:(g + 1) * HEADS_PER_GROUP]
        attn_parts.extend(_attn_prompt(qkvs[g], _prompt_bias(tab, dil)))
        kv_new.append(_kv_rows(qkvs[g], min(window, seq)))

    ops, a_re, a_im = _ssm_block_operators(SSM_CHUNK, *lw['ssm'])
    ys, h_re, h_im = _ssm_prompt(u, ops, a_re, a_im, batch, seq)

    carry = jnp.zeros((batch, CONV_W - 1, D_FF), F32)
    y, conv_state = _merge_ffn(attn_parts, ys, u, gates, x2d, lw['merge'], carry, lw['ffn'],
                               n_seq=batch, tm=MERGE_FFN_TILE, shift=1)
    return y.reshape(batch, seq, D_MODEL), (*kv_new, h_re, h_im, conv_state)


def _sample_layer(x, caches, h0_re, h0_im, conv_buf, rel_bias, lw):
    batch, t_new, _ = x.shape
    m = batch * t_new
    x2d = x.transpose(1, 0, 2).reshape(m, D_MODEL)
    *qkvs, u, gates = _in_proj(x2d, lw['norm1_g'], lw['w_in'], 1, m, (1,) * N_DIL_GROUPS)
    qkvs_bt = [q.reshape(t_new, batch, GROUP_QKV).transpose(1, 0, 2) for q in qkvs]

    tbs, tns, views = [], [], []
    for g, (window, dil) in enumerate(DIL_PATTERNS):
        tab = rel_bias[:, g * HEADS_PER_GROUP:(g + 1) * HEADS_PER_GROUP]
        n_cached = caches[g].shape[1]
        tb, tn = _sample_bias(tab, dil, t_new, n_cached)
        tbs.append(tb)
        tns.append(tn)
        views.append(caches[g].transpose(0, 2, 3, 4, 1).reshape(batch, 2 * GROUP_WIDTH, n_cached))
    parts = _attn_sample(qkvs_bt, views, tbs, jnp.stack(tns))
    attn_parts = [p.transpose(1, 0, 2).reshape(1, 1, m, GROUP_WIDTH) for p in parts]
    kv_new = [_kv_rows(q.reshape(batch, 1, t_new, GROUP_QKV), t_new) for q in qkvs_bt]

    ops, a_re, a_im = _ssm_block_operators(t_new, *lw['ssm'])
    ys, h_re, h_im = _ssm_sample(u, ops, a_re, a_im, h0_re.reshape(batch, -1), h0_im.reshape(batch, -1), t_new)

    carry = conv_buf.transpose(1, 0, 2).reshape(1, (CONV_W - 1) * batch, D_FF)
    y, conv_state = _merge_ffn(attn_parts, ys, u, gates, x2d, lw['merge'], carry, lw['ffn'],
                               n_seq=1, tm=m, shift=batch)
    y = y.reshape(t_new, batch, D_MODEL).transpose(1, 0, 2)
    conv_state = conv_state.reshape(CONV_W - 1, batch, D_FF).transpose(1, 0, 2)
    state_shape = (batch, N_SSM_GROUPS, SSM_STATE)
    return y, (*kv_new, h_re.reshape(state_shape), h_im.reshape(state_shape), conv_state)


IN_PROJ_TILE = 512
MERGE_FFN_TILE = 256


def kernel(x_prompt, x_sample, cache_kv_w128, cache_kv_w512, cache_kv_w2048, state_ssm_re, state_ssm_im, state_ffn_conv, rel_bias, norm1_g, w_in, ssm_log_dt, ssm_lambda_re, ssm_lambda_im, ssm_b_re, ssm_b_im, ssm_c_re, ssm_c_im, ssm_d, w_glu, b_glu, w_branch_attn, w_branch_ssm, w_out, norm2_g, w_up, conv_w, conv_b, w_down, norm_f_g):
    depth = w_in.shape[0]
    hp, hs = x_prompt, x_sample
    st_p, st_s = [], []
    gf = norm_f_g.reshape(1, D_MODEL)
    for l in range(depth):
        last = l == depth - 1
        lw = {
            'norm1_g': norm1_g[l].reshape(1, D_MODEL),
            'w_in': _group_major_columns(w_in[l]).astype(BF16),
            'ssm': (ssm_log_dt[l], ssm_lambda_re[l], ssm_lambda_im[l],
                    ssm_b_re[l], ssm_b_im[l], ssm_c_re[l], ssm_c_im[l]),
            'merge': (ssm_d[l].reshape(1, SSM_WIDTH), w_glu[l].astype(BF16), b_glu[l].reshape(1, SSM_WIDTH),
                      w_branch_attn[l].astype(BF16), w_branch_ssm[l].astype(BF16), w_out[l].astype(BF16)),
            'ffn': (norm2_g[l].reshape(1, D_MODEL), w_up[l].astype(BF16), conv_w[l],
                    conv_b[l].reshape(1, D_FF), w_down[l].astype(BF16), gf),
        }
        assert last, "the final RMSNorm is fused into the last layer's ffn kernel"
        hp, sp = _prompt_layer(hp, rel_bias, lw)
        hs, ss = _sample_layer(hs, (cache_kv_w128[l], cache_kv_w512[l], cache_kv_w2048[l]),
                               state_ssm_re[l], state_ssm_im[l], state_ffn_conv[l], rel_bias, lw)
        st_p.append(sp)
        st_s.append(ss)
    stack = lambda states, i: jnp.stack([st[i] for st in states], axis=0)
    return (hp, hs, *[stack(st_p, i) for i in range(6)], *[stack(st_s, i) for i in range(6)])
```

```python
import functools
import math

import jax
import jax.numpy as jnp
from jax import lax
from jax.experimental import pallas as pl
from jax.experimental.pallas import tpu as pltpu

F32 = jnp.float32
BF16 = jnp.bfloat16

D_MODEL = 1024
HEAD_DIM = 64
HEADS_PER_GROUP = 4
DIL_PATTERNS = ((128, 1), (512, 4), (2048, 16))
N_DIL_GROUPS = len(DIL_PATTERNS)
GROUP_WIDTH = HEADS_PER_GROUP * HEAD_DIM
QK_WIDTH = N_DIL_GROUPS * GROUP_WIDTH
QKV_WIDTH = 3 * QK_WIDTH
GROUP_QKV = 3 * GROUP_WIDTH
QBLOCK = 128
SSM_GROUP = 16
SSM_STATE = 64
SSM_WIDTH = D_MODEL // 2
N_SSM_GROUPS = SSM_WIDTH // SSM_GROUP
SSM_CHUNK = 16
D_FF = 2816
CONV_W = 3
N_BUCKETS = 32
MAX_DISTANCE = 2048
NORM_EPS = 1e-6
NEG_INF = -1e30
U_START = QKV_WIDTH
GATE_START = U_START + SSM_WIDTH
IN_WIDTH = GATE_START + 2 * D_MODEL
QK_SCALE = HEAD_DIM ** -0.5

VMEM_LIMIT_BYTES = 56 * 1024 * 1024
SUBLANES = 8
LANES = 128


def _compiler_params(*semantics):
    return pltpu.CompilerParams(dimension_semantics=semantics, vmem_limit_bytes=VMEM_LIMIT_BYTES)


def _resident(shape):
    nd = len(shape)
    return pl.BlockSpec(shape, lambda *_: (0,) * nd, pipeline_mode=pl.Buffered(1))


def _rmsnorm(xf, g):
    y = xf * lax.rsqrt(jnp.mean(xf * xf, axis=-1, keepdims=True) + NORM_EPS)
    return y * g


def _sigmoid(x):
    return 1.0 / (1.0 + jnp.exp(-x))


def _dot(a, b):
    return jnp.dot(a, b, preferred_element_type=F32)


def _dot_nt(a, b):
    return lax.dot_general(a, b, (((1,), (1,)), ((), ())), preferred_element_type=F32)


def _in_proj_kernel(x_ref, g_ref, w_ref, qkv0_ref, qkv1_ref, qkv2_ref, u_ref, gate_ref, xn_scr, *, tm, dils):
    xf = _rmsnorm(x_ref[...], g_ref[...])
    n_lane_blocks = D_MODEL // LANES
    if any(dil > 1 for dil in dils):
        for k in range(n_lane_blocks):
            xn_scr[k] = xf[:, k * LANES:(k + 1) * LANES]
    by_residue = {1: xf.astype(BF16)}

    def rows_by_residue(dil):
        if dil not in by_residue:
            n = tm // dil
            xr = jnp.concatenate(
                [jnp.concatenate([xn_scr[k, pl.ds(r, n, stride=dil), :] for k in range(n_lane_blocks)], axis=1)
                 for r in range(dil)], axis=0)
            by_residue[dil] = xr.astype(BF16)
        return by_residue[dil]

    outputs = [(ref, g * GROUP_QKV, GROUP_QKV) for g, ref in enumerate((qkv0_ref, qkv1_ref, qkv2_ref))]
    outputs.append((u_ref, U_START, SSM_WIDTH))
    for (ref, start, width), dil in zip(outputs, dils):
        n = tm // dil
        res = _dot(rows_by_residue(dil), w_ref[:, start:start + width])
        for r in range(dil):
            ref[0, r] = res[r * n:(r + 1) * n]
    for c0 in range(0, 2 * D_MODEL, D_MODEL):
        gate_ref[:, c0:c0 + D_MODEL] = _dot(by_residue[1], w_ref[:, GATE_START + c0:GATE_START + c0 + D_MODEL])


def _group_major_columns(w_in):
    parts = []
    for g in range(N_DIL_GROUPS):
        for base in (0, QK_WIDTH, 2 * QK_WIDTH):
            parts.append(w_in[:, base + g * GROUP_WIDTH:base + (g + 1) * GROUP_WIDTH])
    parts.append(w_in[:, U_START:])
    return jnp.concatenate(parts, axis=1)


def _in_proj(x2d, g, w_bf16, n_seq, tm, dils):
    m = x2d.shape[0]
    seq = m // n_seq
    tiles = seq // tm
    row = lambda width: pl.BlockSpec((tm, width), lambda b, j: (b * tiles + j, 0))
    res_spec = lambda dil, width: pl.BlockSpec((1, dil, tm // dil, width), lambda b, j: (b, 0, j, 0))
    widths = (GROUP_QKV,) * N_DIL_GROUPS + (SSM_WIDTH,)
    return pl.pallas_call(
        functools.partial(_in_proj_kernel, tm=tm, dils=dils),
        grid=(n_seq, tiles),
        in_specs=[row(D_MODEL), _resident((1, D_MODEL)), _resident((D_MODEL, IN_WIDTH))],
        out_specs=[*[res_spec(d, w) for d, w in zip(dils, widths)], row(2 * D_MODEL)],
        out_shape=[*[jax.ShapeDtypeStruct((n_seq, d, seq // d, w), F32) for d, w in zip(dils, widths)],
                   jax.ShapeDtypeStruct((m, 2 * D_MODEL), F32)],
        scratch_shapes=[pltpu.VMEM((D_MODEL // LANES, tm, LANES), F32)],
        compiler_params=_compiler_params("arbitrary", "arbitrary"),
        name="in_proj",
    )(x2d, g, w_bf16)


def _rel_bucket(dist):
    max_exact = N_BUCKETS // 2
    n = jnp.maximum(dist, 0)
    nf = jnp.maximum(n, 1).astype(F32)
    large = max_exact + (jnp.log(nf / max_exact) / math.log(MAX_DISTANCE / max_exact)
                         * (N_BUCKETS - max_exact)).astype(jnp.int32)
    large = jnp.minimum(large, N_BUCKETS - 1)
    return jnp.where(n < max_exact, n, large)


def _masked_bias(tab, strides, valid, dil):
    n_dist = QBLOCK
    bucket = _rel_bucket(jnp.clip(strides, 0, n_dist) * dil).reshape(1, -1)
    onehot = (bucket == jnp.arange(N_BUCKETS)[:, None]).astype(F32)
    bias = jnp.dot(tab.astype(F32).T, onehot, precision=lax.Precision.HIGHEST)
    bias = jnp.where(valid.reshape(1, -1), bias, NEG_INF)
    return bias.reshape((tab.shape[1],) + strides.shape)


def _prompt_bias(tab, dil):
    qi = jnp.arange(QBLOCK)[:, None]
    ki = jnp.arange(QBLOCK)[None, :]
    j_prev = qi + QBLOCK - ki
    j_cur = qi - ki
    bias = jnp.stack([_masked_bias(tab, j_prev, j_prev <= QBLOCK, dil),
                      _masked_bias(tab, j_cur, j_cur >= 0, dil)], axis=0)
    return bias.transpose(0, 2, 1, 3).reshape(2, QBLOCK, HEADS_PER_GROUP * QBLOCK)


def _sample_bias(tab, dil, t_new, n_cached):
    t = jnp.arange(t_new)[:, None]
    delta_buf = n_cached + t - jnp.arange(n_cached)[None, :]
    ok_buf = (delta_buf % dil == 0) & (delta_buf // dil <= QBLOCK)
    m = jnp.arange(QBLOCK)[None, :]
    delta_new = t - m
    ok_new = (m < t_new) & (delta_new >= 0) & (delta_new % dil == 0)
    b_buf = _masked_bias(tab, delta_buf // dil, ok_buf, dil)
    b_new = _masked_bias(tab, delta_new // dil, ok_new, dil)
    return (b_buf.reshape(HEADS_PER_GROUP * t_new, n_cached),
            b_new.reshape(HEADS_PER_GROUP * t_new, QBLOCK))


def _attn_prompt_kernel(q_ref, kc_ref, vc_ref, bias_ref, o_ref, lse_ref, kprev_scr, vprev_scr, *, nq):
    first_tile = pl.program_id(2) == 0

    @pl.when(first_tile)
    def _():
        kprev_scr[...] = jnp.zeros_like(kprev_scr)
        vprev_scr[...] = jnp.zeros_like(vprev_scr)

    stacked = (HEADS_PER_GROUP * QBLOCK, GROUP_WIDTH)
    own_head = (lax.broadcasted_iota(jnp.int32, stacked, 0) // QBLOCK
                == lax.broadcasted_iota(jnp.int32, stacked, 1) // HEAD_DIM)
    lane_head = lax.broadcasted_iota(jnp.int32, (QBLOCK, GROUP_WIDTH), 1) // HEAD_DIM

    def per_head(x):
        xb = x.astype(BF16)
        return jnp.where(own_head, jnp.concatenate([xb] * HEADS_PER_GROUP, axis=0), 0)

    def on_head_lanes(cols):
        out = jnp.broadcast_to(cols[-1], (QBLOCK, GROUP_WIDTH))
        for h in range(HEADS_PER_GROUP - 2, -1, -1):
            out = jnp.where(lane_head == h, cols[h], out)
        return out

    k_prev, v_prev = kprev_scr[...], vprev_scr[...]
    for i in range(nq):
        rows = slice(i * QBLOCK, (i + 1) * QBLOCK)
        q = (q_ref[0, 0, rows, :] * QK_SCALE).astype(BF16)
        k_cur, v_cur = per_head(kc_ref[0, 0, rows, :]), per_head(vc_ref[0, 0, rows, :])
        s_prev = _dot_nt(q, k_prev) + bias_ref[0]
        if i == 0:
            s_prev = jnp.where(first_tile, NEG_INF, s_prev)
        s_cur = _dot_nt(q, k_cur) + bias_ref[1]
        p_prev, p_cur, dens, lses = [], [], [], []
        for h in range(HEADS_PER_GROUP):
            keys = slice(h * QBLOCK, (h + 1) * QBLOCK)
            sp, sc = s_prev[:, keys], s_cur[:, keys]
            m = jnp.max(jnp.maximum(sp, sc), axis=-1, keepdims=True)
            pp, pc = jnp.exp(sp - m), jnp.exp(sc - m)
            den = jnp.sum(pp + pc, axis=-1, keepdims=True)
            p_prev.append(pp.astype(BF16))
            p_cur.append(pc.astype(BF16))
            dens.append(den)
            lses.append(m + jnp.log(den))
        o = _dot(jnp.concatenate(p_prev, axis=1), v_prev) + _dot(jnp.concatenate(p_cur, axis=1), v_cur)
        o_ref[0, 0, rows, :] = o / on_head_lanes(dens)
        lse_ref[0, 0, rows, :] = on_head_lanes(lses)
        k_prev, v_prev = k_cur, v_cur
    kprev_scr[...] = k_prev
    vprev_scr[...] = v_prev


def _attn_prompt(qkv, bias):
    batch, dil, length, _ = qkv.shape
    tq = min(4 * QBLOCK, length)
    nq = tq // QBLOCK

    def cur(col):
        return pl.BlockSpec((1, 1, tq, GROUP_WIDTH), lambda b, r, n: (b, r, n, col))

    stacked = pltpu.VMEM((HEADS_PER_GROUP * QBLOCK, GROUP_WIDTH), BF16)
    out_sds = jax.ShapeDtypeStruct((batch, dil, length, GROUP_WIDTH), F32)
    return pl.pallas_call(
        functools.partial(_attn_prompt_kernel, nq=nq),
        grid=(batch, dil, length // tq),
        in_specs=[cur(0), cur(1), cur(2), _resident((2, QBLOCK, HEADS_PER_GROUP * QBLOCK))],
        out_specs=[cur(0), cur(0)],
        out_shape=[out_sds, out_sds],
        scratch_shapes=[stacked, stacked],
        compiler_params=_compiler_params("arbitrary", "arbitrary", "arbitrary"),
        name=f"attn_prompt_d{dil}",
    )(qkv, qkv, qkv, bias)


def _attn_sample_kernel(q0_ref, q1_ref, q2_ref, c0_ref, c1_ref, c2_ref, tb0_ref, tb1_ref, tb2_ref, tn_ref,
                        o0_ref, l0_ref, o1_ref, l1_ref, o2_ref, l2_ref, kn_scr, vn_scr, *, t_new):
    n_rows = HEADS_PER_GROUP * t_new
    row_w = lax.broadcasted_iota(jnp.int32, (n_rows, GROUP_WIDTH), 0)
    lane_w = lax.broadcasted_iota(jnp.int32, (n_rows, GROUP_WIDTH), 1)
    own_head = (row_w // t_new) == (lane_w // HEAD_DIM)

    def fold_heads(x):
        x = jnp.where(own_head, x, 0.0)
        out = x[0:t_new]
        for h in range(1, HEADS_PER_GROUP):
            out = out + x[h * t_new:(h + 1) * t_new]
        return out

    caches = (c0_ref, c1_ref, c2_ref)
    cache_bias = (tb0_ref, tb1_ref, tb2_ref)
    outs = ((o0_ref, l0_ref), (o1_ref, l1_ref), (o2_ref, l2_ref))
    for g in range(N_DIL_GROUPS):
        qkv_ref = (q0_ref, q1_ref, q2_ref)[g]
        q = qkv_ref[0, :, 0:GROUP_WIDTH] * QK_SCALE
        q_rows = jnp.where(own_head, jnp.concatenate([q] * HEADS_PER_GROUP, axis=0), 0.0).astype(BF16)
        kn_scr[...] = jnp.zeros_like(kn_scr)
        vn_scr[...] = jnp.zeros_like(vn_scr)
        kn_scr[0:t_new, :] = qkv_ref[0, :, GROUP_WIDTH:2 * GROUP_WIDTH]
        vn_scr[0:t_new, :] = qkv_ref[0, :, 2 * GROUP_WIDTH:3 * GROUP_WIDTH]
        cache = caches[g]
        k_t = cache[0, 0:GROUP_WIDTH, :].astype(BF16)
        v_t = cache[0, GROUP_WIDTH:2 * GROUP_WIDTH, :].astype(BF16)
        s_buf = _dot(q_rows, k_t) + cache_bias[g][...]
        s_new = _dot_nt(q_rows, kn_scr[...].astype(BF16)) + tn_ref[g]
        m = jnp.maximum(jnp.max(s_buf, axis=-1, keepdims=True), jnp.max(s_new, axis=-1, keepdims=True))
        p_buf = jnp.exp(s_buf - m)
        p_new = jnp.exp(s_new - m)
        den = jnp.sum(p_buf, axis=-1, keepdims=True) + jnp.sum(p_new, axis=-1, keepdims=True)
        o = _dot_nt(p_buf.astype(BF16), v_t) + _dot(p_new.astype(BF16), vn_scr[...].astype(BF16))
        o_ref, l_ref = outs[g]
        o_ref[0] = fold_heads(o / den)
        l_ref[0] = fold_heads(jnp.broadcast_to(m + jnp.log(den), (n_rows, GROUP_WIDTH)))


def _attn_sample(qkvs, caches, tbs, tn):
    batch, t_new, _ = qkvs[0].shape
    n_rows = HEADS_PER_GROUP * t_new
    cache_specs = [pl.BlockSpec((1,) + c.shape[1:], lambda b: (b, 0, 0)) for c in caches]
    qkv_spec = pl.BlockSpec((1, t_new, GROUP_QKV), lambda b: (b, 0, 0))
    out_spec = pl.BlockSpec((1, t_new, GROUP_WIDTH), lambda b: (b, 0, 0))
    out_sds = jax.ShapeDtypeStruct((batch, t_new, GROUP_WIDTH), F32)
    return pl.pallas_call(
        functools.partial(_attn_sample_kernel, t_new=t_new),
        grid=(batch,),
        in_specs=[*[qkv_spec] * N_DIL_GROUPS, *cache_specs, *[_resident(t.shape) for t in tbs],
                  _resident((N_DIL_GROUPS, n_rows, QBLOCK))],
        out_specs=[out_spec] * (2 * N_DIL_GROUPS),
        out_shape=[out_sds] * (2 * N_DIL_GROUPS),
        scratch_shapes=[pltpu.VMEM((QBLOCK, GROUP_WIDTH), F32), pltpu.VMEM((QBLOCK, GROUP_WIDTH), F32)],
        compiler_params=_compiler_params("arbitrary"),
        name="attn_sample",
    )(*qkvs, *caches, *tbs, tn)


def _ssm_chunk_terms(chunk, log_dt, lam_re, lam_im, b_re, b_im, c_re, c_im):
    hi = lax.Precision.HIGHEST
    dt = jnp.exp(log_dt.astype(F32))[:, None]
    lr, li = lam_re.astype(F32), lam_im.astype(F32)
    mag = jnp.exp(lr * dt)
    ab_re, ab_im = mag * jnp.cos(li * dt), mag * jnp.sin(li * dt)
    den = lr * lr + li * li
    nr, ni = ab_re - 1.0, ab_im
    coef_re = (nr * lr + ni * li) / den
    coef_im = (ni * lr - nr * li) / den
    br, bi = b_re.astype(F32), b_im.astype(F32)
    bb_re = coef_re[..., None] * br - coef_im[..., None] * bi
    bb_im = coef_re[..., None] * bi + coef_im[..., None] * br
    pw_re, pw_im = [jnp.ones_like(ab_re)], [jnp.zeros_like(ab_im)]
    for _ in range(chunk):
        pr, pi = pw_re[-1], pw_im[-1]
        pw_re.append(pr * ab_re - pi * ab_im)
        pw_im.append(pr * ab_im + pi * ab_re)
    pw_re, pw_im = jnp.stack(pw_re), jnp.stack(pw_im)
    bt_re, bt_im = bb_re.transpose(0, 2, 1), bb_im.transpose(0, 2, 1)
    pk_re, pk_im = pw_re[:chunk, :, None, :], pw_im[:chunk, :, None, :]
    akb_re = pk_re * bt_re[None] - pk_im * bt_im[None]
    akb_im = pk_re * bt_im[None] + pk_im * bt_re[None]
    cr, ci = c_re.astype(F32), c_im.astype(F32)
    e_re = cr[None] * pw_re[1:, :, None, :] - ci[None] * pw_im[1:, :, None, :]
    e_im = cr[None] * pw_im[1:, :, None, :] + ci[None] * pw_re[1:, :, None, :]
    kern = (jnp.einsum('kgqn,gpn->kgqp', akb_re, cr, precision=hi)
            - jnp.einsum('kgqn,gpn->kgqp', akb_im, ci, precision=hi))
    return (akb_re, akb_im), (e_re, e_im), kern, (pw_re[chunk], pw_im[chunk])


SSM_BLOCK_GROUPS = 8
SSM_BLOCK_CH = SSM_BLOCK_GROUPS * SSM_GROUP
SSM_BLOCK_STATE = SSM_BLOCK_GROUPS * SSM_STATE
N_SSM_BLOCKS = N_SSM_GROUPS // SSM_BLOCK_GROUPS


def _ssm_block_operators(chunk, *params):
    (akb_re, akb_im), (e_re, e_im), kern, (a_re, a_im) = _ssm_chunk_terms(chunk, *params)
    _, g, p, n = akb_re.shape
    gb, nb = SSM_BLOCK_GROUPS, N_SSM_BLOCKS
    block_ch = gb * p

    def group_diag(x):
        k, _, _, w = x.shape
        x = x.reshape(k, nb, block_ch, w).transpose(1, 0, 2, 3)
        row_g = jnp.arange(block_ch)[:, None] // p
        col_g = jnp.arange(gb * w)[None, :] // w
        return jnp.where(row_g == col_g, jnp.tile(x, (1, 1, 1, gb)), 0.0)

    def w_op(akb):
        return group_diag(akb[::-1]).reshape(nb, chunk * block_ch, gb * n).astype(BF16)

    def e_op(e):
        et = group_diag(e).reshape(nb, chunk * block_ch, gb * n)
        return jnp.swapaxes(et, 1, 2).astype(BF16)

    lag_blocks = group_diag(kern).transpose(0, 2, 1, 3).reshape(nb, block_ch, chunk * block_ch)
    padded = jnp.pad(lag_blocks, ((0, 0), (0, 0), ((chunk - 1) * block_ch, 0)))
    m = jnp.stack([padded[:, :, (chunk - 1 - s) * block_ch:(2 * chunk - 1 - s) * block_ch]
                   for s in range(chunk)], axis=1)
    m = m.reshape(nb, chunk * block_ch, chunk * block_ch).astype(BF16)
    ops = [w_op(akb_re), w_op(akb_im), m, e_op(e_re), e_op(-e_im)]
    return ops, a_re.reshape(nb, 1, gb * n), a_im.reshape(nb, 1, gb * n)


SSM_X_WIDTH = SSM_CHUNK * SSM_BLOCK_CH


def _ssm_prompt_kernel(u_ref, wre_ref, wim_ref, m_ref, etre_ref, etim_ref, are_ref, aim_ref, d_ref,
                       y_ref, hre_ref, him_ref, x_scr, sre_scr, sim_scr, *, batch, chunks):
    phase, b = pl.program_id(1), pl.program_id(2)
    for t in range(SSM_CHUNK):
        x_scr[:, t * SSM_BLOCK_CH:(t + 1) * SSM_BLOCK_CH] = u_ref[0, t].astype(BF16)
    rows = pl.ds(b, chunks, stride=batch)
    state_blocks = SSM_BLOCK_STATE // LANES

    @pl.when(phase == 0)
    def _():
        x = x_scr[...]
        g_re, g_im = _dot(x, wre_ref[0]), _dot(x, wim_ref[0])
        for k in range(state_blocks):
            sre_scr[k, rows, :] = g_re[:, k * LANES:(k + 1) * LANES]
            sim_scr[k, rows, :] = g_im[:, k * LANES:(k + 1) * LANES]

    @pl.when((phase == 1) & (b == 0))
    def _():
        same_lanes = lambda ref, k: ref[0][:, k * LANES:(k + 1) * LANES]

        def step(c, carry):
            same_chunk = pl.ds(c * batch, batch)
            out = []
            for k, (h_re, h_im) in enumerate(carry):
                a_re, a_im = same_lanes(are_ref, k), same_lanes(aim_ref, k)
                g_re, g_im = sre_scr[k, same_chunk, :], sim_scr[k, same_chunk, :]
                sre_scr[k, same_chunk, :] = h_re
                sim_scr[k, same_chunk, :] = h_im
                out.append((a_re * h_re - a_im * h_im + g_re, a_re * h_im + a_im * h_re + g_im))
            return tuple(out)

        zero = jnp.zeros((batch, LANES), F32)
        final = lax.fori_loop(0, chunks, step, ((zero, zero),) * state_blocks)
        for k, (h_re, h_im) in enumerate(final):
            hre_ref[0, :, k * LANES:(k + 1) * LANES] = h_re
            him_ref[0, :, k * LANES:(k + 1) * LANES] = h_im

    @pl.when(phase == 1)
    def _():
        h_re = jnp.concatenate([sre_scr[k, rows, :] for k in range(state_blocks)], axis=1).astype(BF16)
        h_im = jnp.concatenate([sim_scr[k, rows, :] for k in range(state_blocks)], axis=1).astype(BF16)
        pair_w = 2 * SSM_BLOCK_CH
        for j in range(SSM_CHUNK // 2):
            cols = slice(j * pair_w, (j + 1) * pair_w)
            k_in = (j + 1) * pair_w
            yj = (_dot(x_scr[:, :k_in], m_ref[0, :k_in, cols])
                  + _dot(h_re, etre_ref[0, :, cols]) + _dot(h_im, etim_ref[0, :, cols]))
            for i in range(2):
                t = 2 * j + i
                y_ref[pl.ds(t, chunks, stride=SSM_CHUNK), :] = (
                    yj[:, i * SSM_BLOCK_CH:(i + 1) * SSM_BLOCK_CH] + d_ref[0] * u_ref[0, t])


def _ssm_prompt(u, ops, a_re, a_im, d, batch, seq):
    chunks = seq // SSM_CHUNK
    d = d.reshape(N_SSM_BLOCKS, 1, SSM_BLOCK_CH)
    op_spec = lambda arr: pl.BlockSpec((1,) + arr.shape[1:], lambda g, ph, b: (g, 0, 0),
                                       pipeline_mode=pl.Buffered(1))
    state_spec = pl.BlockSpec((1, batch, SSM_BLOCK_STATE), lambda g, ph, b: (g, 0, 0))
    state_sds = jax.ShapeDtypeStruct((N_SSM_BLOCKS, batch, SSM_BLOCK_STATE), F32)
    y, h_re, h_im = pl.pallas_call(
        functools.partial(_ssm_prompt_kernel, batch=batch, chunks=chunks),
        grid=(N_SSM_BLOCKS, 2, batch),
        in_specs=[pl.BlockSpec((1, SSM_CHUNK, chunks, SSM_BLOCK_CH), lambda g, ph, b: (b, 0, 0, g)),
                  *[op_spec(o) for o in ops], op_spec(a_re), op_spec(a_im), op_spec(d)],
        out_specs=[pl.BlockSpec((seq, SSM_BLOCK_CH), lambda g, ph, b: (b * ph, g)), state_spec, state_spec],
        out_shape=[jax.ShapeDtypeStruct((batch * seq, SSM_WIDTH), F32), state_sds, state_sds],
        scratch_shapes=[pltpu.VMEM((chunks, SSM_X_WIDTH), BF16),
                        pltpu.VMEM((SSM_BLOCK_STATE // LANES, batch * chunks, LANES), F32),
                        pltpu.VMEM((SSM_BLOCK_STATE // LANES, batch * chunks, LANES), F32)],
        compiler_params=_compiler_params("arbitrary", "arbitrary", "arbitrary"),
        name="ssm_prompt",
    )(u, *ops, a_re, a_im, d)

    def by_sequence(h):
        h = h.reshape(N_SSM_BLOCKS, batch, SSM_BLOCK_GROUPS, SSM_STATE).transpose(1, 0, 2, 3)
        return h.reshape(batch, N_SSM_GROUPS, SSM_STATE)

    return y, by_sequence(h_re), by_sequence(h_im)


def _ssm_sample_kernel(u_ref, wre_ref, wim_ref, m_ref, etre_ref, etim_ref, are_ref, aim_ref, d_ref,
                       h0re_ref, h0im_ref, y_ref, hre_ref, him_ref, *, t_new, batch):
    for nb in range(N_SSM_BLOCKS):
        ch = slice(nb * SSM_BLOCK_CH, (nb + 1) * SSM_BLOCK_CH)
        st = slice(nb * SSM_BLOCK_STATE, (nb + 1) * SSM_BLOCK_STATE)
        x = jnp.concatenate([u_ref[t * batch:(t + 1) * batch, ch] for t in range(t_new)], axis=1).astype(BF16)
        h_re, h_im = h0re_ref[:, st], h0im_ref[:, st]
        a_re, a_im = are_ref[nb], aim_ref[nb]
        hre_ref[:, st] = a_re * h_re - a_im * h_im + _dot(x, wre_ref[nb])
        him_ref[:, st] = a_re * h_im + a_im * h_re + _dot(x, wim_ref[nb])
        y = (_dot(x, m_ref[nb]) + _dot(h_re.astype(BF16), etre_ref[nb])
             + _dot(h_im.astype(BF16), etim_ref[nb]))
        for t in range(t_new):
            rows = slice(t * batch, (t + 1) * batch)
            y_ref[rows, ch] = y[:, t * SSM_BLOCK_CH:(t + 1) * SSM_BLOCK_CH] + d_ref[:, ch] * u_ref[rows, ch]


def _ssm_sample(u, ops, a_re, a_im, d, h0_re, h0_im, t_new):
    batch = u.shape[0] // t_new
    args = (u, *ops, a_re, a_im, d, h0_re, h0_im)
    state_sds = jax.ShapeDtypeStruct((batch, N_SSM_GROUPS * SSM_STATE), F32)
    out_shape = [jax.ShapeDtypeStruct(u.shape, F32), state_sds, state_sds]
    whole = lambda shape: pl.BlockSpec(shape, lambda i, nd=len(shape): (0,) * nd)
    return pl.pallas_call(
        functools.partial(_ssm_sample_kernel, t_new=t_new, batch=batch),
        grid=(1,),
        in_specs=[whole(a.shape) for a in args],
        out_specs=[whole(s.shape) for s in out_shape],
        out_shape=out_shape,
        compiler_params=_compiler_params("arbitrary"),
        name="ssm_sample",
    )(*args)


def _gelu_tanh(x):
    return 0.5 * x * (1.0 + jnp.tanh(math.sqrt(2.0 / math.pi) * (x + 0.044715 * (x * x * x))))


def _merge_tile(o0_ref, l0_ref, o1_ref, l1_ref, o2_ref, l2_ref, ys_ref, gate_ref, x_ref,
                wglu_ref, bglu_ref, wba_ref, wbs_ref, wout_ref, order_scr, tm, dils):
    def row_order(ref, dil, slot):
        if dil == 1:
            return ref[0, 0]
        n = tm // dil
        halves = GROUP_WIDTH // LANES
        for r in range(dil):
            for k in range(halves):
                order_scr[slot * halves + k, pl.ds(r, n, stride=dil), :] = ref[0, r, :, k * LANES:(k + 1) * LANES]
        return jnp.concatenate([order_scr[slot * halves + k] for k in range(halves)], axis=1)

    parts = [row_order(ref, dils[i // 2], i) for i, ref in
             enumerate((o0_ref, l0_ref, o1_ref, l1_ref, o2_ref, l2_ref))]
    o0, l0, o1, l1, o2, l2 = parts
    mx = jnp.maximum(jnp.maximum(l0, l1), l2)
    e0, e1, e2 = jnp.exp(l0 - mx), jnp.exp(l1 - mx), jnp.exp(l2 - mx)
    attn = (e0 * o0 + e1 * o1 + e2 * o2) / (e0 + e1 + e2)
    branch_a = _dot(attn.astype(BF16), wba_ref[...])
    y = _gelu_tanh(ys_ref[...])
    y = y * _sigmoid(_dot(y.astype(BF16), wglu_ref[...]) + bglu_ref[...])
    branch_s = _dot(y.astype(BF16), wbs_ref[...])
    mix = (_sigmoid(gate_ref[:, 0:D_MODEL]) * branch_a
           + _sigmoid(gate_ref[:, D_MODEL:2 * D_MODEL]) * branch_s)
    return x_ref[...] + _dot(mix.astype(BF16), wout_ref[...])


FF_CHUNK = D_FF // 2
N_MERGE_ROW_INPUTS = 2 * N_DIL_GROUPS + 3
N_MERGE_WEIGHTS = 5


def _merge_ffn_kernel(*refs, tm, shift, pad, dils):
    n_merge = N_MERGE_ROW_INPUTS + N_MERGE_WEIGHTS
    merge_refs, rest = refs[:n_merge], refs[n_merge:]
    carry_ref, g2_ref, wup_ref, cw_ref, cb_ref, wdn_ref, gf_ref, y_ref, state_ref, order_scr, a_scr = rest
    hist = 2 * shift

    @pl.when(pl.program_id(1) == 0)
    def _():
        a_scr[pad - hist:pad, :] = carry_ref[0]

    xf = _merge_tile(*merge_refs, order_scr, tm, dils)
    xn = _rmsnorm(xf, g2_ref[...]).astype(BF16)
    acc = jnp.zeros((tm, D_MODEL), F32)
    for c0 in range(0, D_FF, FF_CHUNK):
        cols = slice(c0, c0 + FF_CHUNK)
        a = _dot(xn, wup_ref[:, cols])
        val = _dot(xn, wup_ref[:, D_FF + c0:D_FF + c0 + FF_CHUNK])
        a_scr[pad:pad + tm, cols] = a
        a_m1 = a_scr[pad - shift:pad - shift + tm, cols]
        a_m2 = a_scr[pad - hist:pad - hist + tm, cols]
        conv = cb_ref[:, cols] + cw_ref[0:1, cols] * a_m2
        conv = conv + cw_ref[1:2, cols] * a_m1
        conv = conv + cw_ref[2:3, cols] * a
        act = conv * _sigmoid(conv) * val
        acc = acc + _dot(act.astype(BF16), wdn_ref[cols, :])
    tail = a_scr[pad + tm - hist:pad + tm, :]
    a_scr[pad - hist:pad, :] = tail
    state_ref[0] = tail
    y_ref[...] = _rmsnorm(xf + acc, gf_ref[...])


def _merge_ffn(attn_parts, ys, gates, x2d, merge_weights, carry, ffn_weights, n_seq, tm, shift):
    m = x2d.shape[0]
    tiles = m // n_seq // tm
    hist = 2 * shift
    pad = -(-hist // SUBLANES) * SUBLANES
    dils = tuple(p.shape[1] for p in attn_parts[::2])
    row = lambda width: pl.BlockSpec((tm, width), lambda b, j: (b * tiles + j, 0))
    res_spec = lambda dil: pl.BlockSpec((1, dil, tm // dil, GROUP_WIDTH), lambda b, j: (b, 0, j, 0))
    state_spec = pl.BlockSpec((1, hist, D_FF), lambda b, j: (b, 0, 0))
    assert len(merge_weights) == N_MERGE_WEIGHTS
    return pl.pallas_call(
        functools.partial(_merge_ffn_kernel, tm=tm, shift=shift, pad=pad, dils=dils),
        grid=(n_seq, tiles),
        in_specs=[*[res_spec(p.shape[1]) for p in attn_parts],
                  row(SSM_WIDTH), row(2 * D_MODEL), row(D_MODEL),
                  *[_resident(w.shape) for w in merge_weights],
                  state_spec, *[_resident(w.shape) for w in ffn_weights]],
        out_specs=[row(D_MODEL), state_spec],
        out_shape=[jax.ShapeDtypeStruct((m, D_MODEL), F32),
                   jax.ShapeDtypeStruct((n_seq, hist, D_FF), F32)],
        scratch_shapes=[pltpu.VMEM((len(attn_parts) * GROUP_WIDTH // LANES, tm, LANES), F32),
                        pltpu.VMEM((pad + tm, D_FF), F32)],
        compiler_params=_compiler_params("arbitrary", "arbitrary"),
        name="merge_ffn",
    )(*attn_parts, ys, gates, x2d, *merge_weights, carry, *ffn_weights)


def _kv_rows(qkv, keep):
    batch, dil, length, _ = qkv.shape
    n = keep // dil
    rows = qkv[:, :, length - n:, GROUP_WIDTH:]
    cols = rows.transpose(0, 3, 2, 1).reshape(batch, 2, HEADS_PER_GROUP, HEAD_DIM, keep)
    return cols.transpose(0, 4, 1, 2, 3)


def _prompt_layer(x, rel_bias, lw):
    batch, seq, _ = x.shape
    x2d = x.reshape(batch * seq, D_MODEL)
    dils = tuple(dil for _, dil in DIL_PATTERNS) + (SSM_CHUNK,)
    *qkvs, u, gates = _in_proj(x2d, lw['norm1_g'], lw['w_in'], batch, IN_PROJ_TILE, dils)

    attn_parts, kv_new = [], []
    for g, (window, dil) in enumerate(DIL_PATTERNS):
        tab = rel_bias[:, g * HEADS_PER_GROUP:(g + 1) * HEADS_PER_GROUP]
        attn_parts.extend(_attn_prompt(qkvs[g], _prompt_bias(tab, dil)))
        kv_new.append(_kv_rows(qkvs[g], min(window, seq)))

    ops, a_re, a_im = _ssm_block_operators(SSM_CHUNK, *lw['ssm'])
    ys, h_re, h_im = _ssm_prompt(u, ops, a_re, a_im, lw['ssm_d'], batch, seq)

    carry = jnp.zeros((batch, CONV_W - 1, D_FF), F32)
    y, conv_state = _merge_ffn(attn_parts, ys, gates, x2d, lw['merge'], carry, lw['ffn'],
                               n_seq=batch, tm=MERGE_FFN_TILE, shift=1)
    return y.reshape(batch, seq, D_MODEL), (*kv_new, h_re, h_im, conv_state)


def _sample_layer(x, caches, h0_re, h0_im, conv_buf, rel_bias, lw):
    batch, t_new, _ = x.shape
    m = batch * t_new
    x2d = x.transpose(1, 0, 2).reshape(m, D_MODEL)
    *qkvs, u, gates = _in_proj(x2d, lw['norm1_g'], lw['w_in'], 1, m, (1,) * (N_DIL_GROUPS + 1))
    u = u.reshape(m, SSM_WIDTH)
    qkvs_bt = [q.reshape(t_new, batch, GROUP_QKV).transpose(1, 0, 2) for q in qkvs]

    tbs, tns, views = [], [], []
    for g, (window, dil) in enumerate(DIL_PATTERNS):
        tab = rel_bias[:, g * HEADS_PER_GROUP:(g + 1) * HEADS_PER_GROUP]
        n_cached = caches[g].shape[1]
        tb, tn = _sample_bias(tab, dil, t_new, n_cached)
        tbs.append(tb)
        tns.append(tn)
        views.append(caches[g].transpose(0, 2, 3, 4, 1).reshape(batch, 2 * GROUP_WIDTH, n_cached))
    parts = _attn_sample(qkvs_bt, views, tbs, jnp.stack(tns))
    attn_parts = [p.transpose(1, 0, 2).reshape(1, 1, m, GROUP_WIDTH) for p in parts]
    kv_new = [_kv_rows(q.reshape(batch, 1, t_new, GROUP_QKV), t_new) for q in qkvs_bt]

    ops, a_re, a_im = _ssm_block_operators(t_new, *lw['ssm'])
    ys, h_re, h_im = _ssm_sample(u, ops, a_re, a_im, lw['ssm_d'],
                                 h0_re.reshape(batch, -1), h0_im.reshape(batch, -1), t_new)

    carry = conv_buf.transpose(1, 0, 2).reshape(1, (CONV_W - 1) * batch, D_FF)
    y, conv_state = _merge_ffn(attn_parts, ys, gates, x2d, lw['merge'], carry, lw['ffn'],
                               n_seq=1, tm=m, shift=batch)
    y = y.reshape(t_new, batch, D_MODEL).transpose(1, 0, 2)
    conv_state = conv_state.reshape(CONV_W - 1, batch, D_FF).transpose(1, 0, 2)
    state_shape = (batch, N_SSM_GROUPS, SSM_STATE)
    return y, (*kv_new, h_re.reshape(state_shape), h_im.reshape(state_shape), conv_state)


IN_PROJ_TILE = 512
MERGE_FFN_TILE = 256


def kernel(x_prompt, x_sample, cache_kv_w128, cache_kv_w512, cache_kv_w2048, state_ssm_re, state_ssm_im, state_ffn_conv, rel_bias, norm1_g, w_in, ssm_log_dt, ssm_lambda_re, ssm_lambda_im, ssm_b_re, ssm_b_im, ssm_c_re, ssm_c_im, ssm_d, w_glu, b_glu, w_branch_attn, w_branch_ssm, w_out, norm2_g, w_up, conv_w, conv_b, w_down, norm_f_g):
    depth = w_in.shape[0]
    hp, hs = x_prompt, x_sample
    st_p, st_s = [], []
    gf = norm_f_g.reshape(1, D_MODEL)
    for l in range(depth):
        last = l == depth - 1
        lw = {
            'norm1_g': norm1_g[l].reshape(1, D_MODEL),
            'w_in': _group_major_columns(w_in[l]).astype(BF16),
            'ssm': (ssm_log_dt[l], ssm_lambda_re[l], ssm_lambda_im[l],
                    ssm_b_re[l], ssm_b_im[l], ssm_c_re[l], ssm_c_im[l]),
            'ssm_d': ssm_d[l].reshape(1, SSM_WIDTH),
            'merge': (w_glu[l].astype(BF16), b_glu[l].reshape(1, SSM_WIDTH),
                      w_branch_attn[l].astype(BF16), w_branch_ssm[l].astype(BF16), w_out[l].astype(BF16)),
            'ffn': (norm2_g[l].reshape(1, D_MODEL), w_up[l].astype(BF16), conv_w[l],
                    conv_b[l].reshape(1, D_FF), w_down[l].astype(BF16), gf),
        }
        assert last, "the final RMSNorm is fused into the last layer's ffn kernel"
        hp, sp = _prompt_layer(hp, rel_bias, lw)
        hs, ss = _sample_layer(hs, (cache_kv_w128[l], cache_kv_w512[l], cache_kv_w2048[l]),
                               state_ssm_re[l], state_ssm_im[l], state_ffn_conv[l], rel_bias, lw)
        st_p.append(sp)
        st_s.append(ss)
    stack = lambda states, i: jnp.stack([st[i] for st in states], axis=0)
    return (hp, hs, *[stack(st_p, i) for i in range(6)], *[stack(st_s, i) for i in range(6)])
```

```python
import functools
import math

import jax
import jax.numpy as jnp
from jax import lax
from jax.experimental import pallas as pl
from jax.experimental.pallas import tpu as pltpu

F32 = jnp.float32
BF16 = jnp.bfloat16

D_MODEL = 1024
HEAD_DIM = 64
HEADS_PER_GROUP = 4
DIL_PATTERNS = ((128, 1), (512, 4), (2048, 16))
N_DIL_GROUPS = len(DIL_PATTERNS)
GROUP_WIDTH = HEADS_PER_GROUP * HEAD_DIM
QK_WIDTH = N_DIL_GROUPS * GROUP_WIDTH
QKV_WIDTH = 3 * QK_WIDTH
GROUP_QKV = 3 * GROUP_WIDTH
QBLOCK = 128
SSM_GROUP = 16
SSM_STATE = 64
SSM_WIDTH = D_MODEL // 2
N_SSM_GROUPS = SSM_WIDTH // SSM_GROUP
SSM_CHUNK = 16
D_FF = 2816
CONV_W = 3
N_BUCKETS = 32
MAX_DISTANCE = 2048
NORM_EPS = 1e-6
NEG_INF = -1e30
U_START = QKV_WIDTH
GATE_START = U_START + SSM_WIDTH
IN_WIDTH = GATE_START + 2 * D_MODEL
QK_SCALE = HEAD_DIM ** -0.5

VMEM_LIMIT_BYTES = 56 * 1024 * 1024
SUBLANES = 8
LANES = 128


def _compiler_params(*semantics):
    return pltpu.CompilerParams(dimension_semantics=semantics, vmem_limit_bytes=VMEM_LIMIT_BYTES)


def _resident(shape):
    nd = len(shape)
    return pl.BlockSpec(shape, lambda *_: (0,) * nd, pipeline_mode=pl.Buffered(1))


def _rmsnorm(xf, g):
    y = xf * lax.rsqrt(jnp.mean(xf * xf, axis=-1, keepdims=True) + NORM_EPS)
    return y * g


def _sigmoid(x):
    return 1.0 / (1.0 + jnp.exp(-x))


def _dot(a, b):
    return jnp.dot(a, b, preferred_element_type=F32)


def _dot_nt(a, b):
    return lax.dot_general(a, b, (((1,), (1,)), ((), ())), preferred_element_type=F32)


def _in_proj_kernel(x_ref, g_ref, w_ref, qkv0_ref, qkv1_ref, qkv2_ref, u_ref, gate_ref, *rest,
                    tm, dils, tails, tiles):
    if tails is None:
        tail_refs, (xn_scr,) = (), rest
    else:
        tail_refs, (xn_scr, tok_scr) = rest[:N_DIL_GROUPS], rest[N_DIL_GROUPS:]
    xf = _rmsnorm(x_ref[...], g_ref[...])
    n_lane_blocks = D_MODEL // LANES
    if any(dil > 1 for dil in dils):
        for k in range(n_lane_blocks):
            xn_scr[k] = xf[:, k * LANES:(k + 1) * LANES]
    by_residue = {1: xf.astype(BF16)}

    def rows_by_residue(dil):
        if dil not in by_residue:
            n = tm // dil
            xr = jnp.concatenate(
                [jnp.concatenate([xn_scr[k, pl.ds(r, n, stride=dil), :] for k in range(n_lane_blocks)], axis=1)
                 for r in range(dil)], axis=0)
            by_residue[dil] = xr.astype(BF16)
        return by_residue[dil]

    outputs = [(ref, g * GROUP_QKV, GROUP_QKV) for g, ref in enumerate((qkv0_ref, qkv1_ref, qkv2_ref))]
    outputs.append((u_ref, U_START, SSM_WIDTH))
    kv_blocks = 2 * GROUP_WIDTH // LANES
    for i, ((ref, start, width), dil) in enumerate(zip(outputs, dils)):
        n = tm // dil
        res = _dot(rows_by_residue(dil), w_ref[:, start:start + width])
        for r in range(dil):
            ref[0, r] = res[r * n:(r + 1) * n]
        if tails is not None and i < N_DIL_GROUPS:
            kept = min(tails[i], tm)

            @pl.when(pl.program_id(1) >= tiles - max(tails[i] // tm, 1))
            def _(res=res, dil=dil, n=n, kept=kept, tail_ref=tail_refs[i]):
                if dil == 1:
                    kv = res[:, GROUP_WIDTH:]
                else:
                    for r in range(dil):
                        for k in range(kv_blocks):
                            lanes = slice(GROUP_WIDTH + k * LANES, GROUP_WIDTH + (k + 1) * LANES)
                            tok_scr[k, pl.ds(r, n, stride=dil), :] = res[r * n:(r + 1) * n, lanes]
                    kv = jnp.concatenate([tok_scr[k] for k in range(kv_blocks)], axis=1)
                tail_ref[0] = kv[tm - kept:, :].T
    for c0 in range(0, 2 * D_MODEL, D_MODEL):
        gate_ref[:, c0:c0 + D_MODEL] = _dot(by_residue[1], w_ref[:, GATE_START + c0:GATE_START + c0 + D_MODEL])


def _group_major_columns(w_in):
    parts = []
    for g in range(N_DIL_GROUPS):
        for base in (0, QK_WIDTH, 2 * QK_WIDTH):
            parts.append(w_in[:, base + g * GROUP_WIDTH:base + (g + 1) * GROUP_WIDTH])
    parts.append(w_in[:, U_START:])
    return jnp.concatenate(parts, axis=1)


def _in_proj(x2d, g, w_bf16, n_seq, tm, dils, tails=None):
    m = x2d.shape[0]
    seq = m // n_seq
    tiles = seq // tm
    row = lambda width: pl.BlockSpec((tm, width), lambda b, j: (b * tiles + j, 0))
    res_spec = lambda dil, width: pl.BlockSpec((1, dil, tm // dil, width), lambda b, j: (b, 0, j, 0))
    widths = (GROUP_QKV,) * N_DIL_GROUPS + (SSM_WIDTH,)
    out_specs = [*[res_spec(d, w) for d, w in zip(dils, widths)], row(2 * D_MODEL)]
    out_shape = [*[jax.ShapeDtypeStruct((n_seq, d, seq // d, w), F32) for d, w in zip(dils, widths)],
                 jax.ShapeDtypeStruct((m, 2 * D_MODEL), F32)]
    scratch = [pltpu.VMEM((D_MODEL // LANES, tm, LANES), F32)]
    if tails is not None:
        for keep in tails:
            assert keep % tm == 0 or tm % keep == 0
            first = tiles - max(keep // tm, 1)
            out_specs.append(pl.BlockSpec((1, 2 * GROUP_WIDTH, min(keep, tm)),
                                          lambda b, j, first=first: (b, 0, jnp.maximum(j - first, 0))))
            out_shape.append(jax.ShapeDtypeStruct((n_seq, 2 * GROUP_WIDTH, keep), F32))
        scratch.append(pltpu.VMEM((2 * GROUP_WIDTH // LANES, tm, LANES), F32))
    return pl.pallas_call(
        functools.partial(_in_proj_kernel, tm=tm, dils=dils, tails=tails, tiles=tiles),
        grid=(n_seq, tiles),
        in_specs=[row(D_MODEL), _resident((1, D_MODEL)), _resident((D_MODEL, IN_WIDTH))],
        out_specs=out_specs,
        out_shape=out_shape,
        scratch_shapes=scratch,
        compiler_params=_compiler_params("arbitrary", "arbitrary"),
        name="in_proj",
    )(x2d, g, w_bf16)


def _rel_bucket(dist):
    max_exact = N_BUCKETS // 2
    n = jnp.maximum(dist, 0)
    nf = jnp.maximum(n, 1).astype(F32)
    large = max_exact + (jnp.log(nf / max_exact) / math.log(MAX_DISTANCE / max_exact)
                         * (N_BUCKETS - max_exact)).astype(jnp.int32)
    large = jnp.minimum(large, N_BUCKETS - 1)
    return jnp.where(n < max_exact, n, large)


def _masked_bias(tab, strides, valid, dil):
    n_dist = QBLOCK
    bucket = _rel_bucket(jnp.clip(strides, 0, n_dist) * dil).reshape(1, -1)
    onehot = (bucket == jnp.arange(N_BUCKETS)[:, None]).astype(F32)
    bias = jnp.dot(tab.astype(F32).T, onehot, precision=lax.Precision.HIGHEST)
    bias = jnp.where(valid.reshape(1, -1), bias, NEG_INF)
    return bias.reshape((tab.shape[1],) + strides.shape)


def _prompt_bias(tab, dil):
    qi = jnp.arange(QBLOCK)[:, None]
    ki = jnp.arange(QBLOCK)[None, :]
    j_prev = qi + QBLOCK - ki
    j_cur = qi - ki
    bias = jnp.stack([_masked_bias(tab, j_prev, j_prev <= QBLOCK, dil),
                      _masked_bias(tab, j_cur, j_cur >= 0, dil)], axis=0)
    return bias.transpose(0, 2, 1, 3).reshape(2, QBLOCK, HEADS_PER_GROUP * QBLOCK)


def _sample_bias(tab, dil, t_new, n_cached):
    t = jnp.arange(t_new)[:, None]
    delta_buf = n_cached + t - jnp.arange(n_cached)[None, :]
    ok_buf = (delta_buf % dil == 0) & (delta_buf // dil <= QBLOCK)
    m = jnp.arange(QBLOCK)[None, :]
    delta_new = t - m
    ok_new = (m < t_new) & (delta_new >= 0) & (delta_new % dil == 0)
    b_buf = _masked_bias(tab, delta_buf // dil, ok_buf, dil)
    b_new = _masked_bias(tab, delta_new // dil, ok_new, dil)
    return (b_buf.reshape(HEADS_PER_GROUP * t_new, n_cached),
            b_new.reshape(HEADS_PER_GROUP * t_new, QBLOCK))


def _attn_prompt_kernel(q_ref, kc_ref, vc_ref, bias_ref, o_ref, lse_ref, kprev_scr, vprev_scr, *, nq):
    first_tile = pl.program_id(2) == 0

    @pl.when(first_tile)
    def _():
        kprev_scr[...] = jnp.zeros_like(kprev_scr)
        vprev_scr[...] = jnp.zeros_like(vprev_scr)

    stacked = (HEADS_PER_GROUP * QBLOCK, GROUP_WIDTH)
    own_head = (lax.broadcasted_iota(jnp.int32, stacked, 0) // QBLOCK
                == lax.broadcasted_iota(jnp.int32, stacked, 1) // HEAD_DIM)
    lane_head = lax.broadcasted_iota(jnp.int32, (QBLOCK, GROUP_WIDTH), 1) // HEAD_DIM

    def per_head(x):
        xb = x.astype(BF16)
        return jnp.where(own_head, jnp.concatenate([xb] * HEADS_PER_GROUP, axis=0), 0)

    def on_head_lanes(cols):
        out = jnp.broadcast_to(cols[-1], (QBLOCK, GROUP_WIDTH))
        for h in range(HEADS_PER_GROUP - 2, -1, -1):
            out = jnp.where(lane_head == h, cols[h], out)
        return out

    k_prev, v_prev = kprev_scr[...], vprev_scr[...]
    for i in range(nq):
        rows = slice(i * QBLOCK, (i + 1) * QBLOCK)
        q = (q_ref[0, 0, rows, :] * QK_SCALE).astype(BF16)
        k_cur, v_cur = per_head(kc_ref[0, 0, rows, :]), per_head(vc_ref[0, 0, rows, :])
        s_prev = _dot_nt(q, k_prev) + bias_ref[0]
        if i == 0:
            s_prev = jnp.where(first_tile, NEG_INF, s_prev)
        s_cur = _dot_nt(q, k_cur) + bias_ref[1]
        p_prev, p_cur, dens, lses = [], [], [], []
        for h in range(HEADS_PER_GROUP):
            keys = slice(h * QBLOCK, (h + 1) * QBLOCK)
            sp, sc = s_prev[:, keys], s_cur[:, keys]
            m = jnp.max(jnp.maximum(sp, sc), axis=-1, keepdims=True)
            pp, pc = jnp.exp(sp - m), jnp.exp(sc - m)
            den = jnp.sum(pp + pc, axis=-1, keepdims=True)
            p_prev.append(pp.astype(BF16))
            p_cur.append(pc.astype(BF16))
            dens.append(den)
            lses.append(m + jnp.log(den))
        o = _dot(jnp.concatenate(p_prev, axis=1), v_prev) + _dot(jnp.concatenate(p_cur, axis=1), v_cur)
        o_ref[0, 0, rows, :] = o / on_head_lanes(dens)
        lse_ref[0, 0, rows, :] = on_head_lanes(lses)
        k_prev, v_prev = k_cur, v_cur
    kprev_scr[...] = k_prev
    vprev_scr[...] = v_prev


def _attn_prompt(qkv, bias):
    batch, dil, length, _ = qkv.shape
    tq = min(4 * QBLOCK, length)
    nq = tq // QBLOCK

    def cur(col):
        return pl.BlockSpec((1, 1, tq, GROUP_WIDTH), lambda b, r, n: (b, r, n, col))

    stacked = pltpu.VMEM((HEADS_PER_GROUP * QBLOCK, GROUP_WIDTH), BF16)
    out_sds = jax.ShapeDtypeStruct((batch, dil, length, GROUP_WIDTH), F32)
    return pl.pallas_call(
        functools.partial(_attn_prompt_kernel, nq=nq),
        grid=(batch, dil, length // tq),
        in_specs=[cur(0), cur(1), cur(2), _resident((2, QBLOCK, HEADS_PER_GROUP * QBLOCK))],
        out_specs=[cur(0), cur(0)],
        out_shape=[out_sds, out_sds],
        scratch_shapes=[stacked, stacked],
        compiler_params=_compiler_params("arbitrary", "arbitrary", "arbitrary"),
        name=f"attn_prompt_d{dil}",
    )(qkv, qkv, qkv, bias)


def _attn_sample_kernel(q0_ref, q1_ref, q2_ref, c0_ref, c1_ref, c2_ref, tb0_ref, tb1_ref, tb2_ref, tn_ref,
                        o0_ref, l0_ref, o1_ref, l1_ref, o2_ref, l2_ref, kn_scr, vn_scr, *, t_new):
    n_rows = HEADS_PER_GROUP * t_new
    row_w = lax.broadcasted_iota(jnp.int32, (n_rows, GROUP_WIDTH), 0)
    lane_w = lax.broadcasted_iota(jnp.int32, (n_rows, GROUP_WIDTH), 1)
    own_head = (row_w // t_new) == (lane_w // HEAD_DIM)

    def fold_heads(x):
        x = jnp.where(own_head, x, 0.0)
        out = x[0:t_new]
        for h in range(1, HEADS_PER_GROUP):
            out = out + x[h * t_new:(h + 1) * t_new]
        return out

    caches = (c0_ref, c1_ref, c2_ref)
    cache_bias = (tb0_ref, tb1_ref, tb2_ref)
    outs = ((o0_ref, l0_ref), (o1_ref, l1_ref), (o2_ref, l2_ref))
    for g in range(N_DIL_GROUPS):
        qkv_ref = (q0_ref, q1_ref, q2_ref)[g]
        q = qkv_ref[0, :, 0:GROUP_WIDTH] * QK_SCALE
        q_rows = jnp.where(own_head, jnp.concatenate([q] * HEADS_PER_GROUP, axis=0), 0.0).astype(BF16)
        kn_scr[...] = jnp.zeros_like(kn_scr)
        vn_scr[...] = jnp.zeros_like(vn_scr)
        kn_scr[0:t_new, :] = qkv_ref[0, :, GROUP_WIDTH:2 * GROUP_WIDTH]
        vn_scr[0:t_new, :] = qkv_ref[0, :, 2 * GROUP_WIDTH:3 * GROUP_WIDTH]
        cache = caches[g]
        k_t = cache[0, 0:GROUP_WIDTH, :].astype(BF16)
        v_t = cache[0, GROUP_WIDTH:2 * GROUP_WIDTH, :].astype(BF16)
        s_buf = _dot(q_rows, k_t) + cache_bias[g][...]
        s_new = _dot_nt(q_rows, kn_scr[...].astype(BF16)) + tn_ref[g]
        m = jnp.maximum(jnp.max(s_buf, axis=-1, keepdims=True), jnp.max(s_new, axis=-1, keepdims=True))
        p_buf = jnp.exp(s_buf - m)
        p_new = jnp.exp(s_new - m)
        den = jnp.sum(p_buf, axis=-1, keepdims=True) + jnp.sum(p_new, axis=-1, keepdims=True)
        o = _dot_nt(p_buf.astype(BF16), v_t) + _dot(p_new.astype(BF16), vn_scr[...].astype(BF16))
        o_ref, l_ref = outs[g]
        o_ref[0] = fold_heads(o / den)
        l_ref[0] = fold_heads(jnp.broadcast_to(m + jnp.log(den), (n_rows, GROUP_WIDTH)))


def _attn_sample(qkvs, caches, tbs, tn):
    batch, t_new, _ = qkvs[0].shape
    n_rows = HEADS_PER_GROUP * t_new
    cache_specs = [pl.BlockSpec((1,) + c.shape[1:], lambda b: (b, 0, 0)) for c in caches]
    qkv_spec = pl.BlockSpec((1, t_new, GROUP_QKV), lambda b: (b, 0, 0))
    out_spec = pl.BlockSpec((1, t_new, GROUP_WIDTH), lambda b: (b, 0, 0))
    out_sds = jax.ShapeDtypeStruct((batch, t_new, GROUP_WIDTH), F32)
    return pl.pallas_call(
        functools.partial(_attn_sample_kernel, t_new=t_new),
        grid=(batch,),
        in_specs=[*[qkv_spec] * N_DIL_GROUPS, *cache_specs, *[_resident(t.shape) for t in tbs],
                  _resident((N_DIL_GROUPS, n_rows, QBLOCK))],
        out_specs=[out_spec] * (2 * N_DIL_GROUPS),
        out_shape=[out_sds] * (2 * N_DIL_GROUPS),
        scratch_shapes=[pltpu.VMEM((QBLOCK, GROUP_WIDTH), F32), pltpu.VMEM((QBLOCK, GROUP_WIDTH), F32)],
        compiler_params=_compiler_params("arbitrary"),
        name="attn_sample",
    )(*qkvs, *caches, *tbs, tn)


def _ssm_chunk_terms(chunk, log_dt, lam_re, lam_im, b_re, b_im, c_re, c_im):
    hi = lax.Precision.HIGHEST
    dt = jnp.exp(log_dt.astype(F32))[:, None]
    lr, li = lam_re.astype(F32), lam_im.astype(F32)
    mag = jnp.exp(lr * dt)
    ab_re, ab_im = mag * jnp.cos(li * dt), mag * jnp.sin(li * dt)
    g, n = lr.shape
    p = b_re.shape[-1]
    den = lr * lr + li * li
    nr, ni = ab_re - 1.0, ab_im
    coef_re = (nr * lr + ni * li) / den
    coef_im = (ni * lr - nr * li) / den
    br, bi = b_re.astype(F32), b_im.astype(F32)
    bb_re = coef_re[..., None] * br - coef_im[..., None] * bi
    bb_im = coef_re[..., None] * bi + coef_im[..., None] * br
    k = jnp.arange(chunk + 1, dtype=F32)[:, None, None]
    pw_mag = jnp.exp(k * (lr * dt)[None])
    pw_re, pw_im = pw_mag * jnp.cos(k * (li * dt)[None]), pw_mag * jnp.sin(k * (li * dt)[None])
    bt_re, bt_im = bb_re.transpose(0, 2, 1)[None], bb_im.transpose(0, 2, 1)[None]
    pk_re, pk_im = pw_re[:chunk, :, None, :], pw_im[:chunk, :, None, :]
    akb_re = pk_re * bt_re - pk_im * bt_im
    akb_im = pk_re * bt_im + pk_im * bt_re
    cr, ci = c_re.astype(F32)[None], c_im.astype(F32)[None]
    pe_re, pe_im = pw_re[1:, :, None, :], pw_im[1:, :, None, :]
    e_re = cr * pe_re - ci * pe_im
    e_im = cr * pe_im + ci * pe_re
    kern = (jnp.einsum('kgqn,gpn->kgqp', akb_re, cr[0], precision=hi)
            - jnp.einsum('kgqn,gpn->kgqp', akb_im, ci[0], precision=hi))
    rows = lambda x: x.reshape(chunk, g * p, x.shape[-1])
    return ((rows(akb_re), rows(akb_im)), (rows(e_re), rows(e_im)), rows(kern),
            (pw_re[chunk], pw_im[chunk]))


SSM_BLOCK_GROUPS = 8
SSM_BLOCK_CH = SSM_BLOCK_GROUPS * SSM_GROUP
SSM_BLOCK_STATE = SSM_BLOCK_GROUPS * SSM_STATE
N_SSM_BLOCKS = N_SSM_GROUPS // SSM_BLOCK_GROUPS


def _ssm_operator_kernel(akbre_ref, akbim_ref, ere_ref, eim_ref, kern_ref,
                         wre_ref, wim_ref, m_ref, etre_ref, etim_ref, *, chunk):
    gb, p, n = SSM_BLOCK_GROUPS, SSM_GROUP, SSM_STATE

    def copies(width, count):
        src = lax.broadcasted_iota(jnp.int32, (width, count * width), 0)
        dst = lax.broadcasted_iota(jnp.int32, (width, count * width), 1)
        return jnp.where(src == dst % width, 1.0, 0.0).astype(BF16)

    def own_group(width):
        row = lax.broadcasted_iota(jnp.int32, (SSM_BLOCK_CH, gb * width), 0)
        col = lax.broadcasted_iota(jnp.int32, (SSM_BLOCK_CH, gb * width), 1)
        return row // p == col // width

    to_states, own_states = copies(n, gb), own_group(n)

    def over_states(piece):
        return jnp.where(own_states, _dot(piece.astype(BF16), to_states), 0.0).astype(BF16)

    for s in range(chunk):
        rows = slice(s * SSM_BLOCK_CH, (s + 1) * SSM_BLOCK_CH)
        wre_ref[0, rows, :] = over_states(akbre_ref[chunk - 1 - s, 0])
        wim_ref[0, rows, :] = over_states(akbim_ref[chunk - 1 - s, 0])
        etre_ref[0, rows, :] = over_states(ere_ref[s, 0])
        etim_ref[0, rows, :] = over_states(-eim_ref[s, 0])
    to_channels, own_channels = copies(p, gb), own_group(p)
    lags = [jnp.where(own_channels, _dot(kern_ref[k, 0].astype(BF16), to_channels), 0.0).astype(BF16)
            for k in range(chunk)]
    zero = jnp.zeros((SSM_BLOCK_CH, SSM_BLOCK_CH), BF16)
    for s in range(chunk):
        m_ref[0, s * SSM_BLOCK_CH:(s + 1) * SSM_BLOCK_CH, :] = jnp.concatenate(
            [zero] * s + lags[:chunk - s], axis=1)


def _ssm_block_operators(chunk, *params):
    (akb_re, akb_im), (e_re, e_im), kern, (a_re, a_im) = _ssm_chunk_terms(chunk, *params)
    nb, gb, p, n = N_SSM_BLOCKS, SSM_BLOCK_GROUPS, SSM_GROUP, SSM_STATE
    pieces = [x.reshape(chunk, nb, SSM_BLOCK_CH, x.shape[-1]) for x in (akb_re, akb_im, e_re, e_im, kern)]
    piece_spec = lambda x: pl.BlockSpec((chunk, 1) + x.shape[2:], lambda i: (0, i, 0, 0))
    out_spec = lambda shape: pl.BlockSpec((1,) + shape[1:], lambda i: (i, 0, 0))
    x_width = chunk * SSM_BLOCK_CH
    out_shapes = [(nb, x_width, gb * n), (nb, x_width, gb * n), (nb, x_width, x_width),
                  (nb, x_width, gb * n), (nb, x_width, gb * n)]
    ops = pl.pallas_call(
        functools.partial(_ssm_operator_kernel, chunk=chunk),
        grid=(nb,),
        in_specs=[piece_spec(x) for x in pieces],
        out_specs=[out_spec(s) for s in out_shapes],
        out_shape=[jax.ShapeDtypeStruct(s, BF16) for s in out_shapes],
        compiler_params=_compiler_params("arbitrary"),
        name="ssm_operators",
    )(*pieces)
    return ops, a_re.reshape(nb, 1, gb * n), a_im.reshape(nb, 1, gb * n)


SSM_X_WIDTH = SSM_CHUNK * SSM_BLOCK_CH


def _ssm_prompt_kernel(u_ref, wre_ref, wim_ref, m_ref, etre_ref, etim_ref, are_ref, aim_ref, d_ref,
                       y_ref, hre_ref, him_ref, x_scr, sre_scr, sim_scr, *, batch, chunks):
    phase, b = pl.program_id(1), pl.program_id(2)
    for t in range(SSM_CHUNK):
        x_scr[:, t * SSM_BLOCK_CH:(t + 1) * SSM_BLOCK_CH] = u_ref[0, t].astype(BF16)
    rows = pl.ds(b, chunks, stride=batch)
    state_blocks = SSM_BLOCK_STATE // LANES

    @pl.when(phase == 0)
    def _():
        x = x_scr[...]
        g_re, g_im = _dot(x, wre_ref[0]), _dot(x, wim_ref[0])
        for k in range(state_blocks):
            sre_scr[k, rows, :] = g_re[:, k * LANES:(k + 1) * LANES]
            sim_scr[k, rows, :] = g_im[:, k * LANES:(k + 1) * LANES]

    @pl.when((phase == 1) & (b == 0))
    def _():
        same_lanes = lambda ref, k: ref[0][:, k * LANES:(k + 1) * LANES]

        def step(c, carry):
            same_chunk = pl.ds(c * batch, batch)
            out = []
            for k, (h_re, h_im) in enumerate(carry):
                a_re, a_im = same_lanes(are_ref, k), same_lanes(aim_ref, k)
                g_re, g_im = sre_scr[k, same_chunk, :], sim_scr[k, same_chunk, :]
                sre_scr[k, same_chunk, :] = h_re
                sim_scr[k, same_chunk, :] = h_im
                out.append((a_re * h_re - a_im * h_im + g_re, a_re * h_im + a_im * h_re + g_im))
            return tuple(out)

        zero = jnp.zeros((batch, LANES), F32)
        final = lax.fori_loop(0, chunks, step, ((zero, zero),) * state_blocks)
        for k, (h_re, h_im) in enumerate(final):
            hre_ref[0, :, k * LANES:(k + 1) * LANES] = h_re
            him_ref[0, :, k * LANES:(k + 1) * LANES] = h_im

    @pl.when(phase == 1)
    def _():
        h_re = jnp.concatenate([sre_scr[k, rows, :] for k in range(state_blocks)], axis=1).astype(BF16)
        h_im = jnp.concatenate([sim_scr[k, rows, :] for k in range(state_blocks)], axis=1).astype(BF16)
        pair_w = 2 * SSM_BLOCK_CH
        for j in range(SSM_CHUNK // 2):
            cols = slice(j * pair_w, (j + 1) * pair_w)
            k_in = (j + 1) * pair_w
            yj = (_dot(x_scr[:, :k_in], m_ref[0, :k_in, cols])
                  + _dot_nt(h_re, etre_ref[0, cols, :]) + _dot_nt(h_im, etim_ref[0, cols, :]))
            for i in range(2):
                t = 2 * j + i
                y_ref[pl.ds(t, chunks, stride=SSM_CHUNK), :] = (
                    yj[:, i * SSM_BLOCK_CH:(i + 1) * SSM_BLOCK_CH] + d_ref[0] * u_ref[0, t])


def _ssm_prompt(u, ops, a_re, a_im, d, batch, seq):
    chunks = seq // SSM_CHUNK
    d = d.reshape(N_SSM_BLOCKS, 1, SSM_BLOCK_CH)
    op_spec = lambda arr: pl.BlockSpec((1,) + arr.shape[1:], lambda g, ph, b: (g, 0, 0),
                                       pipeline_mode=pl.Buffered(1))
    state_spec = pl.BlockSpec((1, batch, SSM_BLOCK_STATE), lambda g, ph, b: (g, 0, 0))
    state_sds = jax.ShapeDtypeStruct((N_SSM_BLOCKS, batch, SSM_BLOCK_STATE), F32)
    y, h_re, h_im = pl.pallas_call(
        functools.partial(_ssm_prompt_kernel, batch=batch, chunks=chunks),
        grid=(N_SSM_BLOCKS, 2, batch),
        in_specs=[pl.BlockSpec((1, SSM_CHUNK, chunks, SSM_BLOCK_CH), lambda g, ph, b: (b, 0, 0, g)),
                  *[op_spec(o) for o in ops], op_spec(a_re), op_spec(a_im), op_spec(d)],
        out_specs=[pl.BlockSpec((seq, SSM_BLOCK_CH), lambda g, ph, b: (b * ph, g)), state_spec, state_spec],
        out_shape=[jax.ShapeDtypeStruct((batch * seq, SSM_WIDTH), F32), state_sds, state_sds],
        scratch_shapes=[pltpu.VMEM((chunks, SSM_X_WIDTH), BF16),
                        pltpu.VMEM((SSM_BLOCK_STATE // LANES, batch * chunks, LANES), F32),
                        pltpu.VMEM((SSM_BLOCK_STATE // LANES, batch * chunks, LANES), F32)],
        compiler_params=_compiler_params("arbitrary", "arbitrary", "arbitrary"),
        name="ssm_prompt",
    )(u, *ops, a_re, a_im, d)

    def by_sequence(h):
        h = h.reshape(N_SSM_BLOCKS, batch, SSM_BLOCK_GROUPS, SSM_STATE).transpose(1, 0, 2, 3)
        return h.reshape(batch, N_SSM_GROUPS, SSM_STATE)

    return y, by_sequence(h_re), by_sequence(h_im)


def _ssm_sample_kernel(u_ref, wre_ref, wim_ref, m_ref, etre_ref, etim_ref, are_ref, aim_ref, d_ref,
                       h0re_ref, h0im_ref, y_ref, hre_ref, him_ref, *, t_new, batch):
    for nb in range(N_SSM_BLOCKS):
        ch = slice(nb * SSM_BLOCK_CH, (nb + 1) * SSM_BLOCK_CH)
        st = slice(nb * SSM_BLOCK_STATE, (nb + 1) * SSM_BLOCK_STATE)
        x = jnp.concatenate([u_ref[t * batch:(t + 1) * batch, ch] for t in range(t_new)], axis=1).astype(BF16)
        h_re, h_im = h0re_ref[:, st], h0im_ref[:, st]
        a_re, a_im = are_ref[nb], aim_ref[nb]
        hre_ref[:, st] = a_re * h_re - a_im * h_im + _dot(x, wre_ref[nb])
        him_ref[:, st] = a_re * h_im + a_im * h_re + _dot(x, wim_ref[nb])
        y = (_dot(x, m_ref[nb]) + _dot_nt(h_re.astype(BF16), etre_ref[nb])
             + _dot_nt(h_im.astype(BF16), etim_ref[nb]))
        for t in range(t_new):
            rows = slice(t * batch, (t + 1) * batch)
            y_ref[rows, ch] = y[:, t * SSM_BLOCK_CH:(t + 1) * SSM_BLOCK_CH] + d_ref[:, ch] * u_ref[rows, ch]


def _ssm_sample(u, ops, a_re, a_im, d, h0_re, h0_im, t_new):
    batch = u.shape[0] // t_new
    args = (u, *ops, a_re, a_im, d, h0_re, h0_im)
    state_sds = jax.ShapeDtypeStruct((batch, N_SSM_GROUPS * SSM_STATE), F32)
    out_shape = [jax.ShapeDtypeStruct(u.shape, F32), state_sds, state_sds]
    whole = lambda shape: pl.BlockSpec(shape, lambda i, nd=len(shape): (0,) * nd)
    return pl.pallas_call(
        functools.partial(_ssm_sample_kernel, t_new=t_new, batch=batch),
        grid=(1,),
        in_specs=[whole(a.shape) for a in args],
        out_specs=[whole(s.shape) for s in out_shape],
        out_shape=out_shape,
        compiler_params=_compiler_params("arbitrary"),
        name="ssm_sample",
    )(*args)


def _gelu_tanh(x):
    return 0.5 * x * (1.0 + jnp.tanh(math.sqrt(2.0 / math.pi) * (x + 0.044715 * (x * x * x))))


def _merge_tile(o0_ref, l0_ref, o1_ref, l1_ref, o2_ref, l2_ref, ys_ref, gate_ref, x_ref,
                wglu_ref, bglu_ref, wba_ref, wbs_ref, wout_ref, order_scr, tm, dils):
    def row_order(ref, dil, slot):
        if dil == 1:
            return ref[0, 0]
        n = tm // dil
        halves = GROUP_WIDTH // LANES
        for r in range(dil):
            for k in range(halves):
                order_scr[slot * halves + k, pl.ds(r, n, stride=dil), :] = ref[0, r, :, k * LANES:(k + 1) * LANES]
        return jnp.concatenate([order_scr[slot * halves + k] for k in range(halves)], axis=1)

    parts = [row_order(ref, dils[i // 2], i) for i, ref in
             enumerate((o0_ref, l0_ref, o1_ref, l1_ref, o2_ref, l2_ref))]
    o0, l0, o1, l1, o2, l2 = parts
    mx = jnp.maximum(jnp.maximum(l0, l1), l2)
    e0, e1, e2 = jnp.exp(l0 - mx), jnp.exp(l1 - mx), jnp.exp(l2 - mx)
    attn = (e0 * o0 + e1 * o1 + e2 * o2) / (e0 + e1 + e2)
    branch_a = _dot(attn.astype(BF16), wba_ref[...])
    y = _gelu_tanh(ys_ref[...])
    y = y * _sigmoid(_dot(y.astype(BF16), wglu_ref[...]) + bglu_ref[...])
    branch_s = _dot(y.astype(BF16), wbs_ref[...])
    mix = (_sigmoid(gate_ref[:, 0:D_MODEL]) * branch_a
           + _sigmoid(gate_ref[:, D_MODEL:2 * D_MODEL]) * branch_s)
    return x_ref[...] + _dot(mix.astype(BF16), wout_ref[...])


FF_CHUNK = D_FF // 2
N_MERGE_ROW_INPUTS = 2 * N_DIL_GROUPS + 3
N_MERGE_WEIGHTS = 5


def _merge_ffn_kernel(*refs, tm, shift, pad, dils):
    n_merge = N_MERGE_ROW_INPUTS + N_MERGE_WEIGHTS
    merge_refs, rest = refs[:n_merge], refs[n_merge:]
    carry_ref, g2_ref, wup_ref, cw_ref, cb_ref, wdn_ref, gf_ref, y_ref, state_ref, order_scr, a_scr = rest
    hist = 2 * shift

    @pl.when(pl.program_id(1) == 0)
    def _():
        a_scr[pad - hist:pad, :] = carry_ref[0]

    xf = _merge_tile(*merge_refs, order_scr, tm, dils)
    xn = _rmsnorm(xf, g2_ref[...]).astype(BF16)
    acc = jnp.zeros((tm, D_MODEL), F32)
    for c0 in range(0, D_FF, FF_CHUNK):
        cols = slice(c0, c0 + FF_CHUNK)
        a = _dot(xn, wup_ref[:, cols])
        val = _dot(xn, wup_ref[:, D_FF + c0:D_FF + c0 + FF_CHUNK])
        a_scr[pad:pad + tm, cols] = a
        a_m1 = a_scr[pad - shift:pad - shift + tm, cols]
        a_m2 = a_scr[pad - hist:pad - hist + tm, cols]
        conv = cb_ref[:, cols] + cw_ref[0:1, cols] * a_m2
        conv = conv + cw_ref[1:2, cols] * a_m1
        conv = conv + cw_ref[2:3, cols] * a
        act = conv * _sigmoid(conv) * val
        acc = acc + _dot(act.astype(BF16), wdn_ref[cols, :])
    tail = a_scr[pad + tm - hist:pad + tm, :]
    a_scr[pad - hist:pad, :] = tail
    state_ref[0] = tail
    y_ref[...] = _rmsnorm(xf + acc, gf_ref[...])


def _merge_ffn(attn_parts, ys, gates, x2d, merge_weights, carry, ffn_weights, n_seq, tm, shift):
    m = x2d.shape[0]
    tiles = m // n_seq // tm
    hist = 2 * shift
    pad = -(-hist // SUBLANES) * SUBLANES
    dils = tuple(p.shape[1] for p in attn_parts[::2])
    row = lambda width: pl.BlockSpec((tm, width), lambda b, j: (b * tiles + j, 0))
    res_spec = lambda dil: pl.BlockSpec((1, dil, tm // dil, GROUP_WIDTH), lambda b, j: (b, 0, j, 0))
    state_spec = pl.BlockSpec((1, hist, D_FF), lambda b, j: (b, 0, 0))
    assert len(merge_weights) == N_MERGE_WEIGHTS
    return pl.pallas_call(
        functools.partial(_merge_ffn_kernel, tm=tm, shift=shift, pad=pad, dils=dils),
        grid=(n_seq, tiles),
        in_specs=[*[res_spec(p.shape[1]) for p in attn_parts],
                  row(SSM_WIDTH), row(2 * D_MODEL), row(D_MODEL),
                  *[_resident(w.shape) for w in merge_weights],
                  state_spec, *[_resident(w.shape) for w in ffn_weights]],
        out_specs=[row(D_MODEL), state_spec],
        out_shape=[jax.ShapeDtypeStruct((m, D_MODEL), F32),
                   jax.ShapeDtypeStruct((n_seq, hist, D_FF), F32)],
        scratch_shapes=[pltpu.VMEM((len(attn_parts) * GROUP_WIDTH // LANES, tm, LANES), F32),
                        pltpu.VMEM((pad + tm, D_FF), F32)],
        compiler_params=_compiler_params("arbitrary", "arbitrary"),
        name="merge_ffn",
    )(*attn_parts, ys, gates, x2d, *merge_weights, carry, *ffn_weights)


def _kv_rows(qkv, keep):
    batch, dil, length, _ = qkv.shape
    n = keep // dil
    rows = qkv[:, :, length - n:, GROUP_WIDTH:]
    cols = rows.transpose(0, 3, 2, 1).reshape(batch, 2, HEADS_PER_GROUP, HEAD_DIM, keep)
    return cols.transpose(0, 4, 1, 2, 3)


def _prompt_layer(x, rel_bias, lw):
    batch, seq, _ = x.shape
    x2d = x.reshape(batch * seq, D_MODEL)
    dils = tuple(dil for _, dil in DIL_PATTERNS) + (SSM_CHUNK,)
    keeps = tuple(min(window, seq) for window, _ in DIL_PATTERNS)
    *qkvs, u, gates, kv0, kv1, kv2 = _in_proj(x2d, lw['norm1_g'], lw['w_in'], batch, IN_PROJ_TILE, dils, keeps)

    attn_parts, kv_new = [], []
    for g, (window, dil) in enumerate(DIL_PATTERNS):
        tab = rel_bias[:, g * HEADS_PER_GROUP:(g + 1) * HEADS_PER_GROUP]
        attn_parts.extend(_attn_prompt(qkvs[g], _prompt_bias(tab, dil)))
        tail = (kv0, kv1, kv2)[g].reshape(batch, 2, HEADS_PER_GROUP, HEAD_DIM, keeps[g])
        kv_new.append(tail.transpose(0, 4, 1, 2, 3))

    ops, a_re, a_im = _ssm_block_operators(SSM_CHUNK, *lw['ssm'])
    ys, h_re, h_im = _ssm_prompt(u, ops, a_re, a_im, lw['ssm_d'], batch, seq)

    carry = jnp.zeros((batch, CONV_W - 1, D_FF), F32)
    y, conv_state = _merge_ffn(attn_parts, ys, gates, x2d, lw['merge'], carry, lw['ffn'],
                               n_seq=batch, tm=MERGE_FFN_TILE, shift=1)
    return y.reshape(batch, seq, D_MODEL), (*kv_new, h_re, h_im, conv_state)


def _sample_layer(x, caches, h0_re, h0_im, conv_buf, rel_bias, lw):
    batch, t_new, _ = x.shape
    m = batch * t_new
    x2d = x.transpose(1, 0, 2).reshape(m, D_MODEL)
    *qkvs, u, gates = _in_proj(x2d, lw['norm1_g'], lw['w_in'], 1, m, (1,) * (N_DIL_GROUPS + 1))
    u = u.reshape(m, SSM_WIDTH)
    qkvs_bt = [q.reshape(t_new, batch, GROUP_QKV).transpose(1, 0, 2) for q in qkvs]

    tbs, tns, views = [], [], []
    for g, (window, dil) in enumerate(DIL_PATTERNS):
        tab = rel_bias[:, g * HEADS_PER_GROUP:(g + 1) * HEADS_PER_GROUP]
        n_cached = caches[g].shape[1]
        tb, tn = _sample_bias(tab, dil, t_new, n_cached)
        tbs.append(tb)
        tns.append(tn)
        views.append(caches[g].transpose(0, 2, 3, 4, 1).reshape(batch, 2 * GROUP_WIDTH, n_cached))
    parts = _attn_sample(qkvs_bt, views, tbs, jnp.stack(tns))
    attn_parts = [p.transpose(1, 0, 2).reshape(1, 1, m, GROUP_WIDTH) for p in parts]
    kv_new = [_kv_rows(q.reshape(batch, 1, t_new, GROUP_QKV), t_new) for q in qkvs_bt]

    ops, a_re, a_im = _ssm_block_operators(t_new, *lw['ssm'])
    ys, h_re, h_im = _ssm_sample(u, ops, a_re, a_im, lw['ssm_d'],
                                 h0_re.reshape(batch, -1), h0_im.reshape(batch, -1), t_new)

    carry = conv_buf.transpose(1, 0, 2).reshape(1, (CONV_W - 1) * batch, D_FF)
    y, conv_state = _merge_ffn(attn_parts, ys, gates, x2d, lw['merge'], carry, lw['ffn'],
                               n_seq=1, tm=m, shift=batch)
    y = y.reshape(t_new, batch, D_MODEL).transpose(1, 0, 2)
    conv_state = conv_state.reshape(CONV_W - 1, batch, D_FF).transpose(1, 0, 2)
    state_shape = (batch, N_SSM_GROUPS, SSM_STATE)
    return y, (*kv_new, h_re.reshape(state_shape), h_im.reshape(state_shape), conv_state)


IN_PROJ_TILE = 512
MERGE_FFN_TILE = 256


def kernel(x_prompt, x_sample, cache_kv_w128, cache_kv_w512, cache_kv_w2048, state_ssm_re, state_ssm_im, state_ffn_conv, rel_bias, norm1_g, w_in, ssm_log_dt, ssm_lambda_re, ssm_lambda_im, ssm_b_re, ssm_b_im, ssm_c_re, ssm_c_im, ssm_d, w_glu, b_glu, w_branch_attn, w_branch_ssm, w_out, norm2_g, w_up, conv_w, conv_b, w_down, norm_f_g):
    depth = w_in.shape[0]
    hp, hs = x_prompt, x_sample
    st_p, st_s = [], []
    gf = norm_f_g.reshape(1, D_MODEL)
    for l in range(depth):
        last = l == depth - 1
        lw = {
            'norm1_g': norm1_g[l].reshape(1, D_MODEL),
            'w_in': _group_major_columns(w_in[l]).astype(BF16),
            'ssm': (ssm_log_dt[l], ssm_lambda_re[l], ssm_lambda_im[l],
                    ssm_b_re[l], ssm_b_im[l], ssm_c_re[l], ssm_c_im[l]),
            'ssm_d': ssm_d[l].reshape(1, SSM_WIDTH),
            'merge': (w_glu[l].astype(BF16), b_glu[l].reshape(1, SSM_WIDTH),
                      w_branch_attn[l].astype(BF16), w_branch_ssm[l].astype(BF16), w_out[l].astype(BF16)),
            'ffn': (norm2_g[l].reshape(1, D_MODEL), w_up[l].astype(BF16), conv_w[l],
                    conv_b[l].reshape(1, D_FF), w_down[l].astype(BF16), gf),
        }
        assert last, "the final RMSNorm is fused into the last layer's ffn kernel"
        hp, sp = _prompt_layer(hp, rel_bias, lw)
        hs, ss = _sample_layer(hs, (cache_kv_w128[l], cache_kv_w512[l], cache_kv_w2048[l]),
                               state_ssm_re[l], state_ssm_im[l], state_ffn_conv[l], rel_bias, lw)
        st_p.append(sp)
        st_s.append(ss)
    stack = lambda states, i: jnp.stack([st[i] for st in states], axis=0)
    return (hp, hs, *[stack(st_p, i) for i in range(6)], *[stack(st_s, i) for i in range(6)])
```

```python
import functools
import math

import jax
import jax.numpy as jnp
from jax import lax
from jax.experimental import pallas as pl
from jax.experimental.pallas import tpu as pltpu

F32 = jnp.float32
BF16 = jnp.bfloat16

D_MODEL = 1024
HEAD_DIM = 64
HEADS_PER_GROUP = 4
DIL_PATTERNS = ((128, 1), (512, 4), (2048, 16))
N_DIL_GROUPS = len(DIL_PATTERNS)
GROUP_WIDTH = HEADS_PER_GROUP * HEAD_DIM
QK_WIDTH = N_DIL_GROUPS * GROUP_WIDTH
QKV_WIDTH = 3 * QK_WIDTH
GROUP_QKV = 3 * GROUP_WIDTH
QBLOCK = 128
SSM_GROUP = 16
SSM_STATE = 64
SSM_WIDTH = D_MODEL // 2
N_SSM_GROUPS = SSM_WIDTH // SSM_GROUP
SSM_CHUNK = 16
D_FF = 2816
CONV_W = 3
N_BUCKETS = 32
MAX_DISTANCE = 2048
NORM_EPS = 1e-6
NEG_INF = -1e30
U_START = QKV_WIDTH
GATE_START = U_START + SSM_WIDTH
IN_WIDTH = GATE_START + 2 * D_MODEL
QK_SCALE = HEAD_DIM ** -0.5

VMEM_LIMIT_BYTES = 56 * 1024 * 1024
SUBLANES = 8
LANES = 128


def _compiler_params(*semantics):
    return pltpu.CompilerParams(dimension_semantics=semantics, vmem_limit_bytes=VMEM_LIMIT_BYTES)


def _resident(shape):
    nd = len(shape)
    return pl.BlockSpec(shape, lambda *_: (0,) * nd, pipeline_mode=pl.Buffered(1))


def _rmsnorm(xf, g):
    y = xf * lax.rsqrt(jnp.mean(xf * xf, axis=-1, keepdims=True) + NORM_EPS)
    return y * g


def _sigmoid(x):
    return 1.0 / (1.0 + jnp.exp(-x))


def _dot(a, b):
    return jnp.dot(a, b, preferred_element_type=F32)


def _dot_nt(a, b):
    return lax.dot_general(a, b, (((1,), (1,)), ((), ())), preferred_element_type=F32)


def _in_proj_kernel(x_ref, g_ref, w_ref, qkv0_ref, qkv1_ref, qkv2_ref, u_ref, gate_ref, *rest,
                    tm, dils, tails, tiles):
    if tails is None:
        tail_refs, (xn_scr,) = (), rest
    else:
        tail_refs, (xn_scr, tok_scr) = rest[:N_DIL_GROUPS], rest[N_DIL_GROUPS:]
    xf = _rmsnorm(x_ref[...], g_ref[...])
    n_lane_blocks = D_MODEL // LANES
    if any(dil > 1 for dil in dils):
        for k in range(n_lane_blocks):
            xn_scr[k] = xf[:, k * LANES:(k + 1) * LANES]
    by_residue = {1: xf.astype(BF16)}

    def rows_by_residue(dil):
        if dil not in by_residue:
            n = tm // dil
            xr = jnp.concatenate(
                [jnp.concatenate([xn_scr[k, pl.ds(r, n, stride=dil), :] for k in range(n_lane_blocks)], axis=1)
                 for r in range(dil)], axis=0)
            by_residue[dil] = xr.astype(BF16)
        return by_residue[dil]

    outputs = [(ref, g * GROUP_QKV, GROUP_QKV) for g, ref in enumerate((qkv0_ref, qkv1_ref, qkv2_ref))]
    outputs.append((u_ref, U_START, SSM_WIDTH))
    for (ref, start, width), dil in zip(outputs, dils):
        n = tm // dil
        res = _dot(rows_by_residue(dil), w_ref[:, start:start + width])
        for r in range(dil):
            ref[0, r] = res[r * n:(r + 1) * n]
    for c0 in range(0, 2 * D_MODEL, D_MODEL):
        gate_ref[:, c0:c0 + D_MODEL] = _dot(by_residue[1], w_ref[:, GATE_START + c0:GATE_START + c0 + D_MODEL])
    if tails is None:
        return
    kv_blocks = 2 * GROUP_WIDTH // LANES
    for (ref, _, _), dil, keep, tail_ref in zip(outputs, dils, tails, tail_refs):
        n = tm // dil
        kept = min(keep, tm)

        @pl.when(pl.program_id(1) >= tiles - max(keep // tm, 1))
        def _(ref=ref, dil=dil, n=n, kept=kept, tail_ref=tail_ref):
            if dil == 1:
                kv = ref[0, 0, :, GROUP_WIDTH:]
            else:
                for r in range(dil):
                    for k in range(kv_blocks):
                        lanes = slice(GROUP_WIDTH + k * LANES, GROUP_WIDTH + (k + 1) * LANES)
                        tok_scr[k, pl.ds(r, n, stride=dil), :] = ref[0, r, :, lanes]
                kv = jnp.concatenate([tok_scr[k] for k in range(kv_blocks)], axis=1)
            tail_ref[0] = kv[tm - kept:, :].T


def _group_major_columns(w_in):
    parts = []
    for g in range(N_DIL_GROUPS):
        for base in (0, QK_WIDTH, 2 * QK_WIDTH):
            parts.append(w_in[:, base + g * GROUP_WIDTH:base + (g + 1) * GROUP_WIDTH])
    parts.append(w_in[:, U_START:])
    return jnp.concatenate(parts, axis=1)


def _in_proj(x2d, g, w_bf16, n_seq, tm, dils, tails=None):
    m = x2d.shape[0]
    seq = m // n_seq
    tiles = seq // tm
    row = lambda width: pl.BlockSpec((tm, width), lambda b, j: (b * tiles + j, 0))
    res_spec = lambda dil, width: pl.BlockSpec((1, dil, tm // dil, width), lambda b, j: (b, 0, j, 0))
    widths = (GROUP_QKV,) * N_DIL_GROUPS + (SSM_WIDTH,)
    out_specs = [*[res_spec(d, w) for d, w in zip(dils, widths)], row(2 * D_MODEL)]
    out_shape = [*[jax.ShapeDtypeStruct((n_seq, d, seq // d, w), F32) for d, w in zip(dils, widths)],
                 jax.ShapeDtypeStruct((m, 2 * D_MODEL), F32)]
    scratch = [pltpu.VMEM((D_MODEL // LANES, tm, LANES), F32)]
    if tails is not None:
        for keep in tails:
            assert keep % tm == 0 or tm % keep == 0
            first = tiles - max(keep // tm, 1)
            out_specs.append(pl.BlockSpec((1, 2 * GROUP_WIDTH, min(keep, tm)),
                                          lambda b, j, first=first: (b, 0, jnp.maximum(j - first, 0))))
            out_shape.append(jax.ShapeDtypeStruct((n_seq, 2 * GROUP_WIDTH, keep), F32))
        scratch.append(pltpu.VMEM((2 * GROUP_WIDTH // LANES, tm, LANES), F32))
    return pl.pallas_call(
        functools.partial(_in_proj_kernel, tm=tm, dils=dils, tails=tails, tiles=tiles),
        grid=(n_seq, tiles),
        in_specs=[row(D_MODEL), _resident((1, D_MODEL)), _resident((D_MODEL, IN_WIDTH))],
        out_specs=out_specs,
        out_shape=out_shape,
        scratch_shapes=scratch,
        compiler_params=_compiler_params("arbitrary", "arbitrary"),
        name="in_proj",
    )(x2d, g, w_bf16)


def _rel_bucket(dist):
    max_exact = N_BUCKETS // 2
    n = jnp.maximum(dist, 0)
    nf = jnp.maximum(n, 1).astype(F32)
    large = max_exact + (jnp.log(nf / max_exact) / math.log(MAX_DISTANCE / max_exact)
                         * (N_BUCKETS - max_exact)).astype(jnp.int32)
    large = jnp.minimum(large, N_BUCKETS - 1)
    return jnp.where(n < max_exact, n, large)


def _masked_bias(tab, strides, valid, dil):
    n_dist = QBLOCK
    bucket = _rel_bucket(jnp.clip(strides, 0, n_dist) * dil).reshape(1, -1)
    onehot = (bucket == jnp.arange(N_BUCKETS)[:, None]).astype(F32)
    bias = jnp.dot(tab.astype(F32).T, onehot, precision=lax.Precision.HIGHEST)
    bias = jnp.where(valid.reshape(1, -1), bias, NEG_INF)
    return bias.reshape((tab.shape[1],) + strides.shape)


def _prompt_bias(tab, dil):
    qi = jnp.arange(QBLOCK)[:, None]
    ki = jnp.arange(QBLOCK)[None, :]
    j_prev = qi + QBLOCK - ki
    j_cur = qi - ki
    bias = jnp.stack([_masked_bias(tab, j_prev, j_prev <= QBLOCK, dil),
                      _masked_bias(tab, j_cur, j_cur >= 0, dil)], axis=0)
    return bias.transpose(0, 2, 1, 3).reshape(2, QBLOCK, HEADS_PER_GROUP * QBLOCK)


def _sample_bias(tab, dil, t_new, n_cached):
    t = jnp.arange(t_new)[:, None]
    delta_buf = n_cached + t - jnp.arange(n_cached)[None, :]
    ok_buf = (delta_buf % dil == 0) & (delta_buf // dil <= QBLOCK)
    m = jnp.arange(QBLOCK)[None, :]
    delta_new = t - m
    ok_new = (m < t_new) & (delta_new >= 0) & (delta_new % dil == 0)
    b_buf = _masked_bias(tab, delta_buf // dil, ok_buf, dil)
    b_new = _masked_bias(tab, delta_new // dil, ok_new, dil)
    return (b_buf.reshape(HEADS_PER_GROUP * t_new, n_cached),
            b_new.reshape(HEADS_PER_GROUP * t_new, QBLOCK))


def _attn_prompt_kernel(q_ref, kc_ref, vc_ref, bias_ref, o_ref, lse_ref, kprev_scr, vprev_scr, *, nq):
    first_tile = pl.program_id(2) == 0

    @pl.when(first_tile)
    def _():
        kprev_scr[...] = jnp.zeros_like(kprev_scr)
        vprev_scr[...] = jnp.zeros_like(vprev_scr)

    stacked = (HEADS_PER_GROUP * QBLOCK, GROUP_WIDTH)
    own_head = (lax.broadcasted_iota(jnp.int32, stacked, 0) // QBLOCK
                == lax.broadcasted_iota(jnp.int32, stacked, 1) // HEAD_DIM)
    lane_head = lax.broadcasted_iota(jnp.int32, (QBLOCK, GROUP_WIDTH), 1) // HEAD_DIM

    def per_head(x):
        xb = x.astype(BF16)
        return jnp.where(own_head, jnp.concatenate([xb] * HEADS_PER_GROUP, axis=0), 0)

    def on_head_lanes(cols):
        out = jnp.broadcast_to(cols[-1], (QBLOCK, GROUP_WIDTH))
        for h in range(HEADS_PER_GROUP - 2, -1, -1):
            out = jnp.where(lane_head == h, cols[h], out)
        return out

    block_rows = [slice(i * QBLOCK, (i + 1) * QBLOCK) for i in range(nq)]
    qs = [(q_ref[0, 0, rows, :] * QK_SCALE).astype(BF16) for rows in block_rows]
    k_blocks = [kprev_scr[...]] + [per_head(kc_ref[0, 0, rows, :]) for rows in block_rows]
    v_blocks = [vprev_scr[...]] + [per_head(vc_ref[0, 0, rows, :]) for rows in block_rows]
    kprev_scr[...] = k_blocks[-1]
    vprev_scr[...] = v_blocks[-1]

    def users(per_query_cur, per_query_prev, j):
        parts = ([per_query_cur[j - 1]] if j >= 1 else []) + ([per_query_prev[j]] if j < nq else [])
        return parts[0] if len(parts) == 1 else jnp.concatenate(parts, axis=0)

    s_prev, s_cur = [None] * nq, [None] * nq
    for j in range(nq + 1):
        s = _dot_nt(users(qs, qs, j), k_blocks[j])
        if j >= 1:
            s_cur[j - 1] = s[:QBLOCK] + bias_ref[1]
        if j < nq:
            s_prev[j] = s[-QBLOCK:] + bias_ref[0]
    s_prev[0] = jnp.where(first_tile, NEG_INF, s_prev[0])

    p_prev, p_cur, dens, lses = [], [], [], []
    for i in range(nq):
        pp_heads, pc_heads, den_heads, lse_heads = [], [], [], []
        for h in range(HEADS_PER_GROUP):
            keys = slice(h * QBLOCK, (h + 1) * QBLOCK)
            sp, sc = s_prev[i][:, keys], s_cur[i][:, keys]
            m = jnp.max(jnp.maximum(sp, sc), axis=-1, keepdims=True)
            pp, pc = jnp.exp(sp - m), jnp.exp(sc - m)
            den = jnp.sum(pp + pc, axis=-1, keepdims=True)
            pp_heads.append(pp.astype(BF16))
            pc_heads.append(pc.astype(BF16))
            den_heads.append(den)
            lse_heads.append(m + jnp.log(den))
        p_prev.append(jnp.concatenate(pp_heads, axis=1))
        p_cur.append(jnp.concatenate(pc_heads, axis=1))
        dens.append(on_head_lanes(den_heads))
        lses.append(on_head_lanes(lse_heads))

    o = [None] * nq
    for j in range(nq + 1):
        r = _dot(users(p_cur, p_prev, j), v_blocks[j])
        if j >= 1:
            o[j - 1] = o[j - 1] + r[:QBLOCK]
        if j < nq:
            o[j] = r[-QBLOCK:]
    for i, rows in enumerate(block_rows):
        o_ref[0, 0, rows, :] = o[i] / dens[i]
        lse_ref[0, 0, rows, :] = lses[i]


ATTN_QUERY_BLOCKS = 8


def _attn_prompt(qkv, bias):
    batch, dil, length, _ = qkv.shape
    tq = min(ATTN_QUERY_BLOCKS * QBLOCK, length)
    nq = tq // QBLOCK

    def cur(col):
        return pl.BlockSpec((1, 1, tq, GROUP_WIDTH), lambda b, r, n: (b, r, n, col))

    stacked = pltpu.VMEM((HEADS_PER_GROUP * QBLOCK, GROUP_WIDTH), BF16)
    out_sds = jax.ShapeDtypeStruct((batch, dil, length, GROUP_WIDTH), F32)
    return pl.pallas_call(
        functools.partial(_attn_prompt_kernel, nq=nq),
        grid=(batch, dil, length // tq),
        in_specs=[cur(0), cur(1), cur(2), _resident((2, QBLOCK, HEADS_PER_GROUP * QBLOCK))],
        out_specs=[cur(0), cur(0)],
        out_shape=[out_sds, out_sds],
        scratch_shapes=[stacked, stacked],
        compiler_params=_compiler_params("arbitrary", "arbitrary", "arbitrary"),
        name=f"attn_prompt_d{dil}",
    )(qkv, qkv, qkv, bias)


def _attn_sample_kernel(q0_ref, q1_ref, q2_ref, c0_ref, c1_ref, c2_ref, tb0_ref, tb1_ref, tb2_ref, tn_ref,
                        o0_ref, l0_ref, o1_ref, l1_ref, o2_ref, l2_ref, kn_scr, vn_scr, *, t_new):
    n_rows = HEADS_PER_GROUP * t_new
    row_w = lax.broadcasted_iota(jnp.int32, (n_rows, GROUP_WIDTH), 0)
    lane_w = lax.broadcasted_iota(jnp.int32, (n_rows, GROUP_WIDTH), 1)
    own_head = (row_w // t_new) == (lane_w // HEAD_DIM)

    def fold_heads(x):
        x = jnp.where(own_head, x, 0.0)
        out = x[0:t_new]
        for h in range(1, HEADS_PER_GROUP):
            out = out + x[h * t_new:(h + 1) * t_new]
        return out

    caches = (c0_ref, c1_ref, c2_ref)
    cache_bias = (tb0_ref, tb1_ref, tb2_ref)
    outs = ((o0_ref, l0_ref), (o1_ref, l1_ref), (o2_ref, l2_ref))
    for g in range(N_DIL_GROUPS):
        qkv_ref = (q0_ref, q1_ref, q2_ref)[g]
        q = qkv_ref[0, :, 0:GROUP_WIDTH] * QK_SCALE
        q_rows = jnp.where(own_head, jnp.concatenate([q] * HEADS_PER_GROUP, axis=0), 0.0).astype(BF16)
        kn_scr[...] = jnp.zeros_like(kn_scr)
        vn_scr[...] = jnp.zeros_like(vn_scr)
        kn_scr[0:t_new, :] = qkv_ref[0, :, GROUP_WIDTH:2 * GROUP_WIDTH]
        vn_scr[0:t_new, :] = qkv_ref[0, :, 2 * GROUP_WIDTH:3 * GROUP_WIDTH]
        cache = caches[g]
        k_t = cache[0, 0:GROUP_WIDTH, :].astype(BF16)
        v_t = cache[0, GROUP_WIDTH:2 * GROUP_WIDTH, :].astype(BF16)
        s_buf = _dot(q_rows, k_t) + cache_bias[g][...]
        s_new = _dot_nt(q_rows, kn_scr[...].astype(BF16)) + tn_ref[g]
        m = jnp.maximum(jnp.max(s_buf, axis=-1, keepdims=True), jnp.max(s_new, axis=-1, keepdims=True))
        p_buf = jnp.exp(s_buf - m)
        p_new = jnp.exp(s_new - m)
        den = jnp.sum(p_buf, axis=-1, keepdims=True) + jnp.sum(p_new, axis=-1, keepdims=True)
        o = _dot_nt(p_buf.astype(BF16), v_t) + _dot(p_new.astype(BF16), vn_scr[...].astype(BF16))
        o_ref, l_ref = outs[g]
        o_ref[0] = fold_heads(o / den)
        l_ref[0] = fold_heads(jnp.broadcast_to(m + jnp.log(den), (n_rows, GROUP_WIDTH)))


def _attn_sample(qkvs, caches, tbs, tn):
    batch, t_new, _ = qkvs[0].shape
    n_rows = HEADS_PER_GROUP * t_new
    cache_specs = [pl.BlockSpec((1,) + c.shape[1:], lambda b: (b, 0, 0)) for c in caches]
    qkv_spec = pl.BlockSpec((1, t_new, GROUP_QKV), lambda b: (b, 0, 0))
    out_spec = pl.BlockSpec((1, t_new, GROUP_WIDTH), lambda b: (b, 0, 0))
    out_sds = jax.ShapeDtypeStruct((batch, t_new, GROUP_WIDTH), F32)
    return pl.pallas_call(
        functools.partial(_attn_sample_kernel, t_new=t_new),
        grid=(batch,),
        in_specs=[*[qkv_spec] * N_DIL_GROUPS, *cache_specs, *[_resident(t.shape) for t in tbs],
                  _resident((N_DIL_GROUPS, n_rows, QBLOCK))],
        out_specs=[out_spec] * (2 * N_DIL_GROUPS),
        out_shape=[out_sds] * (2 * N_DIL_GROUPS),
        scratch_shapes=[pltpu.VMEM((QBLOCK, GROUP_WIDTH), F32), pltpu.VMEM((QBLOCK, GROUP_WIDTH), F32)],
        compiler_params=_compiler_params("arbitrary"),
        name="attn_sample",
    )(*qkvs, *caches, *tbs, tn)


def _ssm_chunk_terms(chunk, log_dt, lam_re, lam_im, b_re, b_im, c_re, c_im):
    hi = lax.Precision.HIGHEST
    dt = jnp.exp(log_dt.astype(F32))[:, None]
    lr, li = lam_re.astype(F32), lam_im.astype(F32)
    mag = jnp.exp(lr * dt)
    ab_re, ab_im = mag * jnp.cos(li * dt), mag * jnp.sin(li * dt)
    g, n = lr.shape
    p = b_re.shape[-1]
    den = lr * lr + li * li
    nr, ni = ab_re - 1.0, ab_im
    coef_re = (nr * lr + ni * li) / den
    coef_im = (ni * lr - nr * li) / den
    br, bi = b_re.astype(F32), b_im.astype(F32)
    bb_re = coef_re[..., None] * br - coef_im[..., None] * bi
    bb_im = coef_re[..., None] * bi + coef_im[..., None] * br
    k = jnp.arange(chunk + 1, dtype=F32)[:, None, None]
    pw_mag = jnp.exp(k * (lr * dt)[None])
    pw_re, pw_im = pw_mag * jnp.cos(k * (li * dt)[None]), pw_mag * jnp.sin(k * (li * dt)[None])
    bt_re, bt_im = bb_re.transpose(0, 2, 1)[None], bb_im.transpose(0, 2, 1)[None]
    pk_re, pk_im = pw_re[:chunk, :, None, :], pw_im[:chunk, :, None, :]
    akb_re = pk_re * bt_re - pk_im * bt_im
    akb_im = pk_re * bt_im + pk_im * bt_re
    cr, ci = c_re.astype(F32)[None], c_im.astype(F32)[None]
    pe_re, pe_im = pw_re[1:, :, None, :], pw_im[1:, :, None, :]
    e_re = cr * pe_re - ci * pe_im
    e_im = cr * pe_im + ci * pe_re
    kern = (jnp.einsum('kgqn,gpn->kgqp', akb_re, cr[0], precision=hi)
            - jnp.einsum('kgqn,gpn->kgqp', akb_im, ci[0], precision=hi))
    rows = lambda x: x.reshape(chunk, g * p, x.shape[-1])
    return ((rows(akb_re), rows(akb_im)), (rows(e_re), rows(e_im)), rows(kern),
            (pw_re[chunk], pw_im[chunk]))


SSM_BLOCK_GROUPS = 8
SSM_BLOCK_CH = SSM_BLOCK_GROUPS * SSM_GROUP
SSM_BLOCK_STATE = SSM_BLOCK_GROUPS * SSM_STATE
N_SSM_BLOCKS = N_SSM_GROUPS // SSM_BLOCK_GROUPS


def _ssm_operator_kernel(akbre_ref, akbim_ref, ere_ref, eim_ref, kern_ref,
                         wre_ref, wim_ref, m_ref, etre_ref, etim_ref, *, chunk):
    gb, p, n = SSM_BLOCK_GROUPS, SSM_GROUP, SSM_STATE

    def copies(width, count):
        src = lax.broadcasted_iota(jnp.int32, (width, count * width), 0)
        dst = lax.broadcasted_iota(jnp.int32, (width, count * width), 1)
        return jnp.where(src == dst % width, 1.0, 0.0).astype(BF16)

    def own_group(width):
        row = lax.broadcasted_iota(jnp.int32, (SSM_BLOCK_CH, gb * width), 0)
        col = lax.broadcasted_iota(jnp.int32, (SSM_BLOCK_CH, gb * width), 1)
        return row // p == col // width

    to_states, own_states = copies(n, gb), own_group(n)

    def over_states(piece):
        return jnp.where(own_states, _dot(piece.astype(BF16), to_states), 0.0).astype(BF16)

    for s in range(chunk):
        rows = slice(s * SSM_BLOCK_CH, (s + 1) * SSM_BLOCK_CH)
        wre_ref[0, rows, :] = over_states(akbre_ref[chunk - 1 - s, 0])
        wim_ref[0, rows, :] = over_states(akbim_ref[chunk - 1 - s, 0])
        etre_ref[0, rows, :] = over_states(ere_ref[s, 0])
        etim_ref[0, rows, :] = over_states(-eim_ref[s, 0])
    to_channels, own_channels = copies(p, gb), own_group(p)
    lags = [jnp.where(own_channels, _dot(kern_ref[k, 0].astype(BF16), to_channels), 0.0).astype(BF16)
            for k in range(chunk)]
    zero = jnp.zeros((SSM_BLOCK_CH, SSM_BLOCK_CH), BF16)
    for s in range(chunk):
        m_ref[0, s * SSM_BLOCK_CH:(s + 1) * SSM_BLOCK_CH, :] = jnp.concatenate(
            [zero] * s + lags[:chunk - s], axis=1)


def _ssm_block_operators(chunk, *params):
    (akb_re, akb_im), (e_re, e_im), kern, (a_re, a_im) = _ssm_chunk_terms(chunk, *params)
    nb, gb, p, n = N_SSM_BLOCKS, SSM_BLOCK_GROUPS, SSM_GROUP, SSM_STATE
    pieces = [x.reshape(chunk, nb, SSM_BLOCK_CH, x.shape[-1]) for x in (akb_re, akb_im, e_re, e_im, kern)]
    piece_spec = lambda x: pl.BlockSpec((chunk, 1) + x.shape[2:], lambda i: (0, i, 0, 0))
    out_spec = lambda shape: pl.BlockSpec((1,) + shape[1:], lambda i: (i, 0, 0))
    x_width = chunk * SSM_BLOCK_CH
    out_shapes = [(nb, x_width, gb * n), (nb, x_width, gb * n), (nb, x_width, x_width),
                  (nb, x_width, gb * n), (nb, x_width, gb * n)]
    ops = pl.pallas_call(
        functools.partial(_ssm_operator_kernel, chunk=chunk),
        grid=(nb,),
        in_specs=[piece_spec(x) for x in pieces],
        out_specs=[out_spec(s) for s in out_shapes],
        out_shape=[jax.ShapeDtypeStruct(s, BF16) for s in out_shapes],
        compiler_params=_compiler_params("arbitrary"),
        name="ssm_operators",
    )(*pieces)
    return ops, a_re.reshape(nb, 1, gb * n), a_im.reshape(nb, 1, gb * n)


SSM_X_WIDTH = SSM_CHUNK * SSM_BLOCK_CH


def _ssm_prompt_kernel(u_ref, wre_ref, wim_ref, m_ref, etre_ref, etim_ref, are_ref, aim_ref, d_ref,
                       y_ref, hre_ref, him_ref, x_scr, sre_scr, sim_scr, *, batch, chunks):
    phase, b = pl.program_id(1), pl.program_id(2)
    for t in range(SSM_CHUNK):
        x_scr[:, t * SSM_BLOCK_CH:(t + 1) * SSM_BLOCK_CH] = u_ref[0, t].astype(BF16)
    rows = pl.ds(b, chunks, stride=batch)
    state_blocks = SSM_BLOCK_STATE // LANES

    @pl.when(phase == 0)
    def _():
        x = x_scr[...]
        g_re, g_im = _dot(x, wre_ref[0]), _dot(x, wim_ref[0])
        for k in range(state_blocks):
            sre_scr[k, rows, :] = g_re[:, k * LANES:(k + 1) * LANES]
            sim_scr[k, rows, :] = g_im[:, k * LANES:(k + 1) * LANES]

    @pl.when((phase == 1) & (b == 0))
    def _():
        same_lanes = lambda ref, k: ref[0][:, k * LANES:(k + 1) * LANES]

        def step(c, carry):
            same_chunk = pl.ds(c * batch, batch)
            out = []
            for k, (h_re, h_im) in enumerate(carry):
                a_re, a_im = same_lanes(are_ref, k), same_lanes(aim_ref, k)
                g_re, g_im = sre_scr[k, same_chunk, :], sim_scr[k, same_chunk, :]
                sre_scr[k, same_chunk, :] = h_re
                sim_scr[k, same_chunk, :] = h_im
                out.append((a_re * h_re - a_im * h_im + g_re, a_re * h_im + a_im * h_re + g_im))
            return tuple(out)

        zero = jnp.zeros((batch, LANES), F32)
        final = lax.fori_loop(0, chunks, step, ((zero, zero),) * state_blocks)
        for k, (h_re, h_im) in enumerate(final):
            hre_ref[0, :, k * LANES:(k + 1) * LANES] = h_re
            him_ref[0, :, k * LANES:(k + 1) * LANES] = h_im

    @pl.when(phase == 1)
    def _():
        h_re = jnp.concatenate([sre_scr[k, rows, :] for k in range(state_blocks)], axis=1).astype(BF16)
        h_im = jnp.concatenate([sim_scr[k, rows, :] for k in range(state_blocks)], axis=1).astype(BF16)
        pair_w = 2 * SSM_BLOCK_CH
        for j in range(SSM_CHUNK // 2):
            cols = slice(j * pair_w, (j + 1) * pair_w)
            k_in = (j + 1) * pair_w
            yj = (_dot(x_scr[:, :k_in], m_ref[0, :k_in, cols])
                  + _dot_nt(h_re, etre_ref[0, cols, :]) + _dot_nt(h_im, etim_ref[0, cols, :]))
            for i in range(2):
                t = 2 * j + i
                y_ref[pl.ds(t, chunks, stride=SSM_CHUNK), :] = (
                    yj[:, i * SSM_BLOCK_CH:(i + 1) * SSM_BLOCK_CH] + d_ref[0] * u_ref[0, t])


def _ssm_prompt(u, ops, a_re, a_im, d, batch, seq):
    chunks = seq // SSM_CHUNK
    d = d.reshape(N_SSM_BLOCKS, 1, SSM_BLOCK_CH)
    op_spec = lambda arr: pl.BlockSpec((1,) + arr.shape[1:], lambda g, ph, b: (g, 0, 0),
                                       pipeline_mode=pl.Buffered(1))
    state_spec = pl.BlockSpec((1, batch, SSM_BLOCK_STATE), lambda g, ph, b: (g, 0, 0))
    state_sds = jax.ShapeDtypeStruct((N_SSM_BLOCKS, batch, SSM_BLOCK_STATE), F32)
    y, h_re, h_im = pl.pallas_call(
        functools.partial(_ssm_prompt_kernel, batch=batch, chunks=chunks),
        grid=(N_SSM_BLOCKS, 2, batch),
        in_specs=[pl.BlockSpec((1, SSM_CHUNK, chunks, SSM_BLOCK_CH), lambda g, ph, b: (b, 0, 0, g)),
                  *[op_spec(o) for o in ops], op_spec(a_re), op_spec(a_im), op_spec(d)],
        out_specs=[pl.BlockSpec((seq, SSM_BLOCK_CH), lambda g, ph, b: (b * ph, g)), state_spec, state_spec],
        out_shape=[jax.ShapeDtypeStruct((batch * seq, SSM_WIDTH), F32), state_sds, state_sds],
        scratch_shapes=[pltpu.VMEM((chunks, SSM_X_WIDTH), BF16),
                        pltpu.VMEM((SSM_BLOCK_STATE // LANES, batch * chunks, LANES), F32),
                        pltpu.VMEM((SSM_BLOCK_STATE // LANES, batch * chunks, LANES), F32)],
        compiler_params=_compiler_params("arbitrary", "arbitrary", "arbitrary"),
        name="ssm_prompt",
    )(u, *ops, a_re, a_im, d)

    def by_sequence(h):
        h = h.reshape(N_SSM_BLOCKS, batch, SSM_BLOCK_GROUPS, SSM_STATE).transpose(1, 0, 2, 3)
        return h.reshape(batch, N_SSM_GROUPS, SSM_STATE)

    return y, by_sequence(h_re), by_sequence(h_im)


def _ssm_sample_kernel(u_ref, wre_ref, wim_ref, m_ref, etre_ref, etim_ref, are_ref, aim_ref, d_ref,
                       h0re_ref, h0im_ref, y_ref, hre_ref, him_ref, *, t_new, batch):
    for nb in range(N_SSM_BLOCKS):
        ch = slice(nb * SSM_BLOCK_CH, (nb + 1) * SSM_BLOCK_CH)
        st = slice(nb * SSM_BLOCK_STATE, (nb + 1) * SSM_BLOCK_STATE)
        x = jnp.concatenate([u_ref[t * batch:(t + 1) * batch, ch] for t in range(t_new)], axis=1).astype(BF16)
        h_re, h_im = h0re_ref[:, st], h0im_ref[:, st]
        a_re, a_im = are_ref[nb], aim_ref[nb]
        hre_ref[:, st] = a_re * h_re - a_im * h_im + _dot(x, wre_ref[nb])
        him_ref[:, st] = a_re * h_im + a_im * h_re + _dot(x, wim_ref[nb])
        y = (_dot(x, m_ref[nb]) + _dot_nt(h_re.astype(BF16), etre_ref[nb])
             + _dot_nt(h_im.astype(BF16), etim_ref[nb]))
        for t in range(t_new):
            rows = slice(t * batch, (t + 1) * batch)
            y_ref[rows, ch] = y[:, t * SSM_BLOCK_CH:(t + 1) * SSM_BLOCK_CH] + d_ref[:, ch] * u_ref[rows, ch]


def _ssm_sample(u, ops, a_re, a_im, d, h0_re, h0_im, t_new):
    batch = u.shape[0] // t_new
    args = (u, *ops, a_re, a_im, d, h0_re, h0_im)
    state_sds = jax.ShapeDtypeStruct((batch, N_SSM_GROUPS * SSM_STATE), F32)
    out_shape = [jax.ShapeDtypeStruct(u.shape, F32), state_sds, state_sds]
    whole = lambda shape: pl.BlockSpec(shape, lambda i, nd=len(shape): (0,) * nd)
    return pl.pallas_call(
        functools.partial(_ssm_sample_kernel, t_new=t_new, batch=batch),
        grid=(1,),
        in_specs=[whole(a.shape) for a in args],
        out_specs=[whole(s.shape) for s in out_shape],
        out_shape=out_shape,
        compiler_params=_compiler_params("arbitrary"),
        name="ssm_sample",
    )(*args)


def _gelu_tanh(x):
    return 0.5 * x * (1.0 + jnp.tanh(math.sqrt(2.0 / math.pi) * (x + 0.044715 * (x * x * x))))


def _merge_tile(o0_ref, l0_ref, o1_ref, l1_ref, o2_ref, l2_ref, ys_ref, gate_ref, x_ref,
                wglu_ref, bglu_ref, wba_ref, wbs_ref, wout_ref, order_scr, tm, dils):
    def row_order(ref, dil, slot):
        if dil == 1:
            return ref[0, 0]
        n = tm // dil
        halves = GROUP_WIDTH // LANES
        for r in range(dil):
            for k in range(halves):
                order_scr[slot * halves + k, pl.ds(r, n, stride=dil), :] = ref[0, r, :, k * LANES:(k + 1) * LANES]
        return jnp.concatenate([order_scr[slot * halves + k] for k in range(halves)], axis=1)

    parts = [row_order(ref, dils[i // 2], i) for i, ref in
             enumerate((o0_ref, l0_ref, o1_ref, l1_ref, o2_ref, l2_ref))]
    o0, l0, o1, l1, o2, l2 = parts
    mx = jnp.maximum(jnp.maximum(l0, l1), l2)
    e0, e1, e2 = jnp.exp(l0 - mx), jnp.exp(l1 - mx), jnp.exp(l2 - mx)
    attn = (e0 * o0 + e1 * o1 + e2 * o2) / (e0 + e1 + e2)
    branch_a = _dot(attn.astype(BF16), wba_ref[...])
    y = _gelu_tanh(ys_ref[...])
    y = y * _sigmoid(_dot(y.astype(BF16), wglu_ref[...]) + bglu_ref[...])
    branch_s = _dot(y.astype(BF16), wbs_ref[...])
    mix = (_sigmoid(gate_ref[:, 0:D_MODEL]) * branch_a
           + _sigmoid(gate_ref[:, D_MODEL:2 * D_MODEL]) * branch_s)
    return x_ref[...] + _dot(mix.astype(BF16), wout_ref[...])


FF_CHUNK = D_FF // 2
N_MERGE_ROW_INPUTS = 2 * N_DIL_GROUPS + 3
N_MERGE_WEIGHTS = 5


def _merge_ffn_kernel(*refs, tm, shift, pad, dils):
    n_merge = N_MERGE_ROW_INPUTS + N_MERGE_WEIGHTS
    merge_refs, rest = refs[:n_merge], refs[n_merge:]
    carry_ref, g2_ref, wup_ref, cw_ref, cb_ref, wdn_ref, gf_ref, y_ref, state_ref, order_scr, a_scr = rest
    hist = 2 * shift

    @pl.when(pl.program_id(1) == 0)
    def _():
        a_scr[pad - hist:pad, :] = carry_ref[0]

    xf = _merge_tile(*merge_refs, order_scr, tm, dils)
    xn = _rmsnorm(xf, g2_ref[...]).astype(BF16)
    acc = jnp.zeros((tm, D_MODEL), F32)
    for c0 in range(0, D_FF, FF_CHUNK):
        cols = slice(c0, c0 + FF_CHUNK)
        a = _dot(xn, wup_ref[:, cols])
        val = _dot(xn, wup_ref[:, D_FF + c0:D_FF + c0 + FF_CHUNK])
        a_scr[pad:pad + tm, cols] = a
        a_m1 = a_scr[pad - shift:pad - shift + tm, cols]
        a_m2 = a_scr[pad - hist:pad - hist + tm, cols]
        conv = cb_ref[:, cols] + cw_ref[0:1, cols] * a_m2
        conv = conv + cw_ref[1:2, cols] * a_m1
        conv = conv + cw_ref[2:3, cols] * a
        act = conv * _sigmoid(conv) * val
        acc = acc + _dot(act.astype(BF16), wdn_ref[cols, :])
    tail = a_scr[pad + tm - hist:pad + tm, :]
    a_scr[pad - hist:pad, :] = tail
    state_ref[0] = tail
    y_ref[...] = _rmsnorm(xf + acc, gf_ref[...])


def _merge_ffn(attn_parts, ys, gates, x2d, merge_weights, carry, ffn_weights, n_seq, tm, shift):
    m = x2d.shape[0]
    tiles = m // n_seq // tm
    hist = 2 * shift
    pad = -(-hist // SUBLANES) * SUBLANES
    dils = tuple(p.shape[1] for p in attn_parts[::2])
    row = lambda width: pl.BlockSpec((tm, width), lambda b, j: (b * tiles + j, 0))
    res_spec = lambda dil: pl.BlockSpec((1, dil, tm // dil, GROUP_WIDTH), lambda b, j: (b, 0, j, 0))
    state_spec = pl.BlockSpec((1, hist, D_FF), lambda b, j: (b, 0, 0))
    assert len(merge_weights) == N_MERGE_WEIGHTS
    return pl.pallas_call(
        functools.partial(_merge_ffn_kernel, tm=tm, shift=shift, pad=pad, dils=dils),
        grid=(n_seq, tiles),
        in_specs=[*[res_spec(p.shape[1]) for p in attn_parts],
                  row(SSM_WIDTH), row(2 * D_MODEL), row(D_MODEL),
                  *[_resident(w.shape) for w in merge_weights],
                  state_spec, *[_resident(w.shape) for w in ffn_weights]],
        out_specs=[row(D_MODEL), state_spec],
        out_shape=[jax.ShapeDtypeStruct((m, D_MODEL), F32),
                   jax.ShapeDtypeStruct((n_seq, hist, D_FF), F32)],
        scratch_shapes=[pltpu.VMEM((len(attn_parts) * GROUP_WIDTH // LANES, tm, LANES), F32),
                        pltpu.VMEM((pad + tm, D_FF), F32)],
        compiler_params=_compiler_params("arbitrary", "arbitrary"),
        name="merge_ffn",
    )(*attn_parts, ys, gates, x2d, *merge_weights, carry, *ffn_weights)


def _kv_rows(qkv, keep):
    batch, dil, length, _ = qkv.shape
    n = keep // dil
    rows = qkv[:, :, length - n:, GROUP_WIDTH:]
    cols = rows.transpose(0, 3, 2, 1).reshape(batch, 2, HEADS_PER_GROUP, HEAD_DIM, keep)
    return cols.transpose(0, 4, 1, 2, 3)


def _prompt_layer(x, rel_bias, lw):
    batch, seq, _ = x.shape
    x2d = x.reshape(batch * seq, D_MODEL)
    dils = tuple(dil for _, dil in DIL_PATTERNS) + (SSM_CHUNK,)
    keeps = tuple(min(window, seq) for window, _ in DIL_PATTERNS)
    *qkvs, u, gates, kv0, kv1, kv2 = _in_proj(x2d, lw['norm1_g'], lw['w_in'], batch, IN_PROJ_TILE, dils, keeps)

    attn_parts, kv_new = [], []
    for g, (window, dil) in enumerate(DIL_PATTERNS):
        tab = rel_bias[:, g * HEADS_PER_GROUP:(g + 1) * HEADS_PER_GROUP]
        attn_parts.extend(_attn_prompt(qkvs[g], _prompt_bias(tab, dil)))
        tail = (kv0, kv1, kv2)[g].reshape(batch, 2, HEADS_PER_GROUP, HEAD_DIM, keeps[g])
        kv_new.append(tail.transpose(0, 4, 1, 2, 3))

    ops, a_re, a_im = _ssm_block_operators(SSM_CHUNK, *lw['ssm'])
    ys, h_re, h_im = _ssm_prompt(u, ops, a_re, a_im, lw['ssm_d'], batch, seq)

    carry = jnp.zeros((batch, CONV_W - 1, D_FF), F32)
    y, conv_state = _merge_ffn(attn_parts, ys, gates, x2d, lw['merge'], carry, lw['ffn'],
                               n_seq=batch, tm=MERGE_FFN_TILE, shift=1)
    return y.reshape(batch, seq, D_MODEL), (*kv_new, h_re, h_im, conv_state)


def _sample_layer(x, caches, h0_re, h0_im, conv_buf, rel_bias, lw):
    batch, t_new, _ = x.shape
    m = batch * t_new
    x2d = x.transpose(1, 0, 2).reshape(m, D_MODEL)
    *qkvs, u, gates = _in_proj(x2d, lw['norm1_g'], lw['w_in'], 1, m, (1,) * (N_DIL_GROUPS + 1))
    u = u.reshape(m, SSM_WIDTH)
    qkvs_bt = [q.reshape(t_new, batch, GROUP_QKV).transpose(1, 0, 2) for q in qkvs]

    tbs, tns, views = [], [], []
    for g, (window, dil) in enumerate(DIL_PATTERNS):
        tab = rel_bias[:, g * HEADS_PER_GROUP:(g + 1) * HEADS_PER_GROUP]
        n_cached = caches[g].shape[1]
        tb, tn = _sample_bias(tab, dil, t_new, n_cached)
        tbs.append(tb)
        tns.append(tn)
        views.append(caches[g].transpose(0, 2, 3, 4, 1).reshape(batch, 2 * GROUP_WIDTH, n_cached))
    parts = _attn_sample(qkvs_bt, views, tbs, jnp.stack(tns))
    attn_parts = [p.transpose(1, 0, 2).reshape(1, 1, m, GROUP_WIDTH) for p in parts]
    kv_new = [_kv_rows(q.reshape(batch, 1, t_new, GROUP_QKV), t_new) for q in qkvs_bt]

    ops, a_re, a_im = _ssm_block_operators(t_new, *lw['ssm'])
    ys, h_re, h_im = _ssm_sample(u, ops, a_re, a_im, lw['ssm_d'],
                                 h0_re.reshape(batch, -1), h0_im.reshape(batch, -1), t_new)

    carry = conv_buf.transpose(1, 0, 2).reshape(1, (CONV_W - 1) * batch, D_FF)
    y, conv_state = _merge_ffn(attn_parts, ys, gates, x2d, lw['merge'], carry, lw['ffn'],
                               n_seq=1, tm=m, shift=batch)
    y = y.reshape(t_new, batch, D_MODEL).transpose(1, 0, 2)
    conv_state = conv_state.reshape(CONV_W - 1, batch, D_FF).transpose(1, 0, 2)
    state_shape = (batch, N_SSM_GROUPS, SSM_STATE)
    return y, (*kv_new, h_re.reshape(state_shape), h_im.reshape(state_shape), conv_state)


IN_PROJ_TILE = 512
MERGE_FFN_TILE = 256


def kernel(x_prompt, x_sample, cache_kv_w128, cache_kv_w512, cache_kv_w2048, state_ssm_re, state_ssm_im, state_ffn_conv, rel_bias, norm1_g, w_in, ssm_log_dt, ssm_lambda_re, ssm_lambda_im, ssm_b_re, ssm_b_im, ssm_c_re, ssm_c_im, ssm_d, w_glu, b_glu, w_branch_attn, w_branch_ssm, w_out, norm2_g, w_up, conv_w, conv_b, w_down, norm_f_g):
    depth = w_in.shape[0]
    hp, hs = x_prompt, x_sample
    st_p, st_s = [], []
    gf = norm_f_g.reshape(1, D_MODEL)
    for l in range(depth):
        last = l == depth - 1
        lw = {
            'norm1_g': norm1_g[l].reshape(1, D_MODEL),
            'w_in': _group_major_columns(w_in[l]).astype(BF16),
            'ssm': (ssm_log_dt[l], ssm_lambda_re[l], ssm_lambda_im[l],
                    ssm_b_re[l], ssm_b_im[l], ssm_c_re[l], ssm_c_im[l]),
            'ssm_d': ssm_d[l].reshape(1, SSM_WIDTH),
            'merge': (w_glu[l].astype(BF16), b_glu[l].reshape(1, SSM_WIDTH),
                      w_branch_attn[l].astype(BF16), w_branch_ssm[l].astype(BF16), w_out[l].astype(BF16)),
            'ffn': (norm2_g[l].reshape(1, D_MODEL), w_up[l].astype(BF16), conv_w[l],
                    conv_b[l].reshape(1, D_FF), w_down[l].astype(BF16), gf),
        }
        assert last, "the final RMSNorm is fused into the last layer's ffn kernel"
        hp, sp = _prompt_layer(hp, rel_bias, lw)
        hs, ss = _sample_layer(hs, (cache_kv_w128[l], cache_kv_w512[l], cache_kv_w2048[l]),
                               state_ssm_re[l], state_ssm_im[l], state_ffn_conv[l], rel_bias, lw)
        st_p.append(sp)
        st_s.append(ss)
    stack = lambda states, i: jnp.stack([st[i] for st in states], axis=0)
    return (hp, hs, *[stack(st_p, i) for i in range(6)], *[stack(st_s, i) for i in range(6)])
```

```python
import functools
import math

import jax
import jax.numpy as jnp
import numpy as np
from jax import lax
from jax.experimental import pallas as pl
from jax.experimental.pallas import tpu as pltpu

F32 = jnp.float32
BF16 = jnp.bfloat16

D_MODEL = 1024
HEAD_DIM = 64
HEADS_PER_GROUP = 4
DIL_PATTERNS = ((128, 1), (512, 4), (2048, 16))
N_DIL_GROUPS = len(DIL_PATTERNS)
GROUP_WIDTH = HEADS_PER_GROUP * HEAD_DIM
QK_WIDTH = N_DIL_GROUPS * GROUP_WIDTH
QKV_WIDTH = 3 * QK_WIDTH
GROUP_QKV = 3 * GROUP_WIDTH
QBLOCK = 128
SSM_GROUP = 16
SSM_STATE = 64
SSM_WIDTH = D_MODEL // 2
N_SSM_GROUPS = SSM_WIDTH // SSM_GROUP
SSM_CHUNK = 16
D_FF = 2816
CONV_W = 3
N_BUCKETS = 32
MAX_DISTANCE = 2048
NORM_EPS = 1e-6
NEG_INF = -1e30
U_START = QKV_WIDTH
GATE_START = U_START + SSM_WIDTH
IN_WIDTH = GATE_START + 2 * D_MODEL
QK_SCALE = HEAD_DIM ** -0.5

VMEM_LIMIT_BYTES = 60 * 1024 * 1024
SUBLANES = 8
LANES = 128


def _compiler_params(*semantics):
    return pltpu.CompilerParams(dimension_semantics=semantics, vmem_limit_bytes=VMEM_LIMIT_BYTES)


def _resident(shape):
    nd = len(shape)
    return pl.BlockSpec(shape, lambda *_: (0,) * nd, pipeline_mode=pl.Buffered(1))


def _rmsnorm(xf, g):
    y = xf * lax.rsqrt(jnp.mean(xf * xf, axis=-1, keepdims=True) + NORM_EPS)
    return y * g


def _sigmoid(x):
    return 1.0 / (1.0 + jnp.exp(-x))


def _dot(a, b):
    return jnp.dot(a, b, preferred_element_type=F32)


def _dot_nt(a, b):
    return lax.dot_general(a, b, (((1,), (1,)), ((), ())), preferred_element_type=F32)


def _in_proj_kernel(x_ref, g_ref, w_ref, qkv0_ref, qkv1_ref, qkv2_ref, u_ref, gate_ref, *rest,
                    tm, dils, tails, tiles):
    if tails is None:
        tail_refs, (xn_scr,) = (), rest
    else:
        tail_refs, (xn_scr, tok_scr) = rest[:N_DIL_GROUPS], rest[N_DIL_GROUPS:]
    xf = _rmsnorm(x_ref[...], g_ref[...])
    n_lane_blocks = D_MODEL // LANES
    if any(dil > 1 for dil in dils):
        for k in range(n_lane_blocks):
            xn_scr[k] = xf[:, k * LANES:(k + 1) * LANES]
    by_residue = {1: xf.astype(BF16)}

    def rows_by_residue(dil):
        if dil not in by_residue:
            n = tm // dil
            xr = jnp.concatenate(
                [jnp.concatenate([xn_scr[k, pl.ds(r, n, stride=dil), :] for k in range(n_lane_blocks)], axis=1)
                 for r in range(dil)], axis=0)
            by_residue[dil] = xr.astype(BF16)
        return by_residue[dil]

    outputs = [(ref, g * GROUP_QKV, GROUP_QKV) for g, ref in enumerate((qkv0_ref, qkv1_ref, qkv2_ref))]
    outputs.append((u_ref, U_START, SSM_WIDTH))
    for (ref, start, width), dil in zip(outputs, dils):
        n = tm // dil
        res = _dot(rows_by_residue(dil), w_ref[:, start:start + width])
        for r in range(dil):
            ref[0, r] = res[r * n:(r + 1) * n]
    for c0 in range(0, 2 * D_MODEL, D_MODEL):
        logits = _dot(by_residue[1], w_ref[:, GATE_START + c0:GATE_START + c0 + D_MODEL])
        gate_ref[:, c0:c0 + D_MODEL] = _sigmoid(logits).astype(BF16)
    if tails is None:
        return
    kv_blocks = 2 * GROUP_WIDTH // LANES
    for (ref, _, _), dil, keep, tail_ref in zip(outputs, dils, tails, tail_refs):
        n = tm // dil
        kept = min(keep, tm)

        @pl.when(pl.program_id(1) >= tiles - max(keep // tm, 1))
        def _(ref=ref, dil=dil, n=n, kept=kept, tail_ref=tail_ref):
            if dil == 1:
                kv = ref[0, 0, :, GROUP_WIDTH:]
            else:
                for r in range(dil):
                    for k in range(kv_blocks):
                        lanes = slice(GROUP_WIDTH + k * LANES, GROUP_WIDTH + (k + 1) * LANES)
                        tok_scr[k, pl.ds(r, n, stride=dil), :] = ref[0, r, :, lanes]
                kv = jnp.concatenate([tok_scr[k] for k in range(kv_blocks)], axis=1)
            tail_ref[0] = kv[tm - kept:, :].T


def _group_major_columns(w_in):
    parts = []
    for g in range(N_DIL_GROUPS):
        for base in (0, QK_WIDTH, 2 * QK_WIDTH):
            parts.append(w_in[:, base + g * GROUP_WIDTH:base + (g + 1) * GROUP_WIDTH])
    parts.append(w_in[:, U_START:])
    return jnp.concatenate(parts, axis=1)


def _in_proj(x2d, g, w_bf16, n_seq, tm, dils, tails=None):
    m = x2d.shape[0]
    seq = m // n_seq
    tiles = seq // tm
    row = lambda width: pl.BlockSpec((tm, width), lambda b, j: (b * tiles + j, 0))
    res_spec = lambda dil, width: pl.BlockSpec((1, dil, tm // dil, width), lambda b, j: (b, 0, j, 0))
    widths = (GROUP_QKV,) * N_DIL_GROUPS + (SSM_WIDTH,)
    out_specs = [*[res_spec(d, w) for d, w in zip(dils, widths)], row(2 * D_MODEL)]
    out_shape = [*[jax.ShapeDtypeStruct((n_seq, d, seq // d, w), F32) for d, w in zip(dils, widths)],
                 jax.ShapeDtypeStruct((m, 2 * D_MODEL), BF16)]
    scratch = [pltpu.VMEM((D_MODEL // LANES, tm, LANES), F32)]
    if tails is not None:
        for keep in tails:
            assert keep % tm == 0 or tm % keep == 0
            first = tiles - max(keep // tm, 1)
            out_specs.append(pl.BlockSpec((1, 2 * GROUP_WIDTH, min(keep, tm)),
                                          lambda b, j, first=first: (b, 0, jnp.maximum(j - first, 0))))
            out_shape.append(jax.ShapeDtypeStruct((n_seq, 2 * GROUP_WIDTH, keep), F32))
        scratch.append(pltpu.VMEM((2 * GROUP_WIDTH // LANES, tm, LANES), F32))
    return pl.pallas_call(
        functools.partial(_in_proj_kernel, tm=tm, dils=dils, tails=tails, tiles=tiles),
        grid=(n_seq, tiles),
        in_specs=[row(D_MODEL), _resident((1, D_MODEL)), _resident((D_MODEL, IN_WIDTH))],
        out_specs=out_specs,
        out_shape=out_shape,
        scratch_shapes=scratch,
        compiler_params=_compiler_params("arbitrary", "arbitrary"),
        name="in_proj",
    )(x2d, g, w_bf16)


def _bucket_starts():
    max_exact = N_BUCKETS // 2
    n = np.arange(max_exact, MAX_DISTANCE + 1)
    large = max_exact + (np.log(n.astype(np.float32) / np.float32(max_exact))
                         / np.float32(math.log(MAX_DISTANCE / max_exact))
                         * np.float32(N_BUCKETS - max_exact)).astype(np.int32)
    large = np.minimum(large, N_BUCKETS - 1)
    return [int(n[np.argmax(large >= k)]) for k in range(max_exact + 1, N_BUCKETS)]


def _rel_bucket(dist):
    max_exact = N_BUCKETS // 2
    n = jnp.maximum(dist, 0)
    large = max_exact + sum((n >= start).astype(jnp.int32) for start in _bucket_starts())
    return jnp.where(n < max_exact, n, large)


def _masked_bias(tab, strides, valid, dil):
    n_dist = QBLOCK
    bucket = _rel_bucket(jnp.clip(strides, 0, n_dist) * dil).reshape(1, -1)
    onehot = (bucket == jnp.arange(N_BUCKETS)[:, None]).astype(F32)
    bias = jnp.dot(tab.astype(F32).T, onehot, precision=lax.Precision.HIGHEST)
    bias = jnp.where(valid.reshape(1, -1), bias, NEG_INF)
    return bias.reshape((tab.shape[1],) + strides.shape)


def _prompt_bias(tab, dil):
    qi = jnp.arange(QBLOCK)[:, None]
    ki = jnp.arange(QBLOCK)[None, :]
    j_prev = qi + QBLOCK - ki
    j_cur = qi - ki
    bias = jnp.stack([_masked_bias(tab, j_prev, j_prev <= QBLOCK, dil),
                      _masked_bias(tab, j_cur, j_cur >= 0, dil)], axis=0)
    return bias.transpose(0, 2, 1, 3).reshape(2, QBLOCK, HEADS_PER_GROUP * QBLOCK)


def _sample_bias(tab, dil, t_new, n_cached):
    t = jnp.arange(t_new)[:, None]
    delta_buf = n_cached + t - jnp.arange(n_cached)[None, :]
    ok_buf = (delta_buf % dil == 0) & (delta_buf // dil <= QBLOCK)
    m = jnp.arange(QBLOCK)[None, :]
    delta_new = t - m
    ok_new = (m < t_new) & (delta_new >= 0) & (delta_new % dil == 0)
    b_buf = _masked_bias(tab, delta_buf // dil, ok_buf, dil)
    b_new = _masked_bias(tab, delta_new // dil, ok_new, dil)
    return (b_buf.reshape(HEADS_PER_GROUP * t_new, n_cached),
            b_new.reshape(HEADS_PER_GROUP * t_new, QBLOCK))


def _attn_prompt_kernel(q_ref, kc_ref, vc_ref, bias_ref, o_ref, lse_ref, kprev_scr, vprev_scr, *, nq):
    first_tile = pl.program_id(2) == 0

    @pl.when(first_tile)
    def _():
        kprev_scr[...] = jnp.zeros_like(kprev_scr)
        vprev_scr[...] = jnp.zeros_like(vprev_scr)

    stacked = (HEADS_PER_GROUP * QBLOCK, GROUP_WIDTH)
    own_head = (lax.broadcasted_iota(jnp.int32, stacked, 0) // QBLOCK
                == lax.broadcasted_iota(jnp.int32, stacked, 1) // HEAD_DIM)
    lane_head = lax.broadcasted_iota(jnp.int32, (QBLOCK, GROUP_WIDTH), 1) // HEAD_DIM

    def per_head(x):
        xb = x.astype(BF16)
        return jnp.where(own_head, jnp.concatenate([xb] * HEADS_PER_GROUP, axis=0), 0)

    def on_head_lanes(cols):
        out = jnp.broadcast_to(cols[-1], (QBLOCK, GROUP_WIDTH))
        for h in range(HEADS_PER_GROUP - 2, -1, -1):
            out = jnp.where(lane_head == h, cols[h], out)
        return out

    block_rows = [slice(i * QBLOCK, (i + 1) * QBLOCK) for i in range(nq)]
    qs = [(q_ref[0, 0, rows, :] * QK_SCALE).astype(BF16) for rows in block_rows]
    k_blocks = [kprev_scr[...]] + [per_head(kc_ref[0, 0, rows, :]) for rows in block_rows]
    v_blocks = [vprev_scr[...]] + [per_head(vc_ref[0, 0, rows, :]) for rows in block_rows]
    kprev_scr[...] = k_blocks[-1]
    vprev_scr[...] = v_blocks[-1]

    def users(per_query_cur, per_query_prev, j):
        parts = ([per_query_cur[j - 1]] if j >= 1 else []) + ([per_query_prev[j]] if j < nq else [])
        return parts[0] if len(parts) == 1 else jnp.concatenate(parts, axis=0)

    s_prev, s_cur = [None] * nq, [None] * nq
    for j in range(nq + 1):
        s = _dot_nt(users(qs, qs, j), k_blocks[j])
        if j >= 1:
            s_cur[j - 1] = s[:QBLOCK] + bias_ref[1]
        if j < nq:
            s_prev[j] = s[-QBLOCK:] + bias_ref[0]
    s_prev[0] = jnp.where(first_tile, NEG_INF, s_prev[0])

    p_prev, p_cur, dens, lses = [], [], [], []
    for i in range(nq):
        pp_heads, pc_heads, den_heads, lse_heads = [], [], [], []
        for h in range(HEADS_PER_GROUP):
            keys = slice(h * QBLOCK, (h + 1) * QBLOCK)
            sp, sc = s_prev[i][:, keys], s_cur[i][:, keys]
            m = jnp.max(jnp.maximum(sp, sc), axis=-1, keepdims=True)
            pp, pc = jnp.exp(sp - m), jnp.exp(sc - m)
            den = jnp.sum(pp + pc, axis=-1, keepdims=True)
            pp_heads.append(pp.astype(BF16))
            pc_heads.append(pc.astype(BF16))
            den_heads.append(den)
            lse_heads.append(m + jnp.log(den))
        p_prev.append(jnp.concatenate(pp_heads, axis=1))
        p_cur.append(jnp.concatenate(pc_heads, axis=1))
        dens.append(on_head_lanes(den_heads))
        lses.append(on_head_lanes(lse_heads))

    o = [None] * nq
    for j in range(nq + 1):
        r = _dot(users(p_cur, p_prev, j), v_blocks[j])
        if j >= 1:
            o[j - 1] = o[j - 1] + r[:QBLOCK]
        if j < nq:
            o[j] = r[-QBLOCK:]
    for i, rows in enumerate(block_rows):
        o_ref[0, 0, rows, :] = o[i] / dens[i]
        lse_ref[0, 0, rows, :] = lses[i]


ATTN_QUERY_BLOCKS = 8


def _attn_prompt(qkv, bias):
    batch, dil, length, _ = qkv.shape
    tq = min(ATTN_QUERY_BLOCKS * QBLOCK, length)
    nq = tq // QBLOCK

    def cur(col):
        return pl.BlockSpec((1, 1, tq, GROUP_WIDTH), lambda b, r, n: (b, r, n, col))

    stacked = pltpu.VMEM((HEADS_PER_GROUP * QBLOCK, GROUP_WIDTH), BF16)
    out_sds = jax.ShapeDtypeStruct((batch, dil, length, GROUP_WIDTH), F32)
    return pl.pallas_call(
        functools.partial(_attn_prompt_kernel, nq=nq),
        grid=(batch, dil, length // tq),
        in_specs=[cur(0), cur(1), cur(2), _resident((2, QBLOCK, HEADS_PER_GROUP * QBLOCK))],
        out_specs=[cur(0), cur(0)],
        out_shape=[out_sds, out_sds],
        scratch_shapes=[stacked, stacked],
        compiler_params=_compiler_params("arbitrary", "arbitrary", "arbitrary"),
        name=f"attn_prompt_d{dil}",
    )(qkv, qkv, qkv, bias)


def _attn_sample_kernel(q0_ref, q1_ref, q2_ref, c0_ref, c1_ref, c2_ref, tb0_ref, tb1_ref, tb2_ref, tn_ref,
                        o0_ref, l0_ref, o1_ref, l1_ref, o2_ref, l2_ref, kn_scr, vn_scr, *, t_new):
    n_rows = HEADS_PER_GROUP * t_new
    row_w = lax.broadcasted_iota(jnp.int32, (n_rows, GROUP_WIDTH), 0)
    lane_w = lax.broadcasted_iota(jnp.int32, (n_rows, GROUP_WIDTH), 1)
    own_head = (row_w // t_new) == (lane_w // HEAD_DIM)

    def fold_heads(x):
        x = jnp.where(own_head, x, 0.0)
        out = x[0:t_new]
        for h in range(1, HEADS_PER_GROUP):
            out = out + x[h * t_new:(h + 1) * t_new]
        return out

    caches = (c0_ref, c1_ref, c2_ref)
    cache_bias = (tb0_ref, tb1_ref, tb2_ref)
    outs = ((o0_ref, l0_ref), (o1_ref, l1_ref), (o2_ref, l2_ref))
    for g in range(N_DIL_GROUPS):
        qkv_ref = (q0_ref, q1_ref, q2_ref)[g]
        q = qkv_ref[0, :, 0:GROUP_WIDTH] * QK_SCALE
        q_rows = jnp.where(own_head, jnp.concatenate([q] * HEADS_PER_GROUP, axis=0), 0.0).astype(BF16)
        kn_scr[...] = jnp.zeros_like(kn_scr)
        vn_scr[...] = jnp.zeros_like(vn_scr)
        kn_scr[0:t_new, :] = qkv_ref[0, :, GROUP_WIDTH:2 * GROUP_WIDTH]
        vn_scr[0:t_new, :] = qkv_ref[0, :, 2 * GROUP_WIDTH:3 * GROUP_WIDTH]
        cache = caches[g]
        k_t = cache[0, 0:GROUP_WIDTH, :].astype(BF16)
        v_t = cache[0, GROUP_WIDTH:2 * GROUP_WIDTH, :].astype(BF16)
        s_buf = _dot(q_rows, k_t) + cache_bias[g][...]
        s_new = _dot_nt(q_rows, kn_scr[...].astype(BF16)) + tn_ref[g]
        m = jnp.maximum(jnp.max(s_buf, axis=-1, keepdims=True), jnp.max(s_new, axis=-1, keepdims=True))
        p_buf = jnp.exp(s_buf - m)
        p_new = jnp.exp(s_new - m)
        den = jnp.sum(p_buf, axis=-1, keepdims=True) + jnp.sum(p_new, axis=-1, keepdims=True)
        o = _dot_nt(p_buf.astype(BF16), v_t) + _dot(p_new.astype(BF16), vn_scr[...].astype(BF16))
        o_ref, l_ref = outs[g]
        o_ref[0] = fold_heads(o / den)
        l_ref[0] = fold_heads(jnp.broadcast_to(m + jnp.log(den), (n_rows, GROUP_WIDTH)))


def _attn_sample(qkvs, caches, tbs, tn):
    batch, t_new, _ = qkvs[0].shape
    n_rows = HEADS_PER_GROUP * t_new
    cache_specs = [pl.BlockSpec((1,) + c.shape[1:], lambda b: (b, 0, 0)) for c in caches]
    qkv_spec = pl.BlockSpec((1, t_new, GROUP_QKV), lambda b: (b, 0, 0))
    out_spec = pl.BlockSpec((1, t_new, GROUP_WIDTH), lambda b: (b, 0, 0))
    out_sds = jax.ShapeDtypeStruct((batch, t_new, GROUP_WIDTH), F32)
    return pl.pallas_call(
        functools.partial(_attn_sample_kernel, t_new=t_new),
        grid=(batch,),
        in_specs=[*[qkv_spec] * N_DIL_GROUPS, *cache_specs, *[_resident(t.shape) for t in tbs],
                  _resident((N_DIL_GROUPS, n_rows, QBLOCK))],
        out_specs=[out_spec] * (2 * N_DIL_GROUPS),
        out_shape=[out_sds] * (2 * N_DIL_GROUPS),
        scratch_shapes=[pltpu.VMEM((QBLOCK, GROUP_WIDTH), F32), pltpu.VMEM((QBLOCK, GROUP_WIDTH), F32)],
        compiler_params=_compiler_params("arbitrary"),
        name="attn_sample",
    )(*qkvs, *caches, *tbs, tn)


def _ssm_chunk_terms(chunk, log_dt, lam_re, lam_im, b_re, b_im, c_re, c_im):
    hi = lax.Precision.HIGHEST
    dt = jnp.exp(log_dt.astype(F32))[:, None]
    lr, li = lam_re.astype(F32), lam_im.astype(F32)
    mag = jnp.exp(lr * dt)
    ab_re, ab_im = mag * jnp.cos(li * dt), mag * jnp.sin(li * dt)
    g, n = lr.shape
    p = b_re.shape[-1]
    den = lr * lr + li * li
    nr, ni = ab_re - 1.0, ab_im
    coef_re = (nr * lr + ni * li) / den
    coef_im = (ni * lr - nr * li) / den
    br, bi = b_re.astype(F32), b_im.astype(F32)
    bb_re = coef_re[..., None] * br - coef_im[..., None] * bi
    bb_im = coef_re[..., None] * bi + coef_im[..., None] * br
    k = jnp.arange(chunk + 1, dtype=F32)[:, None, None]
    pw_mag = jnp.exp(k * (lr * dt)[None])
    pw_re, pw_im = pw_mag * jnp.cos(k * (li * dt)[None]), pw_mag * jnp.sin(k * (li * dt)[None])
    bt_re, bt_im = bb_re.transpose(0, 2, 1)[None], bb_im.transpose(0, 2, 1)[None]
    pk_re, pk_im = pw_re[:chunk, :, None, :], pw_im[:chunk, :, None, :]
    akb_re = pk_re * bt_re - pk_im * bt_im
    akb_im = pk_re * bt_im + pk_im * bt_re
    cr, ci = c_re.astype(F32)[None], c_im.astype(F32)[None]
    pe_re, pe_im = pw_re[1:, :, None, :], pw_im[1:, :, None, :]
    e_re = cr * pe_re - ci * pe_im
    e_im = cr * pe_im + ci * pe_re
    kern = (jnp.einsum('kgqn,gpn->kgqp', akb_re, cr[0], precision=hi)
            - jnp.einsum('kgqn,gpn->kgqp', akb_im, ci[0], precision=hi))
    rows = lambda x: x.reshape(chunk, g * p, x.shape[-1])
    return ((rows(akb_re), rows(akb_im)), (rows(e_re), rows(e_im)), rows(kern), (pw_re, pw_im))


SSM_BLOCK_GROUPS = 8
SSM_BLOCK_CH = SSM_BLOCK_GROUPS * SSM_GROUP
SSM_BLOCK_STATE = SSM_BLOCK_GROUPS * SSM_STATE
N_SSM_BLOCKS = N_SSM_GROUPS // SSM_BLOCK_GROUPS


def _ssm_operator_kernel(akbre_ref, akbim_ref, ere_ref, eim_ref, kern_ref,
                         wre_ref, wim_ref, m_ref, etre_ref, etim_ref, *, chunk):
    gb, p, n = SSM_BLOCK_GROUPS, SSM_GROUP, SSM_STATE

    def copies(width, count):
        src = lax.broadcasted_iota(jnp.int32, (width, count * width), 0)
        dst = lax.broadcasted_iota(jnp.int32, (width, count * width), 1)
        return jnp.where(src == dst % width, 1.0, 0.0).astype(BF16)

    def own_group(width):
        row = lax.broadcasted_iota(jnp.int32, (SSM_BLOCK_CH, gb * width), 0)
        col = lax.broadcasted_iota(jnp.int32, (SSM_BLOCK_CH, gb * width), 1)
        return row // p == col // width

    to_states, own_states = copies(n, gb), own_group(n)

    def over_states(piece):
        return jnp.where(own_states, _dot(piece.astype(BF16), to_states), 0.0).astype(BF16)

    for s in range(chunk):
        rows = slice(s * SSM_BLOCK_CH, (s + 1) * SSM_BLOCK_CH)
        wre_ref[0, rows, :] = over_states(akbre_ref[chunk - 1 - s, 0])
        wim_ref[0, rows, :] = over_states(akbim_ref[chunk - 1 - s, 0])
        etre_ref[0, rows, :] = over_states(ere_ref[s, 0])
        etim_ref[0, rows, :] = over_states(-eim_ref[s, 0])
    to_channels, own_channels = copies(p, gb), own_group(p)
    lags = [jnp.where(own_channels, _dot(kern_ref[k, 0].astype(BF16), to_channels), 0.0).astype(BF16)
            for k in range(chunk)]
    zero = jnp.zeros((SSM_BLOCK_CH, SSM_BLOCK_CH), BF16)
    for s in range(chunk):
        m_ref[0, s * SSM_BLOCK_CH:(s + 1) * SSM_BLOCK_CH, :] = jnp.concatenate(
            [zero] * s + lags[:chunk - s], axis=1)


def _ssm_block_operators(chunk, terms):
    (akb_re, akb_im), (e_re, e_im), kern, (pw_re, pw_im) = terms
    assert chunk <= kern.shape[0]
    a_re, a_im = pw_re[chunk], pw_im[chunk]
    nb, gb, p, n = N_SSM_BLOCKS, SSM_BLOCK_GROUPS, SSM_GROUP, SSM_STATE
    pieces = [x[:chunk].reshape(chunk, nb, SSM_BLOCK_CH, x.shape[-1])
              for x in (akb_re, akb_im, e_re, e_im, kern)]
    piece_spec = lambda x: pl.BlockSpec((chunk, 1) + x.shape[2:], lambda i: (0, i, 0, 0))
    out_spec = lambda shape: pl.BlockSpec((1,) + shape[1:], lambda i: (i, 0, 0))
    x_width = chunk * SSM_BLOCK_CH
    out_shapes = [(nb, x_width, gb * n), (nb, x_width, gb * n), (nb, x_width, x_width),
                  (nb, x_width, gb * n), (nb, x_width, gb * n)]
    ops = pl.pallas_call(
        functools.partial(_ssm_operator_kernel, chunk=chunk),
        grid=(nb,),
        in_specs=[piece_spec(x) for x in pieces],
        out_specs=[out_spec(s) for s in out_shapes],
        out_shape=[jax.ShapeDtypeStruct(s, BF16) for s in out_shapes],
        compiler_params=_compiler_params("arbitrary"),
        name="ssm_operators",
    )(*pieces)
    return ops, a_re.reshape(nb, 1, gb * n), a_im.reshape(nb, 1, gb * n)


SSM_X_WIDTH = SSM_CHUNK * SSM_BLOCK_CH


def _ssm_prompt_kernel(u_ref, wre_ref, wim_ref, m_ref, etre_ref, etim_ref, are_ref, aim_ref, d_ref,
                       y_ref, hre_ref, him_ref, x_scr, sre_scr, sim_scr, *, batch, chunks):
    phase, b = pl.program_id(1), pl.program_id(2)
    for t in range(SSM_CHUNK):
        x_scr[:, t * SSM_BLOCK_CH:(t + 1) * SSM_BLOCK_CH] = u_ref[0, t].astype(BF16)
    rows = pl.ds(b, chunks, stride=batch)
    state_blocks = SSM_BLOCK_STATE // LANES

    @pl.when(phase == 0)
    def _():
        x = x_scr[...]
        g_re, g_im = _dot(x, wre_ref[0]), _dot(x, wim_ref[0])
        for k in range(state_blocks):
            sre_scr[k, rows, :] = g_re[:, k * LANES:(k + 1) * LANES]
            sim_scr[k, rows, :] = g_im[:, k * LANES:(k + 1) * LANES]

    @pl.when((phase == 1) & (b == 0))
    def _():
        same_lanes = lambda ref, k: ref[0][:, k * LANES:(k + 1) * LANES]

        def step(c, carry):
            same_chunk = pl.ds(c * batch, batch)
            out = []
            for k, (h_re, h_im) in enumerate(carry):
                a_re, a_im = same_lanes(are_ref, k), same_lanes(aim_ref, k)
                g_re, g_im = sre_scr[k, same_chunk, :], sim_scr[k, same_chunk, :]
                sre_scr[k, same_chunk, :] = h_re
                sim_scr[k, same_chunk, :] = h_im
                out.append((a_re * h_re - a_im * h_im + g_re, a_re * h_im + a_im * h_re + g_im))
            return tuple(out)

        zero = jnp.zeros((batch, LANES), F32)
        final = lax.fori_loop(0, chunks, step, ((zero, zero),) * state_blocks)
        for k, (h_re, h_im) in enumerate(final):
            hre_ref[0, :, k * LANES:(k + 1) * LANES] = h_re
            him_ref[0, :, k * LANES:(k + 1) * LANES] = h_im

    @pl.when(phase == 1)
    def _():
        h_re = jnp.concatenate([sre_scr[k, rows, :] for k in range(state_blocks)], axis=1).astype(BF16)
        h_im = jnp.concatenate([sim_scr[k, rows, :] for k in range(state_blocks)], axis=1).astype(BF16)
        pair_w = 2 * SSM_BLOCK_CH
        for j in range(SSM_CHUNK // 2):
            cols = slice(j * pair_w, (j + 1) * pair_w)
            k_in = (j + 1) * pair_w
            yj = (_dot(x_scr[:, :k_in], m_ref[0, :k_in, cols])
                  + _dot_nt(h_re, etre_ref[0, cols, :]) + _dot_nt(h_im, etim_ref[0, cols, :]))
            for i in range(2):
                t = 2 * j + i
                y_ref[pl.ds(t, chunks, stride=SSM_CHUNK), :] = (
                    yj[:, i * SSM_BLOCK_CH:(i + 1) * SSM_BLOCK_CH] + d_ref[0] * u_ref[0, t])


def _ssm_prompt(u, ops, a_re, a_im, d, batch, seq):
    chunks = seq // SSM_CHUNK
    d = d.reshape(N_SSM_BLOCKS, 1, SSM_BLOCK_CH)
    op_spec = lambda arr: pl.BlockSpec((1,) + arr.shape[1:], lambda g, ph, b: (g, 0, 0),
                                       pipeline_mode=pl.Buffered(1))
    state_spec = pl.BlockSpec((1, batch, SSM_BLOCK_STATE), lambda g, ph, b: (g, 0, 0))
    state_sds = jax.ShapeDtypeStruct((N_SSM_BLOCKS, batch, SSM_BLOCK_STATE), F32)
    y, h_re, h_im = pl.pallas_call(
        functools.partial(_ssm_prompt_kernel, batch=batch, chunks=chunks),
        grid=(N_SSM_BLOCKS, 2, batch),
        in_specs=[pl.BlockSpec((1, SSM_CHUNK, chunks, SSM_BLOCK_CH), lambda g, ph, b: (b, 0, 0, g)),
                  *[op_spec(o) for o in ops], op_spec(a_re), op_spec(a_im), op_spec(d)],
        out_specs=[pl.BlockSpec((seq, SSM_BLOCK_CH), lambda g, ph, b: (b * ph, g)), state_spec, state_spec],
        out_shape=[jax.ShapeDtypeStruct((batch * seq, SSM_WIDTH), F32), state_sds, state_sds],
        scratch_shapes=[pltpu.VMEM((chunks, SSM_X_WIDTH), BF16),
                        pltpu.VMEM((SSM_BLOCK_STATE // LANES, batch * chunks, LANES), F32),
                        pltpu.VMEM((SSM_BLOCK_STATE // LANES, batch * chunks, LANES), F32)],
        compiler_params=_compiler_params("arbitrary", "arbitrary", "arbitrary"),
        name="ssm_prompt",
    )(u, *ops, a_re, a_im, d)

    def by_sequence(h):
        h = h.reshape(N_SSM_BLOCKS, batch, SSM_BLOCK_GROUPS, SSM_STATE).transpose(1, 0, 2, 3)
        return h.reshape(batch, N_SSM_GROUPS, SSM_STATE)

    return y, by_sequence(h_re), by_sequence(h_im)


def _ssm_sample_kernel(u_ref, wre_ref, wim_ref, m_ref, etre_ref, etim_ref, are_ref, aim_ref, d_ref,
                       h0re_ref, h0im_ref, y_ref, hre_ref, him_ref, *, t_new, batch):
    for nb in range(N_SSM_BLOCKS):
        ch = slice(nb * SSM_BLOCK_CH, (nb + 1) * SSM_BLOCK_CH)
        st = slice(nb * SSM_BLOCK_STATE, (nb + 1) * SSM_BLOCK_STATE)
        x = jnp.concatenate([u_ref[t * batch:(t + 1) * batch, ch] for t in range(t_new)], axis=1).astype(BF16)
        h_re, h_im = h0re_ref[:, st], h0im_ref[:, st]
        a_re, a_im = are_ref[nb], aim_ref[nb]
        hre_ref[:, st] = a_re * h_re - a_im * h_im + _dot(x, wre_ref[nb])
        him_ref[:, st] = a_re * h_im + a_im * h_re + _dot(x, wim_ref[nb])
        y = (_dot(x, m_ref[nb]) + _dot_nt(h_re.astype(BF16), etre_ref[nb])
             + _dot_nt(h_im.astype(BF16), etim_ref[nb]))
        for t in range(t_new):
            rows = slice(t * batch, (t + 1) * batch)
            y_ref[rows, ch] = y[:, t * SSM_BLOCK_CH:(t + 1) * SSM_BLOCK_CH] + d_ref[:, ch] * u_ref[rows, ch]


def _ssm_sample(u, ops, a_re, a_im, d, h0_re, h0_im, t_new):
    batch = u.shape[0] // t_new
    args = (u, *ops, a_re, a_im, d, h0_re, h0_im)
    state_sds = jax.ShapeDtypeStruct((batch, N_SSM_GROUPS * SSM_STATE), F32)
    out_shape = [jax.ShapeDtypeStruct(u.shape, F32), state_sds, state_sds]
    whole = lambda shape: pl.BlockSpec(shape, lambda i, nd=len(shape): (0,) * nd)
    return pl.pallas_call(
        functools.partial(_ssm_sample_kernel, t_new=t_new, batch=batch),
        grid=(1,),
        in_specs=[whole(a.shape) for a in args],
        out_specs=[whole(s.shape) for s in out_shape],
        out_shape=out_shape,
        compiler_params=_compiler_params("arbitrary"),
        name="ssm_sample",
    )(*args)


def _gelu_tanh(x):
    return 0.5 * x * (1.0 + jnp.tanh(math.sqrt(2.0 / math.pi) * (x + 0.044715 * (x * x * x))))


def _merge_tile(o0_ref, l0_ref, o1_ref, l1_ref, o2_ref, l2_ref, ys_ref, gate_ref, x_ref,
                wglu_ref, bglu_ref, wba_ref, wbs_ref, wout_ref, order_scr, tm, dils):
    def row_order(ref, dil, slot):
        if dil == 1:
            return ref[0, 0]
        n = tm // dil
        halves = GROUP_WIDTH // LANES
        for r in range(dil):
            for k in range(halves):
                order_scr[slot * halves + k, pl.ds(r, n, stride=dil), :] = ref[0, r, :, k * LANES:(k + 1) * LANES]
        return jnp.concatenate([order_scr[slot * halves + k] for k in range(halves)], axis=1)

    parts = [row_order(ref, dils[i // 2], i) for i, ref in
             enumerate((o0_ref, l0_ref, o1_ref, l1_ref, o2_ref, l2_ref))]
    o0, l0, o1, l1, o2, l2 = parts
    mx = jnp.maximum(jnp.maximum(l0, l1), l2)
    e0, e1, e2 = jnp.exp(l0 - mx), jnp.exp(l1 - mx), jnp.exp(l2 - mx)
    attn = (e0 * o0 + e1 * o1 + e2 * o2) / (e0 + e1 + e2)
    branch_a = _dot(attn.astype(BF16), wba_ref[...])
    y = _gelu_tanh(ys_ref[...])
    y = y * _sigmoid(_dot(y.astype(BF16), wglu_ref[...]) + bglu_ref[...])
    branch_s = _dot(y.astype(BF16), wbs_ref[...])
    mix = (gate_ref[:, 0:D_MODEL].astype(F32) * branch_a
           + gate_ref[:, D_MODEL:2 * D_MODEL].astype(F32) * branch_s)
    return x_ref[...] + _dot(mix.astype(BF16), wout_ref[...])


FF_CHUNK = D_FF // 2
N_MERGE_ROW_INPUTS = 2 * N_DIL_GROUPS + 3
N_MERGE_WEIGHTS = 5


def _merge_ffn_kernel(*refs, tm, shift, pad, dils):
    n_merge = N_MERGE_ROW_INPUTS + N_MERGE_WEIGHTS
    merge_refs, rest = refs[:n_merge], refs[n_merge:]
    carry_ref, g2_ref, wup_ref, cw_ref, cb_ref, wdn_ref, gf_ref, y_ref, state_ref, order_scr, a_scr = rest
    hist = 2 * shift

    @pl.when(pl.program_id(1) == 0)
    def _():
        a_scr[pad - hist:pad, :] = carry_ref[0]

    xf = _merge_tile(*merge_refs, order_scr, tm, dils)
    xn = _rmsnorm(xf, g2_ref[...]).astype(BF16)
    acc = jnp.zeros((tm, D_MODEL), F32)
    for c0 in range(0, D_FF, FF_CHUNK):
        cols = slice(c0, c0 + FF_CHUNK)
        a = _dot(xn, wup_ref[:, cols])
        val = _dot(xn, wup_ref[:, D_FF + c0:D_FF + c0 + FF_CHUNK])
        a_scr[pad:pad + tm, cols] = a
        a_m1 = a_scr[pad - shift:pad - shift + tm, cols]
        a_m2 = a_scr[pad - hist:pad - hist + tm, cols]
        conv = cb_ref[:, cols] + cw_ref[0:1, cols] * a_m2
        conv = conv + cw_ref[1:2, cols] * a_m1
        conv = conv + cw_ref[2:3, cols] * a
        act = conv * _sigmoid(conv) * val
        acc = acc + _dot(act.astype(BF16), wdn_ref[cols, :])
    tail = a_scr[pad + tm - hist:pad + tm, :]
    a_scr[pad - hist:pad, :] = tail
    state_ref[0] = tail
    y_ref[...] = _rmsnorm(xf + acc, gf_ref[...])


def _merge_ffn(attn_parts, ys, gates, x2d, merge_weights, carry, ffn_weights, n_seq, tm, shift):
    m = x2d.shape[0]
    tiles = m // n_seq // tm
    hist = 2 * shift
    pad = -(-hist // SUBLANES) * SUBLANES
    dils = tuple(p.shape[1] for p in attn_parts[::2])
    row = lambda width: pl.BlockSpec((tm, width), lambda b, j: (b * tiles + j, 0))
    res_spec = lambda dil: pl.BlockSpec((1, dil, tm // dil, GROUP_WIDTH), lambda b, j: (b, 0, j, 0))
    state_spec = pl.BlockSpec((1, hist, D_FF), lambda b, j: (b, 0, 0))
    assert len(merge_weights) == N_MERGE_WEIGHTS
    return pl.pallas_call(
        functools.partial(_merge_ffn_kernel, tm=tm, shift=shift, pad=pad, dils=dils),
        grid=(n_seq, tiles),
        in_specs=[*[res_spec(p.shape[1]) for p in attn_parts],
                  row(SSM_WIDTH), row(2 * D_MODEL), row(D_MODEL),
                  *[_resident(w.shape) for w in merge_weights],
                  state_spec, *[_resident(w.shape) for w in ffn_weights]],
        out_specs=[row(D_MODEL), state_spec],
        out_shape=[jax.ShapeDtypeStruct((m, D_MODEL), F32),
                   jax.ShapeDtypeStruct((n_seq, hist, D_FF), F32)],
        scratch_shapes=[pltpu.VMEM((len(attn_parts) * GROUP_WIDTH // LANES, tm, LANES), F32),
                        pltpu.VMEM((pad + tm, D_FF), F32)],
        compiler_params=_compiler_params("arbitrary", "arbitrary"),
        name="merge_ffn",
    )(*attn_parts, ys, gates, x2d, *merge_weights, carry, *ffn_weights)


def _kv_rows(qkv, keep):
    batch, dil, length, _ = qkv.shape
    n = keep // dil
    rows = qkv[:, :, length - n:, GROUP_WIDTH:]
    cols = rows.transpose(0, 3, 2, 1).reshape(batch, 2, HEADS_PER_GROUP, HEAD_DIM, keep)
    return cols.transpose(0, 4, 1, 2, 3)


def _prompt_layer(x, rel_bias, lw):
    batch, seq, _ = x.shape
    x2d = x.reshape(batch * seq, D_MODEL)
    dils = tuple(dil for _, dil in DIL_PATTERNS) + (SSM_CHUNK,)
    keeps = tuple(min(window, seq) for window, _ in DIL_PATTERNS)
    *qkvs, u, gates, kv0, kv1, kv2 = _in_proj(x2d, lw['norm1_g'], lw['w_in'], batch, IN_PROJ_TILE, dils, keeps)

    attn_parts, kv_new = [], []
    for g, (window, dil) in enumerate(DIL_PATTERNS):
        tab = rel_bias[:, g * HEADS_PER_GROUP:(g + 1) * HEADS_PER_GROUP]
        attn_parts.extend(_attn_prompt(qkvs[g], _prompt_bias(tab, dil)))
        tail = (kv0, kv1, kv2)[g].reshape(batch, 2, HEADS_PER_GROUP, HEAD_DIM, keeps[g])
        kv_new.append(tail.transpose(0, 4, 1, 2, 3))

    ops, a_re, a_im = _ssm_block_operators(SSM_CHUNK, lw['ssm_terms'])
    ys, h_re, h_im = _ssm_prompt(u, ops, a_re, a_im, lw['ssm_d'], batch, seq)

    carry = jnp.zeros((batch, CONV_W - 1, D_FF), F32)
    y, conv_state = _merge_ffn(attn_parts, ys, gates, x2d, lw['merge'], carry, lw['ffn'],
                               n_seq=batch, tm=MERGE_FFN_TILE, shift=1)
    return y.reshape(batch, seq, D_MODEL), (*kv_new, h_re, h_im, conv_state)


def _sample_layer(x, caches, h0_re, h0_im, conv_buf, rel_bias, lw):
    batch, t_new, _ = x.shape
    m = batch * t_new
    x2d = x.transpose(1, 0, 2).reshape(m, D_MODEL)
    *qkvs, u, gates = _in_proj(x2d, lw['norm1_g'], lw['w_in'], 1, m, (1,) * (N_DIL_GROUPS + 1))
    u = u.reshape(m, SSM_WIDTH)
    qkvs_bt = [q.reshape(t_new, batch, GROUP_QKV).transpose(1, 0, 2) for q in qkvs]

    tbs, tns, views = [], [], []
    for g, (window, dil) in enumerate(DIL_PATTERNS):
        tab = rel_bias[:, g * HEADS_PER_GROUP:(g + 1) * HEADS_PER_GROUP]
        n_cached = caches[g].shape[1]
        tb, tn = _sample_bias(tab, dil, t_new, n_cached)
        tbs.append(tb)
        tns.append(tn)
        views.append(caches[g].transpose(0, 2, 3, 4, 1).reshape(batch, 2 * GROUP_WIDTH, n_cached))
    parts = _attn_sample(qkvs_bt, views, tbs, jnp.stack(tns))
    attn_parts = [p.transpose(1, 0, 2).reshape(1, 1, m, GROUP_WIDTH) for p in parts]
    kv_new = [_kv_rows(q.reshape(batch, 1, t_new, GROUP_QKV), t_new) for q in qkvs_bt]

    ops, a_re, a_im = _ssm_block_operators(t_new, lw['ssm_terms'])
    ys, h_re, h_im = _ssm_sample(u, ops, a_re, a_im, lw['ssm_d'],
                                 h0_re.reshape(batch, -1), h0_im.reshape(batch, -1), t_new)

    carry = conv_buf.transpose(1, 0, 2).reshape(1, (CONV_W - 1) * batch, D_FF)
    y, conv_state = _merge_ffn(attn_parts, ys, gates, x2d, lw['merge'], carry, lw['ffn'],
                               n_seq=1, tm=m, shift=batch)
    y = y.reshape(t_new, batch, D_MODEL).transpose(1, 0, 2)
    conv_state = conv_state.reshape(CONV_W - 1, batch, D_FF).transpose(1, 0, 2)
    state_shape = (batch, N_SSM_GROUPS, SSM_STATE)
    return y, (*kv_new, h_re.reshape(state_shape), h_im.reshape(state_shape), conv_state)


IN_PROJ_TILE = 512
MERGE_FFN_TILE = 512


def kernel(x_prompt, x_sample, cache_kv_w128, cache_kv_w512, cache_kv_w2048, state_ssm_re, state_ssm_im, state_ffn_conv, rel_bias, norm1_g, w_in, ssm_log_dt, ssm_lambda_re, ssm_lambda_im, ssm_b_re, ssm_b_im, ssm_c_re, ssm_c_im, ssm_d, w_glu, b_glu, w_branch_attn, w_branch_ssm, w_out, norm2_g, w_up, conv_w, conv_b, w_down, norm_f_g):
    depth = w_in.shape[0]
    hp, hs = x_prompt, x_sample
    st_p, st_s = [], []
    gf = norm_f_g.reshape(1, D_MODEL)
    for l in range(depth):
        last = l == depth - 1
        lw = {
            'norm1_g': norm1_g[l].reshape(1, D_MODEL),
            'w_in': _group_major_columns(w_in[l]).astype(BF16),
            'ssm_terms': _ssm_chunk_terms(max(SSM_CHUNK, x_sample.shape[1]), ssm_log_dt[l], ssm_lambda_re[l],
                                          ssm_lambda_im[l], ssm_b_re[l], ssm_b_im[l], ssm_c_re[l], ssm_c_im[l]),
            'ssm_d': ssm_d[l].reshape(1, SSM_WIDTH),
            'merge': (w_glu[l].astype(BF16), b_glu[l].reshape(1, SSM_WIDTH),
                      w_branch_attn[l].astype(BF16), w_branch_ssm[l].astype(BF16), w_out[l].astype(BF16)),
            'ffn': (norm2_g[l].reshape(1, D_MODEL), w_up[l].astype(BF16), conv_w[l],
                    conv_b[l].reshape(1, D_FF), w_down[l].astype(BF16), gf),
        }
        assert last, "the final RMSNorm is fused into the last layer's ffn kernel"
        hp, sp = _prompt_layer(hp, rel_bias, lw)
        hs, ss = _sample_layer(hs, (cache_kv_w128[l], cache_kv_w512[l], cache_kv_w2048[l]),
                               state_ssm_re[l], state_ssm_im[l], state_ffn_conv[l], rel_bias, lw)
        st_p.append(sp)
        st_s.append(ss)
    stack = lambda states, i: jnp.stack([st[i] for st in states], axis=0)
    return (hp, hs, *[stack(st_p, i) for i in range(6)], *[stack(st_s, i) for i in range(6)])
```

```python
import functools
import math

import jax
import jax.numpy as jnp
import numpy as np
from jax import lax
from jax.experimental import pallas as pl
from jax.experimental.pallas import tpu as pltpu

F32 = jnp.float32
BF16 = jnp.bfloat16

D_MODEL = 1024
HEAD_DIM = 64
HEADS_PER_GROUP = 4
DIL_PATTERNS = ((128, 1), (512, 4), (2048, 16))
N_DIL_GROUPS = len(DIL_PATTERNS)
GROUP_WIDTH = HEADS_PER_GROUP * HEAD_DIM
QK_WIDTH = N_DIL_GROUPS * GROUP_WIDTH
QKV_WIDTH = 3 * QK_WIDTH
GROUP_QKV = 3 * GROUP_WIDTH
QBLOCK = 128
SSM_GROUP = 16
SSM_STATE = 64
SSM_WIDTH = D_MODEL // 2
N_SSM_GROUPS = SSM_WIDTH // SSM_GROUP
SSM_CHUNK = 16
D_FF = 2816
CONV_W = 3
N_BUCKETS = 32
MAX_DISTANCE = 2048
NORM_EPS = 1e-6
NEG_INF = -1e30
U_START = QKV_WIDTH
GATE_START = U_START + SSM_WIDTH
IN_WIDTH = GATE_START + 2 * D_MODEL
QK_SCALE = HEAD_DIM ** -0.5

VMEM_LIMIT_BYTES = 60 * 1024 * 1024
SUBLANES = 8
LANES = 128


def _compiler_params(*semantics):
    return pltpu.CompilerParams(dimension_semantics=semantics, vmem_limit_bytes=VMEM_LIMIT_BYTES)


def _resident(shape):
    nd = len(shape)
    return pl.BlockSpec(shape, lambda *_: (0,) * nd, pipeline_mode=pl.Buffered(1))


def _rmsnorm(xf, g):
    y = xf * lax.rsqrt(jnp.mean(xf * xf, axis=-1, keepdims=True) + NORM_EPS)
    return y * g


def _sigmoid(x):
    return 1.0 / (1.0 + jnp.exp(-x))


def _dot(a, b):
    return jnp.dot(a, b, preferred_element_type=F32)


def _dot_nt(a, b):
    return lax.dot_general(a, b, (((1,), (1,)), ((), ())), preferred_element_type=F32)


def _in_proj_kernel(x_ref, g_ref, w_ref, qkv0_ref, qkv1_ref, qkv2_ref, u_ref, gate_ref, *rest,
                    tm, dils, tails, tiles):
    if tails is None:
        tail_refs, (xn_scr,) = (), rest
    else:
        tail_refs, (xn_scr, tok_scr) = rest[:N_DIL_GROUPS], rest[N_DIL_GROUPS:]
    xf = _rmsnorm(x_ref[...], g_ref[...])
    n_lane_blocks = D_MODEL // LANES
    if any(dil > 1 for dil in dils):
        for k in range(n_lane_blocks):
            xn_scr[k] = xf[:, k * LANES:(k + 1) * LANES]
    by_residue = {1: xf.astype(BF16)}

    def rows_by_residue(dil):
        if dil not in by_residue:
            n = tm // dil
            xr = jnp.concatenate(
                [jnp.concatenate([xn_scr[k, pl.ds(r, n, stride=dil), :] for k in range(n_lane_blocks)], axis=1)
                 for r in range(dil)], axis=0)
            by_residue[dil] = xr.astype(BF16)
        return by_residue[dil]

    outputs = [(ref, g * GROUP_QKV, GROUP_QKV) for g, ref in enumerate((qkv0_ref, qkv1_ref, qkv2_ref))]
    outputs.append((u_ref, U_START, SSM_WIDTH))
    for (ref, start, width), dil in zip(outputs, dils):
        n = tm // dil
        res = _dot(rows_by_residue(dil), w_ref[:, start:start + width])
        for r in range(dil):
            ref[0, r] = res[r * n:(r + 1) * n]
    for c0 in range(0, 2 * D_MODEL, D_MODEL):
        logits = _dot(by_residue[1], w_ref[:, GATE_START + c0:GATE_START + c0 + D_MODEL])
        gate_ref[:, c0:c0 + D_MODEL] = _sigmoid(logits).astype(BF16)
    if tails is None:
        return
    kv_blocks = 2 * GROUP_WIDTH // LANES
    for (ref, _, _), dil, keep, tail_ref in zip(outputs, dils, tails, tail_refs):
        n = tm // dil
        kept = min(keep, tm)

        @pl.when(pl.program_id(1) >= tiles - max(keep // tm, 1))
        def _(ref=ref, dil=dil, n=n, kept=kept, tail_ref=tail_ref):
            if dil == 1:
                kv = ref[0, 0, :, GROUP_WIDTH:]
            else:
                for r in range(dil):
                    for k in range(kv_blocks):
                        lanes = slice(GROUP_WIDTH + k * LANES, GROUP_WIDTH + (k + 1) * LANES)
                        tok_scr[k, pl.ds(r, n, stride=dil), :] = ref[0, r, :, lanes]
                kv = jnp.concatenate([tok_scr[k] for k in range(kv_blocks)], axis=1)
            tail_ref[0] = kv[tm - kept:, :].T


def _group_major_columns(w_in):
    parts = []
    for g in range(N_DIL_GROUPS):
        for base in (0, QK_WIDTH, 2 * QK_WIDTH):
            parts.append(w_in[:, base + g * GROUP_WIDTH:base + (g + 1) * GROUP_WIDTH])
    parts.append(w_in[:, U_START:])
    return jnp.concatenate(parts, axis=1)


def _in_proj(x2d, g, w_bf16, n_seq, tm, dils, tails=None):
    m = x2d.shape[0]
    seq = m // n_seq
    tiles = seq // tm
    row = lambda width: pl.BlockSpec((tm, width), lambda b, j: (b * tiles + j, 0))
    res_spec = lambda dil, width: pl.BlockSpec((1, dil, tm // dil, width), lambda b, j: (b, 0, j, 0))
    widths = (GROUP_QKV,) * N_DIL_GROUPS + (SSM_WIDTH,)
    out_specs = [*[res_spec(d, w) for d, w in zip(dils, widths)], row(2 * D_MODEL)]
    out_shape = [*[jax.ShapeDtypeStruct((n_seq, d, seq // d, w), F32) for d, w in zip(dils, widths)],
                 jax.ShapeDtypeStruct((m, 2 * D_MODEL), BF16)]
    scratch = [pltpu.VMEM((D_MODEL // LANES, tm, LANES), F32)]
    if tails is not None:
        for keep in tails:
            assert keep % tm == 0 or tm % keep == 0
            first = tiles - max(keep // tm, 1)
            out_specs.append(pl.BlockSpec((1, 2 * GROUP_WIDTH, min(keep, tm)),
                                          lambda b, j, first=first: (b, 0, jnp.maximum(j - first, 0))))
            out_shape.append(jax.ShapeDtypeStruct((n_seq, 2 * GROUP_WIDTH, keep), F32))
        scratch.append(pltpu.VMEM((2 * GROUP_WIDTH // LANES, tm, LANES), F32))
    return pl.pallas_call(
        functools.partial(_in_proj_kernel, tm=tm, dils=dils, tails=tails, tiles=tiles),
        grid=(n_seq, tiles),
        in_specs=[row(D_MODEL), _resident((1, D_MODEL)), _resident((D_MODEL, IN_WIDTH))],
        out_specs=out_specs,
        out_shape=out_shape,
        scratch_shapes=scratch,
        compiler_params=_compiler_params("arbitrary", "arbitrary"),
        name="in_proj",
    )(x2d, g, w_bf16)


def _bucket_starts():
    max_exact = N_BUCKETS // 2
    n = np.arange(max_exact, MAX_DISTANCE + 1)
    large = max_exact + (np.log(n.astype(np.float32) / np.float32(max_exact))
                         / np.float32(math.log(MAX_DISTANCE / max_exact))
                         * np.float32(N_BUCKETS - max_exact)).astype(np.int32)
    large = np.minimum(large, N_BUCKETS - 1)
    return [int(n[np.argmax(large >= k)]) for k in range(max_exact + 1, N_BUCKETS)]


def _rel_bucket(dist):
    max_exact = N_BUCKETS // 2
    n = jnp.maximum(dist, 0)
    large = max_exact + sum((n >= start).astype(jnp.int32) for start in _bucket_starts())
    return jnp.where(n < max_exact, n, large)


def _masked_bias(tab, strides, valid, dil):
    n_dist = QBLOCK
    bucket = _rel_bucket(jnp.clip(strides, 0, n_dist) * dil).reshape(1, -1)
    onehot = (bucket == jnp.arange(N_BUCKETS)[:, None]).astype(F32)
    bias = jnp.dot(tab.astype(F32).T, onehot, precision=lax.Precision.HIGHEST)
    bias = jnp.where(valid.reshape(1, -1), bias, NEG_INF)
    return bias.reshape((tab.shape[1],) + strides.shape)


def _prompt_bias(tab, dil):
    qi = jnp.arange(QBLOCK)[:, None]
    ki = jnp.arange(QBLOCK)[None, :]
    j_prev = qi + QBLOCK - ki
    j_cur = qi - ki
    bias = jnp.stack([_masked_bias(tab, j_prev, j_prev <= QBLOCK, dil),
                      _masked_bias(tab, j_cur, j_cur >= 0, dil)], axis=0)
    return bias.transpose(0, 2, 1, 3).reshape(2, QBLOCK, HEADS_PER_GROUP * QBLOCK)


def _sample_bias(tab, dil, t_new, n_cached):
    t = jnp.arange(t_new)[:, None]
    delta_buf = n_cached + t - jnp.arange(n_cached)[None, :]
    ok_buf = (delta_buf % dil == 0) & (delta_buf // dil <= QBLOCK)
    m = jnp.arange(QBLOCK)[None, :]
    delta_new = t - m
    ok_new = (m < t_new) & (delta_new >= 0) & (delta_new % dil == 0)
    b_buf = _masked_bias(tab, delta_buf // dil, ok_buf, dil)
    b_new = _masked_bias(tab, delta_new // dil, ok_new, dil)
    return (b_buf.reshape(HEADS_PER_GROUP * t_new, n_cached),
            b_new.reshape(HEADS_PER_GROUP * t_new, QBLOCK))


def _attn_prompt_kernel(q_ref, kc_ref, vc_ref, bias_ref, o_ref, lse_ref, kprev_scr, vprev_scr, *, nq):
    first_tile = pl.program_id(2) == 0

    @pl.when(first_tile)
    def _():
        kprev_scr[...] = jnp.zeros_like(kprev_scr)
        vprev_scr[...] = jnp.zeros_like(vprev_scr)

    stacked = (HEADS_PER_GROUP * QBLOCK, GROUP_WIDTH)
    own_head = (lax.broadcasted_iota(jnp.int32, stacked, 0) // QBLOCK
                == lax.broadcasted_iota(jnp.int32, stacked, 1) // HEAD_DIM)
    lane_head = lax.broadcasted_iota(jnp.int32, (QBLOCK, GROUP_WIDTH), 1) // HEAD_DIM

    def per_head(x):
        xb = x.astype(BF16)
        return jnp.where(own_head, jnp.concatenate([xb] * HEADS_PER_GROUP, axis=0), 0)

    def on_head_lanes(cols):
        out = jnp.broadcast_to(cols[-1], (QBLOCK, GROUP_WIDTH))
        for h in range(HEADS_PER_GROUP - 2, -1, -1):
            out = jnp.where(lane_head == h, cols[h], out)
        return out

    block_rows = [slice(i * QBLOCK, (i + 1) * QBLOCK) for i in range(nq)]
    qs = [(q_ref[0, 0, rows, :] * QK_SCALE).astype(BF16) for rows in block_rows]
    k_blocks = [kprev_scr[...]] + [per_head(kc_ref[0, 0, rows, :]) for rows in block_rows]
    v_blocks = [vprev_scr[...]] + [per_head(vc_ref[0, 0, rows, :]) for rows in block_rows]
    kprev_scr[...] = k_blocks[-1]
    vprev_scr[...] = v_blocks[-1]

    def users(per_query_cur, per_query_prev, j):
        parts = ([per_query_cur[j - 1]] if j >= 1 else []) + ([per_query_prev[j]] if j < nq else [])
        return parts[0] if len(parts) == 1 else jnp.concatenate(parts, axis=0)

    s_prev, s_cur = [None] * nq, [None] * nq
    for j in range(nq + 1):
        s = _dot_nt(users(qs, qs, j), k_blocks[j])
        if j >= 1:
            s_cur[j - 1] = s[:QBLOCK] + bias_ref[1]
        if j < nq:
            s_prev[j] = s[-QBLOCK:] + bias_ref[0]
    s_prev[0] = jnp.where(first_tile, NEG_INF, s_prev[0])

    p_prev, p_cur, dens, lses = [], [], [], []
    for i in range(nq):
        pp_heads, pc_heads, den_heads, lse_heads = [], [], [], []
        for h in range(HEADS_PER_GROUP):
            keys = slice(h * QBLOCK, (h + 1) * QBLOCK)
            sp, sc = s_prev[i][:, keys], s_cur[i][:, keys]
            m = jnp.max(jnp.maximum(sp, sc), axis=-1, keepdims=True)
            pp, pc = jnp.exp(sp - m), jnp.exp(sc - m)
            den = jnp.sum(pp + pc, axis=-1, keepdims=True)
            pp_heads.append(pp.astype(BF16))
            pc_heads.append(pc.astype(BF16))
            den_heads.append(den)
            lse_heads.append(m + jnp.log(den))
        p_prev.append(jnp.concatenate(pp_heads, axis=1))
        p_cur.append(jnp.concatenate(pc_heads, axis=1))
        dens.append(on_head_lanes(den_heads))
        lses.append(on_head_lanes(lse_heads))

    o = [None] * nq
    for j in range(nq + 1):
        r = _dot(users(p_cur, p_prev, j), v_blocks[j])
        if j >= 1:
            o[j - 1] = o[j - 1] + r[:QBLOCK]
        if j < nq:
            o[j] = r[-QBLOCK:]
    for i, rows in enumerate(block_rows):
        o_ref[0, 0, rows, :] = o[i] / dens[i]
        lse_ref[0, 0, rows, :] = lses[i]


ATTN_QUERY_BLOCKS = 8


def _attn_prompt(qkv, bias):
    batch, dil, length, _ = qkv.shape
    tq = min(ATTN_QUERY_BLOCKS * QBLOCK, length)
    nq = tq // QBLOCK

    def cur(col):
        return pl.BlockSpec((1, 1, tq, GROUP_WIDTH), lambda b, r, n: (b, r, n, col))

    stacked = pltpu.VMEM((HEADS_PER_GROUP * QBLOCK, GROUP_WIDTH), BF16)
    out_sds = jax.ShapeDtypeStruct((batch, dil, length, GROUP_WIDTH), F32)
    return pl.pallas_call(
        functools.partial(_attn_prompt_kernel, nq=nq),
        grid=(batch, dil, length // tq),
        in_specs=[cur(0), cur(1), cur(2), _resident((2, QBLOCK, HEADS_PER_GROUP * QBLOCK))],
        out_specs=[cur(0), cur(0)],
        out_shape=[out_sds, out_sds],
        scratch_shapes=[stacked, stacked],
        compiler_params=_compiler_params("arbitrary", "arbitrary", "arbitrary"),
        name=f"attn_prompt_d{dil}",
    )(qkv, qkv, qkv, bias)


def _attn_sample_kernel(q0_ref, q1_ref, q2_ref, c0_ref, c1_ref, c2_ref, tb0_ref, tb1_ref, tb2_ref, tn_ref,
                        o0_ref, l0_ref, o1_ref, l1_ref, o2_ref, l2_ref, kn_scr, vn_scr, *, t_new):
    n_rows = HEADS_PER_GROUP * t_new
    row_w = lax.broadcasted_iota(jnp.int32, (n_rows, GROUP_WIDTH), 0)
    lane_w = lax.broadcasted_iota(jnp.int32, (n_rows, GROUP_WIDTH), 1)
    own_head = (row_w // t_new) == (lane_w // HEAD_DIM)

    def fold_heads(x):
        x = jnp.where(own_head, x, 0.0)
        out = x[0:t_new]
        for h in range(1, HEADS_PER_GROUP):
            out = out + x[h * t_new:(h + 1) * t_new]
        return out

    caches = (c0_ref, c1_ref, c2_ref)
    cache_bias = (tb0_ref, tb1_ref, tb2_ref)
    outs = ((o0_ref, l0_ref), (o1_ref, l1_ref), (o2_ref, l2_ref))
    for g in range(N_DIL_GROUPS):
        qkv_ref = (q0_ref, q1_ref, q2_ref)[g]
        q = qkv_ref[0, :, 0:GROUP_WIDTH] * QK_SCALE
        q_rows = jnp.where(own_head, jnp.concatenate([q] * HEADS_PER_GROUP, axis=0), 0.0).astype(BF16)
        kn_scr[...] = jnp.zeros_like(kn_scr)
        vn_scr[...] = jnp.zeros_like(vn_scr)
        kn_scr[0:t_new, :] = qkv_ref[0, :, GROUP_WIDTH:2 * GROUP_WIDTH]
        vn_scr[0:t_new, :] = qkv_ref[0, :, 2 * GROUP_WIDTH:3 * GROUP_WIDTH]
        cache = caches[g]
        k_t = cache[0, 0:GROUP_WIDTH, :].astype(BF16)
        v_t = cache[0, GROUP_WIDTH:2 * GROUP_WIDTH, :].astype(BF16)
        s_buf = _dot(q_rows, k_t) + cache_bias[g][...]
        s_new = _dot_nt(q_rows, kn_scr[...].astype(BF16)) + tn_ref[g]
        m = jnp.maximum(jnp.max(s_buf, axis=-1, keepdims=True), jnp.max(s_new, axis=-1, keepdims=True))
        p_buf = jnp.exp(s_buf - m)
        p_new = jnp.exp(s_new - m)
        den = jnp.sum(p_buf, axis=-1, keepdims=True) + jnp.sum(p_new, axis=-1, keepdims=True)
        o = _dot_nt(p_buf.astype(BF16), v_t) + _dot(p_new.astype(BF16), vn_scr[...].astype(BF16))
        o_ref, l_ref = outs[g]
        o_ref[0] = fold_heads(o / den)
        l_ref[0] = fold_heads(jnp.broadcast_to(m + jnp.log(den), (n_rows, GROUP_WIDTH)))


def _attn_sample(qkvs, caches, tbs, tn):
    batch, t_new, _ = qkvs[0].shape
    n_rows = HEADS_PER_GROUP * t_new
    cache_specs = [pl.BlockSpec((1,) + c.shape[1:], lambda b: (b, 0, 0)) for c in caches]
    qkv_spec = pl.BlockSpec((1, t_new, GROUP_QKV), lambda b: (b, 0, 0))
    out_spec = pl.BlockSpec((1, t_new, GROUP_WIDTH), lambda b: (b, 0, 0))
    out_sds = jax.ShapeDtypeStruct((batch, t_new, GROUP_WIDTH), F32)
    return pl.pallas_call(
        functools.partial(_attn_sample_kernel, t_new=t_new),
        grid=(batch,),
        in_specs=[*[qkv_spec] * N_DIL_GROUPS, *cache_specs, *[_resident(t.shape) for t in tbs],
                  _resident((N_DIL_GROUPS, n_rows, QBLOCK))],
        out_specs=[out_spec] * (2 * N_DIL_GROUPS),
        out_shape=[out_sds] * (2 * N_DIL_GROUPS),
        scratch_shapes=[pltpu.VMEM((QBLOCK, GROUP_WIDTH), F32), pltpu.VMEM((QBLOCK, GROUP_WIDTH), F32)],
        compiler_params=_compiler_params("arbitrary"),
        name="attn_sample",
    )(*qkvs, *caches, *tbs, tn)


def _ssm_chunk_terms(chunk, log_dt, lam_re, lam_im, b_re, b_im, c_re, c_im):
    hi = lax.Precision.HIGHEST
    dt = jnp.exp(log_dt.astype(F32))[:, None]
    lr, li = lam_re.astype(F32), lam_im.astype(F32)
    mag = jnp.exp(lr * dt)
    ab_re, ab_im = mag * jnp.cos(li * dt), mag * jnp.sin(li * dt)
    g, n = lr.shape
    p = b_re.shape[-1]
    den = lr * lr + li * li
    nr, ni = ab_re - 1.0, ab_im
    coef_re = (nr * lr + ni * li) / den
    coef_im = (ni * lr - nr * li) / den
    br, bi = b_re.astype(F32), b_im.astype(F32)
    bb_re = coef_re[..., None] * br - coef_im[..., None] * bi
    bb_im = coef_re[..., None] * bi + coef_im[..., None] * br
    k = jnp.arange(chunk + 1, dtype=F32)[:, None, None]
    pw_mag = jnp.exp(k * (lr * dt)[None])
    pw_re, pw_im = pw_mag * jnp.cos(k * (li * dt)[None]), pw_mag * jnp.sin(k * (li * dt)[None])
    bt_re, bt_im = bb_re.transpose(0, 2, 1)[None], bb_im.transpose(0, 2, 1)[None]
    pk_re, pk_im = pw_re[:chunk, :, None, :], pw_im[:chunk, :, None, :]
    akb_re = pk_re * bt_re - pk_im * bt_im
    akb_im = pk_re * bt_im + pk_im * bt_re
    cr, ci = c_re.astype(F32)[None], c_im.astype(F32)[None]
    pe_re, pe_im = pw_re[1:, :, None, :], pw_im[1:, :, None, :]
    e_re = cr * pe_re - ci * pe_im
    e_im = cr * pe_im + ci * pe_re
    kern = (jnp.einsum('kgqn,gpn->kgqp', akb_re, cr[0], precision=hi)
            - jnp.einsum('kgqn,gpn->kgqp', akb_im, ci[0], precision=hi))
    rows = lambda x: x.reshape(chunk, g * p, x.shape[-1])
    return ((rows(akb_re), rows(akb_im)), (rows(e_re), rows(e_im)), rows(kern), (pw_re, pw_im))


SSM_BLOCK_GROUPS = 8
SSM_BLOCK_CH = SSM_BLOCK_GROUPS * SSM_GROUP
SSM_BLOCK_STATE = SSM_BLOCK_GROUPS * SSM_STATE
N_SSM_BLOCKS = N_SSM_GROUPS // SSM_BLOCK_GROUPS


def _ssm_operator_kernel(akbre_ref, akbim_ref, ere_ref, eim_ref, kern_ref,
                         wre_ref, wim_ref, m_ref, etre_ref, etim_ref, *, chunk):
    gb, p, n = SSM_BLOCK_GROUPS, SSM_GROUP, SSM_STATE

    def copies(width, count):
        src = lax.broadcasted_iota(jnp.int32, (width, count * width), 0)
        dst = lax.broadcasted_iota(jnp.int32, (width, count * width), 1)
        return jnp.where(src == dst % width, 1.0, 0.0).astype(BF16)

    def own_group(width):
        row = lax.broadcasted_iota(jnp.int32, (SSM_BLOCK_CH, gb * width), 0)
        col = lax.broadcasted_iota(jnp.int32, (SSM_BLOCK_CH, gb * width), 1)
        return row // p == col // width

    to_states, own_states = copies(n, gb), own_group(n)

    def over_states(piece):
        return jnp.where(own_states, _dot(piece.astype(BF16), to_states), 0.0).astype(BF16)

    for s in range(chunk):
        rows = slice(s * SSM_BLOCK_CH, (s + 1) * SSM_BLOCK_CH)
        wre_ref[0, rows, :] = over_states(akbre_ref[chunk - 1 - s, 0])
        wim_ref[0, rows, :] = over_states(akbim_ref[chunk - 1 - s, 0])
        etre_ref[0, rows, :] = over_states(ere_ref[s, 0])
        etim_ref[0, rows, :] = over_states(-eim_ref[s, 0])
    to_channels, own_channels = copies(p, gb), own_group(p)
    lags = [jnp.where(own_channels, _dot(kern_ref[k, 0].astype(BF16), to_channels), 0.0).astype(BF16)
            for k in range(chunk)]
    zero = jnp.zeros((SSM_BLOCK_CH, SSM_BLOCK_CH), BF16)
    for s in range(chunk):
        m_ref[0, s * SSM_BLOCK_CH:(s + 1) * SSM_BLOCK_CH, :] = jnp.concatenate(
            [zero] * s + lags[:chunk - s], axis=1)


def _ssm_block_operators(chunk, terms):
    (akb_re, akb_im), (e_re, e_im), kern, (pw_re, pw_im) = terms
    assert chunk <= kern.shape[0]
    a_re, a_im = pw_re[chunk], pw_im[chunk]
    nb, gb, p, n = N_SSM_BLOCKS, SSM_BLOCK_GROUPS, SSM_GROUP, SSM_STATE
    pieces = [x[:chunk].reshape(chunk, nb, SSM_BLOCK_CH, x.shape[-1])
              for x in (akb_re, akb_im, e_re, e_im, kern)]
    piece_spec = lambda x: pl.BlockSpec((chunk, 1) + x.shape[2:], lambda i: (0, i, 0, 0))
    out_spec = lambda shape: pl.BlockSpec((1,) + shape[1:], lambda i: (i, 0, 0))
    x_width = chunk * SSM_BLOCK_CH
    out_shapes = [(nb, x_width, gb * n), (nb, x_width, gb * n), (nb, x_width, x_width),
                  (nb, x_width, gb * n), (nb, x_width, gb * n)]
    ops = pl.pallas_call(
        functools.partial(_ssm_operator_kernel, chunk=chunk),
        grid=(nb,),
        in_specs=[piece_spec(x) for x in pieces],
        out_specs=[out_spec(s) for s in out_shapes],
        out_shape=[jax.ShapeDtypeStruct(s, BF16) for s in out_shapes],
        compiler_params=_compiler_params("arbitrary"),
        name="ssm_operators",
    )(*pieces)
    return ops, a_re.reshape(nb, 1, gb * n), a_im.reshape(nb, 1, gb * n)


SSM_X_WIDTH = SSM_CHUNK * SSM_BLOCK_CH


def _ssm_prompt_kernel(u_ref, wre_ref, wim_ref, m_ref, etre_ref, etim_ref, are_ref, aim_ref, d_ref,
                       y_ref, hre_ref, him_ref, x_scr, sre_scr, sim_scr, *, batch, chunks):
    phase, b = pl.program_id(1), pl.program_id(2)
    for t in range(SSM_CHUNK):
        x_scr[:, t * SSM_BLOCK_CH:(t + 1) * SSM_BLOCK_CH] = u_ref[0, t].astype(BF16)
    rows = pl.ds(b, chunks, stride=batch)
    state_blocks = SSM_BLOCK_STATE // LANES

    @pl.when(phase == 0)
    def _():
        x = x_scr[...]
        g_re, g_im = _dot(x, wre_ref[0]), _dot(x, wim_ref[0])
        for k in range(state_blocks):
            sre_scr[k, rows, :] = g_re[:, k * LANES:(k + 1) * LANES]
            sim_scr[k, rows, :] = g_im[:, k * LANES:(k + 1) * LANES]

    @pl.when((phase == 1) & (b == 0))
    def _():
        same_lanes = lambda ref, k: ref[0][:, k * LANES:(k + 1) * LANES]

        def step(c, carry):
            same_chunk = pl.ds(c * batch, batch)
            out = []
            for k, (h_re, h_im) in enumerate(carry):
                a_re, a_im = same_lanes(are_ref, k), same_lanes(aim_ref, k)
                g_re, g_im = sre_scr[k, same_chunk, :], sim_scr[k, same_chunk, :]
                sre_scr[k, same_chunk, :] = h_re
                sim_scr[k, same_chunk, :] = h_im
                out.append((a_re * h_re - a_im * h_im + g_re, a_re * h_im + a_im * h_re + g_im))
            return tuple(out)

        zero = jnp.zeros((batch, LANES), F32)
        final = lax.fori_loop(0, chunks, step, ((zero, zero),) * state_blocks)
        for k, (h_re, h_im) in enumerate(final):
            hre_ref[0, :, k * LANES:(k + 1) * LANES] = h_re
            him_ref[0, :, k * LANES:(k + 1) * LANES] = h_im

    @pl.when(phase == 1)
    def _():
        h_re = jnp.concatenate([sre_scr[k, rows, :] for k in range(state_blocks)], axis=1).astype(BF16)
        h_im = jnp.concatenate([sim_scr[k, rows, :] for k in range(state_blocks)], axis=1).astype(BF16)
        pair_w = 2 * SSM_BLOCK_CH
        for j in range(SSM_CHUNK // 2):
            cols = slice(j * pair_w, (j + 1) * pair_w)
            k_in = (j + 1) * pair_w
            yj = (_dot(x_scr[:, :k_in], m_ref[0, :k_in, cols])
                  + _dot_nt(h_re, etre_ref[0, cols, :]) + _dot_nt(h_im, etim_ref[0, cols, :]))
            for i in range(2):
                t = 2 * j + i
                y_ref[pl.ds(t, chunks, stride=SSM_CHUNK), :] = (
                    yj[:, i * SSM_BLOCK_CH:(i + 1) * SSM_BLOCK_CH] + d_ref[0] * u_ref[0, t])


def _ssm_prompt(u, ops, a_re, a_im, d, batch, seq):
    chunks = seq // SSM_CHUNK
    d = d.reshape(N_SSM_BLOCKS, 1, SSM_BLOCK_CH)
    op_spec = lambda arr: pl.BlockSpec((1,) + arr.shape[1:], lambda g, ph, b: (g, 0, 0),
                                       pipeline_mode=pl.Buffered(1))
    state_spec = pl.BlockSpec((1, batch, SSM_BLOCK_STATE), lambda g, ph, b: (g, 0, 0))
    state_sds = jax.ShapeDtypeStruct((N_SSM_BLOCKS, batch, SSM_BLOCK_STATE), F32)
    y, h_re, h_im = pl.pallas_call(
        functools.partial(_ssm_prompt_kernel, batch=batch, chunks=chunks),
        grid=(N_SSM_BLOCKS, 2, batch),
        in_specs=[pl.BlockSpec((1, SSM_CHUNK, chunks, SSM_BLOCK_CH), lambda g, ph, b: (b, 0, 0, g)),
                  *[op_spec(o) for o in ops], op_spec(a_re), op_spec(a_im), op_spec(d)],
        out_specs=[pl.BlockSpec((seq, SSM_BLOCK_CH), lambda g, ph, b: (b * ph, g)), state_spec, state_spec],
        out_shape=[jax.ShapeDtypeStruct((batch * seq, SSM_WIDTH), F32), state_sds, state_sds],
        scratch_shapes=[pltpu.VMEM((chunks, SSM_X_WIDTH), BF16),
                        pltpu.VMEM((SSM_BLOCK_STATE // LANES, batch * chunks, LANES), F32),
                        pltpu.VMEM((SSM_BLOCK_STATE // LANES, batch * chunks, LANES), F32)],
        compiler_params=_compiler_params("arbitrary", "arbitrary", "arbitrary"),
        name="ssm_prompt",
    )(u, *ops, a_re, a_im, d)

    def by_sequence(h):
        h = h.reshape(N_SSM_BLOCKS, batch, SSM_BLOCK_GROUPS, SSM_STATE).transpose(1, 0, 2, 3)
        return h.reshape(batch, N_SSM_GROUPS, SSM_STATE)

    return y, by_sequence(h_re), by_sequence(h_im)


def _ssm_sample_kernel(u_ref, wre_ref, wim_ref, m_ref, etre_ref, etim_ref, are_ref, aim_ref, d_ref,
                       h0re_ref, h0im_ref, y_ref, hre_ref, him_ref, *, t_new, batch):
    for nb in range(N_SSM_BLOCKS):
        ch = slice(nb * SSM_BLOCK_CH, (nb + 1) * SSM_BLOCK_CH)
        st = slice(nb * SSM_BLOCK_STATE, (nb + 1) * SSM_BLOCK_STATE)
        x = jnp.concatenate([u_ref[t * batch:(t + 1) * batch, ch] for t in range(t_new)], axis=1).astype(BF16)
        h_re, h_im = h0re_ref[:, st], h0im_ref[:, st]
        a_re, a_im = are_ref[nb], aim_ref[nb]
        hre_ref[:, st] = a_re * h_re - a_im * h_im + _dot(x, wre_ref[nb])
        him_ref[:, st] = a_re * h_im + a_im * h_re + _dot(x, wim_ref[nb])
        y = (_dot(x, m_ref[nb]) + _dot_nt(h_re.astype(BF16), etre_ref[nb])
             + _dot_nt(h_im.astype(BF16), etim_ref[nb]))
        for t in range(t_new):
            rows = slice(t * batch, (t + 1) * batch)
            y_ref[rows, ch] = y[:, t * SSM_BLOCK_CH:(t + 1) * SSM_BLOCK_CH] + d_ref[:, ch] * u_ref[rows, ch]


def _ssm_sample(u, ops, a_re, a_im, d, h0_re, h0_im, t_new):
    batch = u.shape[0] // t_new
    args = (u, *ops, a_re, a_im, d, h0_re, h0_im)
    state_sds = jax.ShapeDtypeStruct((batch, N_SSM_GROUPS * SSM_STATE), F32)
    out_shape = [jax.ShapeDtypeStruct(u.shape, F32), state_sds, state_sds]
    whole = lambda shape: pl.BlockSpec(shape, lambda i, nd=len(shape): (0,) * nd)
    return pl.pallas_call(
        functools.partial(_ssm_sample_kernel, t_new=t_new, batch=batch),
        grid=(1,),
        in_specs=[whole(a.shape) for a in args],
        out_specs=[whole(s.shape) for s in out_shape],
        out_shape=out_shape,
        compiler_params=_compiler_params("arbitrary"),
        name="ssm_sample",
    )(*args)


def _gelu_tanh(x):
    return 0.5 * x * (1.0 + jnp.tanh(math.sqrt(2.0 / math.pi) * (x + 0.044715 * (x * x * x))))


def _merge_tile(o0_ref, l0_ref, o1_ref, l1_ref, o2_ref, l2_ref, ys_ref, gate_ref, x_ref,
                wglu_ref, bglu_ref, wba_ref, wbs_ref, wout_ref, order_scr, tm, dils):
    def row_order(ref, dil, slot):
        if dil == 1:
            return ref[0, 0]
        n = tm // dil
        halves = GROUP_WIDTH // LANES
        for r in range(dil):
            for k in range(halves):
                order_scr[slot * halves + k, pl.ds(r, n, stride=dil), :] = ref[0, r, :, k * LANES:(k + 1) * LANES]
        return jnp.concatenate([order_scr[slot * halves + k] for k in range(halves)], axis=1)

    parts = [row_order(ref, dils[i // 2], i) for i, ref in
             enumerate((o0_ref, l0_ref, o1_ref, l1_ref, o2_ref, l2_ref))]
    o0, l0, o1, l1, o2, l2 = parts
    mx = jnp.maximum(jnp.maximum(l0, l1), l2)
    e0, e1, e2 = jnp.exp(l0 - mx), jnp.exp(l1 - mx), jnp.exp(l2 - mx)
    attn = (e0 * o0 + e1 * o1 + e2 * o2) / (e0 + e1 + e2)
    branch_a = _dot(attn.astype(BF16), wba_ref[...])
    y = _gelu_tanh(ys_ref[...])
    y = y * _sigmoid(_dot(y.astype(BF16), wglu_ref[...]) + bglu_ref[...])
    branch_s = _dot(y.astype(BF16), wbs_ref[...])
    mix = (gate_ref[:, 0:D_MODEL].astype(F32) * branch_a
           + gate_ref[:, D_MODEL:2 * D_MODEL].astype(F32) * branch_s)
    return x_ref[...] + _dot(mix.astype(BF16), wout_ref[...])


MXU_WIDTH = 256
FF_SPLIT = -(-D_FF // (2 * MXU_WIDTH)) * MXU_WIDTH
FF_CHUNKS = ((0, FF_SPLIT), (FF_SPLIT, D_FF))
N_MERGE_ROW_INPUTS = 2 * N_DIL_GROUPS + 3
N_MERGE_WEIGHTS = 5


def _merge_ffn_kernel(*refs, tm, shift, pad, dils):
    n_merge = N_MERGE_ROW_INPUTS + N_MERGE_WEIGHTS
    merge_refs, rest = refs[:n_merge], refs[n_merge:]
    carry_ref, g2_ref, wup_ref, cw_ref, cb_ref, wdn_ref, gf_ref, y_ref, state_ref, order_scr, a_scr = rest
    hist = 2 * shift

    @pl.when(pl.program_id(1) == 0)
    def _():
        a_scr[pad - hist:pad, :] = carry_ref[0]

    xf = _merge_tile(*merge_refs, order_scr, tm, dils)
    xn = _rmsnorm(xf, g2_ref[...]).astype(BF16)
    acc = jnp.zeros((tm, D_MODEL), F32)
    for c0, c1 in FF_CHUNKS:
        cols = slice(c0, c1)
        a = _dot(xn, wup_ref[:, cols])
        val = _dot(xn, wup_ref[:, D_FF + c0:D_FF + c1])
        a_scr[pad:pad + tm, cols] = a
        a_m1 = a_scr[pad - shift:pad - shift + tm, cols]
        a_m2 = a_scr[pad - hist:pad - hist + tm, cols]
        conv = cb_ref[:, cols] + cw_ref[0:1, cols] * a_m2
        conv = conv + cw_ref[1:2, cols] * a_m1
        conv = conv + cw_ref[2:3, cols] * a
        act = conv * _sigmoid(conv) * val
        acc = acc + _dot(act.astype(BF16), wdn_ref[cols, :])
    tail = a_scr[pad + tm - hist:pad + tm, :]
    a_scr[pad - hist:pad, :] = tail
    state_ref[0] = tail
    y_ref[...] = _rmsnorm(xf + acc, gf_ref[...])


def _merge_ffn(attn_parts, ys, gates, x2d, merge_weights, carry, ffn_weights, n_seq, tm, shift):
    m = x2d.shape[0]
    tiles = m // n_seq // tm
    hist = 2 * shift
    pad = -(-hist // SUBLANES) * SUBLANES
    dils = tuple(p.shape[1] for p in attn_parts[::2])
    row = lambda width: pl.BlockSpec((tm, width), lambda b, j: (b * tiles + j, 0))
    res_spec = lambda dil: pl.BlockSpec((1, dil, tm // dil, GROUP_WIDTH), lambda b, j: (b, 0, j, 0))
    state_spec = pl.BlockSpec((1, hist, D_FF), lambda b, j: (b, 0, 0))
    assert len(merge_weights) == N_MERGE_WEIGHTS
    return pl.pallas_call(
        functools.partial(_merge_ffn_kernel, tm=tm, shift=shift, pad=pad, dils=dils),
        grid=(n_seq, tiles),
        in_specs=[*[res_spec(p.shape[1]) for p in attn_parts],
                  row(SSM_WIDTH), row(2 * D_MODEL), row(D_MODEL),
                  *[_resident(w.shape) for w in merge_weights],
                  state_spec, *[_resident(w.shape) for w in ffn_weights]],
        out_specs=[row(D_MODEL), state_spec],
        out_shape=[jax.ShapeDtypeStruct((m, D_MODEL), F32),
                   jax.ShapeDtypeStruct((n_seq, hist, D_FF), F32)],
        scratch_shapes=[pltpu.VMEM((len(attn_parts) * GROUP_WIDTH // LANES, tm, LANES), F32),
                        pltpu.VMEM((pad + tm, D_FF), F32)],
        compiler_params=_compiler_params("arbitrary", "arbitrary"),
        name="merge_ffn",
    )(*attn_parts, ys, gates, x2d, *merge_weights, carry, *ffn_weights)


def _kv_rows(qkv, keep):
    batch, dil, length, _ = qkv.shape
    n = keep // dil
    rows = qkv[:, :, length - n:, GROUP_WIDTH:]
    cols = rows.transpose(0, 3, 2, 1).reshape(batch, 2, HEADS_PER_GROUP, HEAD_DIM, keep)
    return cols.transpose(0, 4, 1, 2, 3)


def _prompt_layer(x, rel_bias, lw):
    batch, seq, _ = x.shape
    x2d = x.reshape(batch * seq, D_MODEL)
    dils = tuple(dil for _, dil in DIL_PATTERNS) + (SSM_CHUNK,)
    keeps = tuple(min(window, seq) for window, _ in DIL_PATTERNS)
    *qkvs, u, gates, kv0, kv1, kv2 = _in_proj(x2d, lw['norm1_g'], lw['w_in'], batch, IN_PROJ_TILE, dils, keeps)

    attn_parts, kv_new = [], []
    for g, (window, dil) in enumerate(DIL_PATTERNS):
        tab = rel_bias[:, g * HEADS_PER_GROUP:(g + 1) * HEADS_PER_GROUP]
        attn_parts.extend(_attn_prompt(qkvs[g], _prompt_bias(tab, dil)))
        tail = (kv0, kv1, kv2)[g].reshape(batch, 2, HEADS_PER_GROUP, HEAD_DIM, keeps[g])
        kv_new.append(tail.transpose(0, 4, 1, 2, 3))

    ops, a_re, a_im = _ssm_block_operators(SSM_CHUNK, lw['ssm_terms'])
    ys, h_re, h_im = _ssm_prompt(u, ops, a_re, a_im, lw['ssm_d'], batch, seq)

    carry = jnp.zeros((batch, CONV_W - 1, D_FF), F32)
    y, conv_state = _merge_ffn(attn_parts, ys, gates, x2d, lw['merge'], carry, lw['ffn'],
                               n_seq=batch, tm=MERGE_FFN_TILE, shift=1)
    return y.reshape(batch, seq, D_MODEL), (*kv_new, h_re, h_im, conv_state)


def _sample_layer(x, caches, h0_re, h0_im, conv_buf, rel_bias, lw):
    batch, t_new, _ = x.shape
    m = batch * t_new
    x2d = x.transpose(1, 0, 2).reshape(m, D_MODEL)
    *qkvs, u, gates = _in_proj(x2d, lw['norm1_g'], lw['w_in'], 1, m, (1,) * (N_DIL_GROUPS + 1))
    u = u.reshape(m, SSM_WIDTH)
    qkvs_bt = [q.reshape(t_new, batch, GROUP_QKV).transpose(1, 0, 2) for q in qkvs]

    tbs, tns, views = [], [], []
    for g, (window, dil) in enumerate(DIL_PATTERNS):
        tab = rel_bias[:, g * HEADS_PER_GROUP:(g + 1) * HEADS_PER_GROUP]
        n_cached = caches[g].shape[1]
        tb, tn = _sample_bias(tab, dil, t_new, n_cached)
        tbs.append(tb)
        tns.append(tn)
        views.append(caches[g].transpose(0, 2, 3, 4, 1).reshape(batch, 2 * GROUP_WIDTH, n_cached))
    parts = _attn_sample(qkvs_bt, views, tbs, jnp.stack(tns))
    attn_parts = [p.transpose(1, 0, 2).reshape(1, 1, m, GROUP_WIDTH) for p in parts]
    kv_new = [_kv_rows(q.reshape(batch, 1, t_new, GROUP_QKV), t_new) for q in qkvs_bt]

    ops, a_re, a_im = _ssm_block_operators(t_new, lw['ssm_terms'])
    ys, h_re, h_im = _ssm_sample(u, ops, a_re, a_im, lw['ssm_d'],
                                 h0_re.reshape(batch, -1), h0_im.reshape(batch, -1), t_new)

    carry = conv_buf.transpose(1, 0, 2).reshape(1, (CONV_W - 1) * batch, D_FF)
    y, conv_state = _merge_ffn(attn_parts, ys, gates, x2d, lw['merge'], carry, lw['ffn'],
                               n_seq=1, tm=m, shift=batch)
    y = y.reshape(t_new, batch, D_MODEL).transpose(1, 0, 2)
    conv_state = conv_state.reshape(CONV_W - 1, batch, D_FF).transpose(1, 0, 2)
    state_shape = (batch, N_SSM_GROUPS, SSM_STATE)
    return y, (*kv_new, h_re.reshape(state_shape), h_im.reshape(state_shape), conv_state)


IN_PROJ_TILE = 512
MERGE_FFN_TILE = 512


def kernel(x_prompt, x_sample, cache_kv_w128, cache_kv_w512, cache_kv_w2048, state_ssm_re, state_ssm_im, state_ffn_conv, rel_bias, norm1_g, w_in, ssm_log_dt, ssm_lambda_re, ssm_lambda_im, ssm_b_re, ssm_b_im, ssm_c_re, ssm_c_im, ssm_d, w_glu, b_glu, w_branch_attn, w_branch_ssm, w_out, norm2_g, w_up, conv_w, conv_b, w_down, norm_f_g):
    depth = w_in.shape[0]
    hp, hs = x_prompt, x_sample
    st_p, st_s = [], []
    gf = norm_f_g.reshape(1, D_MODEL)
    for l in range(depth):
        last = l == depth - 1
        lw = {
            'norm1_g': norm1_g[l].reshape(1, D_MODEL),
            'w_in': _group_major_columns(w_in[l]).astype(BF16),
            'ssm_terms': _ssm_chunk_terms(max(SSM_CHUNK, x_sample.shape[1]), ssm_log_dt[l], ssm_lambda_re[l],
                                          ssm_lambda_im[l], ssm_b_re[l], ssm_b_im[l], ssm_c_re[l], ssm_c_im[l]),
            'ssm_d': ssm_d[l].reshape(1, SSM_WIDTH),
            'merge': (w_glu[l].astype(BF16), b_glu[l].reshape(1, SSM_WIDTH),
                      w_branch_attn[l].astype(BF16), w_branch_ssm[l].astype(BF16), w_out[l].astype(BF16)),
            'ffn': (norm2_g[l].reshape(1, D_MODEL), w_up[l].astype(BF16), conv_w[l],
                    conv_b[l].reshape(1, D_FF), w_down[l].astype(BF16), gf),
        }
        assert last, "the final RMSNorm is fused into the last layer's ffn kernel"
        hp, sp = _prompt_layer(hp, rel_bias, lw)
        hs, ss = _sample_layer(hs, (cache_kv_w128[l], cache_kv_w512[l], cache_kv_w2048[l]),
                               state_ssm_re[l], state_ssm_im[l], state_ffn_conv[l], rel_bias, lw)
        st_p.append(sp)
        st_s.append(ss)
    stack = lambda states, i: jnp.stack([st[i] for st in states], axis=0)
    return (hp, hs, *[stack(st_p, i) for i in range(6)], *[stack(st_s, i) for i in range(6)])
```

```python
import functools
import math

import jax
import jax.numpy as jnp
import numpy as np
from jax import lax
from jax.experimental import pallas as pl
from jax.experimental.pallas import tpu as pltpu

F32 = jnp.float32
BF16 = jnp.bfloat16

D_MODEL = 1024
HEAD_DIM = 64
HEADS_PER_GROUP = 4
DIL_PATTERNS = ((128, 1), (512, 4), (2048, 16))
N_DIL_GROUPS = len(DIL_PATTERNS)
GROUP_WIDTH = HEADS_PER_GROUP * HEAD_DIM
QK_WIDTH = N_DIL_GROUPS * GROUP_WIDTH
QKV_WIDTH = 3 * QK_WIDTH
GROUP_QKV = 3 * GROUP_WIDTH
QBLOCK = 128
SSM_GROUP = 16
SSM_STATE = 64
SSM_WIDTH = D_MODEL // 2
N_SSM_GROUPS = SSM_WIDTH // SSM_GROUP
SSM_CHUNK = 16
D_FF = 2816
CONV_W = 3
N_BUCKETS = 32
MAX_DISTANCE = 2048
NORM_EPS = 1e-6
NEG_INF = -1e30
U_START = QKV_WIDTH
GATE_START = U_START + SSM_WIDTH
IN_WIDTH = GATE_START + 2 * D_MODEL
QK_SCALE = HEAD_DIM ** -0.5

VMEM_LIMIT_BYTES = 60 * 1024 * 1024
SUBLANES = 8
LANES = 128


def _compiler_params(*semantics):
    return pltpu.CompilerParams(dimension_semantics=semantics, vmem_limit_bytes=VMEM_LIMIT_BYTES)


def _resident(shape):
    nd = len(shape)
    return pl.BlockSpec(shape, lambda *_: (0,) * nd, pipeline_mode=pl.Buffered(1))


def _rmsnorm(xf, g):
    y = xf * lax.rsqrt(jnp.mean(xf * xf, axis=-1, keepdims=True) + NORM_EPS)
    return y * g


def _sigmoid(x):
    return 1.0 / (1.0 + jnp.exp(-x))


def _dot(a, b):
    return jnp.dot(a, b, preferred_element_type=F32)


def _dot_nt(a, b):
    return lax.dot_general(a, b, (((1,), (1,)), ((), ())), preferred_element_type=F32)


def _in_proj_kernel(x_ref, g_ref, w_ref, qkv0_ref, qkv1_ref, qkv2_ref, u_ref, gate_ref, *rest,
                    tm, dils, tails, tiles):
    if tails is None:
        tail_refs, (xn_scr,) = (), rest
    else:
        tail_refs, (xn_scr, tok_scr) = rest[:N_DIL_GROUPS], rest[N_DIL_GROUPS:]
    xf = _rmsnorm(x_ref[...], g_ref[...])
    n_lane_blocks = D_MODEL // LANES
    if any(dil > 1 for dil in dils):
        for k in range(n_lane_blocks):
            xn_scr[k] = xf[:, k * LANES:(k + 1) * LANES]
    by_residue = {1: xf.astype(BF16)}

    def rows_by_residue(dil):
        if dil not in by_residue:
            n = tm // dil
            xr = jnp.concatenate(
                [jnp.concatenate([xn_scr[k, pl.ds(r, n, stride=dil), :] for k in range(n_lane_blocks)], axis=1)
                 for r in range(dil)], axis=0)
            by_residue[dil] = xr.astype(BF16)
        return by_residue[dil]

    outputs = [(ref, g * GROUP_QKV, GROUP_QKV) for g, ref in enumerate((qkv0_ref, qkv1_ref, qkv2_ref))]
    outputs.append((u_ref, U_START, SSM_WIDTH))
    for c0 in range(0, 2 * D_MODEL, D_MODEL):
        logits = _dot(by_residue[1], w_ref[:, GATE_START + c0:GATE_START + c0 + D_MODEL])
        gate_ref[:, c0:c0 + D_MODEL] = _sigmoid(logits).astype(BF16)
    for (ref, start, width), dil in zip(outputs, dils):
        n = tm // dil
        res = _dot(rows_by_residue(dil), w_ref[:, start:start + width])
        for r in range(dil):
            ref[0, r] = res[r * n:(r + 1) * n]
    if tails is None:
        return
    kv_blocks = 2 * GROUP_WIDTH // LANES
    for (ref, _, _), dil, keep, tail_ref in zip(outputs, dils, tails, tail_refs):
        n = tm // dil
        kept = min(keep, tm)

        @pl.when(pl.program_id(1) >= tiles - max(keep // tm, 1))
        def _(ref=ref, dil=dil, n=n, kept=kept, tail_ref=tail_ref):
            if dil == 1:
                kv = ref[0, 0, :, GROUP_WIDTH:]
            else:
                for r in range(dil):
                    for k in range(kv_blocks):
                        lanes = slice(GROUP_WIDTH + k * LANES, GROUP_WIDTH + (k + 1) * LANES)
                        tok_scr[k, pl.ds(r, n, stride=dil), :] = ref[0, r, :, lanes]
                kv = jnp.concatenate([tok_scr[k] for k in range(kv_blocks)], axis=1)
            tail_ref[0] = kv[tm - kept:, :].T


def _group_major_columns(w_in):
    parts = []
    for g in range(N_DIL_GROUPS):
        for base in (0, QK_WIDTH, 2 * QK_WIDTH):
            parts.append(w_in[:, base + g * GROUP_WIDTH:base + (g + 1) * GROUP_WIDTH])
    parts.append(w_in[:, U_START:])
    return jnp.concatenate(parts, axis=1)


def _in_proj(x2d, g, w_bf16, n_seq, tm, dils, tails=None):
    m = x2d.shape[0]
    seq = m // n_seq
    tiles = seq // tm
    row = lambda width: pl.BlockSpec((tm, width), lambda b, j: (b * tiles + j, 0))
    res_spec = lambda dil, width: pl.BlockSpec((1, dil, tm // dil, width), lambda b, j: (b, 0, j, 0))
    widths = (GROUP_QKV,) * N_DIL_GROUPS + (SSM_WIDTH,)
    out_specs = [*[res_spec(d, w) for d, w in zip(dils, widths)], row(2 * D_MODEL)]
    out_shape = [*[jax.ShapeDtypeStruct((n_seq, d, seq // d, w), F32) for d, w in zip(dils, widths)],
                 jax.ShapeDtypeStruct((m, 2 * D_MODEL), BF16)]
    scratch = [pltpu.VMEM((D_MODEL // LANES, tm, LANES), F32)]
    if tails is not None:
        for keep in tails:
            assert keep % tm == 0 or tm % keep == 0
            first = tiles - max(keep // tm, 1)
            out_specs.append(pl.BlockSpec((1, 2 * GROUP_WIDTH, min(keep, tm)),
                                          lambda b, j, first=first: (b, 0, jnp.maximum(j - first, 0))))
            out_shape.append(jax.ShapeDtypeStruct((n_seq, 2 * GROUP_WIDTH, keep), F32))
        scratch.append(pltpu.VMEM((2 * GROUP_WIDTH // LANES, tm, LANES), F32))
    return pl.pallas_call(
        functools.partial(_in_proj_kernel, tm=tm, dils=dils, tails=tails, tiles=tiles),
        grid=(n_seq, tiles),
        in_specs=[row(D_MODEL), _resident((1, D_MODEL)), _resident((D_MODEL, IN_WIDTH))],
        out_specs=out_specs,
        out_shape=out_shape,
        scratch_shapes=scratch,
        compiler_params=_compiler_params("arbitrary", "arbitrary"),
        name="in_proj",
    )(x2d, g, w_bf16)


def _bucket_starts():
    max_exact = N_BUCKETS // 2
    n = np.arange(max_exact, MAX_DISTANCE + 1)
    large = max_exact + (np.log(n.astype(np.float32) / np.float32(max_exact))
                         / np.float32(math.log(MAX_DISTANCE / max_exact))
                         * np.float32(N_BUCKETS - max_exact)).astype(np.int32)
    large = np.minimum(large, N_BUCKETS - 1)
    return [int(n[np.argmax(large >= k)]) for k in range(max_exact + 1, N_BUCKETS)]


def _rel_bucket(dist):
    max_exact = N_BUCKETS // 2
    n = jnp.maximum(dist, 0)
    large = max_exact + sum((n >= start).astype(jnp.int32) for start in _bucket_starts())
    return jnp.where(n < max_exact, n, large)


def _masked_bias(tab, strides, valid, dil):
    n_dist = QBLOCK
    bucket = _rel_bucket(jnp.clip(strides, 0, n_dist) * dil).reshape(1, -1)
    onehot = (bucket == jnp.arange(N_BUCKETS)[:, None]).astype(F32)
    bias = jnp.dot(tab.astype(F32).T, onehot, precision=lax.Precision.HIGHEST)
    bias = jnp.where(valid.reshape(1, -1), bias, NEG_INF)
    return bias.reshape((tab.shape[1],) + strides.shape)


def _prompt_bias(tab, dil):
    qi = jnp.arange(QBLOCK)[:, None]
    ki = jnp.arange(QBLOCK)[None, :]
    j_prev = qi + QBLOCK - ki
    j_cur = qi - ki
    bias = jnp.stack([_masked_bias(tab, j_prev, j_prev <= QBLOCK, dil),
                      _masked_bias(tab, j_cur, j_cur >= 0, dil)], axis=0)
    return bias.transpose(0, 2, 1, 3).reshape(2, QBLOCK, HEADS_PER_GROUP * QBLOCK)


def _sample_bias(tab, dil, t_new, n_cached):
    t = jnp.arange(t_new)[:, None]
    delta_buf = n_cached + t - jnp.arange(n_cached)[None, :]
    ok_buf = (delta_buf % dil == 0) & (delta_buf // dil <= QBLOCK)
    m = jnp.arange(QBLOCK)[None, :]
    delta_new = t - m
    ok_new = (m < t_new) & (delta_new >= 0) & (delta_new % dil == 0)
    b_buf = _masked_bias(tab, delta_buf // dil, ok_buf, dil)
    b_new = _masked_bias(tab, delta_new // dil, ok_new, dil)
    return (b_buf.reshape(HEADS_PER_GROUP * t_new, n_cached),
            b_new.reshape(HEADS_PER_GROUP * t_new, QBLOCK))


def _attn_prompt_kernel(q_ref, kc_ref, vc_ref, bias_ref, o_ref, lse_ref, kprev_scr, vprev_scr, *, nq):
    first_tile = pl.program_id(2) == 0

    @pl.when(first_tile)
    def _():
        kprev_scr[...] = jnp.zeros_like(kprev_scr)
        vprev_scr[...] = jnp.zeros_like(vprev_scr)

    stacked = (HEADS_PER_GROUP * QBLOCK, GROUP_WIDTH)
    own_head = (lax.broadcasted_iota(jnp.int32, stacked, 0) // QBLOCK
                == lax.broadcasted_iota(jnp.int32, stacked, 1) // HEAD_DIM)
    lane_head = lax.broadcasted_iota(jnp.int32, (QBLOCK, GROUP_WIDTH), 1) // HEAD_DIM

    def per_head(x):
        xb = x.astype(BF16)
        return jnp.where(own_head, jnp.concatenate([xb] * HEADS_PER_GROUP, axis=0), 0)

    def on_head_lanes(cols):
        out = jnp.broadcast_to(cols[-1], (QBLOCK, GROUP_WIDTH))
        for h in range(HEADS_PER_GROUP - 2, -1, -1):
            out = jnp.where(lane_head == h, cols[h], out)
        return out

    block_rows = [slice(i * QBLOCK, (i + 1) * QBLOCK) for i in range(nq)]
    qs = [(q_ref[0, 0, rows, :] * QK_SCALE).astype(BF16) for rows in block_rows]
    k_blocks = [kprev_scr[...]] + [per_head(kc_ref[0, 0, rows, :]) for rows in block_rows]
    v_blocks = [vprev_scr[...]] + [per_head(vc_ref[0, 0, rows, :]) for rows in block_rows]
    kprev_scr[...] = k_blocks[-1]
    vprev_scr[...] = v_blocks[-1]

    def users(per_query_cur, per_query_prev, j):
        parts = ([per_query_cur[j - 1]] if j >= 1 else []) + ([per_query_prev[j]] if j < nq else [])
        return parts[0] if len(parts) == 1 else jnp.concatenate(parts, axis=0)

    s_prev, s_cur = [None] * nq, [None] * nq
    for j in range(nq + 1):
        s = _dot_nt(users(qs, qs, j), k_blocks[j])
        if j >= 1:
            s_cur[j - 1] = s[:QBLOCK] + bias_ref[1]
        if j < nq:
            s_prev[j] = s[-QBLOCK:] + bias_ref[0]
    s_prev[0] = jnp.where(first_tile, NEG_INF, s_prev[0])

    p_prev, p_cur, dens, lses = [], [], [], []
    for i in range(nq):
        pp_heads, pc_heads, den_heads, lse_heads = [], [], [], []
        for h in range(HEADS_PER_GROUP):
            keys = slice(h * QBLOCK, (h + 1) * QBLOCK)
            sp, sc = s_prev[i][:, keys], s_cur[i][:, keys]
            m = jnp.max(jnp.maximum(sp, sc), axis=-1, keepdims=True)
            pp, pc = jnp.exp(sp - m), jnp.exp(sc - m)
            den = jnp.sum(pp + pc, axis=-1, keepdims=True)
            pp_heads.append(pp.astype(BF16))
            pc_heads.append(pc.astype(BF16))
            den_heads.append(den)
            lse_heads.append(m + jnp.log(den))
        p_prev.append(jnp.concatenate(pp_heads, axis=1))
        p_cur.append(jnp.concatenate(pc_heads, axis=1))
        dens.append(on_head_lanes(den_heads))
        lses.append(on_head_lanes(lse_heads))

    o = [None] * nq
    for j in range(nq + 1):
        r = _dot(users(p_cur, p_prev, j), v_blocks[j])
        if j >= 1:
            o[j - 1] = o[j - 1] + r[:QBLOCK]
        if j < nq:
            o[j] = r[-QBLOCK:]
    for i, rows in enumerate(block_rows):
        o_ref[0, 0, rows, :] = o[i] / dens[i]
        lse_ref[0, 0, rows, :] = lses[i]


ATTN_QUERY_BLOCKS = 8


def _attn_prompt(qkv, bias):
    batch, dil, length, _ = qkv.shape
    tq = min(ATTN_QUERY_BLOCKS * QBLOCK, length)
    nq = tq // QBLOCK

    def cur(col):
        return pl.BlockSpec((1, 1, tq, GROUP_WIDTH), lambda b, r, n: (b, r, n, col))

    stacked = pltpu.VMEM((HEADS_PER_GROUP * QBLOCK, GROUP_WIDTH), BF16)
    out_sds = jax.ShapeDtypeStruct((batch, dil, length, GROUP_WIDTH), F32)
    return pl.pallas_call(
        functools.partial(_attn_prompt_kernel, nq=nq),
        grid=(batch, dil, length // tq),
        in_specs=[cur(0), cur(1), cur(2), _resident((2, QBLOCK, HEADS_PER_GROUP * QBLOCK))],
        out_specs=[cur(0), cur(0)],
        out_shape=[out_sds, out_sds],
        scratch_shapes=[stacked, stacked],
        compiler_params=_compiler_params("arbitrary", "arbitrary", "arbitrary"),
        name=f"attn_prompt_d{dil}",
    )(qkv, qkv, qkv, bias)


def _attn_sample_kernel(q0_ref, q1_ref, q2_ref, c0_ref, c1_ref, c2_ref, tb0_ref, tb1_ref, tb2_ref, tn_ref,
                        o0_ref, l0_ref, o1_ref, l1_ref, o2_ref, l2_ref, kn_scr, vn_scr, *, t_new):
    n_rows = HEADS_PER_GROUP * t_new
    row_w = lax.broadcasted_iota(jnp.int32, (n_rows, GROUP_WIDTH), 0)
    lane_w = lax.broadcasted_iota(jnp.int32, (n_rows, GROUP_WIDTH), 1)
    own_head = (row_w // t_new) == (lane_w // HEAD_DIM)

    def fold_heads(x):
        x = jnp.where(own_head, x, 0.0)
        out = x[0:t_new]
        for h in range(1, HEADS_PER_GROUP):
            out = out + x[h * t_new:(h + 1) * t_new]
        return out

    caches = (c0_ref, c1_ref, c2_ref)
    cache_bias = (tb0_ref, tb1_ref, tb2_ref)
    outs = ((o0_ref, l0_ref), (o1_ref, l1_ref), (o2_ref, l2_ref))
    for g in range(N_DIL_GROUPS):
        qkv_ref = (q0_ref, q1_ref, q2_ref)[g]
        q = qkv_ref[0, :, 0:GROUP_WIDTH] * QK_SCALE
        q_rows = jnp.where(own_head, jnp.concatenate([q] * HEADS_PER_GROUP, axis=0), 0.0).astype(BF16)
        kn_scr[...] = jnp.zeros_like(kn_scr)
        vn_scr[...] = jnp.zeros_like(vn_scr)
        kn_scr[0:t_new, :] = qkv_ref[0, :, GROUP_WIDTH:2 * GROUP_WIDTH]
        vn_scr[0:t_new, :] = qkv_ref[0, :, 2 * GROUP_WIDTH:3 * GROUP_WIDTH]
        cache = caches[g]
        k_t = cache[0, 0:GROUP_WIDTH, :].astype(BF16)
        v_t = cache[0, GROUP_WIDTH:2 * GROUP_WIDTH, :].astype(BF16)
        s_buf = _dot(q_rows, k_t) + cache_bias[g][...]
        s_new = _dot_nt(q_rows, kn_scr[...].astype(BF16)) + tn_ref[g]
        m = jnp.maximum(jnp.max(s_buf, axis=-1, keepdims=True), jnp.max(s_new, axis=-1, keepdims=True))
        p_buf = jnp.exp(s_buf - m)
        p_new = jnp.exp(s_new - m)
        den = jnp.sum(p_buf, axis=-1, keepdims=True) + jnp.sum(p_new, axis=-1, keepdims=True)
        o = _dot_nt(p_buf.astype(BF16), v_t) + _dot(p_new.astype(BF16), vn_scr[...].astype(BF16))
        o_ref, l_ref = outs[g]
        o_ref[0] = fold_heads(o / den)
        l_ref[0] = fold_heads(jnp.broadcast_to(m + jnp.log(den), (n_rows, GROUP_WIDTH)))


def _attn_sample(qkvs, caches, tbs, tn):
    batch, t_new, _ = qkvs[0].shape
    n_rows = HEADS_PER_GROUP * t_new
    cache_specs = [pl.BlockSpec((1,) + c.shape[1:], lambda b: (b, 0, 0)) for c in caches]
    qkv_spec = pl.BlockSpec((1, t_new, GROUP_QKV), lambda b: (b, 0, 0))
    out_spec = pl.BlockSpec((1, t_new, GROUP_WIDTH), lambda b: (b, 0, 0))
    out_sds = jax.ShapeDtypeStruct((batch, t_new, GROUP_WIDTH), F32)
    return pl.pallas_call(
        functools.partial(_attn_sample_kernel, t_new=t_new),
        grid=(batch,),
        in_specs=[*[qkv_spec] * N_DIL_GROUPS, *cache_specs, *[_resident(t.shape) for t in tbs],
                  _resident((N_DIL_GROUPS, n_rows, QBLOCK))],
        out_specs=[out_spec] * (2 * N_DIL_GROUPS),
        out_shape=[out_sds] * (2 * N_DIL_GROUPS),
        scratch_shapes=[pltpu.VMEM((QBLOCK, GROUP_WIDTH), F32), pltpu.VMEM((QBLOCK, GROUP_WIDTH), F32)],
        compiler_params=_compiler_params("arbitrary"),
        name="attn_sample",
    )(*qkvs, *caches, *tbs, tn)


def _ssm_chunk_terms(chunk, log_dt, lam_re, lam_im, b_re, b_im, c_re, c_im):
    hi = lax.Precision.HIGHEST
    dt = jnp.exp(log_dt.astype(F32))[:, None]
    lr, li = lam_re.astype(F32), lam_im.astype(F32)
    mag = jnp.exp(lr * dt)
    ab_re, ab_im = mag * jnp.cos(li * dt), mag * jnp.sin(li * dt)
    g, n = lr.shape
    p = b_re.shape[-1]
    den = lr * lr + li * li
    nr, ni = ab_re - 1.0, ab_im
    coef_re = (nr * lr + ni * li) / den
    coef_im = (ni * lr - nr * li) / den
    br, bi = b_re.astype(F32), b_im.astype(F32)
    bb_re = coef_re[..., None] * br - coef_im[..., None] * bi
    bb_im = coef_re[..., None] * bi + coef_im[..., None] * br
    k = jnp.arange(chunk + 1, dtype=F32)[:, None, None]
    pw_mag = jnp.exp(k * (lr * dt)[None])
    pw_re, pw_im = pw_mag * jnp.cos(k * (li * dt)[None]), pw_mag * jnp.sin(k * (li * dt)[None])
    bt_re, bt_im = bb_re.transpose(0, 2, 1)[None], bb_im.transpose(0, 2, 1)[None]
    pk_re, pk_im = pw_re[:chunk, :, None, :], pw_im[:chunk, :, None, :]
    akb_re = pk_re * bt_re - pk_im * bt_im
    akb_im = pk_re * bt_im + pk_im * bt_re
    cr, ci = c_re.astype(F32)[None], c_im.astype(F32)[None]
    pe_re, pe_im = pw_re[1:, :, None, :], pw_im[1:, :, None, :]
    e_re = cr * pe_re - ci * pe_im
    e_im = cr * pe_im + ci * pe_re
    kern = (jnp.einsum('kgqn,gpn->kgqp', akb_re, cr[0], precision=hi)
            - jnp.einsum('kgqn,gpn->kgqp', akb_im, ci[0], precision=hi))
    rows = lambda x: x.reshape(chunk, g * p, x.shape[-1])
    return ((rows(akb_re), rows(akb_im)), (rows(e_re), rows(e_im)), rows(kern), (pw_re, pw_im))


SSM_BLOCK_GROUPS = 8
SSM_BLOCK_CH = SSM_BLOCK_GROUPS * SSM_GROUP
SSM_BLOCK_STATE = SSM_BLOCK_GROUPS * SSM_STATE
N_SSM_BLOCKS = N_SSM_GROUPS // SSM_BLOCK_GROUPS


def _ssm_operator_kernel(akbre_ref, akbim_ref, ere_ref, eim_ref, kern_ref,
                         wre_ref, wim_ref, m_ref, etre_ref, etim_ref, *, chunk):
    gb, p, n = SSM_BLOCK_GROUPS, SSM_GROUP, SSM_STATE

    def copies(width, count):
        src = lax.broadcasted_iota(jnp.int32, (width, count * width), 0)
        dst = lax.broadcasted_iota(jnp.int32, (width, count * width), 1)
        return jnp.where(src == dst % width, 1.0, 0.0).astype(BF16)

    def own_group(width):
        row = lax.broadcasted_iota(jnp.int32, (SSM_BLOCK_CH, gb * width), 0)
        col = lax.broadcasted_iota(jnp.int32, (SSM_BLOCK_CH, gb * width), 1)
        return row // p == col // width

    to_states, own_states = copies(n, gb), own_group(n)

    def over_states(piece):
        return jnp.where(own_states, _dot(piece.astype(BF16), to_states), 0.0).astype(BF16)

    for s in range(chunk):
        rows = slice(s * SSM_BLOCK_CH, (s + 1) * SSM_BLOCK_CH)
        wre_ref[0, rows, :] = over_states(akbre_ref[chunk - 1 - s, 0])
        wim_ref[0, rows, :] = over_states(akbim_ref[chunk - 1 - s, 0])
        etre_ref[0, rows, :] = over_states(ere_ref[s, 0])
        etim_ref[0, rows, :] = over_states(-eim_ref[s, 0])
    to_channels, own_channels = copies(p, gb), own_group(p)
    lags = [jnp.where(own_channels, _dot(kern_ref[k, 0].astype(BF16), to_channels), 0.0).astype(BF16)
            for k in range(chunk)]
    zero = jnp.zeros((SSM_BLOCK_CH, SSM_BLOCK_CH), BF16)
    for s in range(chunk):
        m_ref[0, s * SSM_BLOCK_CH:(s + 1) * SSM_BLOCK_CH, :] = jnp.concatenate(
            [zero] * s + lags[:chunk - s], axis=1)


def _ssm_block_operators(chunk, terms):
    (akb_re, akb_im), (e_re, e_im), kern, (pw_re, pw_im) = terms
    assert chunk <= kern.shape[0]
    a_re, a_im = pw_re[chunk], pw_im[chunk]
    nb, gb, p, n = N_SSM_BLOCKS, SSM_BLOCK_GROUPS, SSM_GROUP, SSM_STATE
    pieces = [x[:chunk].reshape(chunk, nb, SSM_BLOCK_CH, x.shape[-1])
              for x in (akb_re, akb_im, e_re, e_im, kern)]
    piece_spec = lambda x: pl.BlockSpec((chunk, 1) + x.shape[2:], lambda i: (0, i, 0, 0))
    out_spec = lambda shape: pl.BlockSpec((1,) + shape[1:], lambda i: (i, 0, 0))
    x_width = chunk * SSM_BLOCK_CH
    out_shapes = [(nb, x_width, gb * n), (nb, x_width, gb * n), (nb, x_width, x_width),
                  (nb, x_width, gb * n), (nb, x_width, gb * n)]
    ops = pl.pallas_call(
        functools.partial(_ssm_operator_kernel, chunk=chunk),
        grid=(nb,),
        in_specs=[piece_spec(x) for x in pieces],
        out_specs=[out_spec(s) for s in out_shapes],
        out_shape=[jax.ShapeDtypeStruct(s, BF16) for s in out_shapes],
        compiler_params=_compiler_params("arbitrary"),
        name="ssm_operators",
    )(*pieces)
    return ops, a_re.reshape(nb, 1, gb * n), a_im.reshape(nb, 1, gb * n)


SSM_X_WIDTH = SSM_CHUNK * SSM_BLOCK_CH


def _ssm_prompt_kernel(u_ref, wre_ref, wim_ref, m_ref, etre_ref, etim_ref, are_ref, aim_ref, d_ref,
                       y_ref, hre_ref, him_ref, x_scr, sre_scr, sim_scr, *, batch, chunks):
    phase, b = pl.program_id(1), pl.program_id(2)
    for t in range(SSM_CHUNK):
        x_scr[:, t * SSM_BLOCK_CH:(t + 1) * SSM_BLOCK_CH] = u_ref[0, t].astype(BF16)
    rows = pl.ds(b, chunks, stride=batch)
    state_blocks = SSM_BLOCK_STATE // LANES

    @pl.when(phase == 0)
    def _():
        x = x_scr[...]
        g_re, g_im = _dot(x, wre_ref[0]), _dot(x, wim_ref[0])
        for k in range(state_blocks):
            sre_scr[k, rows, :] = g_re[:, k * LANES:(k + 1) * LANES]
            sim_scr[k, rows, :] = g_im[:, k * LANES:(k + 1) * LANES]

    @pl.when((phase == 1) & (b == 0))
    def _():
        same_lanes = lambda ref, k: ref[0][:, k * LANES:(k + 1) * LANES]

        def step(c, carry):
            same_chunk = pl.ds(c * batch, batch)
            out = []
            for k, (h_re, h_im) in enumerate(carry):
                a_re, a_im = same_lanes(are_ref, k), same_lanes(aim_ref, k)
                g_re, g_im = sre_scr[k, same_chunk, :], sim_scr[k, same_chunk, :]
                sre_scr[k, same_chunk, :] = h_re
                sim_scr[k, same_chunk, :] = h_im
                out.append((a_re * h_re - a_im * h_im + g_re, a_re * h_im + a_im * h_re + g_im))
            return tuple(out)

        zero = jnp.zeros((batch, LANES), F32)
        final = lax.fori_loop(0, chunks, step, ((zero, zero),) * state_blocks)
        for k, (h_re, h_im) in enumerate(final):
            hre_ref[0, :, k * LANES:(k + 1) * LANES] = h_re
            him_ref[0, :, k * LANES:(k + 1) * LANES] = h_im

    @pl.when(phase == 1)
    def _():
        h_re = jnp.concatenate([sre_scr[k, rows, :] for k in range(state_blocks)], axis=1).astype(BF16)
        h_im = jnp.concatenate([sim_scr[k, rows, :] for k in range(state_blocks)], axis=1).astype(BF16)
        pair_w = 2 * SSM_BLOCK_CH
        for j in range(SSM_CHUNK // 2):
            cols = slice(j * pair_w, (j + 1) * pair_w)
            k_in = (j + 1) * pair_w
            yj = (_dot(x_scr[:, :k_in], m_ref[0, :k_in, cols])
                  + _dot_nt(h_re, etre_ref[0, cols, :]) + _dot_nt(h_im, etim_ref[0, cols, :]))
            for i in range(2):
                t = 2 * j + i
                y_ref[pl.ds(t, chunks, stride=SSM_CHUNK), :] = (
                    yj[:, i * SSM_BLOCK_CH:(i + 1) * SSM_BLOCK_CH] + d_ref[0] * u_ref[0, t])


def _ssm_prompt(u, ops, a_re, a_im, d, batch, seq):
    chunks = seq // SSM_CHUNK
    d = d.reshape(N_SSM_BLOCKS, 1, SSM_BLOCK_CH)
    op_spec = lambda arr: pl.BlockSpec((1,) + arr.shape[1:], lambda g, ph, b: (g, 0, 0),
                                       pipeline_mode=pl.Buffered(1))
    state_spec = pl.BlockSpec((1, batch, SSM_BLOCK_STATE), lambda g, ph, b: (g, 0, 0))
    state_sds = jax.ShapeDtypeStruct((N_SSM_BLOCKS, batch, SSM_BLOCK_STATE), F32)
    y, h_re, h_im = pl.pallas_call(
        functools.partial(_ssm_prompt_kernel, batch=batch, chunks=chunks),
        grid=(N_SSM_BLOCKS, 2, batch),
        in_specs=[pl.BlockSpec((1, SSM_CHUNK, chunks, SSM_BLOCK_CH), lambda g, ph, b: (b, 0, 0, g)),
                  *[op_spec(o) for o in ops], op_spec(a_re), op_spec(a_im), op_spec(d)],
        out_specs=[pl.BlockSpec((seq, SSM_BLOCK_CH), lambda g, ph, b: (b * ph, g)), state_spec, state_spec],
        out_shape=[jax.ShapeDtypeStruct((batch * seq, SSM_WIDTH), F32), state_sds, state_sds],
        scratch_shapes=[pltpu.VMEM((chunks, SSM_X_WIDTH), BF16),
                        pltpu.VMEM((SSM_BLOCK_STATE // LANES, batch * chunks, LANES), F32),
                        pltpu.VMEM((SSM_BLOCK_STATE // LANES, batch * chunks, LANES), F32)],
        compiler_params=_compiler_params("arbitrary", "arbitrary", "arbitrary"),
        name="ssm_prompt",
    )(u, *ops, a_re, a_im, d)

    def by_sequence(h):
        h = h.reshape(N_SSM_BLOCKS, batch, SSM_BLOCK_GROUPS, SSM_STATE).transpose(1, 0, 2, 3)
        return h.reshape(batch, N_SSM_GROUPS, SSM_STATE)

    return y, by_sequence(h_re), by_sequence(h_im)


def _ssm_sample_kernel(u_ref, wre_ref, wim_ref, m_ref, etre_ref, etim_ref, are_ref, aim_ref, d_ref,
                       h0re_ref, h0im_ref, y_ref, hre_ref, him_ref, *, t_new, batch):
    for nb in range(N_SSM_BLOCKS):
        ch = slice(nb * SSM_BLOCK_CH, (nb + 1) * SSM_BLOCK_CH)
        st = slice(nb * SSM_BLOCK_STATE, (nb + 1) * SSM_BLOCK_STATE)
        x = jnp.concatenate([u_ref[t * batch:(t + 1) * batch, ch] for t in range(t_new)], axis=1).astype(BF16)
        h_re, h_im = h0re_ref[:, st], h0im_ref[:, st]
        a_re, a_im = are_ref[nb], aim_ref[nb]
        hre_ref[:, st] = a_re * h_re - a_im * h_im + _dot(x, wre_ref[nb])
        him_ref[:, st] = a_re * h_im + a_im * h_re + _dot(x, wim_ref[nb])
        y = (_dot(x, m_ref[nb]) + _dot_nt(h_re.astype(BF16), etre_ref[nb])
             + _dot_nt(h_im.astype(BF16), etim_ref[nb]))
        for t in range(t_new):
            rows = slice(t * batch, (t + 1) * batch)
            y_ref[rows, ch] = y[:, t * SSM_BLOCK_CH:(t + 1) * SSM_BLOCK_CH] + d_ref[:, ch] * u_ref[rows, ch]


def _ssm_sample(u, ops, a_re, a_im, d, h0_re, h0_im, t_new):
    batch = u.shape[0] // t_new
    args = (u, *ops, a_re, a_im, d, h0_re, h0_im)
    state_sds = jax.ShapeDtypeStruct((batch, N_SSM_GROUPS * SSM_STATE), F32)
    out_shape = [jax.ShapeDtypeStruct(u.shape, F32), state_sds, state_sds]
    whole = lambda shape: pl.BlockSpec(shape, lambda i, nd=len(shape): (0,) * nd)
    return pl.pallas_call(
        functools.partial(_ssm_sample_kernel, t_new=t_new, batch=batch),
        grid=(1,),
        in_specs=[whole(a.shape) for a in args],
        out_specs=[whole(s.shape) for s in out_shape],
        out_shape=out_shape,
        compiler_params=_compiler_params("arbitrary"),
        name="ssm_sample",
    )(*args)


def _gelu_tanh(x):
    return 0.5 * x * (1.0 + jnp.tanh(math.sqrt(2.0 / math.pi) * (x + 0.044715 * (x * x * x))))


def _merge_tile(o0_ref, l0_ref, o1_ref, l1_ref, o2_ref, l2_ref, ys_ref, gate_ref, x_ref,
                wglu_ref, bglu_ref, wba_ref, wbs_ref, wout_ref, order_scr, tm, dils):
    def row_order(ref, dil, slot):
        if dil == 1:
            return ref[0, 0]
        n = tm // dil
        halves = GROUP_WIDTH // LANES
        for r in range(dil):
            for k in range(halves):
                order_scr[slot * halves + k, pl.ds(r, n, stride=dil), :] = ref[0, r, :, k * LANES:(k + 1) * LANES]
        return jnp.concatenate([order_scr[slot * halves + k] for k in range(halves)], axis=1)

    parts = [row_order(ref, dils[i // 2], i) for i, ref in
             enumerate((o0_ref, l0_ref, o1_ref, l1_ref, o2_ref, l2_ref))]
    o0, l0, o1, l1, o2, l2 = parts
    mx = jnp.maximum(jnp.maximum(l0, l1), l2)
    e0, e1, e2 = jnp.exp(l0 - mx), jnp.exp(l1 - mx), jnp.exp(l2 - mx)
    attn = (e0 * o0 + e1 * o1 + e2 * o2) / (e0 + e1 + e2)
    branch_a = _dot(attn.astype(BF16), wba_ref[...])
    y = _gelu_tanh(ys_ref[...])
    y = y * _sigmoid(_dot(y.astype(BF16), wglu_ref[...]) + bglu_ref[...])
    branch_s = _dot(y.astype(BF16), wbs_ref[...])
    mix = (gate_ref[:, 0:D_MODEL].astype(F32) * branch_a
           + gate_ref[:, D_MODEL:2 * D_MODEL].astype(F32) * branch_s)
    return x_ref[...] + _dot(mix.astype(BF16), wout_ref[...])


MXU_WIDTH = 256
FF_SPLIT = -(-D_FF // (2 * MXU_WIDTH)) * MXU_WIDTH
FF_CHUNKS = ((0, FF_SPLIT), (FF_SPLIT, D_FF))
N_MERGE_ROW_INPUTS = 2 * N_DIL_GROUPS + 3
N_MERGE_WEIGHTS = 5


def _merge_ffn_kernel(*refs, tm, shift, pad, dils):
    n_merge = N_MERGE_ROW_INPUTS + N_MERGE_WEIGHTS
    merge_refs, rest = refs[:n_merge], refs[n_merge:]
    carry_ref, g2_ref, wup_ref, cw_ref, cb_ref, wdn_ref, gf_ref, y_ref, state_ref, order_scr, a_scr = rest
    hist = 2 * shift

    @pl.when(pl.program_id(1) == 0)
    def _():
        a_scr[pad - hist:pad, :] = carry_ref[0]

    xf = _merge_tile(*merge_refs, order_scr, tm, dils)
    xn = _rmsnorm(xf, g2_ref[...]).astype(BF16)
    acc = jnp.zeros((tm, D_MODEL), F32)
    for c0, c1 in FF_CHUNKS:
        cols = slice(c0, c1)
        a = _dot(xn, wup_ref[:, cols])
        val = _dot(xn, wup_ref[:, D_FF + c0:D_FF + c1])
        a_scr[pad:pad + tm, cols] = a
        a_m1 = a_scr[pad - shift:pad - shift + tm, cols]
        a_m2 = a_scr[pad - hist:pad - hist + tm, cols]
        conv = cb_ref[:, cols] + cw_ref[0:1, cols] * a_m2
        conv = conv + cw_ref[1:2, cols] * a_m1
        conv = conv + cw_ref[2:3, cols] * a
        act = conv * _sigmoid(conv) * val
        acc = acc + _dot(act.astype(BF16), wdn_ref[cols, :])
    tail = a_scr[pad + tm - hist:pad + tm, :]
    a_scr[pad - hist:pad, :] = tail
    state_ref[0] = tail
    y_ref[...] = _rmsnorm(xf + acc, gf_ref[...])


def _merge_ffn(attn_parts, ys, gates, x2d, merge_weights, carry, ffn_weights, n_seq, tm, shift):
    m = x2d.shape[0]
    tiles = m // n_seq // tm
    hist = 2 * shift
    pad = -(-hist // SUBLANES) * SUBLANES
    dils = tuple(p.shape[1] for p in attn_parts[::2])
    row = lambda width: pl.BlockSpec((tm, width), lambda b, j: (b * tiles + j, 0))
    res_spec = lambda dil: pl.BlockSpec((1, dil, tm // dil, GROUP_WIDTH), lambda b, j: (b, 0, j, 0))
    state_spec = pl.BlockSpec((1, hist, D_FF), lambda b, j: (b, 0, 0))
    assert len(merge_weights) == N_MERGE_WEIGHTS
    return pl.pallas_call(
        functools.partial(_merge_ffn_kernel, tm=tm, shift=shift, pad=pad, dils=dils),
        grid=(n_seq, tiles),
        in_specs=[*[res_spec(p.shape[1]) for p in attn_parts],
                  row(SSM_WIDTH), row(2 * D_MODEL), row(D_MODEL),
                  *[_resident(w.shape) for w in merge_weights],
                  state_spec, *[_resident(w.shape) for w in ffn_weights]],
        out_specs=[row(D_MODEL), state_spec],
        out_shape=[jax.ShapeDtypeStruct((m, D_MODEL), F32),
                   jax.ShapeDtypeStruct((n_seq, hist, D_FF), F32)],
        scratch_shapes=[pltpu.VMEM((len(attn_parts) * GROUP_WIDTH // LANES, tm, LANES), F32),
                        pltpu.VMEM((pad + tm, D_FF), F32)],
        compiler_params=_compiler_params("arbitrary", "arbitrary"),
        name="merge_ffn",
    )(*attn_parts, ys, gates, x2d, *merge_weights, carry, *ffn_weights)


def _kv_rows(qkv, keep):
    batch, dil, length, _ = qkv.shape
    n = keep // dil
    rows = qkv[:, :, length - n:, GROUP_WIDTH:]
    cols = rows.transpose(0, 3, 2, 1).reshape(batch, 2, HEADS_PER_GROUP, HEAD_DIM, keep)
    return cols.transpose(0, 4, 1, 2, 3)


def _prompt_layer(x, rel_bias, lw):
    batch, seq, _ = x.shape
    x2d = x.reshape(batch * seq, D_MODEL)
    dils = tuple(dil for _, dil in DIL_PATTERNS) + (SSM_CHUNK,)
    keeps = tuple(min(window, seq) for window, _ in DIL_PATTERNS)
    *qkvs, u, gates, kv0, kv1, kv2 = _in_proj(x2d, lw['norm1_g'], lw['w_in'], batch, IN_PROJ_TILE, dils, keeps)

    attn_parts, kv_new = [], []
    for g, (window, dil) in enumerate(DIL_PATTERNS):
        tab = rel_bias[:, g * HEADS_PER_GROUP:(g + 1) * HEADS_PER_GROUP]
        attn_parts.extend(_attn_prompt(qkvs[g], _prompt_bias(tab, dil)))
        tail = (kv0, kv1, kv2)[g].reshape(batch, 2, HEADS_PER_GROUP, HEAD_DIM, keeps[g])
        kv_new.append(tail.transpose(0, 4, 1, 2, 3))

    ops, a_re, a_im = _ssm_block_operators(SSM_CHUNK, lw['ssm_terms'])
    ys, h_re, h_im = _ssm_prompt(u, ops, a_re, a_im, lw['ssm_d'], batch, seq)

    carry = jnp.zeros((batch, CONV_W - 1, D_FF), F32)
    y, conv_state = _merge_ffn(attn_parts, ys, gates, x2d, lw['merge'], carry, lw['ffn'],
                               n_seq=batch, tm=MERGE_FFN_TILE, shift=1)
    return y.reshape(batch, seq, D_MODEL), (*kv_new, h_re, h_im, conv_state)


def _sample_layer(x, caches, h0_re, h0_im, conv_buf, rel_bias, lw):
    batch, t_new, _ = x.shape
    m = batch * t_new
    x2d = x.transpose(1, 0, 2).reshape(m, D_MODEL)
    *qkvs, u, gates = _in_proj(x2d, lw['norm1_g'], lw['w_in'], 1, m, (1,) * (N_DIL_GROUPS + 1))
    u = u.reshape(m, SSM_WIDTH)
    qkvs_bt = [q.reshape(t_new, batch, GROUP_QKV).transpose(1, 0, 2) for q in qkvs]

    tbs, tns, views = [], [], []
    for g, (window, dil) in enumerate(DIL_PATTERNS):
        tab = rel_bias[:, g * HEADS_PER_GROUP:(g + 1) * HEADS_PER_GROUP]
        n_cached = caches[g].shape[1]
        tb, tn = _sample_bias(tab, dil, t_new, n_cached)
        tbs.append(tb)
        tns.append(tn)
        views.append(caches[g].transpose(0, 2, 3, 4, 1).reshape(batch, 2 * GROUP_WIDTH, n_cached))
    parts = _attn_sample(qkvs_bt, views, tbs, jnp.stack(tns))
    attn_parts = [p.transpose(1, 0, 2).reshape(1, 1, m, GROUP_WIDTH) for p in parts]
    kv_new = [_kv_rows(q.reshape(batch, 1, t_new, GROUP_QKV), t_new) for q in qkvs_bt]

    ops, a_re, a_im = _ssm_block_operators(t_new, lw['ssm_terms'])
    ys, h_re, h_im = _ssm_sample(u, ops, a_re, a_im, lw['ssm_d'],
                                 h0_re.reshape(batch, -1), h0_im.reshape(batch, -1), t_new)

    carry = conv_buf.transpose(1, 0, 2).reshape(1, (CONV_W - 1) * batch, D_FF)
    y, conv_state = _merge_ffn(attn_parts, ys, gates, x2d, lw['merge'], carry, lw['ffn'],
                               n_seq=1, tm=m, shift=batch)
    y = y.reshape(t_new, batch, D_MODEL).transpose(1, 0, 2)
    conv_state = conv_state.reshape(CONV_W - 1, batch, D_FF).transpose(1, 0, 2)
    state_shape = (batch, N_SSM_GROUPS, SSM_STATE)
    return y, (*kv_new, h_re.reshape(state_shape), h_im.reshape(state_shape), conv_state)


IN_PROJ_TILE = 512
MERGE_FFN_TILE = 512


def kernel(x_prompt, x_sample, cache_kv_w128, cache_kv_w512, cache_kv_w2048, state_ssm_re, state_ssm_im, state_ffn_conv, rel_bias, norm1_g, w_in, ssm_log_dt, ssm_lambda_re, ssm_lambda_im, ssm_b_re, ssm_b_im, ssm_c_re, ssm_c_im, ssm_d, w_glu, b_glu, w_branch_attn, w_branch_ssm, w_out, norm2_g, w_up, conv_w, conv_b, w_down, norm_f_g):
    depth = w_in.shape[0]
    hp, hs = x_prompt, x_sample
    st_p, st_s = [], []
    gf = norm_f_g.reshape(1, D_MODEL)
    for l in range(depth):
        last = l == depth - 1
        lw = {
            'norm1_g': norm1_g[l].reshape(1, D_MODEL),
            'w_in': _group_major_columns(w_in[l]).astype(BF16),
            'ssm_terms': _ssm_chunk_terms(max(SSM_CHUNK, x_sample.shape[1]), ssm_log_dt[l], ssm_lambda_re[l],
                                          ssm_lambda_im[l], ssm_b_re[l], ssm_b_im[l], ssm_c_re[l], ssm_c_im[l]),
            'ssm_d': ssm_d[l].reshape(1, SSM_WIDTH),
            'merge': (w_glu[l].astype(BF16), b_glu[l].reshape(1, SSM_WIDTH),
                      w_branch_attn[l].astype(BF16), w_branch_ssm[l].astype(BF16), w_out[l].astype(BF16)),
            'ffn': (norm2_g[l].reshape(1, D_MODEL), w_up[l].astype(BF16), conv_w[l],
                    conv_b[l].reshape(1, D_FF), w_down[l].astype(BF16), gf),
        }
        assert last, "the final RMSNorm is fused into the last layer's ffn kernel"
        hp, sp = _prompt_layer(hp, rel_bias, lw)
        hs, ss = _sample_layer(hs, (cache_kv_w128[l], cache_kv_w512[l], cache_kv_w2048[l]),
                               state_ssm_re[l], state_ssm_im[l], state_ffn_conv[l], rel_bias, lw)
        st_p.append(sp)
        st_s.append(ss)
    stack = lambda states, i: jnp.stack([st[i] for st in states], axis=0)
    return (hp, hs, *[stack(st_p, i) for i in range(6)], *[stack(st_s, i) for i in range(6)])
```

```python
import functools
import math

import jax
import jax.numpy as jnp
import numpy as np
from jax import lax
from jax.experimental import pallas as pl
from jax.experimental.pallas import tpu as pltpu

F32 = jnp.float32
BF16 = jnp.bfloat16

D_MODEL = 1024
HEAD_DIM = 64
HEADS_PER_GROUP = 4
DIL_PATTERNS = ((128, 1), (512, 4), (2048, 16))
N_DIL_GROUPS = len(DIL_PATTERNS)
GROUP_WIDTH = HEADS_PER_GROUP * HEAD_DIM
QK_WIDTH = N_DIL_GROUPS * GROUP_WIDTH
QKV_WIDTH = 3 * QK_WIDTH
GROUP_QKV = 3 * GROUP_WIDTH
QBLOCK = 128
SSM_GROUP = 16
SSM_STATE = 64
SSM_WIDTH = D_MODEL // 2
N_SSM_GROUPS = SSM_WIDTH // SSM_GROUP
SSM_CHUNK = 16
D_FF = 2816
CONV_W = 3
N_BUCKETS = 32
MAX_DISTANCE = 2048
NORM_EPS = 1e-6
NEG_INF = -1e30
U_START = QKV_WIDTH
GATE_START = U_START + SSM_WIDTH
IN_WIDTH = GATE_START + 2 * D_MODEL
QK_SCALE = HEAD_DIM ** -0.5

VMEM_LIMIT_BYTES = 60 * 1024 * 1024
SUBLANES = 8
LANES = 128


def _compiler_params(*semantics):
    return pltpu.CompilerParams(dimension_semantics=semantics, vmem_limit_bytes=VMEM_LIMIT_BYTES)


def _resident(shape):
    nd = len(shape)
    return pl.BlockSpec(shape, lambda *_: (0,) * nd, pipeline_mode=pl.Buffered(1))


def _rmsnorm(xf, g):
    y = xf * lax.rsqrt(jnp.mean(xf * xf, axis=-1, keepdims=True) + NORM_EPS)
    return y * g


def _sigmoid(x):
    return 1.0 / (1.0 + jnp.exp(-x))


def _dot(a, b):
    return jnp.dot(a, b, preferred_element_type=F32)


def _dot_nt(a, b):
    return lax.dot_general(a, b, (((1,), (1,)), ((), ())), preferred_element_type=F32)


def _in_proj_kernel(x_ref, g_ref, w_ref, qkv0_ref, qkv1_ref, qkv2_ref, u_ref, gate_ref, *rest,
                    tm, dils, tails, tiles):
    if tails is None:
        tail_refs, (xn_scr,) = (), rest
    else:
        tail_refs, (xn_scr, tok_scr) = rest[:N_DIL_GROUPS], rest[N_DIL_GROUPS:]
    xf = _rmsnorm(x_ref[...], g_ref[...])
    n_lane_blocks = D_MODEL // LANES
    if any(dil > 1 for dil in dils):
        for k in range(n_lane_blocks):
            xn_scr[k] = xf[:, k * LANES:(k + 1) * LANES]
    by_residue = {1: xf.astype(BF16)}

    def rows_by_residue(dil):
        if dil not in by_residue:
            n = tm // dil
            xr = jnp.concatenate(
                [jnp.concatenate([xn_scr[k, pl.ds(r, n, stride=dil), :] for k in range(n_lane_blocks)], axis=1)
                 for r in range(dil)], axis=0)
            by_residue[dil] = xr.astype(BF16)
        return by_residue[dil]

    outputs = [(ref, g * GROUP_QKV, GROUP_QKV) for g, ref in enumerate((qkv0_ref, qkv1_ref, qkv2_ref))]
    outputs.append((u_ref, U_START, SSM_WIDTH))
    for c0 in range(0, 2 * D_MODEL, D_MODEL):
        logits = _dot(by_residue[1], w_ref[:, GATE_START + c0:GATE_START + c0 + D_MODEL])
        gate_ref[:, c0:c0 + D_MODEL] = _sigmoid(logits).astype(BF16)
    for (ref, start, width), dil in zip(outputs, dils):
        n = tm // dil
        res = _dot(rows_by_residue(dil), w_ref[:, start:start + width])
        for r in range(dil):
            ref[0, r] = res[r * n:(r + 1) * n]
    if tails is None:
        return
    kv_blocks = 2 * GROUP_WIDTH // LANES
    for (ref, _, _), dil, keep, tail_ref in zip(outputs, dils, tails, tail_refs):
        n = tm // dil
        kept = min(keep, tm)

        @pl.when(pl.program_id(1) >= tiles - max(keep // tm, 1))
        def _(ref=ref, dil=dil, n=n, kept=kept, tail_ref=tail_ref):
            if dil == 1:
                kv = ref[0, 0, :, GROUP_WIDTH:]
            else:
                for r in range(dil):
                    for k in range(kv_blocks):
                        lanes = slice(GROUP_WIDTH + k * LANES, GROUP_WIDTH + (k + 1) * LANES)
                        tok_scr[k, pl.ds(r, n, stride=dil), :] = ref[0, r, :, lanes]
                kv = jnp.concatenate([tok_scr[k] for k in range(kv_blocks)], axis=1)
            tail_ref[0] = kv[tm - kept:, :].T


def _group_major_columns(w_in):
    parts = []
    for g in range(N_DIL_GROUPS):
        for base in (0, QK_WIDTH, 2 * QK_WIDTH):
            parts.append(w_in[:, base + g * GROUP_WIDTH:base + (g + 1) * GROUP_WIDTH])
    parts.append(w_in[:, U_START:])
    return jnp.concatenate(parts, axis=1)


def _in_proj(x2d, g, w_bf16, n_seq, tm, dils, tails=None):
    m = x2d.shape[0]
    seq = m // n_seq
    tiles = seq // tm
    row = lambda width: pl.BlockSpec((tm, width), lambda b, j: (b * tiles + j, 0))
    res_spec = lambda dil, width: pl.BlockSpec((1, dil, tm // dil, width), lambda b, j: (b, 0, j, 0))
    widths = (GROUP_QKV,) * N_DIL_GROUPS + (SSM_WIDTH,)
    out_specs = [*[res_spec(d, w) for d, w in zip(dils, widths)], row(2 * D_MODEL)]
    out_shape = [*[jax.ShapeDtypeStruct((n_seq, d, seq // d, w), F32) for d, w in zip(dils, widths)],
                 jax.ShapeDtypeStruct((m, 2 * D_MODEL), BF16)]
    scratch = [pltpu.VMEM((D_MODEL // LANES, tm, LANES), F32)]
    if tails is not None:
        for keep in tails:
            assert keep % tm == 0 or tm % keep == 0
            first = tiles - max(keep // tm, 1)
            out_specs.append(pl.BlockSpec((1, 2 * GROUP_WIDTH, min(keep, tm)),
                                          lambda b, j, first=first: (b, 0, jnp.maximum(j - first, 0))))
            out_shape.append(jax.ShapeDtypeStruct((n_seq, 2 * GROUP_WIDTH, keep), F32))
        scratch.append(pltpu.VMEM((2 * GROUP_WIDTH // LANES, tm, LANES), F32))
    return pl.pallas_call(
        functools.partial(_in_proj_kernel, tm=tm, dils=dils, tails=tails, tiles=tiles),
        grid=(n_seq, tiles),
        in_specs=[row(D_MODEL), _resident((1, D_MODEL)), _resident((D_MODEL, IN_WIDTH))],
        out_specs=out_specs,
        out_shape=out_shape,
        scratch_shapes=scratch,
        compiler_params=_compiler_params("arbitrary", "arbitrary"),
        name="in_proj",
    )(x2d, g, w_bf16)


def _bucket_starts():
    max_exact = N_BUCKETS // 2
    n = np.arange(max_exact, MAX_DISTANCE + 1)
    large = max_exact + (np.log(n.astype(np.float32) / np.float32(max_exact))
                         / np.float32(math.log(MAX_DISTANCE / max_exact))
                         * np.float32(N_BUCKETS - max_exact)).astype(np.int32)
    large = np.minimum(large, N_BUCKETS - 1)
    return [int(n[np.argmax(large >= k)]) for k in range(max_exact + 1, N_BUCKETS)]


def _rel_bucket(dist):
    max_exact = N_BUCKETS // 2
    n = jnp.maximum(dist, 0)
    large = max_exact + sum((n >= start).astype(jnp.int32) for start in _bucket_starts())
    return jnp.where(n < max_exact, n, large)


def _masked_bias(tab, strides, valid, dil):
    n_dist = QBLOCK
    bucket = _rel_bucket(jnp.clip(strides, 0, n_dist) * dil).reshape(1, -1)
    onehot = (bucket == jnp.arange(N_BUCKETS)[:, None]).astype(F32)
    bias = jnp.dot(tab.astype(F32).T, onehot, precision=lax.Precision.HIGHEST)
    bias = jnp.where(valid.reshape(1, -1), bias, NEG_INF)
    return bias.reshape((tab.shape[1],) + strides.shape)


def _prompt_bias(tab, dil):
    qi = jnp.arange(QBLOCK)[:, None]
    ki = jnp.arange(QBLOCK)[None, :]
    j_prev = qi + QBLOCK - ki
    j_cur = qi - ki
    bias = jnp.stack([_masked_bias(tab, j_prev, j_prev <= QBLOCK, dil),
                      _masked_bias(tab, j_cur, j_cur >= 0, dil)], axis=0)
    return bias.transpose(0, 2, 1, 3).reshape(2, QBLOCK, HEADS_PER_GROUP * QBLOCK)


def _sample_bias(tab, dil, t_new, n_cached):
    t = jnp.arange(t_new)[:, None]
    delta_buf = n_cached + t - jnp.arange(n_cached)[None, :]
    ok_buf = (delta_buf % dil == 0) & (delta_buf // dil <= QBLOCK)
    m = jnp.arange(QBLOCK)[None, :]
    delta_new = t - m
    ok_new = (m < t_new) & (delta_new >= 0) & (delta_new % dil == 0)
    b_buf = _masked_bias(tab, delta_buf // dil, ok_buf, dil)
    b_new = _masked_bias(tab, delta_new // dil, ok_new, dil)
    return (b_buf.reshape(HEADS_PER_GROUP * t_new, n_cached),
            b_new.reshape(HEADS_PER_GROUP * t_new, QBLOCK))


def _attn_prompt_kernel(q_ref, kc_ref, vc_ref, bias_ref, o_ref, lse_ref, kprev_scr, vprev_scr, *, nq, n_res):
    first_tile = pl.program_id(1) == 0

    @pl.when(first_tile)
    def _():
        kprev_scr[...] = jnp.zeros_like(kprev_scr)
        vprev_scr[...] = jnp.zeros_like(vprev_scr)

    stacked = (HEADS_PER_GROUP * QBLOCK, GROUP_WIDTH)
    own_head = (lax.broadcasted_iota(jnp.int32, stacked, 0) // QBLOCK
                == lax.broadcasted_iota(jnp.int32, stacked, 1) // HEAD_DIM)
    lane_head = lax.broadcasted_iota(jnp.int32, (QBLOCK, GROUP_WIDTH), 1) // HEAD_DIM

    def per_head(x):
        xb = x.astype(BF16)
        return jnp.where(own_head, jnp.concatenate([xb] * HEADS_PER_GROUP, axis=0), 0)

    def on_head_lanes(cols):
        out = jnp.broadcast_to(cols[-1], (QBLOCK, GROUP_WIDTH))
        for h in range(HEADS_PER_GROUP - 2, -1, -1):
            out = jnp.where(lane_head == h, cols[h], out)
        return out

    block_rows = [slice(i * QBLOCK, (i + 1) * QBLOCK) for i in range(nq)]

    def users(per_query_cur, per_query_prev, j):
        parts = ([per_query_cur[j - 1]] if j >= 1 else []) + ([per_query_prev[j]] if j < nq else [])
        return parts[0] if len(parts) == 1 else jnp.concatenate(parts, axis=0)

    for res in range(n_res):
        _attend_sequence(q_ref.at[res], kc_ref.at[res], vc_ref.at[res], bias_ref,
                         o_ref.at[res], lse_ref.at[res], kprev_scr.at[res], vprev_scr.at[res],
                         block_rows, per_head, on_head_lanes, users, first_tile)


def _attend_sequence(q_ref, kc_ref, vc_ref, bias_ref, o_ref, lse_ref, kprev_scr, vprev_scr,
                     block_rows, per_head, on_head_lanes, users, first_tile):
    nq = len(block_rows)
    qs = [(q_ref[rows, :] * QK_SCALE).astype(BF16) for rows in block_rows]
    k_blocks = [kprev_scr[...]] + [per_head(kc_ref[rows, :]) for rows in block_rows]
    v_blocks = [vprev_scr[...]] + [per_head(vc_ref[rows, :]) for rows in block_rows]
    kprev_scr[...] = k_blocks[-1]
    vprev_scr[...] = v_blocks[-1]

    s_prev, s_cur = [None] * nq, [None] * nq
    for j in range(nq + 1):
        s = _dot_nt(users(qs, qs, j), k_blocks[j])
        if j >= 1:
            s_cur[j - 1] = s[:QBLOCK] + bias_ref[1]
        if j < nq:
            s_prev[j] = s[-QBLOCK:] + bias_ref[0]
    s_prev[0] = jnp.where(first_tile, NEG_INF, s_prev[0])

    p_prev, p_cur, dens, lses = [], [], [], []
    for i in range(nq):
        pp_heads, pc_heads, den_heads, lse_heads = [], [], [], []
        for h in range(HEADS_PER_GROUP):
            keys = slice(h * QBLOCK, (h + 1) * QBLOCK)
            sp, sc = s_prev[i][:, keys], s_cur[i][:, keys]
            m = jnp.max(jnp.maximum(sp, sc), axis=-1, keepdims=True)
            pp, pc = jnp.exp(sp - m), jnp.exp(sc - m)
            den = jnp.sum(pp + pc, axis=-1, keepdims=True)
            pp_heads.append(pp.astype(BF16))
            pc_heads.append(pc.astype(BF16))
            den_heads.append(den)
            lse_heads.append(m + jnp.log(den))
        p_prev.append(jnp.concatenate(pp_heads, axis=1))
        p_cur.append(jnp.concatenate(pc_heads, axis=1))
        dens.append(on_head_lanes(den_heads))
        lses.append(on_head_lanes(lse_heads))

    o = [None] * nq
    for j in range(nq + 1):
        r = _dot(users(p_cur, p_prev, j), v_blocks[j])
        if j >= 1:
            o[j - 1] = o[j - 1] + r[:QBLOCK]
        if j < nq:
            o[j] = r[-QBLOCK:]
    for i, rows in enumerate(block_rows):
        o_ref[rows, :] = o[i] / dens[i]
        lse_ref[rows, :] = lses[i]


ATTN_QUERY_BLOCKS = 8
ATTN_STEP_BLOCKS = 16


def _attn_prompt(qkv, bias):
    batch, dil, length, _ = qkv.shape
    n_seq = batch * dil
    tq = min(ATTN_QUERY_BLOCKS * QBLOCK, length)
    nq = tq // QBLOCK
    n_res = math.gcd(n_seq, max(ATTN_STEP_BLOCKS // nq, 1))
    seqs = qkv.reshape(n_seq, length, GROUP_QKV)

    def cur(col):
        return pl.BlockSpec((n_res, tq, GROUP_WIDTH), lambda s, n: (s, n, col))

    stacked = pltpu.VMEM((n_res, HEADS_PER_GROUP * QBLOCK, GROUP_WIDTH), BF16)
    out_sds = jax.ShapeDtypeStruct((n_seq, length, GROUP_WIDTH), F32)
    o, lse = pl.pallas_call(
        functools.partial(_attn_prompt_kernel, nq=nq, n_res=n_res),
        grid=(n_seq // n_res, length // tq),
        in_specs=[cur(0), cur(1), cur(2), _resident((2, QBLOCK, HEADS_PER_GROUP * QBLOCK))],
        out_specs=[cur(0), cur(0)],
        out_shape=[out_sds, out_sds],
        scratch_shapes=[stacked, stacked],
        compiler_params=_compiler_params("arbitrary", "arbitrary"),
        name=f"attn_prompt_d{dil}",
    )(seqs, seqs, seqs, bias)
    out_shape = (batch, dil, length, GROUP_WIDTH)
    return o.reshape(out_shape), lse.reshape(out_shape)


def _attn_sample_kernel(q0_ref, q1_ref, q2_ref, c0_ref, c1_ref, c2_ref, tb0_ref, tb1_ref, tb2_ref, tn_ref,
                        o0_ref, l0_ref, o1_ref, l1_ref, o2_ref, l2_ref, kn_scr, vn_scr, *, t_new):
    n_rows = HEADS_PER_GROUP * t_new
    row_w = lax.broadcasted_iota(jnp.int32, (n_rows, GROUP_WIDTH), 0)
    lane_w = lax.broadcasted_iota(jnp.int32, (n_rows, GROUP_WIDTH), 1)
    own_head = (row_w // t_new) == (lane_w // HEAD_DIM)

    def fold_heads(x):
        x = jnp.where(own_head, x, 0.0)
        out = x[0:t_new]
        for h in range(1, HEADS_PER_GROUP):
            out = out + x[h * t_new:(h + 1) * t_new]
        return out

    caches = (c0_ref, c1_ref, c2_ref)
    cache_bias = (tb0_ref, tb1_ref, tb2_ref)
    outs = ((o0_ref, l0_ref), (o1_ref, l1_ref), (o2_ref, l2_ref))
    for g in range(N_DIL_GROUPS):
        qkv_ref = (q0_ref, q1_ref, q2_ref)[g]
        q = qkv_ref[0, :, 0:GROUP_WIDTH] * QK_SCALE
        q_rows = jnp.where(own_head, jnp.concatenate([q] * HEADS_PER_GROUP, axis=0), 0.0).astype(BF16)
        kn_scr[...] = jnp.zeros_like(kn_scr)
        vn_scr[...] = jnp.zeros_like(vn_scr)
        kn_scr[0:t_new, :] = qkv_ref[0, :, GROUP_WIDTH:2 * GROUP_WIDTH]
        vn_scr[0:t_new, :] = qkv_ref[0, :, 2 * GROUP_WIDTH:3 * GROUP_WIDTH]
        cache = caches[g]
        k_t = cache[0, 0:GROUP_WIDTH, :].astype(BF16)
        v_t = cache[0, GROUP_WIDTH:2 * GROUP_WIDTH, :].astype(BF16)
        s_buf = _dot(q_rows, k_t) + cache_bias[g][...]
        s_new = _dot_nt(q_rows, kn_scr[...].astype(BF16)) + tn_ref[g]
        m = jnp.maximum(jnp.max(s_buf, axis=-1, keepdims=True), jnp.max(s_new, axis=-1, keepdims=True))
        p_buf = jnp.exp(s_buf - m)
        p_new = jnp.exp(s_new - m)
        den = jnp.sum(p_buf, axis=-1, keepdims=True) + jnp.sum(p_new, axis=-1, keepdims=True)
        o = _dot_nt(p_buf.astype(BF16), v_t) + _dot(p_new.astype(BF16), vn_scr[...].astype(BF16))
        o_ref, l_ref = outs[g]
        o_ref[0] = fold_heads(o / den)
        l_ref[0] = fold_heads(jnp.broadcast_to(m + jnp.log(den), (n_rows, GROUP_WIDTH)))


def _attn_sample(qkvs, caches, tbs, tn):
    batch, t_new, _ = qkvs[0].shape
    n_rows = HEADS_PER_GROUP * t_new
    cache_specs = [pl.BlockSpec((1,) + c.shape[1:], lambda b: (b, 0, 0)) for c in caches]
    qkv_spec = pl.BlockSpec((1, t_new, GROUP_QKV), lambda b: (b, 0, 0))
    out_spec = pl.BlockSpec((1, t_new, GROUP_WIDTH), lambda b: (b, 0, 0))
    out_sds = jax.ShapeDtypeStruct((batch, t_new, GROUP_WIDTH), F32)
    return pl.pallas_call(
        functools.partial(_attn_sample_kernel, t_new=t_new),
        grid=(batch,),
        in_specs=[*[qkv_spec] * N_DIL_GROUPS, *cache_specs, *[_resident(t.shape) for t in tbs],
                  _resident((N_DIL_GROUPS, n_rows, QBLOCK))],
        out_specs=[out_spec] * (2 * N_DIL_GROUPS),
        out_shape=[out_sds] * (2 * N_DIL_GROUPS),
        scratch_shapes=[pltpu.VMEM((QBLOCK, GROUP_WIDTH), F32), pltpu.VMEM((QBLOCK, GROUP_WIDTH), F32)],
        compiler_params=_compiler_params("arbitrary"),
        name="attn_sample",
    )(*qkvs, *caches, *tbs, tn)


def _ssm_chunk_terms(chunk, log_dt, lam_re, lam_im, b_re, b_im, c_re, c_im):
    hi = lax.Precision.HIGHEST
    dt = jnp.exp(log_dt.astype(F32))[:, None]
    lr, li = lam_re.astype(F32), lam_im.astype(F32)
    mag = jnp.exp(lr * dt)
    ab_re, ab_im = mag * jnp.cos(li * dt), mag * jnp.sin(li * dt)
    g, n = lr.shape
    p = b_re.shape[-1]
    den = lr * lr + li * li
    nr, ni = ab_re - 1.0, ab_im
    coef_re = (nr * lr + ni * li) / den
    coef_im = (ni * lr - nr * li) / den
    br, bi = b_re.astype(F32), b_im.astype(F32)
    bb_re = coef_re[..., None] * br - coef_im[..., None] * bi
    bb_im = coef_re[..., None] * bi + coef_im[..., None] * br
    k = jnp.arange(chunk + 1, dtype=F32)[:, None, None]
    pw_mag = jnp.exp(k * (lr * dt)[None])
    pw_re, pw_im = pw_mag * jnp.cos(k * (li * dt)[None]), pw_mag * jnp.sin(k * (li * dt)[None])
    bt_re, bt_im = bb_re.transpose(0, 2, 1)[None], bb_im.transpose(0, 2, 1)[None]
    pk_re, pk_im = pw_re[:chunk, :, None, :], pw_im[:chunk, :, None, :]
    akb_re = pk_re * bt_re - pk_im * bt_im
    akb_im = pk_re * bt_im + pk_im * bt_re
    cr, ci = c_re.astype(F32)[None], c_im.astype(F32)[None]
    pe_re, pe_im = pw_re[1:, :, None, :], pw_im[1:, :, None, :]
    e_re = cr * pe_re - ci * pe_im
    e_im = cr * pe_im + ci * pe_re
    kern = (jnp.einsum('kgqn,gpn->kgqp', akb_re, cr[0], precision=hi)
            - jnp.einsum('kgqn,gpn->kgqp', akb_im, ci[0], precision=hi))
    rows = lambda x: x.reshape(chunk, g * p, x.shape[-1])
    return ((rows(akb_re), rows(akb_im)), (rows(e_re), rows(e_im)), rows(kern), (pw_re, pw_im))


SSM_BLOCK_GROUPS = 8
SSM_BLOCK_CH = SSM_BLOCK_GROUPS * SSM_GROUP
SSM_BLOCK_STATE = SSM_BLOCK_GROUPS * SSM_STATE
N_SSM_BLOCKS = N_SSM_GROUPS // SSM_BLOCK_GROUPS


def _ssm_operator_kernel(akbre_ref, akbim_ref, ere_ref, eim_ref, kern_ref,
                         wre_ref, wim_ref, m_ref, etre_ref, etim_ref, *, chunk):
    gb, p, n = SSM_BLOCK_GROUPS, SSM_GROUP, SSM_STATE

    def copies(width, count):
        src = lax.broadcasted_iota(jnp.int32, (width, count * width), 0)
        dst = lax.broadcasted_iota(jnp.int32, (width, count * width), 1)
        return jnp.where(src == dst % width, 1.0, 0.0).astype(BF16)

    def own_group(width):
        row = lax.broadcasted_iota(jnp.int32, (SSM_BLOCK_CH, gb * width), 0)
        col = lax.broadcasted_iota(jnp.int32, (SSM_BLOCK_CH, gb * width), 1)
        return row // p == col // width

    to_states, own_states = copies(n, gb), own_group(n)

    def over_states(piece):
        return jnp.where(own_states, _dot(piece.astype(BF16), to_states), 0.0).astype(BF16)

    for s in range(chunk):
        rows = slice(s * SSM_BLOCK_CH, (s + 1) * SSM_BLOCK_CH)
        wre_ref[0, rows, :] = over_states(akbre_ref[chunk - 1 - s, 0])
        wim_ref[0, rows, :] = over_states(akbim_ref[chunk - 1 - s, 0])
        etre_ref[0, rows, :] = over_states(ere_ref[s, 0])
        etim_ref[0, rows, :] = over_states(-eim_ref[s, 0])
    to_channels, own_channels = copies(p, gb), own_group(p)
    lags = [jnp.where(own_channels, _dot(kern_ref[k, 0].astype(BF16), to_channels), 0.0).astype(BF16)
            for k in range(chunk)]
    zero = jnp.zeros((SSM_BLOCK_CH, SSM_BLOCK_CH), BF16)
    for s in range(chunk):
        m_ref[0, s * SSM_BLOCK_CH:(s + 1) * SSM_BLOCK_CH, :] = jnp.concatenate(
            [zero] * s + lags[:chunk - s], axis=1)


def _ssm_block_operators(chunk, terms):
    (akb_re, akb_im), (e_re, e_im), kern, (pw_re, pw_im) = terms
    assert chunk <= kern.shape[0]
    a_re, a_im = pw_re[chunk], pw_im[chunk]
    nb, gb, p, n = N_SSM_BLOCKS, SSM_BLOCK_GROUPS, SSM_GROUP, SSM_STATE
    pieces = [x[:chunk].reshape(chunk, nb, SSM_BLOCK_CH, x.shape[-1])
              for x in (akb_re, akb_im, e_re, e_im, kern)]
    piece_spec = lambda x: pl.BlockSpec((chunk, 1) + x.shape[2:], lambda i: (0, i, 0, 0))
    out_spec = lambda shape: pl.BlockSpec((1,) + shape[1:], lambda i: (i, 0, 0))
    x_width = chunk * SSM_BLOCK_CH
    out_shapes = [(nb, x_width, gb * n), (nb, x_width, gb * n), (nb, x_width, x_width),
                  (nb, x_width, gb * n), (nb, x_width, gb * n)]
    ops = pl.pallas_call(
        functools.partial(_ssm_operator_kernel, chunk=chunk),
        grid=(nb,),
        in_specs=[piece_spec(x) for x in pieces],
        out_specs=[out_spec(s) for s in out_shapes],
        out_shape=[jax.ShapeDtypeStruct(s, BF16) for s in out_shapes],
        compiler_params=_compiler_params("arbitrary"),
        name="ssm_operators",
    )(*pieces)
    return ops, a_re.reshape(nb, 1, gb * n), a_im.reshape(nb, 1, gb * n)


def _shorter_chunk_operators(ops, terms, chunk):
    w_re, w_im, m, et_re, et_im = ops
    pw_re, pw_im = terms[3]
    width = chunk * SSM_BLOCK_CH
    full = m.shape[1]
    assert width <= full
    nb, states = N_SSM_BLOCKS, SSM_BLOCK_STATE
    ops = [w_re[:, full - width:], w_im[:, full - width:], m[:, :width, :width], et_re[:, :width], et_im[:, :width]]
    return ops, pw_re[chunk].reshape(nb, 1, states), pw_im[chunk].reshape(nb, 1, states)


SSM_X_WIDTH = SSM_CHUNK * SSM_BLOCK_CH


def _ssm_prompt_kernel(u_ref, wre_ref, wim_ref, m_ref, etre_ref, etim_ref, are_ref, aim_ref, d_ref,
                       y_ref, hre_ref, him_ref, x_scr, sre_scr, sim_scr, *, batch, chunks):
    phase, b = pl.program_id(1), pl.program_id(2)
    for t in range(SSM_CHUNK):
        x_scr[:, t * SSM_BLOCK_CH:(t + 1) * SSM_BLOCK_CH] = u_ref[0, t].astype(BF16)
    rows = pl.ds(b, chunks, stride=batch)
    state_blocks = SSM_BLOCK_STATE // LANES

    @pl.when(phase == 0)
    def _():
        x = x_scr[...]
        g_re, g_im = _dot(x, wre_ref[0]), _dot(x, wim_ref[0])
        for k in range(state_blocks):
            sre_scr[k, rows, :] = g_re[:, k * LANES:(k + 1) * LANES]
            sim_scr[k, rows, :] = g_im[:, k * LANES:(k + 1) * LANES]

    @pl.when((phase == 1) & (b == 0))
    def _():
        same_lanes = lambda ref, k: ref[0][:, k * LANES:(k + 1) * LANES]

        def step(c, carry):
            same_chunk = pl.ds(c * batch, batch)
            out = []
            for k, (h_re, h_im) in enumerate(carry):
                a_re, a_im = same_lanes(are_ref, k), same_lanes(aim_ref, k)
                g_re, g_im = sre_scr[k, same_chunk, :], sim_scr[k, same_chunk, :]
                sre_scr[k, same_chunk, :] = h_re
                sim_scr[k, same_chunk, :] = h_im
                out.append((a_re * h_re - a_im * h_im + g_re, a_re * h_im + a_im * h_re + g_im))
            return tuple(out)

        zero = jnp.zeros((batch, LANES), F32)
        final = lax.fori_loop(0, chunks, step, ((zero, zero),) * state_blocks)
        for k, (h_re, h_im) in enumerate(final):
            hre_ref[0, :, k * LANES:(k + 1) * LANES] = h_re
            him_ref[0, :, k * LANES:(k + 1) * LANES] = h_im

    @pl.when(phase == 1)
    def _():
        h_re = jnp.concatenate([sre_scr[k, rows, :] for k in range(state_blocks)], axis=1).astype(BF16)
        h_im = jnp.concatenate([sim_scr[k, rows, :] for k in range(state_blocks)], axis=1).astype(BF16)
        pair_w = 2 * SSM_BLOCK_CH
        for j in range(SSM_CHUNK // 2):
            cols = slice(j * pair_w, (j + 1) * pair_w)
            k_in = (j + 1) * pair_w
            yj = (_dot(x_scr[:, :k_in], m_ref[0, :k_in, cols])
                  + _dot_nt(h_re, etre_ref[0, cols, :]) + _dot_nt(h_im, etim_ref[0, cols, :]))
            for i in range(2):
                t = 2 * j + i
                y_ref[pl.ds(t, chunks, stride=SSM_CHUNK), :] = (
                    yj[:, i * SSM_BLOCK_CH:(i + 1) * SSM_BLOCK_CH] + d_ref[0] * u_ref[0, t])


def _ssm_prompt(u, ops, a_re, a_im, d, batch, seq):
    chunks = seq // SSM_CHUNK
    d = d.reshape(N_SSM_BLOCKS, 1, SSM_BLOCK_CH)
    op_spec = lambda arr: pl.BlockSpec((1,) + arr.shape[1:], lambda g, ph, b: (g, 0, 0),
                                       pipeline_mode=pl.Buffered(1))
    state_spec = pl.BlockSpec((1, batch, SSM_BLOCK_STATE), lambda g, ph, b: (g, 0, 0))
    state_sds = jax.ShapeDtypeStruct((N_SSM_BLOCKS, batch, SSM_BLOCK_STATE), F32)
    y, h_re, h_im = pl.pallas_call(
        functools.partial(_ssm_prompt_kernel, batch=batch, chunks=chunks),
        grid=(N_SSM_BLOCKS, 2, batch),
        in_specs=[pl.BlockSpec((1, SSM_CHUNK, chunks, SSM_BLOCK_CH), lambda g, ph, b: (b, 0, 0, g)),
                  *[op_spec(o) for o in ops], op_spec(a_re), op_spec(a_im), op_spec(d)],
        out_specs=[pl.BlockSpec((seq, SSM_BLOCK_CH), lambda g, ph, b: (b * ph, g)), state_spec, state_spec],
        out_shape=[jax.ShapeDtypeStruct((batch * seq, SSM_WIDTH), F32), state_sds, state_sds],
        scratch_shapes=[pltpu.VMEM((chunks, SSM_X_WIDTH), BF16),
                        pltpu.VMEM((SSM_BLOCK_STATE // LANES, batch * chunks, LANES), F32),
                        pltpu.VMEM((SSM_BLOCK_STATE // LANES, batch * chunks, LANES), F32)],
        compiler_params=_compiler_params("arbitrary", "arbitrary", "arbitrary"),
        name="ssm_prompt",
    )(u, *ops, a_re, a_im, d)

    def by_sequence(h):
        h = h.reshape(N_SSM_BLOCKS, batch, SSM_BLOCK_GROUPS, SSM_STATE).transpose(1, 0, 2, 3)
        return h.reshape(batch, N_SSM_GROUPS, SSM_STATE)

    return y, by_sequence(h_re), by_sequence(h_im)


def _ssm_sample_kernel(u_ref, wre_ref, wim_ref, m_ref, etre_ref, etim_ref, are_ref, aim_ref, d_ref,
                       h0re_ref, h0im_ref, y_ref, hre_ref, him_ref, *, t_new, batch):
    for nb in range(N_SSM_BLOCKS):
        ch = slice(nb * SSM_BLOCK_CH, (nb + 1) * SSM_BLOCK_CH)
        st = slice(nb * SSM_BLOCK_STATE, (nb + 1) * SSM_BLOCK_STATE)
        x = jnp.concatenate([u_ref[t * batch:(t + 1) * batch, ch] for t in range(t_new)], axis=1).astype(BF16)
        h_re, h_im = h0re_ref[:, st], h0im_ref[:, st]
        a_re, a_im = are_ref[nb], aim_ref[nb]
        hre_ref[:, st] = a_re * h_re - a_im * h_im + _dot(x, wre_ref[nb])
        him_ref[:, st] = a_re * h_im + a_im * h_re + _dot(x, wim_ref[nb])
        y = (_dot(x, m_ref[nb]) + _dot_nt(h_re.astype(BF16), etre_ref[nb])
             + _dot_nt(h_im.astype(BF16), etim_ref[nb]))
        for t in range(t_new):
            rows = slice(t * batch, (t + 1) * batch)
            y_ref[rows, ch] = y[:, t * SSM_BLOCK_CH:(t + 1) * SSM_BLOCK_CH] + d_ref[:, ch] * u_ref[rows, ch]


def _ssm_sample(u, ops, a_re, a_im, d, h0_re, h0_im, t_new):
    batch = u.shape[0] // t_new
    args = (u, *ops, a_re, a_im, d, h0_re, h0_im)
    state_sds = jax.ShapeDtypeStruct((batch, N_SSM_GROUPS * SSM_STATE), F32)
    out_shape = [jax.ShapeDtypeStruct(u.shape, F32), state_sds, state_sds]
    whole = lambda shape: pl.BlockSpec(shape, lambda i, nd=len(shape): (0,) * nd)
    return pl.pallas_call(
        functools.partial(_ssm_sample_kernel, t_new=t_new, batch=batch),
        grid=(1,),
        in_specs=[whole(a.shape) for a in args],
        out_specs=[whole(s.shape) for s in out_shape],
        out_shape=out_shape,
        compiler_params=_compiler_params("arbitrary"),
        name="ssm_sample",
    )(*args)


def _gelu_tanh(x):
    return 0.5 * x * (1.0 + jnp.tanh(math.sqrt(2.0 / math.pi) * (x + 0.044715 * (x * x * x))))


def _merge_tile(o0_ref, l0_ref, o1_ref, l1_ref, o2_ref, l2_ref, ys_ref, gate_ref, x_ref,
                wglu_ref, bglu_ref, wba_ref, wbs_ref, wout_ref, order_scr, tm, dils):
    def row_order(ref, dil, slot):
        if dil == 1:
            return ref[0, 0]
        n = tm // dil
        halves = GROUP_WIDTH // LANES
        for r in range(dil):
            for k in range(halves):
                order_scr[slot * halves + k, pl.ds(r, n, stride=dil), :] = ref[0, r, :, k * LANES:(k + 1) * LANES]
        return jnp.concatenate([order_scr[slot * halves + k] for k in range(halves)], axis=1)

    parts = [row_order(ref, dils[i // 2], i) for i, ref in
             enumerate((o0_ref, l0_ref, o1_ref, l1_ref, o2_ref, l2_ref))]
    o0, l0, o1, l1, o2, l2 = parts
    mx = jnp.maximum(jnp.maximum(l0, l1), l2)
    e0, e1, e2 = jnp.exp(l0 - mx), jnp.exp(l1 - mx), jnp.exp(l2 - mx)
    attn = (e0 * o0 + e1 * o1 + e2 * o2) / (e0 + e1 + e2)
    branch_a = _dot(attn.astype(BF16), wba_ref[...])
    y = _gelu_tanh(ys_ref[...])
    y = y * _sigmoid(_dot(y.astype(BF16), wglu_ref[...]) + bglu_ref[...])
    branch_s = _dot(y.astype(BF16), wbs_ref[...])
    mix = (gate_ref[:, 0:D_MODEL].astype(F32) * branch_a
           + gate_ref[:, D_MODEL:2 * D_MODEL].astype(F32) * branch_s)
    return x_ref[...] + _dot(mix.astype(BF16), wout_ref[...])


MXU_WIDTH = 256
FF_SPLIT = -(-D_FF // (2 * MXU_WIDTH)) * MXU_WIDTH
FF_CHUNKS = ((0, FF_SPLIT), (FF_SPLIT, D_FF))
N_MERGE_ROW_INPUTS = 2 * N_DIL_GROUPS + 3
N_MERGE_WEIGHTS = 5


def _merge_ffn_kernel(*refs, tm, shift, pad, dils):
    n_merge = N_MERGE_ROW_INPUTS + N_MERGE_WEIGHTS
    merge_refs, rest = refs[:n_merge], refs[n_merge:]
    carry_ref, g2_ref, wup_ref, cw_ref, cb_ref, wdn_ref, gf_ref, y_ref, state_ref, order_scr, a_scr = rest
    hist = 2 * shift

    @pl.when(pl.program_id(1) == 0)
    def _():
        a_scr[pad - hist:pad, :] = carry_ref[0]

    xf = _merge_tile(*merge_refs, order_scr, tm, dils)
    xn = _rmsnorm(xf, g2_ref[...]).astype(BF16)
    acc = jnp.zeros((tm, D_MODEL), F32)
    for c0, c1 in FF_CHUNKS:
        cols = slice(c0, c1)
        a = _dot(xn, wup_ref[:, cols])
        val = _dot(xn, wup_ref[:, D_FF + c0:D_FF + c1])
        a_scr[pad:pad + tm, cols] = a
        a_m1 = a_scr[pad - shift:pad - shift + tm, cols]
        a_m2 = a_scr[pad - hist:pad - hist + tm, cols]
        conv = cb_ref[:, cols] + cw_ref[0:1, cols] * a_m2
        conv = conv + cw_ref[1:2, cols] * a_m1
        conv = conv + cw_ref[2:3, cols] * a
        act = conv * _sigmoid(conv) * val
        acc = acc + _dot(act.astype(BF16), wdn_ref[cols, :])
    tail = a_scr[pad + tm - hist:pad + tm, :]
    a_scr[pad - hist:pad, :] = tail
    state_ref[0] = tail
    y_ref[...] = _rmsnorm(xf + acc, gf_ref[...])


def _merge_ffn(attn_parts, ys, gates, x2d, merge_weights, carry, ffn_weights, n_seq, tm, shift):
    m = x2d.shape[0]
    tiles = m // n_seq // tm
    hist = 2 * shift
    pad = -(-hist // SUBLANES) * SUBLANES
    dils = tuple(p.shape[1] for p in attn_parts[::2])
    row = lambda width: pl.BlockSpec((tm, width), lambda b, j: (b * tiles + j, 0))
    res_spec = lambda dil: pl.BlockSpec((1, dil, tm // dil, GROUP_WIDTH), lambda b, j: (b, 0, j, 0))
    state_spec = pl.BlockSpec((1, hist, D_FF), lambda b, j: (b, 0, 0))
    assert len(merge_weights) == N_MERGE_WEIGHTS
    return pl.pallas_call(
        functools.partial(_merge_ffn_kernel, tm=tm, shift=shift, pad=pad, dils=dils),
        grid=(n_seq, tiles),
        in_specs=[*[res_spec(p.shape[1]) for p in attn_parts],
                  row(SSM_WIDTH), row(2 * D_MODEL), row(D_MODEL),
                  *[_resident(w.shape) for w in merge_weights],
                  state_spec, *[_resident(w.shape) for w in ffn_weights]],
        out_specs=[row(D_MODEL), state_spec],
        out_shape=[jax.ShapeDtypeStruct((m, D_MODEL), F32),
                   jax.ShapeDtypeStruct((n_seq, hist, D_FF), F32)],
        scratch_shapes=[pltpu.VMEM((len(attn_parts) * GROUP_WIDTH // LANES, tm, LANES), F32),
                        pltpu.VMEM((pad + tm, D_FF), F32)],
        compiler_params=_compiler_params("arbitrary", "arbitrary"),
        name="merge_ffn",
    )(*attn_parts, ys, gates, x2d, *merge_weights, carry, *ffn_weights)


def _kv_rows(qkv, keep):
    batch, dil, length, _ = qkv.shape
    n = keep // dil
    rows = qkv[:, :, length - n:, GROUP_WIDTH:]
    cols = rows.transpose(0, 3, 2, 1).reshape(batch, 2, HEADS_PER_GROUP, HEAD_DIM, keep)
    return cols.transpose(0, 4, 1, 2, 3)


def _prompt_layer(x, rel_bias, lw):
    batch, seq, _ = x.shape
    x2d = x.reshape(batch * seq, D_MODEL)
    dils = tuple(dil for _, dil in DIL_PATTERNS) + (SSM_CHUNK,)
    keeps = tuple(min(window, seq) for window, _ in DIL_PATTERNS)
    *qkvs, u, gates, kv0, kv1, kv2 = _in_proj(x2d, lw['norm1_g'], lw['w_in'], batch, IN_PROJ_TILE, dils, keeps)

    attn_parts, kv_new = [], []
    for g, (window, dil) in enumerate(DIL_PATTERNS):
        tab = rel_bias[:, g * HEADS_PER_GROUP:(g + 1) * HEADS_PER_GROUP]
        attn_parts.extend(_attn_prompt(qkvs[g], _prompt_bias(tab, dil)))
        tail = (kv0, kv1, kv2)[g].reshape(batch, 2, HEADS_PER_GROUP, HEAD_DIM, keeps[g])
        kv_new.append(tail.transpose(0, 4, 1, 2, 3))

    ops, a_re, a_im = lw['ssm_ops']
    ys, h_re, h_im = _ssm_prompt(u, ops, a_re, a_im, lw['ssm_d'], batch, seq)

    carry = jnp.zeros((batch, CONV_W - 1, D_FF), F32)
    y, conv_state = _merge_ffn(attn_parts, ys, gates, x2d, lw['merge'], carry, lw['ffn'],
                               n_seq=batch, tm=MERGE_FFN_TILE, shift=1)
    return y.reshape(batch, seq, D_MODEL), (*kv_new, h_re, h_im, conv_state)


def _sample_layer(x, caches, h0_re, h0_im, conv_buf, rel_bias, lw):
    batch, t_new, _ = x.shape
    m = batch * t_new
    x2d = x.transpose(1, 0, 2).reshape(m, D_MODEL)
    *qkvs, u, gates = _in_proj(x2d, lw['norm1_g'], lw['w_in'], 1, m, (1,) * (N_DIL_GROUPS + 1))
    u = u.reshape(m, SSM_WIDTH)
    qkvs_bt = [q.reshape(t_new, batch, GROUP_QKV).transpose(1, 0, 2) for q in qkvs]

    tbs, tns, views = [], [], []
    for g, (window, dil) in enumerate(DIL_PATTERNS):
        tab = rel_bias[:, g * HEADS_PER_GROUP:(g + 1) * HEADS_PER_GROUP]
        n_cached = caches[g].shape[1]
        tb, tn = _sample_bias(tab, dil, t_new, n_cached)
        tbs.append(tb)
        tns.append(tn)
        views.append(caches[g].transpose(0, 2, 3, 4, 1).reshape(batch, 2 * GROUP_WIDTH, n_cached))
    parts = _attn_sample(qkvs_bt, views, tbs, jnp.stack(tns))
    attn_parts = [p.transpose(1, 0, 2).reshape(1, 1, m, GROUP_WIDTH) for p in parts]
    kv_new = [_kv_rows(q.reshape(batch, 1, t_new, GROUP_QKV), t_new) for q in qkvs_bt]

    ops, a_re, a_im = _shorter_chunk_operators(lw['ssm_ops'][0], lw['ssm_terms'], t_new)
    ys, h_re, h_im = _ssm_sample(u, ops, a_re, a_im, lw['ssm_d'],
                                 h0_re.reshape(batch, -1), h0_im.reshape(batch, -1), t_new)

    carry = conv_buf.transpose(1, 0, 2).reshape(1, (CONV_W - 1) * batch, D_FF)
    y, conv_state = _merge_ffn(attn_parts, ys, gates, x2d, lw['merge'], carry, lw['ffn'],
                               n_seq=1, tm=m, shift=batch)
    y = y.reshape(t_new, batch, D_MODEL).transpose(1, 0, 2)
    conv_state = conv_state.reshape(CONV_W - 1, batch, D_FF).transpose(1, 0, 2)
    state_shape = (batch, N_SSM_GROUPS, SSM_STATE)
    return y, (*kv_new, h_re.reshape(state_shape), h_im.reshape(state_shape), conv_state)


IN_PROJ_TILE = 512
MERGE_FFN_TILE = 512


def kernel(x_prompt, x_sample, cache_kv_w128, cache_kv_w512, cache_kv_w2048, state_ssm_re, state_ssm_im, state_ffn_conv, rel_bias, norm1_g, w_in, ssm_log_dt, ssm_lambda_re, ssm_lambda_im, ssm_b_re, ssm_b_im, ssm_c_re, ssm_c_im, ssm_d, w_glu, b_glu, w_branch_attn, w_branch_ssm, w_out, norm2_g, w_up, conv_w, conv_b, w_down, norm_f_g):
    depth = w_in.shape[0]
    hp, hs = x_prompt, x_sample
    st_p, st_s = [], []
    gf = norm_f_g.reshape(1, D_MODEL)
    for l in range(depth):
        last = l == depth - 1
        assert x_sample.shape[1] <= SSM_CHUNK, "the sample chunk operators are cut out of the prompt's"
        ssm_terms = _ssm_chunk_terms(SSM_CHUNK, ssm_log_dt[l], ssm_lambda_re[l], ssm_lambda_im[l],
                                     ssm_b_re[l], ssm_b_im[l], ssm_c_re[l], ssm_c_im[l])
        lw = {
            'norm1_g': norm1_g[l].reshape(1, D_MODEL),
            'w_in': _group_major_columns(w_in[l]).astype(BF16),
            'ssm_terms': ssm_terms,
            'ssm_ops': _ssm_block_operators(SSM_CHUNK, ssm_terms),
            'ssm_d': ssm_d[l].reshape(1, SSM_WIDTH),
            'merge': (w_glu[l].astype(BF16), b_glu[l].reshape(1, SSM_WIDTH),
                      w_branch_attn[l].astype(BF16), w_branch_ssm[l].astype(BF16), w_out[l].astype(BF16)),
            'ffn': (norm2_g[l].reshape(1, D_MODEL), w_up[l].astype(BF16), conv_w[l],
                    conv_b[l].reshape(1, D_FF), w_down[l].astype(BF16), gf),
        }
        assert last, "the final RMSNorm is fused into the last layer's ffn kernel"
        hp, sp = _prompt_layer(hp, rel_bias, lw)
        hs, ss = _sample_layer(hs, (cache_kv_w128[l], cache_kv_w512[l], cache_kv_w2048[l]),
                               state_ssm_re[l], state_ssm_im[l], state_ffn_conv[l], rel_bias, lw)
        st_p.append(sp)
        st_s.append(ss)
    stack = lambda states, i: jnp.stack([st[i] for st in states], axis=0)
    return (hp, hs, *[stack(st_p, i) for i in range(6)], *[stack(st_s, i) for i in range(6)])
```

```python
import functools
import math

import jax
import jax.numpy as jnp
import numpy as np
from jax import lax
from jax.experimental import pallas as pl
from jax.experimental.pallas import tpu as pltpu

F32 = jnp.float32
BF16 = jnp.bfloat16

D_MODEL = 1024
HEAD_DIM = 64
HEADS_PER_GROUP = 4
DIL_PATTERNS = ((128, 1), (512, 4), (2048, 16))
N_DIL_GROUPS = len(DIL_PATTERNS)
GROUP_WIDTH = HEADS_PER_GROUP * HEAD_DIM
QK_WIDTH = N_DIL_GROUPS * GROUP_WIDTH
QKV_WIDTH = 3 * QK_WIDTH
GROUP_QKV = 3 * GROUP_WIDTH
QBLOCK = 128
SSM_GROUP = 16
SSM_STATE = 64
SSM_WIDTH = D_MODEL // 2
N_SSM_GROUPS = SSM_WIDTH // SSM_GROUP
SSM_CHUNK = 16
D_FF = 2816
CONV_W = 3
N_BUCKETS = 32
MAX_DISTANCE = 2048
NORM_EPS = 1e-6
NEG_INF = -1e30
U_START = QKV_WIDTH
GATE_START = U_START + SSM_WIDTH
IN_WIDTH = GATE_START + 2 * D_MODEL
QK_SCALE = HEAD_DIM ** -0.5

VMEM_LIMIT_BYTES = 60 * 1024 * 1024
SUBLANES = 8
LANES = 128


def _compiler_params(*semantics):
    return pltpu.CompilerParams(dimension_semantics=semantics, vmem_limit_bytes=VMEM_LIMIT_BYTES)


def _resident(shape):
    nd = len(shape)
    return pl.BlockSpec(shape, lambda *_: (0,) * nd, pipeline_mode=pl.Buffered(1))


def _rmsnorm(xf, g):
    y = xf * lax.rsqrt(jnp.mean(xf * xf, axis=-1, keepdims=True) + NORM_EPS)
    return y * g


def _sigmoid(x):
    return 1.0 / (1.0 + jnp.exp(-x))


def _dot(a, b):
    return jnp.dot(a, b, preferred_element_type=F32)


def _dot_nt(a, b):
    return lax.dot_general(a, b, (((1,), (1,)), ((), ())), preferred_element_type=F32)


def _in_proj_kernel(x_ref, g_ref, w_ref, qkv0_ref, qkv1_ref, qkv2_ref, u_ref, gate_ref, *rest,
                    tm, dils, tails, tiles):
    if tails is None:
        tail_refs, (xn_scr,) = (), rest
    else:
        tail_refs, (xn_scr, tok_scr) = rest[:N_DIL_GROUPS], rest[N_DIL_GROUPS:]
    xf = _rmsnorm(x_ref[...], g_ref[...])
    n_lane_blocks = D_MODEL // LANES
    if any(dil > 1 for dil in dils):
        for k in range(n_lane_blocks):
            xn_scr[k] = xf[:, k * LANES:(k + 1) * LANES]
    by_residue = {1: xf.astype(BF16)}

    def rows_by_residue(dil):
        if dil not in by_residue:
            n = tm // dil
            xr = jnp.concatenate(
                [jnp.concatenate([xn_scr[k, pl.ds(r, n, stride=dil), :] for k in range(n_lane_blocks)], axis=1)
                 for r in range(dil)], axis=0)
            by_residue[dil] = xr.astype(BF16)
        return by_residue[dil]

    outputs = [(ref, g * GROUP_QKV, GROUP_QKV) for g, ref in enumerate((qkv0_ref, qkv1_ref, qkv2_ref))]
    outputs.append((u_ref, U_START, SSM_WIDTH))
    for c0 in range(0, 2 * D_MODEL, D_MODEL):
        logits = _dot(by_residue[1], w_ref[:, GATE_START + c0:GATE_START + c0 + D_MODEL])
        gate_ref[:, c0:c0 + D_MODEL] = _sigmoid(logits).astype(BF16)
    for (ref, start, width), dil in zip(outputs, dils):
        n = tm // dil
        res = _dot(rows_by_residue(dil), w_ref[:, start:start + width])
        for r in range(dil):
            ref[0, r] = res[r * n:(r + 1) * n]
    if tails is None:
        return
    kv_blocks = 2 * GROUP_WIDTH // LANES
    for (ref, _, _), dil, keep, tail_ref in zip(outputs, dils, tails, tail_refs):
        n = tm // dil
        kept = min(keep, tm)

        @pl.when(pl.program_id(1) >= tiles - max(keep // tm, 1))
        def _(ref=ref, dil=dil, n=n, kept=kept, tail_ref=tail_ref):
            if dil == 1:
                kv = ref[0, 0, :, GROUP_WIDTH:]
            else:
                for r in range(dil):
                    for k in range(kv_blocks):
                        lanes = slice(GROUP_WIDTH + k * LANES, GROUP_WIDTH + (k + 1) * LANES)
                        tok_scr[k, pl.ds(r, n, stride=dil), :] = ref[0, r, :, lanes]
                kv = jnp.concatenate([tok_scr[k] for k in range(kv_blocks)], axis=1)
            tail_ref[0] = kv[tm - kept:, :].T


def _group_major_columns(w_in):
    parts = []
    for g in range(N_DIL_GROUPS):
        for base in (0, QK_WIDTH, 2 * QK_WIDTH):
            parts.append(w_in[:, base + g * GROUP_WIDTH:base + (g + 1) * GROUP_WIDTH])
    parts.append(w_in[:, U_START:])
    return jnp.concatenate(parts, axis=1)


def _in_proj(x2d, g, w_bf16, n_seq, tm, dils, tails=None):
    m = x2d.shape[0]
    seq = m // n_seq
    tiles = seq // tm
    row = lambda width: pl.BlockSpec((tm, width), lambda b, j: (b * tiles + j, 0))
    res_spec = lambda dil, width: pl.BlockSpec((1, dil, tm // dil, width), lambda b, j: (b, 0, j, 0))
    widths = (GROUP_QKV,) * N_DIL_GROUPS + (SSM_WIDTH,)
    out_specs = [*[res_spec(d, w) for d, w in zip(dils, widths)], row(2 * D_MODEL)]
    out_shape = [*[jax.ShapeDtypeStruct((n_seq, d, seq // d, w), F32) for d, w in zip(dils, widths)],
                 jax.ShapeDtypeStruct((m, 2 * D_MODEL), BF16)]
    scratch = [pltpu.VMEM((D_MODEL // LANES, tm, LANES), F32)]
    if tails is not None:
        for keep in tails:
            assert keep % tm == 0 or tm % keep == 0
            first = tiles - max(keep // tm, 1)
            out_specs.append(pl.BlockSpec((1, 2 * GROUP_WIDTH, min(keep, tm)),
                                          lambda b, j, first=first: (b, 0, jnp.maximum(j - first, 0))))
            out_shape.append(jax.ShapeDtypeStruct((n_seq, 2 * GROUP_WIDTH, keep), F32))
        scratch.append(pltpu.VMEM((2 * GROUP_WIDTH // LANES, tm, LANES), F32))
    return pl.pallas_call(
        functools.partial(_in_proj_kernel, tm=tm, dils=dils, tails=tails, tiles=tiles),
        grid=(n_seq, tiles),
        in_specs=[row(D_MODEL), _resident((1, D_MODEL)), _resident((D_MODEL, IN_WIDTH))],
        out_specs=out_specs,
        out_shape=out_shape,
        scratch_shapes=scratch,
        compiler_params=_compiler_params("arbitrary", "arbitrary"),
        name="in_proj",
    )(x2d, g, w_bf16)


def _bucket_starts():
    max_exact = N_BUCKETS // 2
    n = np.arange(max_exact, MAX_DISTANCE + 1)
    large = max_exact + (np.log(n.astype(np.float32) / np.float32(max_exact))
                         / np.float32(math.log(MAX_DISTANCE / max_exact))
                         * np.float32(N_BUCKETS - max_exact)).astype(np.int32)
    large = np.minimum(large, N_BUCKETS - 1)
    return [int(n[np.argmax(large >= k)]) for k in range(max_exact + 1, N_BUCKETS)]


def _rel_bucket(dist):
    max_exact = N_BUCKETS // 2
    n = jnp.maximum(dist, 0)
    large = max_exact + sum((n >= start).astype(jnp.int32) for start in _bucket_starts())
    return jnp.where(n < max_exact, n, large)


def _masked_bias(tab, strides, valid, dil):
    n_dist = QBLOCK
    bucket = _rel_bucket(jnp.clip(strides, 0, n_dist) * dil).reshape(1, -1)
    onehot = (bucket == jnp.arange(N_BUCKETS)[:, None]).astype(F32)
    bias = jnp.dot(tab.astype(F32).T, onehot, precision=lax.Precision.HIGHEST)
    bias = jnp.where(valid.reshape(1, -1), bias, NEG_INF)
    return bias.reshape((tab.shape[1],) + strides.shape)


def _prompt_bias(tab, dil):
    qi = jnp.arange(QBLOCK)[:, None]
    ki = jnp.arange(QBLOCK)[None, :]
    j_prev = qi + QBLOCK - ki
    j_cur = qi - ki
    bias = jnp.stack([_masked_bias(tab, j_prev, j_prev <= QBLOCK, dil),
                      _masked_bias(tab, j_cur, j_cur >= 0, dil)], axis=0)
    return bias.transpose(0, 2, 1, 3).reshape(2, QBLOCK, HEADS_PER_GROUP * QBLOCK)


def _sample_bias(tab, dil, t_new, n_cached):
    t = jnp.arange(t_new)[:, None]
    delta_buf = n_cached + t - jnp.arange(n_cached)[None, :]
    ok_buf = (delta_buf % dil == 0) & (delta_buf // dil <= QBLOCK)
    m = jnp.arange(QBLOCK)[None, :]
    delta_new = t - m
    ok_new = (m < t_new) & (delta_new >= 0) & (delta_new % dil == 0)
    b_buf = _masked_bias(tab, delta_buf // dil, ok_buf, dil)
    b_new = _masked_bias(tab, delta_new // dil, ok_new, dil)
    return (b_buf.reshape(HEADS_PER_GROUP * t_new, n_cached),
            b_new.reshape(HEADS_PER_GROUP * t_new, QBLOCK))


def _attn_prompt_kernel(q_ref, kc_ref, vc_ref, bias_ref, o_ref, lse_ref, kprev_scr, vprev_scr, *, nq, n_res):
    first_tile = pl.program_id(1) == 0

    @pl.when(first_tile)
    def _():
        kprev_scr[...] = jnp.zeros_like(kprev_scr)
        vprev_scr[...] = jnp.zeros_like(vprev_scr)

    stacked = (HEADS_PER_GROUP * QBLOCK, GROUP_WIDTH)
    own_head = (lax.broadcasted_iota(jnp.int32, stacked, 0) // QBLOCK
                == lax.broadcasted_iota(jnp.int32, stacked, 1) // HEAD_DIM)
    lane_head = lax.broadcasted_iota(jnp.int32, (QBLOCK, GROUP_WIDTH), 1) // HEAD_DIM

    def per_head(x):
        xb = x.astype(BF16)
        return jnp.where(own_head, jnp.concatenate([xb] * HEADS_PER_GROUP, axis=0), 0)

    def on_head_lanes(cols):
        out = jnp.broadcast_to(cols[-1], (QBLOCK, GROUP_WIDTH))
        for h in range(HEADS_PER_GROUP - 2, -1, -1):
            out = jnp.where(lane_head == h, cols[h], out)
        return out

    block_rows = [slice(i * QBLOCK, (i + 1) * QBLOCK) for i in range(nq)]

    def users(per_query_cur, per_query_prev, j):
        parts = ([per_query_cur[j - 1]] if j >= 1 else []) + ([per_query_prev[j]] if j < nq else [])
        return parts[0] if len(parts) == 1 else jnp.concatenate(parts, axis=0)

    for res in range(n_res):
        _attend_sequence(q_ref.at[res], kc_ref.at[res], vc_ref.at[res], bias_ref,
                         o_ref.at[res], lse_ref.at[res], kprev_scr.at[res], vprev_scr.at[res],
                         block_rows, per_head, on_head_lanes, users, first_tile)


def _attend_sequence(q_ref, kc_ref, vc_ref, bias_ref, o_ref, lse_ref, kprev_scr, vprev_scr,
                     block_rows, per_head, on_head_lanes, users, first_tile):
    nq = len(block_rows)
    qs = [(q_ref[rows, :] * QK_SCALE).astype(BF16) for rows in block_rows]
    k_blocks = [kprev_scr[...]] + [per_head(kc_ref[rows, :]) for rows in block_rows]
    v_blocks = [vprev_scr[...]] + [per_head(vc_ref[rows, :]) for rows in block_rows]
    kprev_scr[...] = k_blocks[-1]
    vprev_scr[...] = v_blocks[-1]

    s_prev, s_cur = [None] * nq, [None] * nq
    for j in range(nq + 1):
        s = _dot_nt(users(qs, qs, j), k_blocks[j])
        if j >= 1:
            s_cur[j - 1] = s[:QBLOCK] + bias_ref[1]
        if j < nq:
            s_prev[j] = s[-QBLOCK:] + bias_ref[0]
    s_prev[0] = jnp.where(first_tile, NEG_INF, s_prev[0])

    p_prev, p_cur, dens, lses = [], [], [], []
    for i in range(nq):
        pp_heads, pc_heads, den_heads, lse_heads = [], [], [], []
        for h in range(HEADS_PER_GROUP):
            keys = slice(h * QBLOCK, (h + 1) * QBLOCK)
            sp, sc = s_prev[i][:, keys], s_cur[i][:, keys]
            m = jnp.max(jnp.maximum(sp, sc), axis=-1, keepdims=True)
            pp, pc = jnp.exp(sp - m), jnp.exp(sc - m)
            den = jnp.sum(pp + pc, axis=-1, keepdims=True)
            pp_heads.append(pp.astype(BF16))
            pc_heads.append(pc.astype(BF16))
            den_heads.append(den)
            lse_heads.append(m + jnp.log(den))
        p_prev.append(jnp.concatenate(pp_heads, axis=1))
        p_cur.append(jnp.concatenate(pc_heads, axis=1))
        dens.append(on_head_lanes(den_heads))
        lses.append(on_head_lanes(lse_heads))

    o = [None] * nq
    for j in range(nq + 1):
        r = _dot(users(p_cur, p_prev, j), v_blocks[j])
        if j >= 1:
            o[j - 1] = o[j - 1] + r[:QBLOCK]
        if j < nq:
            o[j] = r[-QBLOCK:]
    for i, rows in enumerate(block_rows):
        o_ref[rows, :] = o[i] / dens[i]
        lse_ref[rows, :] = lses[i]


ATTN_QUERY_BLOCKS = 8
ATTN_STEP_BLOCKS = 16


def _attn_prompt(qkv, bias):
    batch, dil, length, _ = qkv.shape
    n_seq = batch * dil
    tq = min(ATTN_QUERY_BLOCKS * QBLOCK, length)
    nq = tq // QBLOCK
    n_res = math.gcd(n_seq, max(ATTN_STEP_BLOCKS // nq, 1))
    seqs = qkv.reshape(n_seq, length, GROUP_QKV)

    def cur(col):
        return pl.BlockSpec((n_res, tq, GROUP_WIDTH), lambda s, n: (s, n, col))

    stacked = pltpu.VMEM((n_res, HEADS_PER_GROUP * QBLOCK, GROUP_WIDTH), BF16)
    out_sds = jax.ShapeDtypeStruct((n_seq, length, GROUP_WIDTH), F32)
    o, lse = pl.pallas_call(
        functools.partial(_attn_prompt_kernel, nq=nq, n_res=n_res),
        grid=(n_seq // n_res, length // tq),
        in_specs=[cur(0), cur(1), cur(2), _resident((2, QBLOCK, HEADS_PER_GROUP * QBLOCK))],
        out_specs=[cur(0), cur(0)],
        out_shape=[out_sds, out_sds],
        scratch_shapes=[stacked, stacked],
        compiler_params=_compiler_params("arbitrary", "arbitrary"),
        name=f"attn_prompt_d{dil}",
    )(seqs, seqs, seqs, bias)
    out_shape = (batch, dil, length, GROUP_WIDTH)
    return o.reshape(out_shape), lse.reshape(out_shape)


def _attn_sample_kernel(q0_ref, q1_ref, q2_ref, c0_ref, c1_ref, c2_ref, tb0_ref, tb1_ref, tb2_ref, tn_ref,
                        o0_ref, l0_ref, o1_ref, l1_ref, o2_ref, l2_ref, kn_scr, vn_scr, *, t_new, n_seq):
    n_rows = HEADS_PER_GROUP * t_new
    row_w = lax.broadcasted_iota(jnp.int32, (n_rows, GROUP_WIDTH), 0)
    lane_w = lax.broadcasted_iota(jnp.int32, (n_rows, GROUP_WIDTH), 1)
    own_head = (row_w // t_new) == (lane_w // HEAD_DIM)

    def fold_heads(x):
        x = jnp.where(own_head, x, 0.0)
        out = x[0:t_new]
        for h in range(1, HEADS_PER_GROUP):
            out = out + x[h * t_new:(h + 1) * t_new]
        return out

    caches = (c0_ref, c1_ref, c2_ref)
    cache_bias = (tb0_ref, tb1_ref, tb2_ref)
    outs = ((o0_ref, l0_ref), (o1_ref, l1_ref), (o2_ref, l2_ref))
    for b, g in [(b, g) for b in range(n_seq) for g in range(N_DIL_GROUPS)]:
        qkv_ref = (q0_ref, q1_ref, q2_ref)[g]
        new_k, new_v = kn_scr.at[b * N_DIL_GROUPS + g], vn_scr.at[b * N_DIL_GROUPS + g]
        q = qkv_ref[b, :, 0:GROUP_WIDTH] * QK_SCALE
        q_rows = jnp.where(own_head, jnp.concatenate([q] * HEADS_PER_GROUP, axis=0), 0.0).astype(BF16)
        new_k[...] = jnp.zeros_like(new_k)
        new_v[...] = jnp.zeros_like(new_v)
        new_k[0:t_new, :] = qkv_ref[b, :, GROUP_WIDTH:2 * GROUP_WIDTH]
        new_v[0:t_new, :] = qkv_ref[b, :, 2 * GROUP_WIDTH:3 * GROUP_WIDTH]
        cache = caches[g]
        k_t = cache[b, 0:GROUP_WIDTH, :].astype(BF16)
        v_t = cache[b, GROUP_WIDTH:2 * GROUP_WIDTH, :].astype(BF16)
        s_buf = _dot(q_rows, k_t) + cache_bias[g][...]
        s_new = _dot_nt(q_rows, new_k[...].astype(BF16)) + tn_ref[g]
        m = jnp.maximum(jnp.max(s_buf, axis=-1, keepdims=True), jnp.max(s_new, axis=-1, keepdims=True))
        p_buf = jnp.exp(s_buf - m)
        p_new = jnp.exp(s_new - m)
        den = jnp.sum(p_buf, axis=-1, keepdims=True) + jnp.sum(p_new, axis=-1, keepdims=True)
        o = _dot_nt(p_buf.astype(BF16), v_t) + _dot(p_new.astype(BF16), new_v[...].astype(BF16))
        o_ref, l_ref = outs[g]
        o_ref[b] = fold_heads(o / den)
        l_ref[b] = fold_heads(jnp.broadcast_to(m + jnp.log(den), (n_rows, GROUP_WIDTH)))


ATTN_SAMPLE_SEQS = 2


def _attn_sample(qkvs, caches, tbs, tn):
    batch, t_new, _ = qkvs[0].shape
    n_rows = HEADS_PER_GROUP * t_new
    n_seq = math.gcd(batch, ATTN_SAMPLE_SEQS)
    cache_specs = [pl.BlockSpec((n_seq,) + c.shape[1:], lambda b: (b, 0, 0)) for c in caches]
    qkv_spec = pl.BlockSpec((n_seq, t_new, GROUP_QKV), lambda b: (b, 0, 0))
    out_spec = pl.BlockSpec((n_seq, t_new, GROUP_WIDTH), lambda b: (b, 0, 0))
    out_sds = jax.ShapeDtypeStruct((batch, t_new, GROUP_WIDTH), F32)
    new_rows = pltpu.VMEM((n_seq * N_DIL_GROUPS, QBLOCK, GROUP_WIDTH), F32)
    return pl.pallas_call(
        functools.partial(_attn_sample_kernel, t_new=t_new, n_seq=n_seq),
        grid=(batch // n_seq,),
        in_specs=[*[qkv_spec] * N_DIL_GROUPS, *cache_specs, *[_resident(t.shape) for t in tbs],
                  _resident((N_DIL_GROUPS, n_rows, QBLOCK))],
        out_specs=[out_spec] * (2 * N_DIL_GROUPS),
        out_shape=[out_sds] * (2 * N_DIL_GROUPS),
        scratch_shapes=[new_rows, new_rows],
        compiler_params=_compiler_params("arbitrary"),
        name="attn_sample",
    )(*qkvs, *caches, *tbs, tn)


def _ssm_chunk_terms(chunk, log_dt, lam_re, lam_im, b_re, b_im, c_re, c_im):
    hi = lax.Precision.HIGHEST
    dt = jnp.exp(log_dt.astype(F32))[:, None]
    lr, li = lam_re.astype(F32), lam_im.astype(F32)
    mag = jnp.exp(lr * dt)
    ab_re, ab_im = mag * jnp.cos(li * dt), mag * jnp.sin(li * dt)
    g, n = lr.shape
    p = b_re.shape[-1]
    den = lr * lr + li * li
    nr, ni = ab_re - 1.0, ab_im
    coef_re = (nr * lr + ni * li) / den
    coef_im = (ni * lr - nr * li) / den
    br, bi = b_re.astype(F32), b_im.astype(F32)
    bb_re = coef_re[..., None] * br - coef_im[..., None] * bi
    bb_im = coef_re[..., None] * bi + coef_im[..., None] * br
    k = jnp.arange(chunk + 1, dtype=F32)[:, None, None]
    pw_mag = jnp.exp(k * (lr * dt)[None])
    pw_re, pw_im = pw_mag * jnp.cos(k * (li * dt)[None]), pw_mag * jnp.sin(k * (li * dt)[None])
    bt_re, bt_im = bb_re.transpose(0, 2, 1)[None], bb_im.transpose(0, 2, 1)[None]
    pk_re, pk_im = pw_re[:chunk, :, None, :], pw_im[:chunk, :, None, :]
    akb_re = pk_re * bt_re - pk_im * bt_im
    akb_im = pk_re * bt_im + pk_im * bt_re
    cr, ci = c_re.astype(F32)[None], c_im.astype(F32)[None]
    pe_re, pe_im = pw_re[1:, :, None, :], pw_im[1:, :, None, :]
    e_re = cr * pe_re - ci * pe_im
    e_im = cr * pe_im + ci * pe_re
    kern = (jnp.einsum('kgqn,gpn->kgqp', akb_re, cr[0], precision=hi)
            - jnp.einsum('kgqn,gpn->kgqp', akb_im, ci[0], precision=hi))
    rows = lambda x: x.reshape(chunk, g * p, x.shape[-1])
    return ((rows(akb_re), rows(akb_im)), (rows(e_re), rows(e_im)), rows(kern), (pw_re, pw_im))


SSM_BLOCK_GROUPS = 8
SSM_BLOCK_CH = SSM_BLOCK_GROUPS * SSM_GROUP
SSM_BLOCK_STATE = SSM_BLOCK_GROUPS * SSM_STATE
N_SSM_BLOCKS = N_SSM_GROUPS // SSM_BLOCK_GROUPS


def _ssm_operator_kernel(akbre_ref, akbim_ref, ere_ref, eim_ref, kern_ref,
                         wre_ref, wim_ref, m_ref, etre_ref, etim_ref, *, chunk):
    gb, p, n = SSM_BLOCK_GROUPS, SSM_GROUP, SSM_STATE

    def copies(width, count):
        src = lax.broadcasted_iota(jnp.int32, (width, count * width), 0)
        dst = lax.broadcasted_iota(jnp.int32, (width, count * width), 1)
        return jnp.where(src == dst % width, 1.0, 0.0).astype(BF16)

    def own_group(width):
        row = lax.broadcasted_iota(jnp.int32, (SSM_BLOCK_CH, gb * width), 0)
        col = lax.broadcasted_iota(jnp.int32, (SSM_BLOCK_CH, gb * width), 1)
        return row // p == col // width

    to_states, own_states = copies(n, gb), own_group(n)

    def over_states(piece):
        return jnp.where(own_states, _dot(piece.astype(BF16), to_states), 0.0).astype(BF16)

    for s in range(chunk):
        rows = slice(s * SSM_BLOCK_CH, (s + 1) * SSM_BLOCK_CH)
        wre_ref[0, rows, :] = over_states(akbre_ref[chunk - 1 - s, 0])
        wim_ref[0, rows, :] = over_states(akbim_ref[chunk - 1 - s, 0])
        etre_ref[0, rows, :] = over_states(ere_ref[s, 0])
        etim_ref[0, rows, :] = over_states(-eim_ref[s, 0])
    to_channels, own_channels = copies(p, gb), own_group(p)
    lags = [jnp.where(own_channels, _dot(kern_ref[k, 0].astype(BF16), to_channels), 0.0).astype(BF16)
            for k in range(chunk)]
    zero = jnp.zeros((SSM_BLOCK_CH, SSM_BLOCK_CH), BF16)
    for s in range(chunk):
        m_ref[0, s * SSM_BLOCK_CH:(s + 1) * SSM_BLOCK_CH, :] = jnp.concatenate(
            [zero] * s + lags[:chunk - s], axis=1)


def _ssm_block_operators(chunk, terms):
    (akb_re, akb_im), (e_re, e_im), kern, (pw_re, pw_im) = terms
    assert chunk <= kern.shape[0]
    a_re, a_im = pw_re[chunk], pw_im[chunk]
    nb, gb, p, n = N_SSM_BLOCKS, SSM_BLOCK_GROUPS, SSM_GROUP, SSM_STATE
    pieces = [x[:chunk].reshape(chunk, nb, SSM_BLOCK_CH, x.shape[-1])
              for x in (akb_re, akb_im, e_re, e_im, kern)]
    piece_spec = lambda x: pl.BlockSpec((chunk, 1) + x.shape[2:], lambda i: (0, i, 0, 0))
    out_spec = lambda shape: pl.BlockSpec((1,) + shape[1:], lambda i: (i, 0, 0))
    x_width = chunk * SSM_BLOCK_CH
    out_shapes = [(nb, x_width, gb * n), (nb, x_width, gb * n), (nb, x_width, x_width),
                  (nb, x_width, gb * n), (nb, x_width, gb * n)]
    ops = pl.pallas_call(
        functools.partial(_ssm_operator_kernel, chunk=chunk),
        grid=(nb,),
        in_specs=[piece_spec(x) for x in pieces],
        out_specs=[out_spec(s) for s in out_shapes],
        out_shape=[jax.ShapeDtypeStruct(s, BF16) for s in out_shapes],
        compiler_params=_compiler_params("arbitrary"),
        name="ssm_operators",
    )(*pieces)
    return ops, a_re.reshape(nb, 1, gb * n), a_im.reshape(nb, 1, gb * n)


def _shorter_chunk_operators(ops, terms, chunk):
    w_re, w_im, m, et_re, et_im = ops
    pw_re, pw_im = terms[3]
    width = chunk * SSM_BLOCK_CH
    full = m.shape[1]
    assert width <= full
    nb, states = N_SSM_BLOCKS, SSM_BLOCK_STATE
    ops = [w_re[:, full - width:], w_im[:, full - width:], m[:, :width, :width], et_re[:, :width], et_im[:, :width]]
    return ops, pw_re[chunk].reshape(nb, 1, states), pw_im[chunk].reshape(nb, 1, states)


SSM_X_WIDTH = SSM_CHUNK * SSM_BLOCK_CH


def _ssm_prompt_kernel(u_ref, wre_ref, wim_ref, m_ref, etre_ref, etim_ref, are_ref, aim_ref, d_ref,
                       y_ref, hre_ref, him_ref, x_scr, sre_scr, sim_scr, *, batch, chunks):
    phase, b = pl.program_id(1), pl.program_id(2)
    for t in range(SSM_CHUNK):
        x_scr[:, t * SSM_BLOCK_CH:(t + 1) * SSM_BLOCK_CH] = u_ref[0, t].astype(BF16)
    rows = pl.ds(b, chunks, stride=batch)
    state_blocks = SSM_BLOCK_STATE // LANES

    @pl.when(phase == 0)
    def _():
        x = x_scr[...]
        g_re, g_im = _dot(x, wre_ref[0]), _dot(x, wim_ref[0])
        for k in range(state_blocks):
            sre_scr[k, rows, :] = g_re[:, k * LANES:(k + 1) * LANES]
            sim_scr[k, rows, :] = g_im[:, k * LANES:(k + 1) * LANES]

    @pl.when((phase == 1) & (b == 0))
    def _():
        same_lanes = lambda ref, k: ref[0][:, k * LANES:(k + 1) * LANES]

        def step(c, carry):
            same_chunk = pl.ds(c * batch, batch)
            out = []
            for k, (h_re, h_im) in enumerate(carry):
                a_re, a_im = same_lanes(are_ref, k), same_lanes(aim_ref, k)
                g_re, g_im = sre_scr[k, same_chunk, :], sim_scr[k, same_chunk, :]
                sre_scr[k, same_chunk, :] = h_re
                sim_scr[k, same_chunk, :] = h_im
                out.append((a_re * h_re - a_im * h_im + g_re, a_re * h_im + a_im * h_re + g_im))
            return tuple(out)

        zero = jnp.zeros((batch, LANES), F32)
        final = lax.fori_loop(0, chunks, step, ((zero, zero),) * state_blocks)
        for k, (h_re, h_im) in enumerate(final):
            hre_ref[0, :, k * LANES:(k + 1) * LANES] = h_re
            him_ref[0, :, k * LANES:(k + 1) * LANES] = h_im

    @pl.when(phase == 1)
    def _():
        h_re = jnp.concatenate([sre_scr[k, rows, :] for k in range(state_blocks)], axis=1).astype(BF16)
        h_im = jnp.concatenate([sim_scr[k, rows, :] for k in range(state_blocks)], axis=1).astype(BF16)
        pair_w = 2 * SSM_BLOCK_CH
        for j in range(SSM_CHUNK // 2):
            cols = slice(j * pair_w, (j + 1) * pair_w)
            k_in = (j + 1) * pair_w
            yj = (_dot(x_scr[:, :k_in], m_ref[0, :k_in, cols])
                  + _dot_nt(h_re, etre_ref[0, cols, :]) + _dot_nt(h_im, etim_ref[0, cols, :]))
            for i in range(2):
                t = 2 * j + i
                y_ref[pl.ds(t, chunks, stride=SSM_CHUNK), :] = (
                    yj[:, i * SSM_BLOCK_CH:(i + 1) * SSM_BLOCK_CH] + d_ref[0] * u_ref[0, t])


def _ssm_prompt(u, ops, a_re, a_im, d, batch, seq):
    chunks = seq // SSM_CHUNK
    d = d.reshape(N_SSM_BLOCKS, 1, SSM_BLOCK_CH)
    largest = max(o.size for o in ops)
    op_spec = lambda arr: pl.BlockSpec((1,) + arr.shape[1:], lambda g, ph, b: (g, 0, 0),
                                       pipeline_mode=pl.Buffered(1 if arr.size == largest else 2))
    state_spec = pl.BlockSpec((1, batch, SSM_BLOCK_STATE), lambda g, ph, b: (g, 0, 0))
    state_sds = jax.ShapeDtypeStruct((N_SSM_BLOCKS, batch, SSM_BLOCK_STATE), F32)
    y, h_re, h_im = pl.pallas_call(
        functools.partial(_ssm_prompt_kernel, batch=batch, chunks=chunks),
        grid=(N_SSM_BLOCKS, 2, batch),
        in_specs=[pl.BlockSpec((1, SSM_CHUNK, chunks, SSM_BLOCK_CH), lambda g, ph, b: (b, 0, 0, g)),
                  *[op_spec(o) for o in ops], op_spec(a_re), op_spec(a_im), op_spec(d)],
        out_specs=[pl.BlockSpec((seq, SSM_BLOCK_CH), lambda g, ph, b: (b * ph, g)), state_spec, state_spec],
        out_shape=[jax.ShapeDtypeStruct((batch * seq, SSM_WIDTH), F32), state_sds, state_sds],
        scratch_shapes=[pltpu.VMEM((chunks, SSM_X_WIDTH), BF16),
                        pltpu.VMEM((SSM_BLOCK_STATE // LANES, batch * chunks, LANES), F32),
                        pltpu.VMEM((SSM_BLOCK_STATE // LANES, batch * chunks, LANES), F32)],
        compiler_params=_compiler_params("arbitrary", "arbitrary", "arbitrary"),
        name="ssm_prompt",
    )(u, *ops, a_re, a_im, d)

    def by_sequence(h):
        h = h.reshape(N_SSM_BLOCKS, batch, SSM_BLOCK_GROUPS, SSM_STATE).transpose(1, 0, 2, 3)
        return h.reshape(batch, N_SSM_GROUPS, SSM_STATE)

    return y, by_sequence(h_re), by_sequence(h_im)


def _ssm_sample_kernel(u_ref, wre_ref, wim_ref, m_ref, etre_ref, etim_ref, are_ref, aim_ref, d_ref,
                       h0re_ref, h0im_ref, y_ref, hre_ref, him_ref, *, t_new, batch):
    for nb in range(N_SSM_BLOCKS):
        ch = slice(nb * SSM_BLOCK_CH, (nb + 1) * SSM_BLOCK_CH)
        st = slice(nb * SSM_BLOCK_STATE, (nb + 1) * SSM_BLOCK_STATE)
        x = jnp.concatenate([u_ref[t * batch:(t + 1) * batch, ch] for t in range(t_new)], axis=1).astype(BF16)
        h_re, h_im = h0re_ref[:, st], h0im_ref[:, st]
        a_re, a_im = are_ref[nb], aim_ref[nb]
        hre_ref[:, st] = a_re * h_re - a_im * h_im + _dot(x, wre_ref[nb])
        him_ref[:, st] = a_re * h_im + a_im * h_re + _dot(x, wim_ref[nb])
        y = (_dot(x, m_ref[nb]) + _dot_nt(h_re.astype(BF16), etre_ref[nb])
             + _dot_nt(h_im.astype(BF16), etim_ref[nb]))
        for t in range(t_new):
            rows = slice(t * batch, (t + 1) * batch)
            y_ref[rows, ch] = y[:, t * SSM_BLOCK_CH:(t + 1) * SSM_BLOCK_CH] + d_ref[:, ch] * u_ref[rows, ch]


def _ssm_sample(u, ops, a_re, a_im, d, h0_re, h0_im, t_new):
    batch = u.shape[0] // t_new
    args = (u, *ops, a_re, a_im, d, h0_re, h0_im)
    state_sds = jax.ShapeDtypeStruct((batch, N_SSM_GROUPS * SSM_STATE), F32)
    out_shape = [jax.ShapeDtypeStruct(u.shape, F32), state_sds, state_sds]
    whole = lambda shape: pl.BlockSpec(shape, lambda i, nd=len(shape): (0,) * nd)
    return pl.pallas_call(
        functools.partial(_ssm_sample_kernel, t_new=t_new, batch=batch),
        grid=(1,),
        in_specs=[whole(a.shape) for a in args],
        out_specs=[whole(s.shape) for s in out_shape],
        out_shape=out_shape,
        compiler_params=_compiler_params("arbitrary"),
        name="ssm_sample",
    )(*args)


def _gelu_tanh(x):
    return 0.5 * x * (1.0 + jnp.tanh(math.sqrt(2.0 / math.pi) * (x + 0.044715 * (x * x * x))))


def _merge_tile(o0_ref, l0_ref, o1_ref, l1_ref, o2_ref, l2_ref, ys_ref, gate_ref, x_ref,
                wglu_ref, bglu_ref, wba_ref, wbs_ref, wout_ref, order_scr, tm, dils):
    def row_order(ref, dil, slot):
        if dil == 1:
            return ref[0, 0]
        n = tm // dil
        halves = GROUP_WIDTH // LANES
        for r in range(dil):
            for k in range(halves):
                order_scr[slot * halves + k, pl.ds(r, n, stride=dil), :] = ref[0, r, :, k * LANES:(k + 1) * LANES]
        return jnp.concatenate([order_scr[slot * halves + k] for k in range(halves)], axis=1)

    parts = [row_order(ref, dils[i // 2], i) for i, ref in
             enumerate((o0_ref, l0_ref, o1_ref, l1_ref, o2_ref, l2_ref))]
    o0, l0, o1, l1, o2, l2 = parts
    mx = jnp.maximum(jnp.maximum(l0, l1), l2)
    e0, e1, e2 = jnp.exp(l0 - mx), jnp.exp(l1 - mx), jnp.exp(l2 - mx)
    attn = (e0 * o0 + e1 * o1 + e2 * o2) / (e0 + e1 + e2)
    branch_a = _dot(attn.astype(BF16), wba_ref[...])
    y = _gelu_tanh(ys_ref[...])
    y = y * _sigmoid(_dot(y.astype(BF16), wglu_ref[...]) + bglu_ref[...])
    branch_s = _dot(y.astype(BF16), wbs_ref[...])
    mix = (gate_ref[:, 0:D_MODEL].astype(F32) * branch_a
           + gate_ref[:, D_MODEL:2 * D_MODEL].astype(F32) * branch_s)
    return x_ref[...] + _dot(mix.astype(BF16), wout_ref[...])


MXU_WIDTH = 256
FF_SPLIT = -(-D_FF // (2 * MXU_WIDTH)) * MXU_WIDTH
FF_CHUNKS = ((0, FF_SPLIT), (FF_SPLIT, D_FF))
N_MERGE_ROW_INPUTS = 2 * N_DIL_GROUPS + 3
N_MERGE_WEIGHTS = 5


def _merge_ffn_kernel(*refs, tm, shift, pad, dils):
    n_merge = N_MERGE_ROW_INPUTS + N_MERGE_WEIGHTS
    merge_refs, rest = refs[:n_merge], refs[n_merge:]
    carry_ref, g2_ref, wup_ref, cw_ref, cb_ref, wdn_ref, gf_ref, y_ref, state_ref, order_scr, a_scr = rest
    hist = 2 * shift

    @pl.when(pl.program_id(1) == 0)
    def _():
        a_scr[pad - hist:pad, :] = carry_ref[0]

    xf = _merge_tile(*merge_refs, order_scr, tm, dils)
    xn = _rmsnorm(xf, g2_ref[...]).astype(BF16)
    acc = jnp.zeros((tm, D_MODEL), F32)
    for c0, c1 in FF_CHUNKS:
        cols = slice(c0, c1)
        a = _dot(xn, wup_ref[:, cols])
        val = _dot(xn, wup_ref[:, D_FF + c0:D_FF + c1])
        a_scr[pad:pad + tm, cols] = a
        a_m1 = a_scr[pad - shift:pad - shift + tm, cols]
        a_m2 = a_scr[pad - hist:pad - hist + tm, cols]
        conv = cb_ref[:, cols] + cw_ref[0:1, cols] * a_m2
        conv = conv + cw_ref[1:2, cols] * a_m1
        conv = conv + cw_ref[2:3, cols] * a
        act = conv * _sigmoid(conv) * val
        acc = acc + _dot(act.astype(BF16), wdn_ref[cols, :])
    tail = a_scr[pad + tm - hist:pad + tm, :]
    a_scr[pad - hist:pad, :] = tail
    state_ref[0] = tail
    y_ref[...] = _rmsnorm(xf + acc, gf_ref[...])


def _merge_ffn(attn_parts, ys, gates, x2d, merge_weights, carry, ffn_weights, n_seq, tm, shift):
    m = x2d.shape[0]
    tiles = m // n_seq // tm
    hist = 2 * shift
    pad = -(-hist // SUBLANES) * SUBLANES
    dils = tuple(p.shape[1] for p in attn_parts[::2])
    row = lambda width: pl.BlockSpec((tm, width), lambda b, j: (b * tiles + j, 0))
    res_spec = lambda dil: pl.BlockSpec((1, dil, tm // dil, GROUP_WIDTH), lambda b, j: (b, 0, j, 0))
    state_spec = pl.BlockSpec((1, hist, D_FF), lambda b, j: (b, 0, 0))
    assert len(merge_weights) == N_MERGE_WEIGHTS
    return pl.pallas_call(
        functools.partial(_merge_ffn_kernel, tm=tm, shift=shift, pad=pad, dils=dils),
        grid=(n_seq, tiles),
        in_specs=[*[res_spec(p.shape[1]) for p in attn_parts],
                  row(SSM_WIDTH), row(2 * D_MODEL), row(D_MODEL),
                  *[_resident(w.shape) for w in merge_weights],
                  state_spec, *[_resident(w.shape) for w in ffn_weights]],
        out_specs=[row(D_MODEL), state_spec],
        out_shape=[jax.ShapeDtypeStruct((m, D_MODEL), F32),
                   jax.ShapeDtypeStruct((n_seq, hist, D_FF), F32)],
        scratch_shapes=[pltpu.VMEM((len(attn_parts) * GROUP_WIDTH // LANES, tm, LANES), F32),
                        pltpu.VMEM((pad + tm, D_FF), F32)],
        compiler_params=_compiler_params("arbitrary", "arbitrary"),
        name="merge_ffn",
    )(*attn_parts, ys, gates, x2d, *merge_weights, carry, *ffn_weights)


def _kv_rows(qkv, keep):
    batch, dil, length, _ = qkv.shape
    n = keep // dil
    rows = qkv[:, :, length - n:, GROUP_WIDTH:]
    cols = rows.transpose(0, 3, 2, 1).reshape(batch, 2, HEADS_PER_GROUP, HEAD_DIM, keep)
    return cols.transpose(0, 4, 1, 2, 3)


def _prompt_layer(x, rel_bias, lw):
    batch, seq, _ = x.shape
    x2d = x.reshape(batch * seq, D_MODEL)
    dils = tuple(dil for _, dil in DIL_PATTERNS) + (SSM_CHUNK,)
    keeps = tuple(min(window, seq) for window, _ in DIL_PATTERNS)
    *qkvs, u, gates, kv0, kv1, kv2 = _in_proj(x2d, lw['norm1_g'], lw['w_in'], batch, IN_PROJ_TILE, dils, keeps)

    attn_parts, kv_new = [], []
    for g, (window, dil) in enumerate(DIL_PATTERNS):
        tab = rel_bias[:, g * HEADS_PER_GROUP:(g + 1) * HEADS_PER_GROUP]
        attn_parts.extend(_attn_prompt(qkvs[g], _prompt_bias(tab, dil)))
        tail = (kv0, kv1, kv2)[g].reshape(batch, 2, HEADS_PER_GROUP, HEAD_DIM, keeps[g])
        kv_new.append(tail.transpose(0, 4, 1, 2, 3))

    ops, a_re, a_im = lw['ssm_ops']
    ys, h_re, h_im = _ssm_prompt(u, ops, a_re, a_im, lw['ssm_d'], batch, seq)

    carry = jnp.zeros((batch, CONV_W - 1, D_FF), F32)
    y, conv_state = _merge_ffn(attn_parts, ys, gates, x2d, lw['merge'], carry, lw['ffn'],
                               n_seq=batch, tm=MERGE_FFN_TILE, shift=1)
    return y.reshape(batch, seq, D_MODEL), (*kv_new, h_re, h_im, conv_state)


def _sample_layer(x, caches, h0_re, h0_im, conv_buf, rel_bias, lw):
    batch, t_new, _ = x.shape
    m = batch * t_new
    x2d = x.transpose(1, 0, 2).reshape(m, D_MODEL)
    *qkvs, u, gates = _in_proj(x2d, lw['norm1_g'], lw['w_in'], 1, m, (1,) * (N_DIL_GROUPS + 1))
    u = u.reshape(m, SSM_WIDTH)
    qkvs_bt = [q.reshape(t_new, batch, GROUP_QKV).transpose(1, 0, 2) for q in qkvs]

    tbs, tns, views = [], [], []
    for g, (window, dil) in enumerate(DIL_PATTERNS):
        tab = rel_bias[:, g * HEADS_PER_GROUP:(g + 1) * HEADS_PER_GROUP]
        n_cached = caches[g].shape[1]
        tb, tn = _sample_bias(tab, dil, t_new, n_cached)
        tbs.append(tb)
        tns.append(tn)
        views.append(caches[g].transpose(0, 2, 3, 4, 1).reshape(batch, 2 * GROUP_WIDTH, n_cached))
    parts = _attn_sample(qkvs_bt, views, tbs, jnp.stack(tns))
    attn_parts = [p.transpose(1, 0, 2).reshape(1, 1, m, GROUP_WIDTH) for p in parts]
    kv_new = [_kv_rows(q.reshape(batch, 1, t_new, GROUP_QKV), t_new) for q in qkvs_bt]

    ops, a_re, a_im = _shorter_chunk_operators(lw['ssm_ops'][0], lw['ssm_terms'], t_new)
    ys, h_re, h_im = _ssm_sample(u, ops, a_re, a_im, lw['ssm_d'],
                                 h0_re.reshape(batch, -1), h0_im.reshape(batch, -1), t_new)

    carry = conv_buf.transpose(1, 0, 2).reshape(1, (CONV_W - 1) * batch, D_FF)
    y, conv_state = _merge_ffn(attn_parts, ys, gates, x2d, lw['merge'], carry, lw['ffn'],
                               n_seq=1, tm=m, shift=batch)
    y = y.reshape(t_new, batch, D_MODEL).transpose(1, 0, 2)
    conv_state = conv_state.reshape(CONV_W - 1, batch, D_FF).transpose(1, 0, 2)
    state_shape = (batch, N_SSM_GROUPS, SSM_STATE)
    return y, (*kv_new, h_re.reshape(state_shape), h_im.reshape(state_shape), conv_state)


IN_PROJ_TILE = 512
MERGE_FFN_TILE = 512


def kernel(x_prompt, x_sample, cache_kv_w128, cache_kv_w512, cache_kv_w2048, state_ssm_re, state_ssm_im, state_ffn_conv, rel_bias, norm1_g, w_in, ssm_log_dt, ssm_lambda_re, ssm_lambda_im, ssm_b_re, ssm_b_im, ssm_c_re, ssm_c_im, ssm_d, w_glu, b_glu, w_branch_attn, w_branch_ssm, w_out, norm2_g, w_up, conv_w, conv_b, w_down, norm_f_g):
    depth = w_in.shape[0]
    hp, hs = x_prompt, x_sample
    st_p, st_s = [], []
    gf = norm_f_g.reshape(1, D_MODEL)
    for l in range(depth):
        last = l == depth - 1
        assert x_sample.shape[1] <= SSM_CHUNK, "the sample chunk operators are cut out of the prompt's"
        ssm_terms = _ssm_chunk_terms(SSM_CHUNK, ssm_log_dt[l], ssm_lambda_re[l], ssm_lambda_im[l],
                                     ssm_b_re[l], ssm_b_im[l], ssm_c_re[l], ssm_c_im[l])
        lw = {
            'norm1_g': norm1_g[l].reshape(1, D_MODEL),
            'w_in': _group_major_columns(w_in[l]).astype(BF16),
            'ssm_terms': ssm_terms,
            'ssm_ops': _ssm_block_operators(SSM_CHUNK, ssm_terms),
            'ssm_d': ssm_d[l].reshape(1, SSM_WIDTH),
            'merge': (w_glu[l].astype(BF16), b_glu[l].reshape(1, SSM_WIDTH),
                      w_branch_attn[l].astype(BF16), w_branch_ssm[l].astype(BF16), w_out[l].astype(BF16)),
            'ffn': (norm2_g[l].reshape(1, D_MODEL), w_up[l].astype(BF16), conv_w[l],
                    conv_b[l].reshape(1, D_FF), w_down[l].astype(BF16), gf),
        }
        assert last, "the final RMSNorm is fused into the last layer's ffn kernel"
        hp, sp = _prompt_layer(hp, rel_bias, lw)
        hs, ss = _sample_layer(hs, (cache_kv_w128[l], cache_kv_w512[l], cache_kv_w2048[l]),
                               state_ssm_re[l], state_ssm_im[l], state_ffn_conv[l], rel_bias, lw)
        st_p.append(sp)
        st_s.append(ss)
    stack = lambda states, i: jnp.stack([st[i] for st in states], axis=0)
    return (hp, hs, *[stack(st_p, i) for i in range(6)], *[stack(st_s, i) for i in range(6)])
```

```python
import functools
import math

import jax
import jax.numpy as jnp
import numpy as np
from jax import lax
from jax.experimental import pallas as pl
from jax.experimental.pallas import tpu as pltpu

F32 = jnp.float32
BF16 = jnp.bfloat16

D_MODEL = 1024
HEAD_DIM = 64
HEADS_PER_GROUP = 4
DIL_PATTERNS = ((128, 1), (512, 4), (2048, 16))
N_DIL_GROUPS = len(DIL_PATTERNS)
GROUP_WIDTH = HEADS_PER_GROUP * HEAD_DIM
QK_WIDTH = N_DIL_GROUPS * GROUP_WIDTH
QKV_WIDTH = 3 * QK_WIDTH
GROUP_QKV = 3 * GROUP_WIDTH
QBLOCK = 128
SSM_GROUP = 16
SSM_STATE = 64
SSM_WIDTH = D_MODEL // 2
N_SSM_GROUPS = SSM_WIDTH // SSM_GROUP
SSM_CHUNK = 16
D_FF = 2816
CONV_W = 3
N_BUCKETS = 32
MAX_DISTANCE = 2048
NORM_EPS = 1e-6
NEG_INF = -1e30
U_START = QKV_WIDTH
GATE_START = U_START + SSM_WIDTH
IN_WIDTH = GATE_START + 2 * D_MODEL
QK_SCALE = HEAD_DIM ** -0.5

VMEM_LIMIT_BYTES = 60 * 1024 * 1024
SUBLANES = 8
LANES = 128


def _compiler_params(*semantics):
    return pltpu.CompilerParams(dimension_semantics=semantics, vmem_limit_bytes=VMEM_LIMIT_BYTES)


def _resident(shape):
    nd = len(shape)
    return pl.BlockSpec(shape, lambda *_: (0,) * nd, pipeline_mode=pl.Buffered(1))


def _rmsnorm(xf, g):
    y = xf * lax.rsqrt(jnp.mean(xf * xf, axis=-1, keepdims=True) + NORM_EPS)
    return y * g


def _sigmoid(x):
    return 1.0 / (1.0 + jnp.exp(-x))


def _dot(a, b):
    return jnp.dot(a, b, preferred_element_type=F32)


def _dot_nt(a, b):
    return lax.dot_general(a, b, (((1,), (1,)), ((), ())), preferred_element_type=F32)


def _in_proj_kernel(x_ref, g_ref, w_ref, qkv0_ref, qkv1_ref, qkv2_ref, u_ref, gate_ref, *rest,
                    tm, dils, tails, tiles):
    if tails is None:
        tail_refs, (xn_scr,) = (), rest
    else:
        tail_refs, (xn_scr, tok_scr) = rest[:N_DIL_GROUPS], rest[N_DIL_GROUPS:]
    xf = _rmsnorm(x_ref[...], g_ref[...])
    n_lane_blocks = D_MODEL // LANES
    if any(dil > 1 for dil in dils):
        for k in range(n_lane_blocks):
            xn_scr[k] = xf[:, k * LANES:(k + 1) * LANES]
    by_residue = {1: xf.astype(BF16)}

    def rows_by_residue(dil):
        if dil not in by_residue:
            n = tm // dil
            xr = jnp.concatenate(
                [jnp.concatenate([xn_scr[k, pl.ds(r, n, stride=dil), :] for k in range(n_lane_blocks)], axis=1)
                 for r in range(dil)], axis=0)
            by_residue[dil] = xr.astype(BF16)
        return by_residue[dil]

    outputs = [(ref, g * GROUP_QKV, GROUP_QKV) for g, ref in enumerate((qkv0_ref, qkv1_ref, qkv2_ref))]
    outputs.append((u_ref, U_START, SSM_WIDTH))
    for c0 in range(0, 2 * D_MODEL, D_MODEL):
        logits = _dot(by_residue[1], w_ref[:, GATE_START + c0:GATE_START + c0 + D_MODEL])
        gate_ref[:, c0:c0 + D_MODEL] = _sigmoid(logits).astype(BF16)
    for (ref, start, width), dil in zip(outputs, dils):
        n = tm // dil
        res = _dot(rows_by_residue(dil), w_ref[:, start:start + width])
        for r in range(dil):
            ref[0, r] = res[r * n:(r + 1) * n]
    if tails is None:
        return
    kv_blocks = 2 * GROUP_WIDTH // LANES
    for (ref, _, _), dil, keep, tail_ref in zip(outputs, dils, tails, tail_refs):
        n = tm // dil
        kept = min(keep, tm)

        @pl.when(pl.program_id(1) >= tiles - max(keep // tm, 1))
        def _(ref=ref, dil=dil, n=n, kept=kept, tail_ref=tail_ref):
            if dil == 1:
                kv = ref[0, 0, :, GROUP_WIDTH:]
            else:
                for r in range(dil):
                    for k in range(kv_blocks):
                        lanes = slice(GROUP_WIDTH + k * LANES, GROUP_WIDTH + (k + 1) * LANES)
                        tok_scr[k, pl.ds(r, n, stride=dil), :] = ref[0, r, :, lanes]
                kv = jnp.concatenate([tok_scr[k] for k in range(kv_blocks)], axis=1)
            tail_ref[0] = kv[tm - kept:, :].T


def _group_major_columns(w_in):
    parts = []
    for g in range(N_DIL_GROUPS):
        for base in (0, QK_WIDTH, 2 * QK_WIDTH):
            parts.append(w_in[:, base + g * GROUP_WIDTH:base + (g + 1) * GROUP_WIDTH])
    parts.append(w_in[:, U_START:])
    return jnp.concatenate(parts, axis=1)


def _in_proj(x2d, g, w_bf16, n_seq, tm, dils, tails=None):
    m = x2d.shape[0]
    seq = m // n_seq
    tiles = seq // tm
    row = lambda width: pl.BlockSpec((tm, width), lambda b, j: (b * tiles + j, 0))
    res_spec = lambda dil, width: pl.BlockSpec((1, dil, tm // dil, width), lambda b, j: (b, 0, j, 0))
    widths = (GROUP_QKV,) * N_DIL_GROUPS + (SSM_WIDTH,)
    out_specs = [*[res_spec(d, w) for d, w in zip(dils, widths)], row(2 * D_MODEL)]
    out_shape = [*[jax.ShapeDtypeStruct((n_seq, d, seq // d, w), F32) for d, w in zip(dils, widths)],
                 jax.ShapeDtypeStruct((m, 2 * D_MODEL), BF16)]
    scratch = [pltpu.VMEM((D_MODEL // LANES, tm, LANES), F32)]
    if tails is not None:
        for keep in tails:
            assert keep % tm == 0 or tm % keep == 0
            first = tiles - max(keep // tm, 1)
            out_specs.append(pl.BlockSpec((1, 2 * GROUP_WIDTH, min(keep, tm)),
                                          lambda b, j, first=first: (b, 0, jnp.maximum(j - first, 0))))
            out_shape.append(jax.ShapeDtypeStruct((n_seq, 2 * GROUP_WIDTH, keep), F32))
        scratch.append(pltpu.VMEM((2 * GROUP_WIDTH // LANES, tm, LANES), F32))
    return pl.pallas_call(
        functools.partial(_in_proj_kernel, tm=tm, dils=dils, tails=tails, tiles=tiles),
        grid=(n_seq, tiles),
        in_specs=[row(D_MODEL), _resident((1, D_MODEL)), _resident((D_MODEL, IN_WIDTH))],
        out_specs=out_specs,
        out_shape=out_shape,
        scratch_shapes=scratch,
        compiler_params=_compiler_params("arbitrary", "arbitrary"),
        name="in_proj",
    )(x2d, g, w_bf16)


def _bucket_starts():
    max_exact = N_BUCKETS // 2
    n = np.arange(max_exact, MAX_DISTANCE + 1)
    large = max_exact + (np.log(n.astype(np.float32) / np.float32(max_exact))
                         / np.float32(math.log(MAX_DISTANCE / max_exact))
                         * np.float32(N_BUCKETS - max_exact)).astype(np.int32)
    large = np.minimum(large, N_BUCKETS - 1)
    return [int(n[np.argmax(large >= k)]) for k in range(max_exact + 1, N_BUCKETS)]


def _rel_bucket(dist):
    max_exact = N_BUCKETS // 2
    n = jnp.maximum(dist, 0)
    large = max_exact + sum((n >= start).astype(jnp.int32) for start in _bucket_starts())
    return jnp.where(n < max_exact, n, large)


def _masked_bias(tab, strides, valid, dil):
    n_dist = QBLOCK
    bucket = _rel_bucket(jnp.clip(strides, 0, n_dist) * dil).reshape(1, -1)
    onehot = (bucket == jnp.arange(N_BUCKETS)[:, None]).astype(F32)
    bias = jnp.dot(tab.astype(F32).T, onehot, precision=lax.Precision.HIGHEST)
    bias = jnp.where(valid.reshape(1, -1), bias, NEG_INF)
    return bias.reshape((tab.shape[1],) + strides.shape)


def _prompt_bias(tab, dil):
    qi = jnp.arange(QBLOCK)[:, None]
    ki = jnp.arange(QBLOCK)[None, :]
    j_prev = qi + QBLOCK - ki
    j_cur = qi - ki
    bias = jnp.stack([_masked_bias(tab, j_prev, j_prev <= QBLOCK, dil),
                      _masked_bias(tab, j_cur, j_cur >= 0, dil)], axis=0)
    return bias.transpose(0, 2, 1, 3).reshape(2, QBLOCK, HEADS_PER_GROUP * QBLOCK)


def _sample_bias(tab, dil, t_new, n_cached):
    t = jnp.arange(t_new)[:, None]
    delta_buf = n_cached + t - jnp.arange(n_cached)[None, :]
    ok_buf = (delta_buf % dil == 0) & (delta_buf // dil <= QBLOCK)
    m = jnp.arange(QBLOCK)[None, :]
    delta_new = t - m
    ok_new = (m < t_new) & (delta_new >= 0) & (delta_new % dil == 0)
    b_buf = _masked_bias(tab, delta_buf // dil, ok_buf, dil)
    b_new = _masked_bias(tab, delta_new // dil, ok_new, dil)
    return (b_buf.reshape(HEADS_PER_GROUP * t_new, n_cached),
            b_new.reshape(HEADS_PER_GROUP * t_new, QBLOCK))


def _attn_prompt_kernel(q_ref, kc_ref, vc_ref, bias_ref, o_ref, lse_ref, kprev_scr, vprev_scr, *, nq, n_res):
    first_tile = pl.program_id(1) == 0

    @pl.when(first_tile)
    def _():
        kprev_scr[...] = jnp.zeros_like(kprev_scr)
        vprev_scr[...] = jnp.zeros_like(vprev_scr)

    stacked = (HEADS_PER_GROUP * QBLOCK, GROUP_WIDTH)
    own_head = (lax.broadcasted_iota(jnp.int32, stacked, 0) // QBLOCK
                == lax.broadcasted_iota(jnp.int32, stacked, 1) // HEAD_DIM)
    lane_head = lax.broadcasted_iota(jnp.int32, (QBLOCK, GROUP_WIDTH), 1) // HEAD_DIM

    def per_head(x):
        xb = x.astype(BF16)
        return jnp.where(own_head, jnp.concatenate([xb] * HEADS_PER_GROUP, axis=0), 0)

    def on_head_lanes(cols):
        out = jnp.broadcast_to(cols[-1], (QBLOCK, GROUP_WIDTH))
        for h in range(HEADS_PER_GROUP - 2, -1, -1):
            out = jnp.where(lane_head == h, cols[h], out)
        return out

    block_rows = [slice(i * QBLOCK, (i + 1) * QBLOCK) for i in range(nq)]

    def users(per_query_cur, per_query_prev, j):
        parts = ([per_query_cur[j - 1]] if j >= 1 else []) + ([per_query_prev[j]] if j < nq else [])
        return parts[0] if len(parts) == 1 else jnp.concatenate(parts, axis=0)

    for res in range(n_res):
        _attend_sequence(q_ref.at[res], kc_ref.at[res], vc_ref.at[res], bias_ref,
                         o_ref.at[res], lse_ref.at[res], kprev_scr.at[res], vprev_scr.at[res],
                         block_rows, per_head, on_head_lanes, users, first_tile)


def _attend_sequence(q_ref, kc_ref, vc_ref, bias_ref, o_ref, lse_ref, kprev_scr, vprev_scr,
                     block_rows, per_head, on_head_lanes, users, first_tile):
    nq = len(block_rows)
    qs = [(q_ref[rows, :] * QK_SCALE).astype(BF16) for rows in block_rows]
    k_blocks = [kprev_scr[...]] + [per_head(kc_ref[rows, :]) for rows in block_rows]
    v_blocks = [vprev_scr[...]] + [per_head(vc_ref[rows, :]) for rows in block_rows]
    kprev_scr[...] = k_blocks[-1]
    vprev_scr[...] = v_blocks[-1]

    s_prev, s_cur = [None] * nq, [None] * nq
    for j in range(nq + 1):
        s = _dot_nt(users(qs, qs, j), k_blocks[j])
        if j >= 1:
            s_cur[j - 1] = s[:QBLOCK] + bias_ref[1]
        if j < nq:
            s_prev[j] = s[-QBLOCK:] + bias_ref[0]
    s_prev[0] = jnp.where(first_tile, NEG_INF, s_prev[0])

    p_prev, p_cur, dens, lses = [], [], [], []
    for i in range(nq):
        pp_heads, pc_heads, den_heads, lse_heads = [], [], [], []
        for h in range(HEADS_PER_GROUP):
            keys = slice(h * QBLOCK, (h + 1) * QBLOCK)
            sp, sc = s_prev[i][:, keys], s_cur[i][:, keys]
            m = jnp.max(jnp.maximum(sp, sc), axis=-1, keepdims=True)
            pp, pc = jnp.exp(sp - m), jnp.exp(sc - m)
            den = jnp.sum(pp + pc, axis=-1, keepdims=True)
            pp_heads.append(pp.astype(BF16))
            pc_heads.append(pc.astype(BF16))
            den_heads.append(den)
            lse_heads.append(m + jnp.log(den))
        p_prev.append(jnp.concatenate(pp_heads, axis=1))
        p_cur.append(jnp.concatenate(pc_heads, axis=1))
        dens.append(on_head_lanes(den_heads))
        lses.append(on_head_lanes(lse_heads))

    o = [None] * nq
    for j in range(nq + 1):
        r = _dot(users(p_cur, p_prev, j), v_blocks[j])
        if j >= 1:
            o[j - 1] = o[j - 1] + r[:QBLOCK]
        if j < nq:
            o[j] = r[-QBLOCK:]
    for i, rows in enumerate(block_rows):
        o_ref[rows, :] = o[i] / dens[i]
        lse_ref[rows, :] = lses[i]


ATTN_QUERY_BLOCKS = 8
ATTN_STEP_BLOCKS = 32


def _attn_prompt(qkv, bias):
    batch, dil, length, _ = qkv.shape
    n_seq = batch * dil
    tq = min(ATTN_QUERY_BLOCKS * QBLOCK, length)
    nq = tq // QBLOCK
    n_res = math.gcd(n_seq, max(ATTN_STEP_BLOCKS // nq, 1))
    seqs = qkv.reshape(n_seq, length, GROUP_QKV)

    def cur(col):
        return pl.BlockSpec((n_res, tq, GROUP_WIDTH), lambda s, n: (s, n, col))

    stacked = pltpu.VMEM((n_res, HEADS_PER_GROUP * QBLOCK, GROUP_WIDTH), BF16)
    out_sds = jax.ShapeDtypeStruct((n_seq, length, GROUP_WIDTH), F32)
    o, lse = pl.pallas_call(
        functools.partial(_attn_prompt_kernel, nq=nq, n_res=n_res),
        grid=(n_seq // n_res, length // tq),
        in_specs=[cur(0), cur(1), cur(2), _resident((2, QBLOCK, HEADS_PER_GROUP * QBLOCK))],
        out_specs=[cur(0), cur(0)],
        out_shape=[out_sds, out_sds],
        scratch_shapes=[stacked, stacked],
        compiler_params=_compiler_params("arbitrary", "arbitrary"),
        name=f"attn_prompt_d{dil}",
    )(seqs, seqs, seqs, bias)
    out_shape = (batch, dil, length, GROUP_WIDTH)
    return o.reshape(out_shape), lse.reshape(out_shape)


def _attn_sample_kernel(q0_ref, q1_ref, q2_ref, c0_ref, c1_ref, c2_ref, tb0_ref, tb1_ref, tb2_ref, tn_ref,
                        o0_ref, l0_ref, o1_ref, l1_ref, o2_ref, l2_ref, kn_scr, vn_scr, *, t_new, n_seq):
    n_rows = HEADS_PER_GROUP * t_new
    row_w = lax.broadcasted_iota(jnp.int32, (n_rows, GROUP_WIDTH), 0)
    lane_w = lax.broadcasted_iota(jnp.int32, (n_rows, GROUP_WIDTH), 1)
    own_head = (row_w // t_new) == (lane_w // HEAD_DIM)

    def fold_heads(x):
        x = jnp.where(own_head, x, 0.0)
        out = x[0:t_new]
        for h in range(1, HEADS_PER_GROUP):
            out = out + x[h * t_new:(h + 1) * t_new]
        return out

    caches = (c0_ref, c1_ref, c2_ref)
    cache_bias = (tb0_ref, tb1_ref, tb2_ref)
    outs = ((o0_ref, l0_ref), (o1_ref, l1_ref), (o2_ref, l2_ref))
    for b, g in [(b, g) for b in range(n_seq) for g in range(N_DIL_GROUPS)]:
        qkv_ref = (q0_ref, q1_ref, q2_ref)[g]
        new_k, new_v = kn_scr.at[b * N_DIL_GROUPS + g], vn_scr.at[b * N_DIL_GROUPS + g]
        q = qkv_ref[b, :, 0:GROUP_WIDTH] * QK_SCALE
        q_rows = jnp.where(own_head, jnp.concatenate([q] * HEADS_PER_GROUP, axis=0), 0.0).astype(BF16)
        new_k[...] = jnp.zeros_like(new_k)
        new_v[...] = jnp.zeros_like(new_v)
        new_k[0:t_new, :] = qkv_ref[b, :, GROUP_WIDTH:2 * GROUP_WIDTH]
        new_v[0:t_new, :] = qkv_ref[b, :, 2 * GROUP_WIDTH:3 * GROUP_WIDTH]
        cache = caches[g]
        k_t = cache[b, 0:GROUP_WIDTH, :].astype(BF16)
        v_t = cache[b, GROUP_WIDTH:2 * GROUP_WIDTH, :].astype(BF16)
        s_buf = _dot(q_rows, k_t) + cache_bias[g][...]
        s_new = _dot_nt(q_rows, new_k[...].astype(BF16)) + tn_ref[g]
        m = jnp.maximum(jnp.max(s_buf, axis=-1, keepdims=True), jnp.max(s_new, axis=-1, keepdims=True))
        p_buf = jnp.exp(s_buf - m)
        p_new = jnp.exp(s_new - m)
        den = jnp.sum(p_buf, axis=-1, keepdims=True) + jnp.sum(p_new, axis=-1, keepdims=True)
        o = _dot_nt(p_buf.astype(BF16), v_t) + _dot(p_new.astype(BF16), new_v[...].astype(BF16))
        o_ref, l_ref = outs[g]
        o_ref[b] = fold_heads(o / den)
        l_ref[b] = fold_heads(jnp.broadcast_to(m + jnp.log(den), (n_rows, GROUP_WIDTH)))


ATTN_SAMPLE_SEQS = 2


def _attn_sample(qkvs, caches, tbs, tn):
    batch, t_new, _ = qkvs[0].shape
    n_rows = HEADS_PER_GROUP * t_new
    n_seq = math.gcd(batch, ATTN_SAMPLE_SEQS)
    cache_specs = [pl.BlockSpec((n_seq,) + c.shape[1:], lambda b: (b, 0, 0)) for c in caches]
    qkv_spec = pl.BlockSpec((n_seq, t_new, GROUP_QKV), lambda b: (b, 0, 0))
    out_spec = pl.BlockSpec((n_seq, t_new, GROUP_WIDTH), lambda b: (b, 0, 0))
    out_sds = jax.ShapeDtypeStruct((batch, t_new, GROUP_WIDTH), F32)
    new_rows = pltpu.VMEM((n_seq * N_DIL_GROUPS, QBLOCK, GROUP_WIDTH), F32)
    return pl.pallas_call(
        functools.partial(_attn_sample_kernel, t_new=t_new, n_seq=n_seq),
        grid=(batch // n_seq,),
        in_specs=[*[qkv_spec] * N_DIL_GROUPS, *cache_specs, *[_resident(t.shape) for t in tbs],
                  _resident((N_DIL_GROUPS, n_rows, QBLOCK))],
        out_specs=[out_spec] * (2 * N_DIL_GROUPS),
        out_shape=[out_sds] * (2 * N_DIL_GROUPS),
        scratch_shapes=[new_rows, new_rows],
        compiler_params=_compiler_params("arbitrary"),
        name="attn_sample",
    )(*qkvs, *caches, *tbs, tn)


def _ssm_chunk_terms(chunk, log_dt, lam_re, lam_im, b_re, b_im, c_re, c_im):
    hi = lax.Precision.HIGHEST
    dt = jnp.exp(log_dt.astype(F32))[:, None]
    lr, li = lam_re.astype(F32), lam_im.astype(F32)
    mag = jnp.exp(lr * dt)
    ab_re, ab_im = mag * jnp.cos(li * dt), mag * jnp.sin(li * dt)
    g, n = lr.shape
    p = b_re.shape[-1]
    den = lr * lr + li * li
    nr, ni = ab_re - 1.0, ab_im
    coef_re = (nr * lr + ni * li) / den
    coef_im = (ni * lr - nr * li) / den
    br, bi = b_re.astype(F32), b_im.astype(F32)
    bb_re = coef_re[..., None] * br - coef_im[..., None] * bi
    bb_im = coef_re[..., None] * bi + coef_im[..., None] * br
    k = jnp.arange(chunk + 1, dtype=F32)[:, None, None]
    pw_mag = jnp.exp(k * (lr * dt)[None])
    pw_re, pw_im = pw_mag * jnp.cos(k * (li * dt)[None]), pw_mag * jnp.sin(k * (li * dt)[None])
    bt_re, bt_im = bb_re.transpose(0, 2, 1)[None], bb_im.transpose(0, 2, 1)[None]
    pk_re, pk_im = pw_re[:chunk, :, None, :], pw_im[:chunk, :, None, :]
    akb_re = pk_re * bt_re - pk_im * bt_im
    akb_im = pk_re * bt_im + pk_im * bt_re
    cr, ci = c_re.astype(F32)[None], c_im.astype(F32)[None]
    pe_re, pe_im = pw_re[1:, :, None, :], pw_im[1:, :, None, :]
    e_re = cr * pe_re - ci * pe_im
    e_im = cr * pe_im + ci * pe_re
    kern = (jnp.einsum('kgqn,gpn->kgqp', akb_re, cr[0], precision=hi)
            - jnp.einsum('kgqn,gpn->kgqp', akb_im, ci[0], precision=hi))
    rows = lambda x: x.reshape(chunk, g * p, x.shape[-1])
    return ((rows(akb_re), rows(akb_im)), (rows(e_re), rows(e_im)), rows(kern), (pw_re, pw_im))


SSM_BLOCK_GROUPS = 8
SSM_BLOCK_CH = SSM_BLOCK_GROUPS * SSM_GROUP
SSM_BLOCK_STATE = SSM_BLOCK_GROUPS * SSM_STATE
N_SSM_BLOCKS = N_SSM_GROUPS // SSM_BLOCK_GROUPS


def _ssm_operator_kernel(akbre_ref, akbim_ref, ere_ref, eim_ref, kern_ref,
                         wre_ref, wim_ref, m_ref, etre_ref, etim_ref, *, chunk):
    gb, p, n = SSM_BLOCK_GROUPS, SSM_GROUP, SSM_STATE

    def copies(width, count):
        src = lax.broadcasted_iota(jnp.int32, (width, count * width), 0)
        dst = lax.broadcasted_iota(jnp.int32, (width, count * width), 1)
        return jnp.where(src == dst % width, 1.0, 0.0).astype(BF16)

    def own_group(width):
        row = lax.broadcasted_iota(jnp.int32, (SSM_BLOCK_CH, gb * width), 0)
        col = lax.broadcasted_iota(jnp.int32, (SSM_BLOCK_CH, gb * width), 1)
        return row // p == col // width

    to_states, own_states = copies(n, gb), own_group(n)

    def over_states(piece):
        return jnp.where(own_states, _dot(piece.astype(BF16), to_states), 0.0).astype(BF16)

    for s in range(chunk):
        rows = slice(s * SSM_BLOCK_CH, (s + 1) * SSM_BLOCK_CH)
        wre_ref[0, rows, :] = over_states(akbre_ref[chunk - 1 - s, 0])
        wim_ref[0, rows, :] = over_states(akbim_ref[chunk - 1 - s, 0])
        etre_ref[0, rows, :] = over_states(ere_ref[s, 0])
        etim_ref[0, rows, :] = over_states(-eim_ref[s, 0])
    to_channels, own_channels = copies(p, gb), own_group(p)
    lags = [jnp.where(own_channels, _dot(kern_ref[k, 0].astype(BF16), to_channels), 0.0).astype(BF16)
            for k in range(chunk)]
    zero = jnp.zeros((SSM_BLOCK_CH, SSM_BLOCK_CH), BF16)
    for s in range(chunk):
        m_ref[0, s * SSM_BLOCK_CH:(s + 1) * SSM_BLOCK_CH, :] = jnp.concatenate(
            [zero] * s + lags[:chunk - s], axis=1)


def _ssm_block_operators(chunk, terms):
    (akb_re, akb_im), (e_re, e_im), kern, (pw_re, pw_im) = terms
    assert chunk <= kern.shape[0]
    a_re, a_im = pw_re[chunk], pw_im[chunk]
    nb, gb, p, n = N_SSM_BLOCKS, SSM_BLOCK_GROUPS, SSM_GROUP, SSM_STATE
    pieces = [x[:chunk].reshape(chunk, nb, SSM_BLOCK_CH, x.shape[-1])
              for x in (akb_re, akb_im, e_re, e_im, kern)]
    piece_spec = lambda x: pl.BlockSpec((chunk, 1) + x.shape[2:], lambda i: (0, i, 0, 0))
    out_spec = lambda shape: pl.BlockSpec((1,) + shape[1:], lambda i: (i, 0, 0))
    x_width = chunk * SSM_BLOCK_CH
    out_shapes = [(nb, x_width, gb * n), (nb, x_width, gb * n), (nb, x_width, x_width),
                  (nb, x_width, gb * n), (nb, x_width, gb * n)]
    ops = pl.pallas_call(
        functools.partial(_ssm_operator_kernel, chunk=chunk),
        grid=(nb,),
        in_specs=[piece_spec(x) for x in pieces],
        out_specs=[out_spec(s) for s in out_shapes],
        out_shape=[jax.ShapeDtypeStruct(s, BF16) for s in out_shapes],
        compiler_params=_compiler_params("arbitrary"),
        name="ssm_operators",
    )(*pieces)
    return ops, a_re.reshape(nb, 1, gb * n), a_im.reshape(nb, 1, gb * n)


def _shorter_chunk_operators(ops, terms, chunk):
    w_re, w_im, m, et_re, et_im = ops
    pw_re, pw_im = terms[3]
    width = chunk * SSM_BLOCK_CH
    full = m.shape[1]
    assert width <= full
    nb, states = N_SSM_BLOCKS, SSM_BLOCK_STATE
    ops = [w_re[:, full - width:], w_im[:, full - width:], m[:, :width, :width], et_re[:, :width], et_im[:, :width]]
    return ops, pw_re[chunk].reshape(nb, 1, states), pw_im[chunk].reshape(nb, 1, states)


SSM_X_WIDTH = SSM_CHUNK * SSM_BLOCK_CH


def _ssm_prompt_kernel(u_ref, wre_ref, wim_ref, m_ref, etre_ref, etim_ref, are_ref, aim_ref, d_ref,
                       y_ref, hre_ref, him_ref, x_scr, sre_scr, sim_scr, *, batch, chunks):
    phase, b = pl.program_id(1), pl.program_id(2)
    for t in range(SSM_CHUNK):
        x_scr[:, t * SSM_BLOCK_CH:(t + 1) * SSM_BLOCK_CH] = u_ref[0, t].astype(BF16)
    rows = pl.ds(b, chunks, stride=batch)
    state_blocks = SSM_BLOCK_STATE // LANES

    @pl.when(phase == 0)
    def _():
        x = x_scr[...]
        g_re, g_im = _dot(x, wre_ref[0]), _dot(x, wim_ref[0])
        for k in range(state_blocks):
            sre_scr[k, rows, :] = g_re[:, k * LANES:(k + 1) * LANES]
            sim_scr[k, rows, :] = g_im[:, k * LANES:(k + 1) * LANES]

    @pl.when((phase == 1) & (b == 0))
    def _():
        same_lanes = lambda ref, k: ref[0][:, k * LANES:(k + 1) * LANES]

        def step(c, carry):
            same_chunk = pl.ds(c * batch, batch)
            out = []
            for k, (h_re, h_im) in enumerate(carry):
                a_re, a_im = same_lanes(are_ref, k), same_lanes(aim_ref, k)
                g_re, g_im = sre_scr[k, same_chunk, :], sim_scr[k, same_chunk, :]
                sre_scr[k, same_chunk, :] = h_re
                sim_scr[k, same_chunk, :] = h_im
                out.append((a_re * h_re - a_im * h_im + g_re, a_re * h_im + a_im * h_re + g_im))
            return tuple(out)

        zero = jnp.zeros((batch, LANES), F32)
        final = lax.fori_loop(0, chunks, step, ((zero, zero),) * state_blocks)
        for k, (h_re, h_im) in enumerate(final):
            hre_ref[0, :, k * LANES:(k + 1) * LANES] = h_re
            him_ref[0, :, k * LANES:(k + 1) * LANES] = h_im

    @pl.when(phase == 1)
    def _():
        h_re = jnp.concatenate([sre_scr[k, rows, :] for k in range(state_blocks)], axis=1).astype(BF16)
        h_im = jnp.concatenate([sim_scr[k, rows, :] for k in range(state_blocks)], axis=1).astype(BF16)
        pair_w = 2 * SSM_BLOCK_CH
        for j in range(SSM_CHUNK // 2):
            cols = slice(j * pair_w, (j + 1) * pair_w)
            k_in = (j + 1) * pair_w
            yj = (_dot(x_scr[:, :k_in], m_ref[0, :k_in, cols])
                  + _dot_nt(h_re, etre_ref[0, cols, :]) + _dot_nt(h_im, etim_ref[0, cols, :]))
            for i in range(2):
                t = 2 * j + i
                y_ref[pl.ds(t, chunks, stride=SSM_CHUNK), :] = (
                    yj[:, i * SSM_BLOCK_CH:(i + 1) * SSM_BLOCK_CH] + d_ref[0] * u_ref[0, t])


def _ssm_prompt(u, ops, a_re, a_im, d, batch, seq):
    chunks = seq // SSM_CHUNK
    d = d.reshape(N_SSM_BLOCKS, 1, SSM_BLOCK_CH)
    largest = max(o.size for o in ops)
    op_spec = lambda arr: pl.BlockSpec((1,) + arr.shape[1:], lambda g, ph, b: (g, 0, 0),
                                       pipeline_mode=pl.Buffered(1 if arr.size == largest else 2))
    state_spec = pl.BlockSpec((1, batch, SSM_BLOCK_STATE), lambda g, ph, b: (g, 0, 0))
    state_sds = jax.ShapeDtypeStruct((N_SSM_BLOCKS, batch, SSM_BLOCK_STATE), F32)
    y, h_re, h_im = pl.pallas_call(
        functools.partial(_ssm_prompt_kernel, batch=batch, chunks=chunks),
        grid=(N_SSM_BLOCKS, 2, batch),
        in_specs=[pl.BlockSpec((1, SSM_CHUNK, chunks, SSM_BLOCK_CH), lambda g, ph, b: (b, 0, 0, g)),
                  *[op_spec(o) for o in ops], op_spec(a_re), op_spec(a_im), op_spec(d)],
        out_specs=[pl.BlockSpec((seq, SSM_BLOCK_CH), lambda g, ph, b: (b * ph, g)), state_spec, state_spec],
        out_shape=[jax.ShapeDtypeStruct((batch * seq, SSM_WIDTH), F32), state_sds, state_sds],
        scratch_shapes=[pltpu.VMEM((chunks, SSM_X_WIDTH), BF16),
                        pltpu.VMEM((SSM_BLOCK_STATE // LANES, batch * chunks, LANES), F32),
                        pltpu.VMEM((SSM_BLOCK_STATE // LANES, batch * chunks, LANES), F32)],
        compiler_params=_compiler_params("arbitrary", "arbitrary", "arbitrary"),
        name="ssm_prompt",
    )(u, *ops, a_re, a_im, d)

    def by_sequence(h):
        h = h.reshape(N_SSM_BLOCKS, batch, SSM_BLOCK_GROUPS, SSM_STATE).transpose(1, 0, 2, 3)
        return h.reshape(batch, N_SSM_GROUPS, SSM_STATE)

    return y, by_sequence(h_re), by_sequence(h_im)


def _ssm_sample_kernel(u_ref, wre_ref, wim_ref, m_ref, etre_ref, etim_ref, are_ref, aim_ref, d_ref,
                       h0re_ref, h0im_ref, y_ref, hre_ref, him_ref, *, t_new, batch):
    for nb in range(N_SSM_BLOCKS):
        ch = slice(nb * SSM_BLOCK_CH, (nb + 1) * SSM_BLOCK_CH)
        st = slice(nb * SSM_BLOCK_STATE, (nb + 1) * SSM_BLOCK_STATE)
        x = jnp.concatenate([u_ref[t * batch:(t + 1) * batch, ch] for t in range(t_new)], axis=1).astype(BF16)
        h_re, h_im = h0re_ref[:, st], h0im_ref[:, st]
        a_re, a_im = are_ref[nb], aim_ref[nb]
        hre_ref[:, st] = a_re * h_re - a_im * h_im + _dot(x, wre_ref[nb])
        him_ref[:, st] = a_re * h_im + a_im * h_re + _dot(x, wim_ref[nb])
        y = (_dot(x, m_ref[nb]) + _dot_nt(h_re.astype(BF16), etre_ref[nb])
             + _dot_nt(h_im.astype(BF16), etim_ref[nb]))
        for t in range(t_new):
            rows = slice(t * batch, (t + 1) * batch)
            y_ref[rows, ch] = y[:, t * SSM_BLOCK_CH:(t + 1) * SSM_BLOCK_CH] + d_ref[:, ch] * u_ref[rows, ch]


def _ssm_sample(u, ops, a_re, a_im, d, h0_re, h0_im, t_new):
    batch = u.shape[0] // t_new
    args = (u, *ops, a_re, a_im, d, h0_re, h0_im)
    state_sds = jax.ShapeDtypeStruct((batch, N_SSM_GROUPS * SSM_STATE), F32)
    out_shape = [jax.ShapeDtypeStruct(u.shape, F32), state_sds, state_sds]
    whole = lambda shape: pl.BlockSpec(shape, lambda i, nd=len(shape): (0,) * nd)
    return pl.pallas_call(
        functools.partial(_ssm_sample_kernel, t_new=t_new, batch=batch),
        grid=(1,),
        in_specs=[whole(a.shape) for a in args],
        out_specs=[whole(s.shape) for s in out_shape],
        out_shape=out_shape,
        compiler_params=_compiler_params("arbitrary"),
        name="ssm_sample",
    )(*args)


def _gelu_tanh(x):
    return 0.5 * x * (1.0 + jnp.tanh(math.sqrt(2.0 / math.pi) * (x + 0.044715 * (x * x * x))))


def _merge_tile(o0_ref, l0_ref, o1_ref, l1_ref, o2_ref, l2_ref, ys_ref, gate_ref, x_ref,
                wglu_ref, bglu_ref, wba_ref, wbs_ref, wout_ref, order_scr, tm, dils):
    def row_order(ref, dil, slot):
        if dil == 1:
            return ref[0, 0]
        n = tm // dil
        halves = GROUP_WIDTH // LANES
        for r in range(dil):
            for k in range(halves):
                order_scr[slot * halves + k, pl.ds(r, n, stride=dil), :] = ref[0, r, :, k * LANES:(k + 1) * LANES]
        return jnp.concatenate([order_scr[slot * halves + k] for k in range(halves)], axis=1)

    parts = [row_order(ref, dils[i // 2], i) for i, ref in
             enumerate((o0_ref, l0_ref, o1_ref, l1_ref, o2_ref, l2_ref))]
    o0, l0, o1, l1, o2, l2 = parts
    mx = jnp.maximum(jnp.maximum(l0, l1), l2)
    e0, e1, e2 = jnp.exp(l0 - mx), jnp.exp(l1 - mx), jnp.exp(l2 - mx)
    attn = (e0 * o0 + e1 * o1 + e2 * o2) / (e0 + e1 + e2)
    branch_a = _dot(attn.astype(BF16), wba_ref[...])
    y = _gelu_tanh(ys_ref[...])
    y = y * _sigmoid(_dot(y.astype(BF16), wglu_ref[...]) + bglu_ref[...])
    branch_s = _dot(y.astype(BF16), wbs_ref[...])
    mix = (gate_ref[:, 0:D_MODEL].astype(F32) * branch_a
           + gate_ref[:, D_MODEL:2 * D_MODEL].astype(F32) * branch_s)
    return x_ref[...] + _dot(mix.astype(BF16), wout_ref[...])


MXU_WIDTH = 256
FF_SPLIT = -(-D_FF // (2 * MXU_WIDTH)) * MXU_WIDTH
FF_CHUNKS = ((0, FF_SPLIT), (FF_SPLIT, D_FF))
N_MERGE_ROW_INPUTS = 2 * N_DIL_GROUPS + 3
N_MERGE_WEIGHTS = 5


def _merge_ffn_kernel(*refs, tm, shift, pad, dils):
    n_merge = N_MERGE_ROW_INPUTS + N_MERGE_WEIGHTS
    merge_refs, rest = refs[:n_merge], refs[n_merge:]
    carry_ref, g2_ref, wup_ref, cw_ref, cb_ref, wdn_ref, gf_ref, y_ref, state_ref, order_scr, a_scr = rest
    hist = 2 * shift

    @pl.when(pl.program_id(1) == 0)
    def _():
        a_scr[pad - hist:pad, :] = carry_ref[0]

    xf = _merge_tile(*merge_refs, order_scr, tm, dils)
    xn = _rmsnorm(xf, g2_ref[...]).astype(BF16)
    acc = jnp.zeros((tm, D_MODEL), F32)
    for c0, c1 in FF_CHUNKS:
        cols = slice(c0, c1)
        a = _dot(xn, wup_ref[:, cols])
        val = _dot(xn, wup_ref[:, D_FF + c0:D_FF + c1])
        a_scr[pad:pad + tm, cols] = a
        a_m1 = a_scr[pad - shift:pad - shift + tm, cols]
        a_m2 = a_scr[pad - hist:pad - hist + tm, cols]
        conv = cb_ref[:, cols] + cw_ref[0:1, cols] * a_m2
        conv = conv + cw_ref[1:2, cols] * a_m1
        conv = conv + cw_ref[2:3, cols] * a
        act = conv * _sigmoid(conv) * val
        acc = acc + _dot(act.astype(BF16), wdn_ref[cols, :])
    tail = a_scr[pad + tm - hist:pad + tm, :]
    a_scr[pad - hist:pad, :] = tail
    state_ref[0] = tail
    y_ref[...] = _rmsnorm(xf + acc, gf_ref[...])


def _merge_ffn(attn_parts, ys, gates, x2d, merge_weights, carry, ffn_weights, n_seq, tm, shift):
    m = x2d.shape[0]
    tiles = m // n_seq // tm
    hist = 2 * shift
    pad = -(-hist // SUBLANES) * SUBLANES
    dils = tuple(p.shape[1] for p in attn_parts[::2])
    row = lambda width: pl.BlockSpec((tm, width), lambda b, j: (b * tiles + j, 0))
    res_spec = lambda dil: pl.BlockSpec((1, dil, tm // dil, GROUP_WIDTH), lambda b, j: (b, 0, j, 0))
    state_spec = pl.BlockSpec((1, hist, D_FF), lambda b, j: (b, 0, 0))
    assert len(merge_weights) == N_MERGE_WEIGHTS
    return pl.pallas_call(
        functools.partial(_merge_ffn_kernel, tm=tm, shift=shift, pad=pad, dils=dils),
        grid=(n_seq, tiles),
        in_specs=[*[res_spec(p.shape[1]) for p in attn_parts],
                  row(SSM_WIDTH), row(2 * D_MODEL), row(D_MODEL),
                  *[_resident(w.shape) for w in merge_weights],
                  state_spec, *[_resident(w.shape) for w in ffn_weights]],
        out_specs=[row(D_MODEL), state_spec],
        out_shape=[jax.ShapeDtypeStruct((m, D_MODEL), F32),
                   jax.ShapeDtypeStruct((n_seq, hist, D_FF), F32)],
        scratch_shapes=[pltpu.VMEM((len(attn_parts) * GROUP_WIDTH // LANES, tm, LANES), F32),
                        pltpu.VMEM((pad + tm, D_FF), F32)],
        compiler_params=_compiler_params("arbitrary", "arbitrary"),
        name="merge_ffn",
    )(*attn_parts, ys, gates, x2d, *merge_weights, carry, *ffn_weights)


def _kv_rows(qkv, keep):
    batch, dil, length, _ = qkv.shape
    n = keep // dil
    rows = qkv[:, :, length - n:, GROUP_WIDTH:]
    cols = rows.transpose(0, 3, 2, 1).reshape(batch, 2, HEADS_PER_GROUP, HEAD_DIM, keep)
    return cols.transpose(0, 4, 1, 2, 3)


def _prompt_layer(x, rel_bias, lw):
    batch, seq, _ = x.shape
    x2d = x.reshape(batch * seq, D_MODEL)
    dils = tuple(dil for _, dil in DIL_PATTERNS) + (SSM_CHUNK,)
    keeps = tuple(min(window, seq) for window, _ in DIL_PATTERNS)
    *qkvs, u, gates, kv0, kv1, kv2 = _in_proj(x2d, lw['norm1_g'], lw['w_in'], batch, IN_PROJ_TILE, dils, keeps)

    attn_parts, kv_new = [], []
    for g, (window, dil) in enumerate(DIL_PATTERNS):
        tab = rel_bias[:, g * HEADS_PER_GROUP:(g + 1) * HEADS_PER_GROUP]
        attn_parts.extend(_attn_prompt(qkvs[g], _prompt_bias(tab, dil)))
        tail = (kv0, kv1, kv2)[g].reshape(batch, 2, HEADS_PER_GROUP, HEAD_DIM, keeps[g])
        kv_new.append(tail.transpose(0, 4, 1, 2, 3))

    ops, a_re, a_im = lw['ssm_ops']
    ys, h_re, h_im = _ssm_prompt(u, ops, a_re, a_im, lw['ssm_d'], batch, seq)

    carry = jnp.zeros((batch, CONV_W - 1, D_FF), F32)
    y, conv_state = _merge_ffn(attn_parts, ys, gates, x2d, lw['merge'], carry, lw['ffn'],
                               n_seq=batch, tm=MERGE_FFN_TILE, shift=1)
    return y.reshape(batch, seq, D_MODEL), (*kv_new, h_re, h_im, conv_state)


def _sample_layer(x, caches, h0_re, h0_im, conv_buf, rel_bias, lw):
    batch, t_new, _ = x.shape
    m = batch * t_new
    x2d = x.transpose(1, 0, 2).reshape(m, D_MODEL)
    *qkvs, u, gates = _in_proj(x2d, lw['norm1_g'], lw['w_in'], 1, m, (1,) * (N_DIL_GROUPS + 1))
    u = u.reshape(m, SSM_WIDTH)
    qkvs_bt = [q.reshape(t_new, batch, GROUP_QKV).transpose(1, 0, 2) for q in qkvs]

    tbs, tns, views = [], [], []
    for g, (window, dil) in enumerate(DIL_PATTERNS):
        tab = rel_bias[:, g * HEADS_PER_GROUP:(g + 1) * HEADS_PER_GROUP]
        n_cached = caches[g].shape[1]
        tb, tn = _sample_bias(tab, dil, t_new, n_cached)
        tbs.append(tb)
        tns.append(tn)
        views.append(caches[g].transpose(0, 2, 3, 4, 1).reshape(batch, 2 * GROUP_WIDTH, n_cached))
    parts = _attn_sample(qkvs_bt, views, tbs, jnp.stack(tns))
    attn_parts = [p.transpose(1, 0, 2).reshape(1, 1, m, GROUP_WIDTH) for p in parts]
    kv_new = [_kv_rows(q.reshape(batch, 1, t_new, GROUP_QKV), t_new) for q in qkvs_bt]

    ops, a_re, a_im = _shorter_chunk_operators(lw['ssm_ops'][0], lw['ssm_terms'], t_new)
    ys, h_re, h_im = _ssm_sample(u, ops, a_re, a_im, lw['ssm_d'],
                                 h0_re.reshape(batch, -1), h0_im.reshape(batch, -1), t_new)

    carry = conv_buf.transpose(1, 0, 2).reshape(1, (CONV_W - 1) * batch, D_FF)
    y, conv_state = _merge_ffn(attn_parts, ys, gates, x2d, lw['merge'], carry, lw['ffn'],
                               n_seq=1, tm=m, shift=batch)
    y = y.reshape(t_new, batch, D_MODEL).transpose(1, 0, 2)
    conv_state = conv_state.reshape(CONV_W - 1, batch, D_FF).transpose(1, 0, 2)
    state_shape = (batch, N_SSM_GROUPS, SSM_STATE)
    return y, (*kv_new, h_re.reshape(state_shape), h_im.reshape(state_shape), conv_state)


IN_PROJ_TILE = 512
MERGE_FFN_TILE = 512


def kernel(x_prompt, x_sample, cache_kv_w128, cache_kv_w512, cache_kv_w2048, state_ssm_re, state_ssm_im, state_ffn_conv, rel_bias, norm1_g, w_in, ssm_log_dt, ssm_lambda_re, ssm_lambda_im, ssm_b_re, ssm_b_im, ssm_c_re, ssm_c_im, ssm_d, w_glu, b_glu, w_branch_attn, w_branch_ssm, w_out, norm2_g, w_up, conv_w, conv_b, w_down, norm_f_g):
    depth = w_in.shape[0]
    hp, hs = x_prompt, x_sample
    st_p, st_s = [], []
    gf = norm_f_g.reshape(1, D_MODEL)
    for l in range(depth):
        last = l == depth - 1
        assert x_sample.shape[1] <= SSM_CHUNK, "the sample chunk operators are cut out of the prompt's"
        ssm_terms = _ssm_chunk_terms(SSM_CHUNK, ssm_log_dt[l], ssm_lambda_re[l], ssm_lambda_im[l],
                                     ssm_b_re[l], ssm_b_im[l], ssm_c_re[l], ssm_c_im[l])
        lw = {
            'norm1_g': norm1_g[l].reshape(1, D_MODEL),
            'w_in': _group_major_columns(w_in[l]).astype(BF16),
            'ssm_terms': ssm_terms,
            'ssm_ops': _ssm_block_operators(SSM_CHUNK, ssm_terms),
            'ssm_d': ssm_d[l].reshape(1, SSM_WIDTH),
            'merge': (w_glu[l].astype(BF16), b_glu[l].reshape(1, SSM_WIDTH),
                      w_branch_attn[l].astype(BF16), w_branch_ssm[l].astype(BF16), w_out[l].astype(BF16)),
            'ffn': (norm2_g[l].reshape(1, D_MODEL), w_up[l].astype(BF16), conv_w[l],
                    conv_b[l].reshape(1, D_FF), w_down[l].astype(BF16), gf),
        }
        assert last, "the final RMSNorm is fused into the last layer's ffn kernel"
        hp, sp = _prompt_layer(hp, rel_bias, lw)
        hs, ss = _sample_layer(hs, (cache_kv_w128[l], cache_kv_w512[l], cache_kv_w2048[l]),
                               state_ssm_re[l], state_ssm_im[l], state_ffn_conv[l], rel_bias, lw)
        st_p.append(sp)
        st_s.append(ss)
    stack = lambda states, i: jnp.stack([st[i] for st in states], axis=0)
    return (hp, hs, *[stack(st_p, i) for i in range(6)], *[stack(st_s, i) for i in range(6)])
```

```python
import functools
import math

import jax
import jax.numpy as jnp
import numpy as np
from jax import lax
from jax.experimental import pallas as pl
from jax.experimental.pallas import tpu as pltpu

F32 = jnp.float32
BF16 = jnp.bfloat16

D_MODEL = 1024
HEAD_DIM = 64
HEADS_PER_GROUP = 4
DIL_PATTERNS = ((128, 1), (512, 4), (2048, 16))
N_DIL_GROUPS = len(DIL_PATTERNS)
GROUP_WIDTH = HEADS_PER_GROUP * HEAD_DIM
QK_WIDTH = N_DIL_GROUPS * GROUP_WIDTH
QKV_WIDTH = 3 * QK_WIDTH
GROUP_QKV = 3 * GROUP_WIDTH
QBLOCK = 128
SSM_GROUP = 16
SSM_STATE = 64
SSM_WIDTH = D_MODEL // 2
N_SSM_GROUPS = SSM_WIDTH // SSM_GROUP
SSM_CHUNK = 16
D_FF = 2816
CONV_W = 3
N_BUCKETS = 32
MAX_DISTANCE = 2048
NORM_EPS = 1e-6
NEG_INF = -1e30
U_START = QKV_WIDTH
GATE_START = U_START + SSM_WIDTH
IN_WIDTH = GATE_START + 2 * D_MODEL
QK_SCALE = HEAD_DIM ** -0.5

VMEM_LIMIT_BYTES = 60 * 1024 * 1024
SUBLANES = 8
LANES = 128


def _compiler_params(*semantics):
    return pltpu.CompilerParams(dimension_semantics=semantics, vmem_limit_bytes=VMEM_LIMIT_BYTES)


def _resident(shape):
    nd = len(shape)
    return pl.BlockSpec(shape, lambda *_: (0,) * nd, pipeline_mode=pl.Buffered(1))


def _rmsnorm(xf, g):
    y = xf * lax.rsqrt(jnp.mean(xf * xf, axis=-1, keepdims=True) + NORM_EPS)
    return y * g


def _sigmoid(x):
    return 1.0 / (1.0 + jnp.exp(-x))


def _dot(a, b):
    return jnp.dot(a, b, preferred_element_type=F32)


def _dot_nt(a, b):
    return lax.dot_general(a, b, (((1,), (1,)), ((), ())), preferred_element_type=F32)


def _in_proj_kernel(x_ref, g_ref, w_ref, qkv0_ref, qkv1_ref, qkv2_ref, u_ref, gate_ref, *rest,
                    tm, dils, tails, tiles):
    if tails is None:
        tail_refs, (xn_scr,) = (), rest
    else:
        tail_refs, (xn_scr, tok_scr) = rest[:N_DIL_GROUPS], rest[N_DIL_GROUPS:]
    xf = _rmsnorm(x_ref[...], g_ref[...])
    n_lane_blocks = D_MODEL // LANES
    if any(dil > 1 for dil in dils):
        for k in range(n_lane_blocks):
            xn_scr[k] = xf[:, k * LANES:(k + 1) * LANES]
    by_residue = {1: xf.astype(BF16)}

    def rows_by_residue(dil):
        if dil not in by_residue:
            n = tm // dil
            xr = jnp.concatenate(
                [jnp.concatenate([xn_scr[k, pl.ds(r, n, stride=dil), :] for k in range(n_lane_blocks)], axis=1)
                 for r in range(dil)], axis=0)
            by_residue[dil] = xr.astype(BF16)
        return by_residue[dil]

    outputs = [(ref, g * GROUP_QKV, GROUP_QKV) for g, ref in enumerate((qkv0_ref, qkv1_ref, qkv2_ref))]
    outputs.append((u_ref, U_START, SSM_WIDTH))
    for c0 in range(0, 2 * D_MODEL, D_MODEL):
        logits = _dot(by_residue[1], w_ref[:, GATE_START + c0:GATE_START + c0 + D_MODEL])
        gate_ref[:, c0:c0 + D_MODEL] = _sigmoid(logits).astype(BF16)
    for (ref, start, width), dil in zip(outputs, dils):
        n = tm // dil
        res = _dot(rows_by_residue(dil), w_ref[:, start:start + width])
        for r in range(dil):
            ref[0, r] = res[r * n:(r + 1) * n]
    if tails is None:
        return
    kv_blocks = 2 * GROUP_WIDTH // LANES
    for (ref, _, _), dil, keep, tail_ref in zip(outputs, dils, tails, tail_refs):
        n = tm // dil
        kept = min(keep, tm)

        @pl.when(pl.program_id(1) >= tiles - max(keep // tm, 1))
        def _(ref=ref, dil=dil, n=n, kept=kept, tail_ref=tail_ref):
            if dil == 1:
                kv = ref[0, 0, :, GROUP_WIDTH:]
            else:
                for r in range(dil):
                    for k in range(kv_blocks):
                        lanes = slice(GROUP_WIDTH + k * LANES, GROUP_WIDTH + (k + 1) * LANES)
                        tok_scr[k, pl.ds(r, n, stride=dil), :] = ref[0, r, :, lanes]
                kv = jnp.concatenate([tok_scr[k] for k in range(kv_blocks)], axis=1)
            tail_ref[0] = kv[tm - kept:, :].T


def _group_major_columns(w_in):
    parts = []
    for g in range(N_DIL_GROUPS):
        for base in (0, QK_WIDTH, 2 * QK_WIDTH):
            parts.append(w_in[:, base + g * GROUP_WIDTH:base + (g + 1) * GROUP_WIDTH])
    parts.append(w_in[:, U_START:])
    return jnp.concatenate(parts, axis=1)


def _in_proj(x2d, g, w_bf16, n_seq, tm, dils, tails=None):
    m = x2d.shape[0]
    seq = m // n_seq
    tiles = seq // tm
    row = lambda width: pl.BlockSpec((tm, width), lambda b, j: (b * tiles + j, 0))
    res_spec = lambda dil, width: pl.BlockSpec((1, dil, tm // dil, width), lambda b, j: (b, 0, j, 0))
    widths = (GROUP_QKV,) * N_DIL_GROUPS + (SSM_WIDTH,)
    out_specs = [*[res_spec(d, w) for d, w in zip(dils, widths)], row(2 * D_MODEL)]
    out_shape = [*[jax.ShapeDtypeStruct((n_seq, d, seq // d, w), F32) for d, w in zip(dils, widths)],
                 jax.ShapeDtypeStruct((m, 2 * D_MODEL), BF16)]
    scratch = [pltpu.VMEM((D_MODEL // LANES, tm, LANES), F32)]
    if tails is not None:
        for keep in tails:
            assert keep % tm == 0 or tm % keep == 0
            first = tiles - max(keep // tm, 1)
            out_specs.append(pl.BlockSpec((1, 2 * GROUP_WIDTH, min(keep, tm)),
                                          lambda b, j, first=first: (b, 0, jnp.maximum(j - first, 0))))
            out_shape.append(jax.ShapeDtypeStruct((n_seq, 2 * GROUP_WIDTH, keep), F32))
        scratch.append(pltpu.VMEM((2 * GROUP_WIDTH // LANES, tm, LANES), F32))
    return pl.pallas_call(
        functools.partial(_in_proj_kernel, tm=tm, dils=dils, tails=tails, tiles=tiles),
        grid=(n_seq, tiles),
        in_specs=[row(D_MODEL), _resident((1, D_MODEL)), _resident((D_MODEL, IN_WIDTH))],
        out_specs=out_specs,
        out_shape=out_shape,
        scratch_shapes=scratch,
        compiler_params=_compiler_params("arbitrary", "arbitrary"),
        name="in_proj",
    )(x2d, g, w_bf16)


def _bucket_starts():
    max_exact = N_BUCKETS // 2
    n = np.arange(max_exact, MAX_DISTANCE + 1)
    large = max_exact + (np.log(n.astype(np.float32) / np.float32(max_exact))
                         / np.float32(math.log(MAX_DISTANCE / max_exact))
                         * np.float32(N_BUCKETS - max_exact)).astype(np.int32)
    large = np.minimum(large, N_BUCKETS - 1)
    return [int(n[np.argmax(large >= k)]) for k in range(max_exact + 1, N_BUCKETS)]


def _rel_bucket(dist):
    max_exact = N_BUCKETS // 2
    n = jnp.maximum(dist, 0)
    large = max_exact + sum((n >= start).astype(jnp.int32) for start in _bucket_starts())
    return jnp.where(n < max_exact, n, large)


def _masked_bias(tab, strides, valid, dil):
    n_dist = QBLOCK
    bucket = _rel_bucket(jnp.clip(strides, 0, n_dist) * dil).reshape(1, -1)
    onehot = (bucket == jnp.arange(N_BUCKETS)[:, None]).astype(F32)
    bias = jnp.dot(tab.astype(F32).T, onehot, precision=lax.Precision.HIGHEST)
    bias = jnp.where(valid.reshape(1, -1), bias, NEG_INF)
    return bias.reshape((tab.shape[1],) + strides.shape)


def _prompt_bias(tab, dil):
    qi = jnp.arange(QBLOCK)[:, None]
    ki = jnp.arange(QBLOCK)[None, :]
    j_prev = qi + QBLOCK - ki
    j_cur = qi - ki
    bias = jnp.stack([_masked_bias(tab, j_prev, j_prev <= QBLOCK, dil),
                      _masked_bias(tab, j_cur, j_cur >= 0, dil)], axis=0)
    return bias.transpose(0, 2, 1, 3).reshape(2, QBLOCK, HEADS_PER_GROUP * QBLOCK)


def _sample_bias(tab, dil, t_new, n_cached):
    t = jnp.arange(t_new)[:, None]
    delta_buf = n_cached + t - jnp.arange(n_cached)[None, :]
    ok_buf = (delta_buf % dil == 0) & (delta_buf // dil <= QBLOCK)
    m = jnp.arange(QBLOCK)[None, :]
    delta_new = t - m
    ok_new = (m < t_new) & (delta_new >= 0) & (delta_new % dil == 0)
    b_buf = _masked_bias(tab, delta_buf // dil, ok_buf, dil)
    b_new = _masked_bias(tab, delta_new // dil, ok_new, dil)
    return (b_buf.reshape(HEADS_PER_GROUP * t_new, n_cached),
            b_new.reshape(HEADS_PER_GROUP * t_new, QBLOCK))


def _attn_prompt_kernel(q_ref, kc_ref, vc_ref, bias_ref, o_ref, lse_ref, kprev_scr, vprev_scr, *, nq, n_res):
    first_tile = pl.program_id(1) == 0

    @pl.when(first_tile)
    def _():
        kprev_scr[...] = jnp.zeros_like(kprev_scr)
        vprev_scr[...] = jnp.zeros_like(vprev_scr)

    stacked = (HEADS_PER_GROUP * QBLOCK, GROUP_WIDTH)
    own_head = (lax.broadcasted_iota(jnp.int32, stacked, 0) // QBLOCK
                == lax.broadcasted_iota(jnp.int32, stacked, 1) // HEAD_DIM)
    lane_head = lax.broadcasted_iota(jnp.int32, (QBLOCK, GROUP_WIDTH), 1) // HEAD_DIM

    def per_head(x):
        xb = x.astype(BF16)
        return jnp.where(own_head, jnp.concatenate([xb] * HEADS_PER_GROUP, axis=0), 0)

    def on_head_lanes(cols):
        out = jnp.broadcast_to(cols[-1], (QBLOCK, GROUP_WIDTH))
        for h in range(HEADS_PER_GROUP - 2, -1, -1):
            out = jnp.where(lane_head == h, cols[h], out)
        return out

    block_rows = [slice(i * QBLOCK, (i + 1) * QBLOCK) for i in range(nq)]

    def users(per_query_cur, per_query_prev, j):
        parts = ([per_query_cur[j - 1]] if j >= 1 else []) + ([per_query_prev[j]] if j < nq else [])
        return parts[0] if len(parts) == 1 else jnp.concatenate(parts, axis=0)

    for res in range(n_res):
        _attend_sequence(q_ref.at[res], kc_ref.at[res], vc_ref.at[res], bias_ref,
                         o_ref.at[res], lse_ref.at[res], kprev_scr.at[res], vprev_scr.at[res],
                         block_rows, per_head, on_head_lanes, users, first_tile)


def _attend_sequence(q_ref, kc_ref, vc_ref, bias_ref, o_ref, lse_ref, kprev_scr, vprev_scr,
                     block_rows, per_head, on_head_lanes, users, first_tile):
    nq = len(block_rows)
    qs = [(q_ref[rows, :] * QK_SCALE).astype(BF16) for rows in block_rows]
    def k_block(j):
        return kprev_scr[...] if j == 0 else per_head(kc_ref[block_rows[j - 1], :])

    def v_block(j):
        return vprev_scr[...] if j == 0 else per_head(vc_ref[block_rows[j - 1], :])

    s_prev, s_cur = [None] * nq, [None] * nq
    for j in range(nq + 1):
        k_stack = k_block(j)
        if j == nq:
            kprev_scr[...] = k_stack
        s = _dot_nt(users(qs, qs, j), k_stack)
        if j >= 1:
            s_cur[j - 1] = s[:QBLOCK] + bias_ref[1]
        if j < nq:
            s_prev[j] = s[-QBLOCK:] + bias_ref[0]
    s_prev[0] = jnp.where(first_tile, NEG_INF, s_prev[0])

    p_prev, p_cur, dens, lses = [], [], [], []
    for i in range(nq):
        pp_heads, pc_heads, den_heads, lse_heads = [], [], [], []
        for h in range(HEADS_PER_GROUP):
            keys = slice(h * QBLOCK, (h + 1) * QBLOCK)
            sp, sc = s_prev[i][:, keys], s_cur[i][:, keys]
            m = jnp.max(jnp.maximum(sp, sc), axis=-1, keepdims=True)
            pp, pc = jnp.exp(sp - m), jnp.exp(sc - m)
            den = jnp.sum(pp + pc, axis=-1, keepdims=True)
            pp_heads.append(pp.astype(BF16))
            pc_heads.append(pc.astype(BF16))
            den_heads.append(den)
            lse_heads.append(m + jnp.log(den))
        p_prev.append(jnp.concatenate(pp_heads, axis=1))
        p_cur.append(jnp.concatenate(pc_heads, axis=1))
        dens.append(on_head_lanes(den_heads))
        lses.append(on_head_lanes(lse_heads))

    o = [None] * nq
    for j in range(nq + 1):
        v_stack = v_block(j)
        if j == nq:
            vprev_scr[...] = v_stack
        r = _dot(users(p_cur, p_prev, j), v_stack)
        if j >= 1:
            o[j - 1] = o[j - 1] + r[:QBLOCK]
        if j < nq:
            o[j] = r[-QBLOCK:]
    for i, rows in enumerate(block_rows):
        o_ref[rows, :] = o[i] / dens[i]
        lse_ref[rows, :] = lses[i]


ATTN_QUERY_BLOCKS = 8
ATTN_STEP_BLOCKS = 16


def _attn_prompt(qkv, bias):
    batch, dil, length, _ = qkv.shape
    n_seq = batch * dil
    tq = min(ATTN_QUERY_BLOCKS * QBLOCK, length)
    nq = tq // QBLOCK
    n_res = math.gcd(n_seq, max(ATTN_STEP_BLOCKS // nq, 1))
    seqs = qkv.reshape(n_seq, length, GROUP_QKV)

    def cur(col):
        return pl.BlockSpec((n_res, tq, GROUP_WIDTH), lambda s, n: (s, n, col))

    stacked = pltpu.VMEM((n_res, HEADS_PER_GROUP * QBLOCK, GROUP_WIDTH), BF16)
    out_sds = jax.ShapeDtypeStruct((n_seq, length, GROUP_WIDTH), F32)
    o, lse = pl.pallas_call(
        functools.partial(_attn_prompt_kernel, nq=nq, n_res=n_res),
        grid=(n_seq // n_res, length // tq),
        in_specs=[cur(0), cur(1), cur(2), _resident((2, QBLOCK, HEADS_PER_GROUP * QBLOCK))],
        out_specs=[cur(0), cur(0)],
        out_shape=[out_sds, out_sds],
        scratch_shapes=[stacked, stacked],
        compiler_params=_compiler_params("arbitrary", "arbitrary"),
        name=f"attn_prompt_d{dil}",
    )(seqs, seqs, seqs, bias)
    out_shape = (batch, dil, length, GROUP_WIDTH)
    return o.reshape(out_shape), lse.reshape(out_shape)


def _attn_sample_kernel(q0_ref, q1_ref, q2_ref, c0_ref, c1_ref, c2_ref, tb0_ref, tb1_ref, tb2_ref, tn_ref,
                        o0_ref, l0_ref, o1_ref, l1_ref, o2_ref, l2_ref, kn_scr, vn_scr, *, t_new, n_seq):
    n_rows = HEADS_PER_GROUP * t_new
    row_w = lax.broadcasted_iota(jnp.int32, (n_rows, GROUP_WIDTH), 0)
    lane_w = lax.broadcasted_iota(jnp.int32, (n_rows, GROUP_WIDTH), 1)
    own_head = (row_w // t_new) == (lane_w // HEAD_DIM)

    def fold_heads(x):
        x = jnp.where(own_head, x, 0.0)
        out = x[0:t_new]
        for h in range(1, HEADS_PER_GROUP):
            out = out + x[h * t_new:(h + 1) * t_new]
        return out

    caches = (c0_ref, c1_ref, c2_ref)
    cache_bias = (tb0_ref, tb1_ref, tb2_ref)
    outs = ((o0_ref, l0_ref), (o1_ref, l1_ref), (o2_ref, l2_ref))
    for b, g in [(b, g) for b in range(n_seq) for g in range(N_DIL_GROUPS)]:
        qkv_ref = (q0_ref, q1_ref, q2_ref)[g]
        new_k, new_v = kn_scr.at[b * N_DIL_GROUPS + g], vn_scr.at[b * N_DIL_GROUPS + g]
        q = qkv_ref[b, :, 0:GROUP_WIDTH] * QK_SCALE
        q_rows = jnp.where(own_head, jnp.concatenate([q] * HEADS_PER_GROUP, axis=0), 0.0).astype(BF16)
        new_k[...] = jnp.zeros_like(new_k)
        new_v[...] = jnp.zeros_like(new_v)
        new_k[0:t_new, :] = qkv_ref[b, :, GROUP_WIDTH:2 * GROUP_WIDTH]
        new_v[0:t_new, :] = qkv_ref[b, :, 2 * GROUP_WIDTH:3 * GROUP_WIDTH]
        cache = caches[g]
        k_t = cache[b, 0:GROUP_WIDTH, :].astype(BF16)
        v_t = cache[b, GROUP_WIDTH:2 * GROUP_WIDTH, :].astype(BF16)
        s_buf = _dot(q_rows, k_t) + cache_bias[g][...]
        s_new = _dot_nt(q_rows, new_k[...].astype(BF16)) + tn_ref[g]
        m = jnp.maximum(jnp.max(s_buf, axis=-1, keepdims=True), jnp.max(s_new, axis=-1, keepdims=True))
        p_buf = jnp.exp(s_buf - m)
        p_new = jnp.exp(s_new - m)
        den = jnp.sum(p_buf, axis=-1, keepdims=True) + jnp.sum(p_new, axis=-1, keepdims=True)
        o = _dot_nt(p_buf.astype(BF16), v_t) + _dot(p_new.astype(BF16), new_v[...].astype(BF16))
        o_ref, l_ref = outs[g]
        o_ref[b] = fold_heads(o / den)
        l_ref[b] = fold_heads(jnp.broadcast_to(m + jnp.log(den), (n_rows, GROUP_WIDTH)))


ATTN_SAMPLE_SEQS = 2


def _attn_sample(qkvs, caches, tbs, tn):
    batch, t_new, _ = qkvs[0].shape
    n_rows = HEADS_PER_GROUP * t_new
    n_seq = math.gcd(batch, ATTN_SAMPLE_SEQS)
    cache_specs = [pl.BlockSpec((n_seq,) + c.shape[1:], lambda b: (b, 0, 0)) for c in caches]
    qkv_spec = pl.BlockSpec((n_seq, t_new, GROUP_QKV), lambda b: (b, 0, 0))
    out_spec = pl.BlockSpec((n_seq, t_new, GROUP_WIDTH), lambda b: (b, 0, 0))
    out_sds = jax.ShapeDtypeStruct((batch, t_new, GROUP_WIDTH), F32)
    new_rows = pltpu.VMEM((n_seq * N_DIL_GROUPS, QBLOCK, GROUP_WIDTH), F32)
    return pl.pallas_call(
        functools.partial(_attn_sample_kernel, t_new=t_new, n_seq=n_seq),
        grid=(batch // n_seq,),
        in_specs=[*[qkv_spec] * N_DIL_GROUPS, *cache_specs, *[_resident(t.shape) for t in tbs],
                  _resident((N_DIL_GROUPS, n_rows, QBLOCK))],
        out_specs=[out_spec] * (2 * N_DIL_GROUPS),
        out_shape=[out_sds] * (2 * N_DIL_GROUPS),
        scratch_shapes=[new_rows, new_rows],
        compiler_params=_compiler_params("arbitrary"),
        name="attn_sample",
    )(*qkvs, *caches, *tbs, tn)


def _ssm_chunk_terms(chunk, log_dt, lam_re, lam_im, b_re, b_im, c_re, c_im):
    hi = lax.Precision.HIGHEST
    dt = jnp.exp(log_dt.astype(F32))[:, None]
    lr, li = lam_re.astype(F32), lam_im.astype(F32)
    mag = jnp.exp(lr * dt)
    ab_re, ab_im = mag * jnp.cos(li * dt), mag * jnp.sin(li * dt)
    g, n = lr.shape
    p = b_re.shape[-1]
    den = lr * lr + li * li
    nr, ni = ab_re - 1.0, ab_im
    coef_re = (nr * lr + ni * li) / den
    coef_im = (ni * lr - nr * li) / den
    br, bi = b_re.astype(F32), b_im.astype(F32)
    bb_re = coef_re[..., None] * br - coef_im[..., None] * bi
    bb_im = coef_re[..., None] * bi + coef_im[..., None] * br
    k = jnp.arange(chunk + 1, dtype=F32)[:, None, None]
    pw_mag = jnp.exp(k * (lr * dt)[None])
    pw_re, pw_im = pw_mag * jnp.cos(k * (li * dt)[None]), pw_mag * jnp.sin(k * (li * dt)[None])
    bt_re, bt_im = bb_re.transpose(0, 2, 1)[None], bb_im.transpose(0, 2, 1)[None]
    pk_re, pk_im = pw_re[:chunk, :, None, :], pw_im[:chunk, :, None, :]
    akb_re = pk_re * bt_re - pk_im * bt_im
    akb_im = pk_re * bt_im + pk_im * bt_re
    cr, ci = c_re.astype(F32)[None], c_im.astype(F32)[None]
    pe_re, pe_im = pw_re[1:, :, None, :], pw_im[1:, :, None, :]
    e_re = cr * pe_re - ci * pe_im
    e_im = cr * pe_im + ci * pe_re
    kern = (jnp.einsum('kgqn,gpn->kgqp', akb_re, cr[0], precision=hi)
            - jnp.einsum('kgqn,gpn->kgqp', akb_im, ci[0], precision=hi))
    rows = lambda x: x.reshape(chunk, g * p, x.shape[-1])
    return ((rows(akb_re), rows(akb_im)), (rows(e_re), rows(e_im)), rows(kern), (pw_re, pw_im))


SSM_BLOCK_GROUPS = 8
SSM_BLOCK_CH = SSM_BLOCK_GROUPS * SSM_GROUP
SSM_BLOCK_STATE = SSM_BLOCK_GROUPS * SSM_STATE
N_SSM_BLOCKS = N_SSM_GROUPS // SSM_BLOCK_GROUPS


def _ssm_operator_kernel(akbre_ref, akbim_ref, ere_ref, eim_ref, kern_ref,
                         wre_ref, wim_ref, m_ref, etre_ref, etim_ref, *, chunk):
    gb, p, n = SSM_BLOCK_GROUPS, SSM_GROUP, SSM_STATE

    def copies(width, count):
        src = lax.broadcasted_iota(jnp.int32, (width, count * width), 0)
        dst = lax.broadcasted_iota(jnp.int32, (width, count * width), 1)
        return jnp.where(src == dst % width, 1.0, 0.0).astype(BF16)

    def own_group(width):
        row = lax.broadcasted_iota(jnp.int32, (SSM_BLOCK_CH, gb * width), 0)
        col = lax.broadcasted_iota(jnp.int32, (SSM_BLOCK_CH, gb * width), 1)
        return row // p == col // width

    to_states, own_states = copies(n, gb), own_group(n)

    def over_states(piece):
        return jnp.where(own_states, _dot(piece.astype(BF16), to_states), 0.0).astype(BF16)

    for s in range(chunk):
        rows = slice(s * SSM_BLOCK_CH, (s + 1) * SSM_BLOCK_CH)
        wre_ref[0, rows, :] = over_states(akbre_ref[chunk - 1 - s, 0])
        wim_ref[0, rows, :] = over_states(akbim_ref[chunk - 1 - s, 0])
        etre_ref[0, rows, :] = over_states(ere_ref[s, 0])
        etim_ref[0, rows, :] = over_states(-eim_ref[s, 0])
    to_channels, own_channels = copies(p, gb), own_group(p)
    lags = [jnp.where(own_channels, _dot(kern_ref[k, 0].astype(BF16), to_channels), 0.0).astype(BF16)
            for k in range(chunk)]
    zero = jnp.zeros((SSM_BLOCK_CH, SSM_BLOCK_CH), BF16)
    for s in range(chunk):
        m_ref[0, s * SSM_BLOCK_CH:(s + 1) * SSM_BLOCK_CH, :] = jnp.concatenate(
            [zero] * s + lags[:chunk - s], axis=1)


def _ssm_block_operators(chunk, terms):
    (akb_re, akb_im), (e_re, e_im), kern, (pw_re, pw_im) = terms
    assert chunk <= kern.shape[0]
    a_re, a_im = pw_re[chunk], pw_im[chunk]
    nb, gb, p, n = N_SSM_BLOCKS, SSM_BLOCK_GROUPS, SSM_GROUP, SSM_STATE
    pieces = [x[:chunk].reshape(chunk, nb, SSM_BLOCK_CH, x.shape[-1])
              for x in (akb_re, akb_im, e_re, e_im, kern)]
    piece_spec = lambda x: pl.BlockSpec((chunk, 1) + x.shape[2:], lambda i: (0, i, 0, 0))
    out_spec = lambda shape: pl.BlockSpec((1,) + shape[1:], lambda i: (i, 0, 0))
    x_width = chunk * SSM_BLOCK_CH
    out_shapes = [(nb, x_width, gb * n), (nb, x_width, gb * n), (nb, x_width, x_width),
                  (nb, x_width, gb * n), (nb, x_width, gb * n)]
    ops = pl.pallas_call(
        functools.partial(_ssm_operator_kernel, chunk=chunk),
        grid=(nb,),
        in_specs=[piece_spec(x) for x in pieces],
        out_specs=[out_spec(s) for s in out_shapes],
        out_shape=[jax.ShapeDtypeStruct(s, BF16) for s in out_shapes],
        compiler_params=_compiler_params("arbitrary"),
        name="ssm_operators",
    )(*pieces)
    return ops, a_re.reshape(nb, 1, gb * n), a_im.reshape(nb, 1, gb * n)


def _shorter_chunk_operators(ops, terms, chunk):
    w_re, w_im, m, et_re, et_im = ops
    pw_re, pw_im = terms[3]
    width = chunk * SSM_BLOCK_CH
    full = m.shape[1]
    assert width <= full
    nb, states = N_SSM_BLOCKS, SSM_BLOCK_STATE
    ops = [w_re[:, full - width:], w_im[:, full - width:], m[:, :width, :width], et_re[:, :width], et_im[:, :width]]
    return ops, pw_re[chunk].reshape(nb, 1, states), pw_im[chunk].reshape(nb, 1, states)


SSM_X_WIDTH = SSM_CHUNK * SSM_BLOCK_CH


def _ssm_prompt_kernel(u_ref, wre_ref, wim_ref, m_ref, etre_ref, etim_ref, are_ref, aim_ref, d_ref,
                       y_ref, hre_ref, him_ref, x_scr, sre_scr, sim_scr, *, batch, chunks):
    phase, b = pl.program_id(1), pl.program_id(2)
    for t in range(SSM_CHUNK):
        x_scr[:, t * SSM_BLOCK_CH:(t + 1) * SSM_BLOCK_CH] = u_ref[0, t].astype(BF16)
    rows = pl.ds(b, chunks, stride=batch)
    state_blocks = SSM_BLOCK_STATE // LANES

    @pl.when(phase == 0)
    def _():
        x = x_scr[...]
        g_re, g_im = _dot(x, wre_ref[0]), _dot(x, wim_ref[0])
        for k in range(state_blocks):
            sre_scr[k, rows, :] = g_re[:, k * LANES:(k + 1) * LANES]
            sim_scr[k, rows, :] = g_im[:, k * LANES:(k + 1) * LANES]

    @pl.when((phase == 1) & (b == 0))
    def _():
        same_lanes = lambda ref, k: ref[0][:, k * LANES:(k + 1) * LANES]

        def step(c, carry):
            same_chunk = pl.ds(c * batch, batch)
            out = []
            for k, (h_re, h_im) in enumerate(carry):
                a_re, a_im = same_lanes(are_ref, k), same_lanes(aim_ref, k)
                g_re, g_im = sre_scr[k, same_chunk, :], sim_scr[k, same_chunk, :]
                sre_scr[k, same_chunk, :] = h_re
                sim_scr[k, same_chunk, :] = h_im
                out.append((a_re * h_re - a_im * h_im + g_re, a_re * h_im + a_im * h_re + g_im))
            return tuple(out)

        zero = jnp.zeros((batch, LANES), F32)
        final = lax.fori_loop(0, chunks, step, ((zero, zero),) * state_blocks)
        for k, (h_re, h_im) in enumerate(final):
            hre_ref[0, :, k * LANES:(k + 1) * LANES] = h_re
            him_ref[0, :, k * LANES:(k + 1) * LANES] = h_im

    @pl.when(phase == 1)
    def _():
        h_re = jnp.concatenate([sre_scr[k, rows, :] for k in range(state_blocks)], axis=1).astype(BF16)
        h_im = jnp.concatenate([sim_scr[k, rows, :] for k in range(state_blocks)], axis=1).astype(BF16)
        pair_w = 2 * SSM_BLOCK_CH
        for j in range(SSM_CHUNK // 2):
            cols = slice(j * pair_w, (j + 1) * pair_w)
            k_in = (j + 1) * pair_w
            yj = (_dot(x_scr[:, :k_in], m_ref[0, :k_in, cols])
                  + _dot_nt(h_re, etre_ref[0, cols, :]) + _dot_nt(h_im, etim_ref[0, cols, :]))
            for i in range(2):
                t = 2 * j + i
                y_ref[pl.ds(t, chunks, stride=SSM_CHUNK), :] = (
                    yj[:, i * SSM_BLOCK_CH:(i + 1) * SSM_BLOCK_CH] + d_ref[0] * u_ref[0, t])


def _ssm_prompt(u, ops, a_re, a_im, d, batch, seq):
    chunks = seq // SSM_CHUNK
    d = d.reshape(N_SSM_BLOCKS, 1, SSM_BLOCK_CH)
    largest = max(o.size for o in ops)
    op_spec = lambda arr: pl.BlockSpec((1,) + arr.shape[1:], lambda g, ph, b: (g, 0, 0),
                                       pipeline_mode=pl.Buffered(1 if arr.size == largest else 2))
    state_spec = pl.BlockSpec((1, batch, SSM_BLOCK_STATE), lambda g, ph, b: (g, 0, 0))
    state_sds = jax.ShapeDtypeStruct((N_SSM_BLOCKS, batch, SSM_BLOCK_STATE), F32)
    y, h_re, h_im = pl.pallas_call(
        functools.partial(_ssm_prompt_kernel, batch=batch, chunks=chunks),
        grid=(N_SSM_BLOCKS, 2, batch),
        in_specs=[pl.BlockSpec((1, SSM_CHUNK, chunks, SSM_BLOCK_CH), lambda g, ph, b: (b, 0, 0, g)),
                  *[op_spec(o) for o in ops], op_spec(a_re), op_spec(a_im), op_spec(d)],
        out_specs=[pl.BlockSpec((seq, SSM_BLOCK_CH), lambda g, ph, b: (b * ph, g)), state_spec, state_spec],
        out_shape=[jax.ShapeDtypeStruct((batch * seq, SSM_WIDTH), F32), state_sds, state_sds],
        scratch_shapes=[pltpu.VMEM((chunks, SSM_X_WIDTH), BF16),
                        pltpu.VMEM((SSM_BLOCK_STATE // LANES, batch * chunks, LANES), F32),
                        pltpu.VMEM((SSM_BLOCK_STATE // LANES, batch * chunks, LANES), F32)],
        compiler_params=_compiler_params("arbitrary", "arbitrary", "arbitrary"),
        name="ssm_prompt",
    )(u, *ops, a_re, a_im, d)

    def by_sequence(h):
        h = h.reshape(N_SSM_BLOCKS, batch, SSM_BLOCK_GROUPS, SSM_STATE).transpose(1, 0, 2, 3)
        return h.reshape(batch, N_SSM_GROUPS, SSM_STATE)

    return y, by_sequence(h_re), by_sequence(h_im)


def _ssm_sample_kernel(u_ref, wre_ref, wim_ref, m_ref, etre_ref, etim_ref, are_ref, aim_ref, d_ref,
                       h0re_ref, h0im_ref, y_ref, hre_ref, him_ref, *, t_new, batch):
    for nb in range(N_SSM_BLOCKS):
        ch = slice(nb * SSM_BLOCK_CH, (nb + 1) * SSM_BLOCK_CH)
        st = slice(nb * SSM_BLOCK_STATE, (nb + 1) * SSM_BLOCK_STATE)
        x = jnp.concatenate([u_ref[t * batch:(t + 1) * batch, ch] for t in range(t_new)], axis=1).astype(BF16)
        h_re, h_im = h0re_ref[:, st], h0im_ref[:, st]
        a_re, a_im = are_ref[nb], aim_ref[nb]
        hre_ref[:, st] = a_re * h_re - a_im * h_im + _dot(x, wre_ref[nb])
        him_ref[:, st] = a_re * h_im + a_im * h_re + _dot(x, wim_ref[nb])
        y = (_dot(x, m_ref[nb]) + _dot_nt(h_re.astype(BF16), etre_ref[nb])
             + _dot_nt(h_im.astype(BF16), etim_ref[nb]))
        for t in range(t_new):
            rows = slice(t * batch, (t + 1) * batch)
            y_ref[rows, ch] = y[:, t * SSM_BLOCK_CH:(t + 1) * SSM_BLOCK_CH] + d_ref[:, ch] * u_ref[rows, ch]


def _ssm_sample(u, ops, a_re, a_im, d, h0_re, h0_im, t_new):
    batch = u.shape[0] // t_new
    args = (u, *ops, a_re, a_im, d, h0_re, h0_im)
    state_sds = jax.ShapeDtypeStruct((batch, N_SSM_GROUPS * SSM_STATE), F32)
    out_shape = [jax.ShapeDtypeStruct(u.shape, F32), state_sds, state_sds]
    whole = lambda shape: pl.BlockSpec(shape, lambda i, nd=len(shape): (0,) * nd)
    return pl.pallas_call(
        functools.partial(_ssm_sample_kernel, t_new=t_new, batch=batch),
        grid=(1,),
        in_specs=[whole(a.shape) for a in args],
        out_specs=[whole(s.shape) for s in out_shape],
        out_shape=out_shape,
        compiler_params=_compiler_params("arbitrary"),
        name="ssm_sample",
    )(*args)


def _gelu_tanh(x):
    return 0.5 * x * (1.0 + jnp.tanh(math.sqrt(2.0 / math.pi) * (x + 0.044715 * (x * x * x))))


def _merge_tile(o0_ref, l0_ref, o1_ref, l1_ref, o2_ref, l2_ref, ys_ref, gate_ref, x_ref,
                wglu_ref, bglu_ref, wba_ref, wbs_ref, wout_ref, order_scr, tm, dils):
    def row_order(ref, dil, slot):
        if dil == 1:
            return ref[0, 0]
        n = tm // dil
        halves = GROUP_WIDTH // LANES
        for r in range(dil):
            for k in range(halves):
                order_scr[slot * halves + k, pl.ds(r, n, stride=dil), :] = ref[0, r, :, k * LANES:(k + 1) * LANES]
        return jnp.concatenate([order_scr[slot * halves + k] for k in range(halves)], axis=1)

    parts = [row_order(ref, dils[i // 2], i) for i, ref in
             enumerate((o0_ref, l0_ref, o1_ref, l1_ref, o2_ref, l2_ref))]
    o0, l0, o1, l1, o2, l2 = parts
    mx = jnp.maximum(jnp.maximum(l0, l1), l2)
    e0, e1, e2 = jnp.exp(l0 - mx), jnp.exp(l1 - mx), jnp.exp(l2 - mx)
    attn = (e0 * o0 + e1 * o1 + e2 * o2) / (e0 + e1 + e2)
    attn = attn.astype(BF16)
    y = _gelu_tanh(ys_ref[...])
    y = (y * _sigmoid(_dot(y.astype(BF16), wglu_ref[...]) + bglu_ref[...])).astype(BF16)
    out = x_ref[...]
    half = D_MODEL // 2
    for c0 in (0, half):
        cols = slice(c0, c0 + half)
        mix = (gate_ref[:, cols].astype(F32) * _dot(attn, wba_ref[:, cols])
               + gate_ref[:, D_MODEL + c0:D_MODEL + c0 + half].astype(F32) * _dot(y, wbs_ref[:, cols]))
        out = out + _dot(mix.astype(BF16), wout_ref[cols, :])
    return out


MXU_WIDTH = 256
FF_SPLIT = -(-D_FF // (2 * MXU_WIDTH)) * MXU_WIDTH
FF_CHUNKS = ((0, FF_SPLIT), (FF_SPLIT, D_FF))
N_MERGE_ROW_INPUTS = 2 * N_DIL_GROUPS + 3
N_MERGE_WEIGHTS = 5


def _merge_ffn_kernel(*refs, tm, shift, pad, dils):
    n_merge = N_MERGE_ROW_INPUTS + N_MERGE_WEIGHTS
    merge_refs, rest = refs[:n_merge], refs[n_merge:]
    carry_ref, g2_ref, wup_ref, cw_ref, cb_ref, wdn_ref, gf_ref, y_ref, state_ref, order_scr, a_scr = rest
    hist = 2 * shift

    @pl.when(pl.program_id(1) == 0)
    def _():
        a_scr[pad - hist:pad, :] = carry_ref[0]

    xf = _merge_tile(*merge_refs, order_scr, tm, dils)
    xn = _rmsnorm(xf, g2_ref[...]).astype(BF16)
    acc = jnp.zeros((tm, D_MODEL), F32)
    for c0, c1 in FF_CHUNKS:
        cols = slice(c0, c1)
        a = _dot(xn, wup_ref[:, cols])
        val = _dot(xn, wup_ref[:, D_FF + c0:D_FF + c1])
        a_scr[pad:pad + tm, cols] = a
        a_m1 = a_scr[pad - shift:pad - shift + tm, cols]
        a_m2 = a_scr[pad - hist:pad - hist + tm, cols]
        conv = cb_ref[:, cols] + cw_ref[0:1, cols] * a_m2
        conv = conv + cw_ref[1:2, cols] * a_m1
        conv = conv + cw_ref[2:3, cols] * a
        act = conv * _sigmoid(conv) * val
        acc = acc + _dot(act.astype(BF16), wdn_ref[cols, :])
    tail = a_scr[pad + tm - hist:pad + tm, :]
    a_scr[pad - hist:pad, :] = tail
    state_ref[0] = tail
    y_ref[...] = _rmsnorm(xf + acc, gf_ref[...])


def _merge_ffn(attn_parts, ys, gates, x2d, merge_weights, carry, ffn_weights, n_seq, tm, shift):
    m = x2d.shape[0]
    tiles = m // n_seq // tm
    hist = 2 * shift
    pad = -(-hist // SUBLANES) * SUBLANES
    dils = tuple(p.shape[1] for p in attn_parts[::2])
    row = lambda width: pl.BlockSpec((tm, width), lambda b, j: (b * tiles + j, 0))
    res_spec = lambda dil: pl.BlockSpec((1, dil, tm // dil, GROUP_WIDTH), lambda b, j: (b, 0, j, 0))
    state_spec = pl.BlockSpec((1, hist, D_FF), lambda b, j: (b, 0, 0))
    assert len(merge_weights) == N_MERGE_WEIGHTS
    return pl.pallas_call(
        functools.partial(_merge_ffn_kernel, tm=tm, shift=shift, pad=pad, dils=dils),
        grid=(n_seq, tiles),
        in_specs=[*[res_spec(p.shape[1]) for p in attn_parts],
                  row(SSM_WIDTH), row(2 * D_MODEL), row(D_MODEL),
                  *[_resident(w.shape) for w in merge_weights],
                  state_spec, *[_resident(w.shape) for w in ffn_weights]],
        out_specs=[row(D_MODEL), state_spec],
        out_shape=[jax.ShapeDtypeStruct((m, D_MODEL), F32),
                   jax.ShapeDtypeStruct((n_seq, hist, D_FF), F32)],
        scratch_shapes=[pltpu.VMEM((len(attn_parts) * GROUP_WIDTH // LANES, tm, LANES), F32),
                        pltpu.VMEM((pad + tm, D_FF), F32)],
        compiler_params=_compiler_params("arbitrary", "arbitrary"),
        name="merge_ffn",
    )(*attn_parts, ys, gates, x2d, *merge_weights, carry, *ffn_weights)


def _kv_rows(qkv, keep):
    batch, dil, length, _ = qkv.shape
    n = keep // dil
    rows = qkv[:, :, length - n:, GROUP_WIDTH:]
    cols = rows.transpose(0, 3, 2, 1).reshape(batch, 2, HEADS_PER_GROUP, HEAD_DIM, keep)
    return cols.transpose(0, 4, 1, 2, 3)


def _prompt_layer(x, rel_bias, lw):
    batch, seq, _ = x.shape
    x2d = x.reshape(batch * seq, D_MODEL)
    dils = tuple(dil for _, dil in DIL_PATTERNS) + (SSM_CHUNK,)
    keeps = tuple(min(window, seq) for window, _ in DIL_PATTERNS)
    *qkvs, u, gates, kv0, kv1, kv2 = _in_proj(x2d, lw['norm1_g'], lw['w_in'], batch, IN_PROJ_TILE, dils, keeps)

    attn_parts, kv_new = [], []
    for g, (window, dil) in enumerate(DIL_PATTERNS):
        tab = rel_bias[:, g * HEADS_PER_GROUP:(g + 1) * HEADS_PER_GROUP]
        attn_parts.extend(_attn_prompt(qkvs[g], _prompt_bias(tab, dil)))
        tail = (kv0, kv1, kv2)[g].reshape(batch, 2, HEADS_PER_GROUP, HEAD_DIM, keeps[g])
        kv_new.append(tail.transpose(0, 4, 1, 2, 3))

    ops, a_re, a_im = lw['ssm_ops']
    ys, h_re, h_im = _ssm_prompt(u, ops, a_re, a_im, lw['ssm_d'], batch, seq)

    carry = jnp.zeros((batch, CONV_W - 1, D_FF), F32)
    y, conv_state = _merge_ffn(attn_parts, ys, gates, x2d, lw['merge'], carry, lw['ffn'],
                               n_seq=batch, tm=MERGE_FFN_TILE, shift=1)
    return y.reshape(batch, seq, D_MODEL), (*kv_new, h_re, h_im, conv_state)


def _sample_layer(x, caches, h0_re, h0_im, conv_buf, rel_bias, lw):
    batch, t_new, _ = x.shape
    m = batch * t_new
    x2d = x.transpose(1, 0, 2).reshape(m, D_MODEL)
    *qkvs, u, gates = _in_proj(x2d, lw['norm1_g'], lw['w_in'], 1, m, (1,) * (N_DIL_GROUPS + 1))
    u = u.reshape(m, SSM_WIDTH)
    qkvs_bt = [q.reshape(t_new, batch, GROUP_QKV).transpose(1, 0, 2) for q in qkvs]

    tbs, tns, views = [], [], []
    for g, (window, dil) in enumerate(DIL_PATTERNS):
        tab = rel_bias[:, g * HEADS_PER_GROUP:(g + 1) * HEADS_PER_GROUP]
        n_cached = caches[g].shape[1]
        tb, tn = _sample_bias(tab, dil, t_new, n_cached)
        tbs.append(tb)
        tns.append(tn)
        views.append(caches[g].transpose(0, 2, 3, 4, 1).reshape(batch, 2 * GROUP_WIDTH, n_cached))
    parts = _attn_sample(qkvs_bt, views, tbs, jnp.stack(tns))
    attn_parts = [p.transpose(1, 0, 2).reshape(1, 1, m, GROUP_WIDTH) for p in parts]
    kv_new = [_kv_rows(q.reshape(batch, 1, t_new, GROUP_QKV), t_new) for q in qkvs_bt]

    ops, a_re, a_im = _shorter_chunk_operators(lw['ssm_ops'][0], lw['ssm_terms'], t_new)
    ys, h_re, h_im = _ssm_sample(u, ops, a_re, a_im, lw['ssm_d'],
                                 h0_re.reshape(batch, -1), h0_im.reshape(batch, -1), t_new)

    carry = conv_buf.transpose(1, 0, 2).reshape(1, (CONV_W - 1) * batch, D_FF)
    y, conv_state = _merge_ffn(attn_parts, ys, gates, x2d, lw['merge'], carry, lw['ffn'],
                               n_seq=1, tm=m, shift=batch)
    y = y.reshape(t_new, batch, D_MODEL).transpose(1, 0, 2)
    conv_state = conv_state.reshape(CONV_W - 1, batch, D_FF).transpose(1, 0, 2)
    state_shape = (batch, N_SSM_GROUPS, SSM_STATE)
    return y, (*kv_new, h_re.reshape(state_shape), h_im.reshape(state_shape), conv_state)


IN_PROJ_TILE = 512
MERGE_FFN_TILE = 512


def kernel(x_prompt, x_sample, cache_kv_w128, cache_kv_w512, cache_kv_w2048, state_ssm_re, state_ssm_im, state_ffn_conv, rel_bias, norm1_g, w_in, ssm_log_dt, ssm_lambda_re, ssm_lambda_im, ssm_b_re, ssm_b_im, ssm_c_re, ssm_c_im, ssm_d, w_glu, b_glu, w_branch_attn, w_branch_ssm, w_out, norm2_g, w_up, conv_w, conv_b, w_down, norm_f_g):
    depth = w_in.shape[0]
    hp, hs = x_prompt, x_sample
    st_p, st_s = [], []
    gf = norm_f_g.reshape(1, D_MODEL)
    for l in range(depth):
        last = l == depth - 1
        assert x_sample.shape[1] <= SSM_CHUNK, "the sample chunk operators are cut out of the prompt's"
        ssm_terms = _ssm_chunk_terms(SSM_CHUNK, ssm_log_dt[l], ssm_lambda_re[l], ssm_lambda_im[l],
                                     ssm_b_re[l], ssm_b_im[l], ssm_c_re[l], ssm_c_im[l])
        lw = {
            'norm1_g': norm1_g[l].reshape(1, D_MODEL),
            'w_in': _group_major_columns(w_in[l]).astype(BF16),
            'ssm_terms': ssm_terms,
            'ssm_ops': _ssm_block_operators(SSM_CHUNK, ssm_terms),
            'ssm_d': ssm_d[l].reshape(1, SSM_WIDTH),
            'merge': (w_glu[l].astype(BF16), b_glu[l].reshape(1, SSM_WIDTH),
                      w_branch_attn[l].astype(BF16), w_branch_ssm[l].astype(BF16), w_out[l].astype(BF16)),
            'ffn': (norm2_g[l].reshape(1, D_MODEL), w_up[l].astype(BF16), conv_w[l],
                    conv_b[l].reshape(1, D_FF), w_down[l].astype(BF16), gf),
        }
        assert last, "the final RMSNorm is fused into the last layer's ffn kernel"
        hp, sp = _prompt_layer(hp, rel_bias, lw)
        hs, ss = _sample_layer(hs, (cache_kv_w128[l], cache_kv_w512[l], cache_kv_w2048[l]),
                               state_ssm_re[l], state_ssm_im[l], state_ffn_conv[l], rel_bias, lw)
        st_p.append(sp)
        st_s.append(ss)
    stack = lambda states, i: jnp.stack([st[i] for st in states], axis=0)
    return (hp, hs, *[stack(st_p, i) for i in range(6)], *[stack(st_s, i) for i in range(6)])
```
